```python
import jax, jax.numpy as jnp
from jax import lax
import numpy as np

D_MODEL = 2048
BATCH = 16
SEQ = 256
DEPTH = 2
DEC_BATCH = 4
DEC_SEQ = 1024
PAST_LEN = 256

GRID_W = 64
D_MIX = D_MODEL
HEAD_DIM = 128
N_HEADS = (D_MIX // 2) // HEAD_DIM
N_KV_HEADS = 2
GROUP = N_HEADS // N_KV_HEADS
D_ATTN = N_HEADS * HEAD_DIM
D_KV = N_KV_HEADS * HEAD_DIM
ROPE_THETA = 10000.0
Q_BLOCK = 128
D_CONV = D_MIX // 4
CONV_WIDTH = 3
D_RWKV = D_MIX // 4
RWKV_HEAD = 64
RWKV_HEADS = D_RWKV // RWKV_HEAD
DECAY_RANK = 32
A_RANK = 32
GATE_RANK = 96
D_RWKV_IN = 3 * D_RWKV + DECAY_RANK + A_RANK + GATE_RANK
D_IN = D_ATTN + 2 * D_KV + 3 * D_CONV + D_RWKV_IN
SPLIT_IN = [D_ATTN, D_ATTN + D_KV, D_ATTN + 2 * D_KV, D_ATTN + 2 * D_KV + D_CONV,
            D_ATTN + 2 * D_KV + 2 * D_CONV, D_ATTN + 2 * D_KV + 3 * D_CONV]
SPLIT_RWKV = [D_RWKV, 2 * D_RWKV, 3 * D_RWKV, 3 * D_RWKV + DECAY_RANK,
              3 * D_RWKV + DECAY_RANK + A_RANK]
D_FF = -(-(8 * D_MODEL) // (3 * 256)) * 256
N_MOD = 6
EPS = 1e-6
GN_EPS = 64e-5

kernel_name = 'hybrid_dit_parallel_heads_step'


def rmsnorm(x, g):
    xf = x.astype(jnp.float32)
    y = xf * lax.rsqrt(jnp.mean(xf * xf, axis=-1, keepdims=True) + EPS)
    return (y * g.astype(jnp.float32)).astype(x.dtype)


def axial_rope(n):
    rows = n // GRID_W
    row = jnp.repeat(jnp.arange(rows), GRID_W).astype(jnp.float32)
    col = jnp.tile(jnp.arange(GRID_W), rows).astype(jnp.float32)
    half = HEAD_DIM // 2
    inv = 1.0 / (ROPE_THETA ** (jnp.arange(0, half, 2, dtype=jnp.float32) / half))
    ar = row[:, None] * inv
    ac = col[:, None] * inv
    ang = jnp.concatenate([ar, ar, ac, ac], axis=-1)
    return jnp.cos(ang), jnp.sin(ang)


def apply_rope(x, cos, sin):
    x1, x2, x3, x4 = jnp.split(x, 4, axis=-1)
    rot = jnp.concatenate([-x2, x1, -x4, x3], axis=-1)
    cos = cos[None, :, None, :].astype(x.dtype)
    sin = sin[None, :, None, :].astype(x.dtype)
    return x * cos + rot * sin


def shift_prev(u):
    return jnp.pad(u, ((0, 0), (1, 0), (0, 0)))[:, :-1]


def shift_next(u):
    return jnp.pad(u, ((0, 0), (0, 1), (0, 0)))[:, 1:]


def short_conv(u, w):
    return shift_prev(u) * w[0] + u * w[1] + shift_next(u) * w[2]


def attention(q, k, v):
    b, n = q.shape[:2]
    nb = n // Q_BLOCK
    qb = q.reshape(b, nb, Q_BLOCK, N_KV_HEADS, GROUP, HEAD_DIM).swapaxes(0, 1)
    scale = HEAD_DIM ** -0.5

    def block(qi):
        s = jnp.einsum('bqkgd,bskd->bkgqs', qi, k).astype(jnp.float32) * scale
        p = jax.nn.softmax(s, axis=-1).astype(v.dtype)
        return jnp.einsum('bkgqs,bskd->bqkgd', p, v)

    o = lax.map(block, qb)
    return o.swapaxes(0, 1).reshape(b, n, D_ATTN)


def rwkv_scan(r, w, k, v, kk, a, s0, reverse):
    xs = tuple(t.swapaxes(0, 1) for t in (r, w, k, v, kk, a))

    def step(S, inp):
        r_t, w_t, k_t, v_t, kk_t, a_t = inp
        sa = jnp.einsum('bhij,bhj->bhi', S, -kk_t)
        S = (S * w_t[:, :, None, :] + sa[..., None] * (kk_t * a_t)[:, :, None, :]
             + v_t[..., None] * k_t[:, :, None, :])
        return S, jnp.einsum('bhij,bhj->bhi', S, r_t)

    S, y = lax.scan(step, s0, xs, reverse=reverse)
    return y.swapaxes(0, 1), S


def rwkv_mixer(u, s0, P, l):
    dt = u.dtype
    b, n = u.shape[:2]
    u = u + P['rwkv_mu'][l] * (0.5 * (shift_prev(u) + shift_next(u)) - u)
    u = u.astype(jnp.float32)
    r, k, v, xw, xa, xg = jnp.split(u, SPLIT_RWKV, axis=-1)

    def hs(t):
        return t.reshape(b, n, RWKV_HEADS, RWKV_HEAD)

    def f(name):
        return P[name][l].astype(jnp.float32)

    g = jax.nn.sigmoid(xg) @ f('rwkv_g_up')
    kk = hs(k * f('rwkv_k_k'))
    kk = kk * lax.rsqrt(jnp.sum(kk * kk, axis=-1, keepdims=True) + 1e-12)
    rh, vh = hs(r), hs(v)
    r_k = f('rwkv_r_k')
    s0 = s0.astype(jnp.float32)
    ys, bonuses, states = [], [], []
    for d in range(2):
        w = -jax.nn.softplus(-(f('rwkv_w0')[d] + jnp.tanh(xw) @ f('rwkv_w_up')[d])) - 0.5
        decay = jnp.exp(-jnp.exp(w))
        a = jax.nn.sigmoid(f('rwkv_a0')[d] + xa @ f('rwkv_a_up')[d])
        kd = hs(k * (1.0 + (a - 1.0) * f('rwkv_k_a')))
        yd, Sd = rwkv_scan(rh, hs(decay), kd, vh, kk, hs(a), s0[:, d], reverse=(d == 1))
        ys.append(yd)
        bonuses.append(jnp.sum(rh * kd * r_k, axis=-1, keepdims=True) * vh)
        states.append(Sd)
    y = ys[0] + ys[1]
    mean = jnp.mean(y, axis=-1, keepdims=True)
    var = jnp.mean(jnp.square(y - mean), axis=-1, keepdims=True)
    yn = ((y - mean) * lax.rsqrt(var + GN_EPS)).reshape(b, n, D_RWKV) * f('rwkv_ln_g') + f('rwkv_ln_b')
    out = (yn + (bonuses[0] + bonuses[1]).reshape(b, n, D_RWKV)) * g
    return out.astype(dt), jnp.stack(states, axis=1)


def layer(x, mod, P, l, rope, ctx_k, ctx_v, s0):
    b, n = x.shape[:2]
    sh_a, sc_a, g_a, sh_f, sc_f, g_f = jnp.split(mod, N_MOD, axis=-1)
    ng = P['norm_g'][l]
    h = rmsnorm(x, ng[0]) * (1 + sc_a) + sh_a
    proj = h @ P['w_in'][l]
    q, k, v, cb, cc, ch, ur = jnp.split(proj, SPLIT_IN, axis=-1)
    q = rmsnorm(q.reshape(b, n, N_HEADS, HEAD_DIM), P['q_norm'][l])
    k = rmsnorm(k.reshape(b, n, N_KV_HEADS, HEAD_DIM), P['k_norm'][l])
    v = v.reshape(b, n, N_KV_HEADS, HEAD_DIM)
    if rope is None:
        keys, vals = k, v
    else:
        q = apply_rope(q, rope[0], rope[1])
        k = apply_rope(k, rope[0], rope[1])
        keys = jnp.concatenate([k, ctx_k], axis=1)
        vals = jnp.concatenate([v, ctx_v], axis=1)
    attn = attention(q.reshape(b, n, N_KV_HEADS, GROUP, HEAD_DIM), keys, vals)
    conv = cb * short_conv(cc * ch, P['conv_w'][l])
    rw, S = rwkv_mixer(ur, s0, P, l)
    o = jnp.concatenate([attn, conv, rw], axis=-1) @ P['w_out'][l]
    x = x + g_a * rmsnorm(o, ng[1])
    h = rmsnorm(x, ng[2]) * (1 + sc_f) + sh_f
    ff = (jax.nn.silu(h @ P['w_gate'][l]) * (h @ P['w_up'][l])) @ P['w_down'][l]
    x = x + g_f * rmsnorm(ff, ng[3])
    return x, k, v, S


def setup_inputs(seed: int = 0) -> dict:
    key = jax.random.key(seed)
    ks = jax.random.split(key, 32)

    def nrm(i, shape, scale):
        return jax.random.normal(ks[i], shape, jnp.float32) * scale

    L, D = DEPTH, D_MODEL
    return {
        'x_prompt': nrm(0, (BATCH, SEQ, D), 1.0),
        'x_sample': nrm(1, (DEC_BATCH, DEC_SEQ, D), 1.0),
        'cache_k': nrm(2, (DEC_BATCH, L, PAST_LEN, N_KV_HEADS, HEAD_DIM), 1.0),
        'cache_v': nrm(3, (DEC_BATCH, L, PAST_LEN, N_KV_HEADS, HEAD_DIM), 1.0),
        'state_rwkv': nrm(4, (DEC_BATCH, L, 2, RWKV_HEADS, RWKV_HEAD, RWKV_HEAD), 0.1),
        'c': nrm(5, (DEC_BATCH, D), 1.0),
        'c_ctx': nrm(6, (D,), 1.0),
        'w_ada': nrm(7, (L, D, N_MOD * D), D ** -0.5),
        'b_ada': nrm(8, (L, N_MOD * D), 0.01),
        'norm_g': 1.0 + nrm(9, (L, 4, D), 0.05),
        'w_in': nrm(10, (L, D, D_IN), D ** -0.5),
        'q_norm': 1.0 + nrm(11, (L, HEAD_DIM), 0.05),
        'k_norm': 1.0 + nrm(12, (L, HEAD_DIM), 0.05),
        'conv_w': nrm(13, (L, CONV_WIDTH, D_CONV), CONV_WIDTH ** -0.5),
        'rwkv_mu': 0.5 + nrm(14, (L, D_RWKV_IN), 0.1),
        'rwkv_w0': nrm(15, (L, 2, D_RWKV), 1.0),
        'rwkv_w_up': nrm(16, (L, 2, DECAY_RANK, D_RWKV), 0.1),
        'rwkv_a0': nrm(17, (L, 2, D_RWKV), 0.1),
        'rwkv_a_up': nrm(18, (L, 2, A_RANK, D_RWKV), 0.1),
        'rwkv_g_up': nrm(19, (L, GATE_RANK, D_RWKV), GATE_RANK ** -0.5),
        'rwkv_k_k': 0.85 + nrm(20, (L, D_RWKV), 0.02),
        'rwkv_k_a': 1.0 + nrm(21, (L, D_RWKV), 0.02),
        'rwkv_r_k': nrm(22, (L, RWKV_HEADS, RWKV_HEAD), 0.1),
        'rwkv_ln_g': 1.0 + nrm(23, (L, D_RWKV), 0.05),
        'rwkv_ln_b': nrm(24, (L, D_RWKV), 0.01),
        'w_out': nrm(25, (L, D_MIX, D), D_MIX ** -0.5),
        'w_gate': nrm(26, (L, D, D_FF), D ** -0.5),
        'w_up': nrm(27, (L, D, D_FF), D ** -0.5),
        'w_down': nrm(28, (L, D_FF, D), D_FF ** -0.5),
    }


def reference(x_prompt, x_sample, cache_k, cache_v, state_rwkv, c, c_ctx, w_ada, b_ada, norm_g,
              w_in, q_norm, k_norm, conv_w, rwkv_mu, rwkv_w0, rwkv_w_up, rwkv_a0, rwkv_a_up,
              rwkv_g_up, rwkv_k_k, rwkv_k_a, rwkv_r_k, rwkv_ln_g, rwkv_ln_b, w_out, w_gate, w_up,
              w_down):
    P = {'norm_g': norm_g, 'w_in': w_in, 'q_norm': q_norm, 'k_norm': k_norm, 'conv_w': conv_w,
         'rwkv_mu': rwkv_mu, 'rwkv_w0': rwkv_w0, 'rwkv_w_up': rwkv_w_up, 'rwkv_a0': rwkv_a0,
         'rwkv_a_up': rwkv_a_up, 'rwkv_g_up': rwkv_g_up, 'rwkv_k_k': rwkv_k_k,
         'rwkv_k_a': rwkv_k_a, 'rwkv_r_k': rwkv_r_k, 'rwkv_ln_g': rwkv_ln_g,
         'rwkv_ln_b': rwkv_ln_b, 'w_out': w_out, 'w_gate': w_gate, 'w_up': w_up,
         'w_down': w_down}

    xp = x_prompt
    s_zero = jnp.zeros((xp.shape[0], 2, RWKV_HEADS, RWKV_HEAD, RWKV_HEAD), jnp.float32)
    ks, vs, ss = [], [], []
    for l in range(DEPTH):
        mod = (jax.nn.silu(c_ctx) @ w_ada[l] + b_ada[l])[None, None, :]
        xp, k_l, v_l, s_l = layer(xp, mod, P, l, None, None, None, s_zero)
        ks.append(k_l)
        vs.append(v_l)
        ss.append(s_l.astype(x_prompt.dtype))
    new_k = jnp.stack(ks, axis=1)
    new_v = jnp.stack(vs, axis=1)
    new_rwkv = jnp.stack(ss, axis=1)

    rope = axial_rope(x_sample.shape[1])
    xs = x_sample
    for l in range(DEPTH):
        mod = (jax.nn.silu(c) @ w_ada[l] + b_ada[l])[:, None, :]
        xs, _, _, _ = layer(xs, mod, P, l, rope, cache_k[:, l], cache_v[:, l], state_rwkv[:, l])

    return (xp, xs, new_k, new_v, new_rwkv)
```

```python
import functools
import math

import jax
import jax.numpy as jnp
from jax import lax
from jax.experimental import pallas as pl
from jax.experimental.pallas import tpu as pltpu

D_MODEL = 2048
DEPTH = 2
GRID_W = 64
HEAD_DIM = 128
N_HEADS = 8
N_KV_HEADS = 2
GROUP = N_HEADS // N_KV_HEADS
D_ATTN = N_HEADS * HEAD_DIM
D_KV = N_KV_HEADS * HEAD_DIM
ROPE_THETA = 10000.0
D_CONV = 512
D_RWKV = 512
RWKV_HEAD = 64
RWKV_HEADS = D_RWKV // RWKV_HEAD
DECAY_RANK = 32
A_RANK = 32
GATE_RANK = 96
D_TAIL = DECAY_RANK + A_RANK + GATE_RANK
D_IN = D_ATTN + 2 * D_KV + 3 * D_CONV + 3 * D_RWKV + D_TAIL
D_MAIN = D_IN - D_TAIL
D_FF = 5632
N_MOD = 6
EPS = 1e-6
GN_EPS = 64e-5
PAST_LEN = 256

VMEM_LIMIT_BYTES = 56 * 1024 * 1024
CHUNK = 64
HI = lax.Precision.HIGHEST
BF16 = jnp.bfloat16
F32 = jnp.float32


def _params(n_grid, **kw):
    sem = ("parallel",) + ("arbitrary",) * (n_grid - 1)
    return pltpu.CompilerParams(dimension_semantics=sem, vmem_limit_bytes=VMEM_LIMIT_BYTES, **kw)


def _rms(x, g):
    return x * lax.rsqrt(jnp.mean(x * x, axis=-1, keepdims=True) + EPS) * g


def _dot(a, b):
    return jnp.dot(a.astype(BF16), b.astype(BF16), preferred_element_type=F32)


_NN = (((1,), (0,)), ((), ()))
_NT = (((1,), (1,)), ((), ()))
_TN = (((0,), (0,)), ((), ()))


def _dot_hi(a, b, dims=_NN):
    return lax.dot_general(a, b, dims, precision=HI, preferred_element_type=F32)


def _ada_kernel(c_ref, w_ref, b_ref, o_ref):
    c = c_ref[...]
    s = c * jax.nn.sigmoid(c)
    o_ref[...] = _dot(s, w_ref[...]) + b_ref[...]


def _ada_call(cc, w_ada, b_ada):
    tn = 512
    n_out = N_MOD * D_MODEL
    return pl.pallas_call(
        _ada_kernel,
        grid=(DEPTH, n_out // tn),
        in_specs=[
            pl.BlockSpec((8, D_MODEL), lambda l, n: (0, 0)),
            pl.BlockSpec((None, D_MODEL, tn), lambda l, n: (l, 0, n)),
            pl.BlockSpec((None, 1, tn), lambda l, n: (l, 0, n)),
        ],
        out_specs=pl.BlockSpec((None, 8, tn), lambda l, n: (l, 0, n)),
        out_shape=jax.ShapeDtypeStruct((DEPTH, 8, n_out), F32),
        compiler_params=_params(2),
        name="adaln",
    )(cc, w_ada, b_ada.reshape(DEPTH, 1, n_out))


def _mod_spec(row_fn, chunk):
    return pl.BlockSpec((None, 1, D_MODEL), lambda i, *_: (row_fn(i), 0, chunk))


def _ng_spec(l, j):
    return pl.BlockSpec((None, 1, D_MODEL), lambda i, *_: (l * 4 + j, 0, 0))


def _inproj_kernel(x_ref, sh_ref, sc_ref, g_ref, w_ref, wt_ref, o_ref, ot_ref, h_scr):
    @pl.when(pl.program_id(1) == 0)
    def _():
        h = _rms(x_ref[...], g_ref[...]) * (1.0 + sc_ref[...]) + sh_ref[...]
        h_scr[...] = h.astype(BF16)
        ot_ref[...] = _dot(h_scr[...], wt_ref[...])

    o_ref[...] = _dot(h_scr[...], w_ref[...])


def _inproj_call(x, mod3, ng3, w_in, w_tail, l, row_fn, tm):
    T = x.shape[0]
    tn = 512
    return pl.pallas_call(
        _inproj_kernel,
        grid=(T // tm, D_MAIN // tn),
        in_specs=[
            pl.BlockSpec((tm, D_MODEL), lambda i, n: (i, 0)),
            _mod_spec(row_fn, 0),
            _mod_spec(row_fn, 1),
            _ng_spec(l, 0),
            pl.BlockSpec((None, D_MODEL, tn), lambda i, n: (l, 0, n)),
            pl.BlockSpec((D_MODEL, D_TAIL), lambda i, n: (0, 0)),
        ],
        out_specs=[
            pl.BlockSpec((tm, tn), lambda i, n: (i, n)),
            pl.BlockSpec((tm, D_TAIL), lambda i, n: (i, 0)),
        ],
        out_shape=[
            jax.ShapeDtypeStruct((T, D_MAIN), F32),
            jax.ShapeDtypeStruct((T, D_TAIL), F32),
        ],
        scratch_shapes=[pltpu.VMEM((tm, D_MODEL), BF16)],
        compiler_params=_params(2),
        name="in_proj",
    )(x, mod3, mod3, ng3, w_in, w_tail)


def _rope(x, cos, sin):
    lane = lax.broadcasted_iota(jnp.int32, x.shape, 1)
    first = (lane % (HEAD_DIM // 2)) < (HEAD_DIM // 4)
    rot = jnp.where(first, -pltpu.roll(x, HEAD_DIM - HEAD_DIM // 4, 1), pltpu.roll(x, HEAD_DIM // 4, 1))
    return x * cos + rot * sin


def _attn_kernel(*refs, n, past, rope, emit_kv, qb):
    it = iter(refs)
    q_ref, k_ref, v_ref, qn_ref, kn_ref = (next(it) for _ in range(5))
    cos_ref = sin_ref = ck_ref = cv_ref = None
    if rope:
        cos_ref, sin_ref = next(it), next(it)
    if past:
        ck_ref, cv_ref = next(it), next(it)
    o_ref = next(it)
    if emit_kv:
        ko_ref, vo_ref = next(it), next(it)
    k_scr, v_scr = next(it), next(it)

    k = _rms(k_ref[...], kn_ref[...])
    v = v_ref[...]
    if emit_kv:
        ko_ref[...] = k
        vo_ref[...] = v
    if rope:
        k = _rope(k, cos_ref[...], sin_ref[...])
    k_scr[0:n, :] = k.astype(BF16)
    v_scr[0:n, :] = v.astype(BF16)
    if past:
        k_scr[n:n + past, :] = ck_ref[...].astype(BF16)
        v_scr[n:n + past, :] = cv_ref[...].astype(BF16)
    scale = HEAD_DIM ** -0.5

    def block(b, carry):
        rows = pl.ds(pl.multiple_of(b * qb, qb), qb)
        for g in range(GROUP):
            cols = slice(g * HEAD_DIM, (g + 1) * HEAD_DIM)
            q = _rms(q_ref[rows, cols], qn_ref[...])
            if rope:
                q = _rope(q, cos_ref[rows, :], sin_ref[rows, :])
            s = lax.dot_general(q.astype(BF16), k_scr[...], _NT, preferred_element_type=F32) * scale
            p = jnp.exp(s - jnp.max(s, axis=-1, keepdims=True))
            den = jnp.sum(p, axis=-1, keepdims=True)
            o = jnp.dot(p.astype(BF16), v_scr[...], preferred_element_type=F32)
            o_ref[rows, cols] = o / den
        return carry

    lax.fori_loop(0, n // qb, block, 0)


def _attn_call(proj, qn, kn, l, n, rope_tabs=None, cache=None, emit_kv=False):
    T = proj.shape[0]
    B = T // n
    past = PAST_LEN if cache is not None else 0
    qw = GROUP * HEAD_DIM
    k_blk = D_ATTN // HEAD_DIM
    v_blk = (D_ATTN + D_KV) // HEAD_DIM
    in_specs = [
        pl.BlockSpec((n, qw), lambda b, h: (b, h)),
        pl.BlockSpec((n, HEAD_DIM), lambda b, h: (b, k_blk + h)),
        pl.BlockSpec((n, HEAD_DIM), lambda b, h: (b, v_blk + h)),
        pl.BlockSpec((1, HEAD_DIM), lambda b, h: (0, 0)),
        pl.BlockSpec((1, HEAD_DIM), lambda b, h: (0, 0)),
    ]
    args = [proj, proj, proj, qn, kn]
    if rope_tabs is not None:
        in_specs += [pl.BlockSpec((n, HEAD_DIM), lambda b, h: (0, 0))] * 2
        args += list(rope_tabs)
    if cache is not None:
        in_specs += [pl.BlockSpec((None, None, PAST_LEN, HEAD_DIM), lambda b, h: (b, l, 0, h))] * 2
        args += list(cache)
    out_specs = [pl.BlockSpec((n, qw), lambda b, h: (b, h))]
    out_shape = [jax.ShapeDtypeStruct((T, D_ATTN), F32)]
    if emit_kv:
        out_specs += [pl.BlockSpec((None, n, HEAD_DIM), lambda b, h: (b, 0, h))] * 2
        out_shape += [jax.ShapeDtypeStruct((B, n, D_KV), F32)] * 2
    return pl.pallas_call(
        functools.partial(_attn_kernel, n=n, past=past, rope=rope_tabs is not None,
                          emit_kv=emit_kv, qb=256),
        grid=(B, N_KV_HEADS),
        in_specs=in_specs,
        out_specs=out_specs,
        out_shape=out_shape,
        scratch_shapes=[pltpu.VMEM((n + past, HEAD_DIM), BF16)] * 2,
        compiler_params=_params(2),
        name="attention",
    )(*args)


ROWS_A = 128


def _mix_kernel(*refs, n, has_s0, emit_state):
    it = iter(refs)
    cv_ref, um_ref, ut_ref = next(it), next(it), next(it)
    (mum_ref, mut_ref, cw_ref, w0_ref, wup_ref, a0_ref, aup_ref, gup_ref, kk_ref, ka_ref,
     rk_ref, lng_ref, lnb_ref, seg_ref) = (next(it) for _ in range(14))
    s0_ref = next(it) if has_s0 else None
    o_ref = next(it)
    so_ref = next(it) if emit_state else None
    um_s, ut_s, kk_s, y_s, s_scr = (next(it) for _ in range(5))
    n_blocks = n // ROWS_A
    n_chunks = n // CHUNK

    def seg_sum(x):
        return _dot_hi(x, seg_ref[...])

    def pass1(c, carry):
        c0 = pl.multiple_of(c * ROWS_A, ROWS_A)
        rows = pl.ds(c0, ROWS_A)
        prev_row = pl.ds(jnp.maximum(c0 - 1, 0), 1)
        next_row = pl.ds(jnp.minimum(c0 + ROWS_A, n - 1), 1)

        def neighbours(load):
            cur = load(rows)
            rid = lax.broadcasted_iota(jnp.int32, cur.shape, 0)
            before = jnp.where(c == 0, 0.0, load(prev_row))
            after = jnp.where(c == n_blocks - 1, 0.0, load(next_row))
            prev = jnp.where(rid == 0, before, pltpu.roll(cur, 1, 0))
            nxt = jnp.where(rid == ROWS_A - 1, after, pltpu.roll(cur, ROWS_A - 1, 0))
            return cur, prev, nxt

        z, zp, zn = neighbours(lambda rs: cv_ref[rs, D_CONV:2 * D_CONV] * cv_ref[rs, 2 * D_CONV:3 * D_CONV])
        cw = cw_ref[...]
        o_ref[rows, 0:D_CONV] = cv_ref[rows, 0:D_CONV] * (zp * cw[0:1, :] + z * cw[1:2, :] + zn * cw[2:3, :])

        u, up, un = neighbours(lambda rs: um_ref[rs, :])
        u = u + mum_ref[...] * (0.5 * (up + un) - u)
        um_s[rows, :] = u
        t, tp, tn = neighbours(lambda rs: ut_ref[rs, :])
        ut_s[rows, :] = t + mut_ref[...] * (0.5 * (tp + tn) - t)
        kk = u[:, D_RWKV:2 * D_RWKV] * kk_ref[...]
        kk_s[rows, :] = kk * lax.rsqrt(seg_sum(kk * kk) + 1e-12)
        y_s[rows, :] = jnp.zeros((ROWS_A, D_RWKV), F32)
        return carry

    lax.fori_loop(0, n_blocks, pass1, 0)

    if has_s0:
        s_scr[...] = s0_ref[...]
    else:
        s_scr[...] = jnp.zeros(s_scr.shape, F32)

    def in_context_rate(ut, d):
        return jax.nn.sigmoid(a0_ref[d:d + 1, :] + _dot_hi(ut, aup_ref[d]))

    ri = lax.broadcasted_iota(jnp.int32, (CHUNK, CHUNK), 0)
    ci = lax.broadcasted_iota(jnp.int32, (CHUNK, CHUNK), 1)
    eye = (ri == ci).astype(F32)

    def chunk_step(c, carry):
        for d in range(2):
            cc = c if d == 0 else n_chunks - 1 - c
            rows = pl.ds(pl.multiple_of(cc * CHUNK, CHUNK), CHUNK)
            strict = (ci < ri) if d == 0 else (ci > ri)
            incl = (ci <= ri) if d == 0 else (ci >= ri)
            r = um_s[rows, 0:D_RWKV]
            k = um_s[rows, D_RWKV:2 * D_RWKV]
            vv = um_s[rows, 2 * D_RWKV:3 * D_RWKV]
            kk = kk_s[rows, :]
            ut = ut_s[rows, :]
            zz = w0_ref[d:d + 1, :] + _dot_hi(jnp.tanh(ut), wup_ref[d])
            w = -(jnp.maximum(-zz, 0.0) + jnp.log1p(jnp.exp(-jnp.abs(zz)))) - 0.5
            lw = -jnp.exp(w)
            a = in_context_rate(ut, d)
            kd = k * (1.0 + (a - 1.0) * ka_ref[...])
            cum = _dot_hi(incl.astype(F32), lw)
            e_out = jnp.exp(-cum)
            al_t = -kk * jnp.exp(cum - lw)
            be_t = kk * a * e_out
            kd_t = kd * e_out
            r_t = r * jnp.exp(cum)
            p_all = jnp.exp(jnp.sum(lw, axis=0, keepdims=True))
            for h in range(RWKV_HEADS):
                hs = slice(h * RWKV_HEAD, (h + 1) * RWKV_HEAD)
                al_h, be_h, kd_h, r_h, v_h = al_t[:, hs], be_t[:, hs], kd_t[:, hs], r_t[:, hs], vv[:, hs]
                a_ab = jnp.where(strict, _dot_hi(al_h, be_h, _NT), 0.0)
                a_ak = jnp.where(strict, _dot_hi(al_h, kd_h, _NT), 0.0)
                q_rb = jnp.where(incl, _dot_hi(r_h, be_h, _NT), 0.0)
                q_rk = jnp.where(incl, _dot_hi(r_h, kd_h, _NT), 0.0)
                x = a_ab
                t = eye + x
                for _ in range(int(math.log2(CHUNK)) - 1):
                    x = _dot_hi(x, x)
                    t = t + _dot_hi(t, x)
                w1 = _dot_hi(t, al_h)
                w2 = _dot_hi(t, _dot_hi(a_ak, v_h))
                s0 = s_scr[d, h]
                u = _dot_hi(w1, s0, _NT) + w2
                y = _dot_hi(r_h, s0, _NT) + _dot_hi(q_rb, u) + _dot_hi(q_rk, v_h)
                s_scr[d, h] = (s0 + _dot_hi(u, be_h, _TN) + _dot_hi(v_h, kd_h, _TN)) * p_all[:, hs]
                y_s[rows, hs] += y
        return carry

    lax.fori_loop(0, n_chunks, chunk_step, 0)

    if emit_state:
        so_ref[...] = s_scr[...]

    def pass3(c, carry):
        rows = pl.ds(pl.multiple_of(c * ROWS_A, ROWS_A), ROWS_A)
        r = um_s[rows, 0:D_RWKV]
        k = um_s[rows, D_RWKV:2 * D_RWKV]
        vv = um_s[rows, 2 * D_RWKV:3 * D_RWKV]
        ut = ut_s[rows, :]
        a_sum = in_context_rate(ut, 0) + in_context_rate(ut, 1)
        kd_sum = k * (2.0 + (a_sum - 2.0) * ka_ref[...])
        bonus = seg_sum(r * kd_sum * rk_ref[...]) * vv
        y = y_s[rows, :]
        yc = y - seg_sum(y) * (1.0 / RWKV_HEAD)
        var = seg_sum(yc * yc) * (1.0 / RWKV_HEAD)
        yn = yc * lax.rsqrt(var + GN_EPS) * lng_ref[...] + lnb_ref[...]
        gate = _dot_hi(jax.nn.sigmoid(ut), gup_ref[...])
        o_ref[rows, D_CONV:D_CONV + D_RWKV] = (yn + bonus) * gate
        return carry

    lax.fori_loop(0, n_blocks, pass3, 0)


def _pad_rows(w, lo):
    return jnp.pad(w, [(0, 0)] * (w.ndim - 2) + [(lo, D_TAIL - lo - w.shape[-2]), (0, 0)])


def _mix_call(proj, tail, P, l, n, s0=None, emit_state=False):
    T = proj.shape[0]
    B = T // n
    full = lambda shape: pl.BlockSpec(shape, lambda b: (0,) * len(shape))
    once = pl.Buffered(1)
    in_specs = [
        pl.BlockSpec((n, 3 * D_CONV), lambda b: (b, 1), pipeline_mode=once),
        pl.BlockSpec((n, 3 * D_RWKV), lambda b: (b, 2), pipeline_mode=once),
        pl.BlockSpec((n, D_TAIL), lambda b: (b, 0), pipeline_mode=once),
        full((1, 3 * D_RWKV)), full((1, D_TAIL)), full((3, D_CONV)),
        full((2, D_RWKV)), full((2, D_TAIL, D_RWKV)), full((2, D_RWKV)), full((2, D_TAIL, D_RWKV)),
        full((D_TAIL, D_RWKV)), full((1, D_RWKV)), full((1, D_RWKV)), full((1, D_RWKV)),
        full((1, D_RWKV)), full((1, D_RWKV)), full((D_RWKV, D_RWKV)),
    ]
    head = jnp.arange(D_RWKV) // RWKV_HEAD
    seg = (head[:, None] == head[None, :]).astype(F32)
    mu = P['rwkv_mu'][l]
    row = lambda a: a.reshape(1, -1)
    args = [proj, proj, tail, row(mu[:3 * D_RWKV]), row(mu[3 * D_RWKV:]), P['conv_w'][l],
            P['rwkv_w0'][l], _pad_rows(P['rwkv_w_up'][l], 0), P['rwkv_a0'][l],
            _pad_rows(P['rwkv_a_up'][l], DECAY_RANK), _pad_rows(P['rwkv_g_up'][l], DECAY_RANK + A_RANK),
            row(P['rwkv_k_k'][l]), row(P['rwkv_k_a'][l]), row(P['rwkv_r_k'][l]),
            row(P['rwkv_ln_g'][l]), row(P['rwkv_ln_b'][l]), seg]
    if s0 is not None:
        in_specs.append(pl.BlockSpec((None, None, 2, RWKV_HEADS, RWKV_HEAD, RWKV_HEAD),
                                     lambda b: (b, l, 0, 0, 0, 0)))
        args.append(s0)
    out_specs = [pl.BlockSpec((n, D_CONV + D_RWKV), lambda b: (b, 0))]
    out_shape = [jax.ShapeDtypeStruct((T, D_CONV + D_RWKV), F32)]
    if emit_state:
        out_specs.append(pl.BlockSpec((None, 2, RWKV_HEADS, RWKV_HEAD, RWKV_HEAD), lambda b: (b, 0, 0, 0, 0)))
        out_shape.append(jax.ShapeDtypeStruct((B, 2, RWKV_HEADS, RWKV_HEAD, RWKV_HEAD), F32))
    return pl.pallas_call(
        functools.partial(_mix_kernel, n=n, has_s0=s0 is not None, emit_state=emit_state),
        grid=(B,),
        in_specs=in_specs,
        out_specs=out_specs,
        out_shape=out_shape,
        scratch_shapes=[pltpu.VMEM((n, 3 * D_RWKV), F32), pltpu.VMEM((n, D_TAIL), F32),
                        pltpu.VMEM((n, D_RWKV), F32), pltpu.VMEM((n, D_RWKV), F32),
                        pltpu.VMEM((2, RWKV_HEADS, RWKV_HEAD, RWKV_HEAD), F32)],
        compiler_params=_params(1),
        name="conv_rwkv",
    )(*args)


def _outproj_kernel(at_ref, cr_ref, w_ref, x_ref, ga_ref, shf_ref, scf_ref, g1_ref, g2_ref,
                    x1_ref, h2_ref, acc):
    kt = pl.program_id(1)

    @pl.when(kt == 0)
    def _():
        acc[...] = jnp.zeros(acc.shape, F32)

    @pl.when(kt < 2)
    def _():
        acc[...] += _dot(at_ref[...], w_ref[...])

    @pl.when(kt >= 2)
    def _():
        acc[...] += _dot(cr_ref[...], w_ref[...])

    @pl.when(kt == pl.num_programs(1) - 1)
    def _():
        x1 = x_ref[...] + ga_ref[...] * _rms(acc[...], g1_ref[...])
        x1_ref[...] = x1
        h2_ref[...] = (_rms(x1, g2_ref[...]) * (1.0 + scf_ref[...]) + shf_ref[...]).astype(BF16)


def _outproj_call(attn, cr, x, mod3, ng3, w_out, l, row_fn, tm):
    T = x.shape[0]
    tk = 512
    return pl.pallas_call(
        _outproj_kernel,
        grid=(T // tm, D_MODEL // tk),
        in_specs=[
            pl.BlockSpec((tm, tk), lambda i, k: (i, jnp.minimum(k, 1))),
            pl.BlockSpec((tm, tk), lambda i, k: (i, jnp.maximum(k - 2, 0))),
            pl.BlockSpec((None, tk, D_MODEL), lambda i, k: (l, k, 0)),
            pl.BlockSpec((tm, D_MODEL), lambda i, k: (i, 0)),
            _mod_spec(row_fn, 2), _mod_spec(row_fn, 3), _mod_spec(row_fn, 4),
            _ng_spec(l, 1), _ng_spec(l, 2),
        ],
        out_specs=[
            pl.BlockSpec((tm, D_MODEL), lambda i, k: (i, 0)),
            pl.BlockSpec((tm, D_MODEL), lambda i, k: (i, 0)),
        ],
        out_shape=[jax.ShapeDtypeStruct((T, D_MODEL), F32), jax.ShapeDtypeStruct((T, D_MODEL), BF16)],
        scratch_shapes=[pltpu.VMEM((tm, D_MODEL), F32)],
        compiler_params=_params(2),
        name="out_proj",
    )(attn, cr, w_out, x, mod3, mod3, mod3, ng3, ng3)


def _ffn_kernel(h_ref, x1_ref, wg_ref, wu_ref, wd_ref, gf_ref, g3_ref, o_ref, acc):
    f = pl.program_id(1)

    @pl.when(f == 0)
    def _():
        acc[...] = jnp.zeros(acc.shape, F32)

    h = h_ref[...]
    gate = _dot(h, wg_ref[...])
    up = _dot(h, wu_ref[...])
    acc[...] += _dot(gate * jax.nn.sigmoid(gate) * up, wd_ref[...])

    @pl.when(f == pl.num_programs(1) - 1)
    def _():
        o_ref[...] = x1_ref[...] + gf_ref[...] * _rms(acc[...], g3_ref[...])


def _ffn_call(h2, x1, mod3, ng3, w_gate, w_up, w_down, l, row_fn, tm):
    T = x1.shape[0]
    tf = 256
    return pl.pallas_call(
        _ffn_kernel,
        grid=(T // tm, D_FF // tf),
        in_specs=[
            pl.BlockSpec((tm, D_MODEL), lambda i, f: (i, 0)),
            pl.BlockSpec((tm, D_MODEL), lambda i, f: (i, 0)),
            pl.BlockSpec((None, D_MODEL, tf), lambda i, f: (l, 0, f)),
            pl.BlockSpec((None, D_MODEL, tf), lambda i, f: (l, 0, f)),
            pl.BlockSpec((None, tf, D_MODEL), lambda i, f: (l, f, 0)),
            _mod_spec(row_fn, 5),
            _ng_spec(l, 3),
        ],
        out_specs=pl.BlockSpec((tm, D_MODEL), lambda i, f: (i, 0)),
        out_shape=jax.ShapeDtypeStruct((T, D_MODEL), F32),
        scratch_shapes=[pltpu.VMEM((tm, D_MODEL), F32)],
        compiler_params=_params(2),
        name="ffn",
    )(h2, x1, w_gate, w_up, w_down, mod3, ng3)


def _rope_tables(n):
    rows = n // GRID_W
    row = jnp.repeat(jnp.arange(rows), GRID_W).astype(F32)
    col = jnp.tile(jnp.arange(GRID_W), rows).astype(F32)
    half = HEAD_DIM // 2
    inv = 1.0 / (ROPE_THETA ** (jnp.arange(0, half, 2, dtype=F32) / half))
    ar = row[:, None] * inv
    ac = col[:, None] * inv
    ang = jnp.concatenate([ar, ar, ac, ac], axis=-1)
    return jnp.cos(ang), jnp.sin(ang)


def _layer(x, mod3, ng3, P, l, n, row_fn_of_tm, rope_tabs=None, cache=None, s0=None, emit=False):
    w_tail = P['w_in'][l][:, D_MAIN:]
    proj, tail = _inproj_call(x, mod3, ng3, P['w_in'], w_tail, l, row_fn_of_tm(1024), 1024)
    attn_out = _attn_call(proj, P['q_norm'][l].reshape(1, -1), P['k_norm'][l].reshape(1, -1), l, n,
                          rope_tabs=rope_tabs, cache=cache, emit_kv=emit)
    mix_out = _mix_call(proj, tail, P, l, n, s0=s0, emit_state=emit)
    attn = attn_out[0]
    cr = mix_out[0]
    x1, h2 = _outproj_call(attn, cr, x, mod3, ng3, P['w_out'], l, row_fn_of_tm(512), 512)
    x2 = _ffn_call(h2, x1, mod3, ng3, P['w_gate'], P['w_up'], P['w_down'], l, row_fn_of_tm(512), 512)
    if emit:
        return x2, attn_out[1], attn_out[2], mix_out[1]
    return x2


def kernel(x_prompt, x_sample, cache_k, cache_v, state_rwkv, c, c_ctx, w_ada, b_ada, norm_g, w_in, q_norm, k_norm, conv_w, rwkv_mu, rwkv_w0, rwkv_w_up, rwkv_a0, rwkv_a_up, rwkv_g_up, rwkv_k_k, rwkv_k_a, rwkv_r_k, rwkv_ln_g, rwkv_ln_b, w_out, w_gate, w_up, w_down):
    P = {'w_in': w_in, 'q_norm': q_norm, 'k_norm': k_norm, 'conv_w': conv_w, 'rwkv_mu': rwkv_mu,
         'rwkv_w0': rwkv_w0, 'rwkv_w_up': rwkv_w_up, 'rwkv_a0': rwkv_a0, 'rwkv_a_up': rwkv_a_up,
         'rwkv_g_up': rwkv_g_up, 'rwkv_k_k': rwkv_k_k, 'rwkv_k_a': rwkv_k_a, 'rwkv_r_k': rwkv_r_k,
         'rwkv_ln_g': rwkv_ln_g, 'rwkv_ln_b': rwkv_ln_b, 'w_out': w_out, 'w_gate': w_gate,
         'w_up': w_up, 'w_down': w_down}
    batch, seq, _ = x_prompt.shape
    dec_batch, dec_seq, _ = x_sample.shape

    cc = jnp.zeros((8, D_MODEL), F32).at[0].set(c_ctx).at[1:1 + dec_batch].set(c)
    mod3 = _ada_call(cc, w_ada, b_ada).reshape(DEPTH * 8, 1, N_MOD * D_MODEL)
    ng3 = norm_g.reshape(DEPTH * 4, 1, D_MODEL)

    ck = cache_k.reshape(dec_batch, DEPTH, PAST_LEN, D_KV)
    cv = cache_v.reshape(dec_batch, DEPTH, PAST_LEN, D_KV)
    rope_tabs = _rope_tables(dec_seq)

    xp = x_prompt.reshape(batch * seq, D_MODEL)
    xs = x_sample.reshape(dec_batch * dec_seq, D_MODEL)
    ks, vs, ss = [], [], []
    for l in range(DEPTH):
        ctx_row = lambda tm, l=l: (lambda i: l * 8)
        smp_row = lambda tm, l=l: (lambda i: l * 8 + 1 + (i * tm) // dec_seq)
        xp, k_l, v_l, s_l = _layer(xp, mod3, ng3, P, l, seq, ctx_row, emit=True)
        ks.append(k_l.reshape(batch, seq, N_KV_HEADS, HEAD_DIM))
        vs.append(v_l.reshape(batch, seq, N_KV_HEADS, HEAD_DIM))
        ss.append(s_l)
        xs = _layer(xs, mod3, ng3, P, l, dec_seq, smp_row, rope_tabs=rope_tabs, cache=(ck, cv),
                    s0=state_rwkv)
    return (xp.reshape(batch, seq, D_MODEL), xs.reshape(dec_batch, dec_seq, D_MODEL),
            jnp.stack(ks, axis=1), jnp.stack(vs, axis=1), jnp.stack(ss, axis=1))
```

```python
import functools
import math

import jax
import jax.numpy as jnp
from jax import lax
from jax.experimental import pallas as pl
from jax.experimental.pallas import tpu as pltpu

D_MODEL = 2048
DEPTH = 2
GRID_W = 64
HEAD_DIM = 128
N_HEADS = 8
N_KV_HEADS = 2
GROUP = N_HEADS // N_KV_HEADS
D_ATTN = N_HEADS * HEAD_DIM
D_KV = N_KV_HEADS * HEAD_DIM
ROPE_THETA = 10000.0
D_CONV = 512
D_RWKV = 512
RWKV_HEAD = 64
RWKV_HEADS = D_RWKV // RWKV_HEAD
DECAY_RANK = 32
A_RANK = 32
GATE_RANK = 96
D_TAIL = DECAY_RANK + A_RANK + GATE_RANK
D_IN = D_ATTN + 2 * D_KV + 3 * D_CONV + 3 * D_RWKV + D_TAIL
D_MAIN = D_IN - D_TAIL
D_FF = 5632
N_MOD = 6
EPS = 1e-6
GN_EPS = 64e-5
PAST_LEN = 256

VMEM_LIMIT_BYTES = 56 * 1024 * 1024
CHUNK = 64
BF16 = jnp.bfloat16
F32 = jnp.float32


def _params(n_grid, **kw):
    sem = ("parallel",) + ("arbitrary",) * (n_grid - 1)
    return pltpu.CompilerParams(dimension_semantics=sem, vmem_limit_bytes=VMEM_LIMIT_BYTES, **kw)


def _rms(x, g):
    return x * lax.rsqrt(jnp.mean(x * x, axis=-1, keepdims=True) + EPS) * g


def _dot(a, b):
    return jnp.dot(a.astype(BF16), b.astype(BF16), preferred_element_type=F32)


_NN = (((1,), (0,)), ((), ()))
_NT = (((1,), (1,)), ((), ()))
_TN = (((0,), (0,)), ((), ()))


def _mm(a, b, dims=_NN):
    return lax.dot_general(a.astype(BF16), b.astype(BF16), dims, preferred_element_type=F32)


def _ada_kernel(c_ref, w_ref, b_ref, o_ref):
    c = c_ref[...]
    s = c * jax.nn.sigmoid(c)
    o_ref[...] = _dot(s, w_ref[...]) + b_ref[...]


def _ada_call(cc, w_ada, b_ada):
    tn = 512
    n_out = N_MOD * D_MODEL
    return pl.pallas_call(
        _ada_kernel,
        grid=(DEPTH, n_out // tn),
        in_specs=[
            pl.BlockSpec((8, D_MODEL), lambda l, n: (0, 0)),
            pl.BlockSpec((None, D_MODEL, tn), lambda l, n: (l, 0, n)),
            pl.BlockSpec((None, 1, tn), lambda l, n: (l, 0, n)),
        ],
        out_specs=pl.BlockSpec((None, 8, tn), lambda l, n: (l, 0, n)),
        out_shape=jax.ShapeDtypeStruct((DEPTH, 8, n_out), F32),
        compiler_params=_params(2),
        name="adaln",
    )(cc, w_ada, b_ada.reshape(DEPTH, 1, n_out))


def _mod_spec(row_fn, chunk):
    return pl.BlockSpec((None, 1, D_MODEL), lambda i, *_: (row_fn(i), 0, chunk))


def _ng_spec(l, j):
    return pl.BlockSpec((None, 1, D_MODEL), lambda i, *_: (l * 4 + j, 0, 0))


def _inproj_kernel(x_ref, sh_ref, sc_ref, g_ref, w_ref, wt_ref, o_ref, ot_ref, h_scr):
    @pl.when(pl.program_id(1) == 0)
    def _():
        h = _rms(x_ref[...], g_ref[...]) * (1.0 + sc_ref[...]) + sh_ref[...]
        h_scr[...] = h.astype(BF16)
        ot_ref[...] = _dot(h_scr[...], wt_ref[...])

    o_ref[...] = _dot(h_scr[...], w_ref[...])


def _inproj_call(x, mod3, ng3, w_in, w_tail, l, row_fn, tm):
    T = x.shape[0]
    tn = 512
    return pl.pallas_call(
        _inproj_kernel,
        grid=(T // tm, D_MAIN // tn),
        in_specs=[
            pl.BlockSpec((tm, D_MODEL), lambda i, n: (i, 0)),
            _mod_spec(row_fn, 0),
            _mod_spec(row_fn, 1),
            _ng_spec(l, 0),
            pl.BlockSpec((None, D_MODEL, tn), lambda i, n: (l, 0, n)),
            pl.BlockSpec((D_MODEL, D_TAIL), lambda i, n: (0, 0)),
        ],
        out_specs=[
            pl.BlockSpec((tm, tn), lambda i, n: (i, n)),
            pl.BlockSpec((tm, D_TAIL), lambda i, n: (i, 0)),
        ],
        out_shape=[
            jax.ShapeDtypeStruct((T, D_MAIN), F32),
            jax.ShapeDtypeStruct((T, D_TAIL), F32),
        ],
        scratch_shapes=[pltpu.VMEM((tm, D_MODEL), BF16)],
        compiler_params=_params(2),
        name="in_proj",
    )(x, mod3, mod3, ng3, w_in, w_tail)


def _rope(x, cos, sin):
    lane = lax.broadcasted_iota(jnp.int32, x.shape, 1)
    first = (lane % (HEAD_DIM // 2)) < (HEAD_DIM // 4)
    rot = jnp.where(first, -pltpu.roll(x, HEAD_DIM - HEAD_DIM // 4, 1), pltpu.roll(x, HEAD_DIM // 4, 1))
    return x * cos + rot * sin


def _attn_kernel(*refs, n, past, rope, emit_kv, qb):
    it = iter(refs)
    q_ref, k_ref, v_ref, qn_ref, kn_ref = (next(it) for _ in range(5))
    cos_ref = sin_ref = ck_ref = cv_ref = None
    if rope:
        cos_ref, sin_ref = next(it), next(it)
    if past:
        ck_ref, cv_ref = next(it), next(it)
    o_ref = next(it)
    if emit_kv:
        ko_ref, vo_ref = next(it), next(it)
    k_scr, v_scr = next(it), next(it)

    k = _rms(k_ref[...], kn_ref[...])
    v = v_ref[...]
    if emit_kv:
        ko_ref[...] = k
        vo_ref[...] = v
    if rope:
        k = _rope(k, cos_ref[...], sin_ref[...])
    k_scr[0:n, :] = k.astype(BF16)
    v_scr[0:n, :] = v.astype(BF16)
    if past:
        k_scr[n:n + past, :] = ck_ref[...].astype(BF16)
        v_scr[n:n + past, :] = cv_ref[...].astype(BF16)
    scale = HEAD_DIM ** -0.5

    def block(b, carry):
        rows = pl.ds(pl.multiple_of(b * qb, qb), qb)
        for g in range(GROUP):
            cols = slice(g * HEAD_DIM, (g + 1) * HEAD_DIM)
            q = _rms(q_ref[rows, cols], qn_ref[...])
            if rope:
                q = _rope(q, cos_ref[rows, :], sin_ref[rows, :])
            s = lax.dot_general(q.astype(BF16), k_scr[...], _NT, preferred_element_type=F32) * scale
            p = jnp.exp(s - jnp.max(s, axis=-1, keepdims=True))
            den = jnp.sum(p, axis=-1, keepdims=True)
            o = jnp.dot(p.astype(BF16), v_scr[...], preferred_element_type=F32)
            o_ref[rows, cols] = o / den
        return carry

    lax.fori_loop(0, n // qb, block, 0)


def _attn_call(proj, qn, kn, l, n, rope_tabs=None, cache=None, emit_kv=False):
    T = proj.shape[0]
    B = T // n
    past = PAST_LEN if cache is not None else 0
    qw = GROUP * HEAD_DIM
    k_blk = D_ATTN // HEAD_DIM
    v_blk = (D_ATTN + D_KV) // HEAD_DIM
    in_specs = [
        pl.BlockSpec((n, qw), lambda b, h: (b, h)),
        pl.BlockSpec((n, HEAD_DIM), lambda b, h: (b, k_blk + h)),
        pl.BlockSpec((n, HEAD_DIM), lambda b, h: (b, v_blk + h)),
        pl.BlockSpec((1, HEAD_DIM), lambda b, h: (0, 0)),
        pl.BlockSpec((1, HEAD_DIM), lambda b, h: (0, 0)),
    ]
    args = [proj, proj, proj, qn, kn]
    if rope_tabs is not None:
        in_specs += [pl.BlockSpec((n, HEAD_DIM), lambda b, h: (0, 0))] * 2
        args += list(rope_tabs)
    if cache is not None:
        in_specs += [pl.BlockSpec((None, None, PAST_LEN, HEAD_DIM), lambda b, h: (b, l, 0, h))] * 2
        args += list(cache)
    out_specs = [pl.BlockSpec((n, qw), lambda b, h: (b, h))]
    out_shape = [jax.ShapeDtypeStruct((T, D_ATTN), F32)]
    if emit_kv:
        out_specs += [pl.BlockSpec((None, n, HEAD_DIM), lambda b, h: (b, 0, h))] * 2
        out_shape += [jax.ShapeDtypeStruct((B, n, D_KV), F32)] * 2
    return pl.pallas_call(
        functools.partial(_attn_kernel, n=n, past=past, rope=rope_tabs is not None,
                          emit_kv=emit_kv, qb=256),
        grid=(B, N_KV_HEADS),
        in_specs=in_specs,
        out_specs=out_specs,
        out_shape=out_shape,
        scratch_shapes=[pltpu.VMEM((n + past, HEAD_DIM), BF16)] * 2,
        compiler_params=_params(2),
        name="attention",
    )(*args)


ROWS_A = 256
SUB = 16
assert CHUNK // SUB == 4


def _split2(x):
    hi = x.astype(BF16)
    return hi, (x - hi.astype(F32)).astype(BF16)


def _dot_x3(a, b):
    ah, al = _split2(a)
    bh, bl = _split2(b)
    mm = lambda p, q: jnp.dot(p, q, preferred_element_type=F32)
    return mm(ah, bh) + mm(ah, bl) + mm(al, bh)


def _split_sum(x, pieces, dot_piece):
    acc = None
    rest = x
    for _ in range(pieces):
        part = rest.astype(BF16)
        rest = rest - part.astype(F32)
        term = dot_piece(part)
        acc = term if acc is None else acc + term
    return acc


def _mix_kernel(*refs, n, has_s0, emit_state):
    it = iter(refs)
    cv_ref, um_ref, ut_ref = next(it), next(it), next(it)
    (mum_ref, mut_ref, cw_ref, w0_ref, wup_ref, a0_ref, aup_ref, gup_ref, kk_ref, ka_ref,
     rk_ref, lng_ref, lnb_ref, seg_ref) = (next(it) for _ in range(14))
    s0_ref = next(it) if has_s0 else None
    o_ref = next(it)
    so_ref = next(it) if emit_state else None
    al_s, be_s, kd_s, r_s, v_s, p_s, bon_s, g_s, y_s, s_scr = (next(it) for _ in range(10))
    n_blocks = n // ROWS_A
    n_chunks = n // CHUNK
    per_block = ROWS_A // CHUNK

    def seg_sum(x):
        return _split_sum(x, 2, lambda part: jnp.dot(part, seg_ref[...], preferred_element_type=F32))

    def heads(x):
        return [x[:, h * RWKV_HEAD:(h + 1) * RWKV_HEAD] for h in range(RWKV_HEADS)]

    bi = lax.broadcasted_iota(jnp.int32, (ROWS_A, ROWS_A), 0)
    bj = lax.broadcasted_iota(jnp.int32, (ROWS_A, ROWS_A), 1)
    same_chunk = (bi // CHUNK) == (bj // CHUNK)
    cum_mat = [(same_chunk & (bj <= bi)).astype(BF16), (same_chunk & (bj >= bi)).astype(BF16)]

    def pass1(c, carry):
        c0 = pl.multiple_of(c * ROWS_A, ROWS_A)
        rows = pl.ds(c0, ROWS_A)
        prev_row = pl.ds(jnp.maximum(c0 - 1, 0), 1)
        next_row = pl.ds(jnp.minimum(c0 + ROWS_A, n - 1), 1)

        def neighbours(load):
            cur = load(rows)
            rid = lax.broadcasted_iota(jnp.int32, cur.shape, 0)
            before = jnp.where(c == 0, 0.0, load(prev_row))
            after = jnp.where(c == n_blocks - 1, 0.0, load(next_row))
            prev = jnp.where(rid == 0, before, pltpu.roll(cur, 1, 0))
            nxt = jnp.where(rid == ROWS_A - 1, after, pltpu.roll(cur, ROWS_A - 1, 0))
            return cur, prev, nxt

        z, zp, zn = neighbours(lambda rs: cv_ref[rs, D_CONV:2 * D_CONV] * cv_ref[rs, 2 * D_CONV:3 * D_CONV])
        cw = cw_ref[...]
        o_ref[rows, 0:D_CONV] = cv_ref[rows, 0:D_CONV] * (zp * cw[0:1, :] + z * cw[1:2, :] + zn * cw[2:3, :])

        u, up, un = neighbours(lambda rs: um_ref[rs, :])
        u = u + mum_ref[...] * (0.5 * (up + un) - u)
        t, tp, tn = neighbours(lambda rs: ut_ref[rs, :])
        ut = t + mut_ref[...] * (0.5 * (tp + tn) - t)
        r = u[:, 0:D_RWKV]
        k = u[:, D_RWKV:2 * D_RWKV]
        v = u[:, 2 * D_RWKV:3 * D_RWKV]
        kk = k * kk_ref[...]
        kk = kk * lax.rsqrt(seg_sum(kk * kk) + 1e-12)
        for h, v_h in enumerate(heads(v)):
            v_s[h, rows, :] = v_h.astype(BF16)
        tw = jnp.tanh(ut)
        a_sum = jnp.zeros((ROWS_A, D_RWKV), F32)
        for d in range(2):
            zz = w0_ref[d:d + 1, :] + _dot_x3(tw, wup_ref[d])
            w = -(jnp.maximum(-zz, 0.0) + jnp.log1p(jnp.exp(-jnp.abs(zz)))) - 0.5
            lw = -jnp.exp(w)
            a = jax.nn.sigmoid(a0_ref[d:d + 1, :] + _dot_x3(ut, aup_ref[d]))
            a_sum = a_sum + a
            cum = _split_sum(lw, 3, lambda part: jnp.dot(cum_mat[d], part, preferred_element_type=F32))
            e_out = jnp.exp(-cum)
            kd = k * (1.0 + (a - 1.0) * ka_ref[...])
            scaled = ((al_s, -kk * jnp.exp(cum - lw)), (be_s, kk * a * e_out), (kd_s, kd * e_out),
                      (r_s, r * jnp.exp(cum)))
            for ref, val in scaled:
                for h, val_h in enumerate(heads(val)):
                    ref[d, h, rows, :] = val_h.astype(BF16)
            for j in range(per_block):
                last = j * CHUNK + (CHUNK - 1 if d == 0 else 0)
                p_s[d, pl.ds(c * per_block + j, 1), :] = jnp.exp(cum[last:last + 1, :])
        kd_sum = k * (2.0 + (a_sum - 2.0) * ka_ref[...])
        bon_s[rows, :] = seg_sum(r * kd_sum * rk_ref[...]) * v
        g_s[rows, :] = _dot_x3(jax.nn.sigmoid(ut), gup_ref[...])
        y_s[rows, :] = jnp.zeros((ROWS_A, D_RWKV), F32)
        return carry

    lax.fori_loop(0, n_blocks, pass1, 0)

    if has_s0:
        s_scr[...] = s0_ref[...]
    else:
        s_scr[...] = jnp.zeros(s_scr.shape, F32)

    ri = lax.broadcasted_iota(jnp.int32, (CHUNK, 2 * CHUNK), 0)
    ci = lax.broadcasted_iota(jnp.int32, (CHUNK, 2 * CHUNK), 1) % CHUNK
    ei = lax.broadcasted_iota(jnp.int32, (CHUNK, CHUNK), 0)
    ej = lax.broadcasted_iota(jnp.int32, (CHUNK, CHUNK), 1)
    eye = (ei == ej).astype(F32)
    diag_blk = (ei // SUB) == (ej // SUB)
    zeros_b = jnp.zeros((CHUNK, RWKV_HEAD), BF16)

    def chunk_step(c, carry):
        chains = []
        for d in range(2):
            cc = c if d == 0 else n_chunks - 1 - c
            rows = pl.ds(pl.multiple_of(cc * CHUNK, CHUNK), CHUNK)
            p_heads = heads(p_s[d, pl.ds(cc, 1), :])
            for h in range(RWKV_HEADS):
                chains.append(dict(
                    d=d, h=h, rows=rows, p=p_heads[h],
                    strict=(ci < ri) if d == 0 else (ci > ri), incl=(ci <= ri) if d == 0 else (ci >= ri),
                    al=al_s[d, h, rows, :], be=be_s[d, h, rows, :], kd=kd_s[d, h, rows, :],
                    r=r_s[d, h, rows, :], v=v_s[h, rows, :]))

        def each(fn, *lists):
            return [fn(*args) for args in zip(chains, *lists)]

        bk = each(lambda ch: jnp.concatenate([ch['be'], ch['kd']], axis=0))
        g = each(lambda ch, m: _mm(jnp.concatenate([ch['al'], ch['r']], axis=0), m, _NT), bk)
        top = each(lambda ch, m: jnp.where(ch['strict'], m[0:CHUNK], 0.0), g)
        bot = each(lambda ch, m: jnp.where(ch['incl'], m[CHUNK:2 * CHUNK], 0.0).astype(BF16), g)
        akv = each(lambda ch, m: _mm(m, jnp.concatenate([zeros_b, ch['v']], axis=0)), top)
        a = [m[:, 0:CHUNK] for m in top]
        xs = [jnp.where(diag_blk, m, 0.0) for m in a]
        low = [jnp.where(diag_blk, 0.0, m) for m in a]
        ts = [eye + x for x in xs]
        for _ in range(int(math.log2(SUB)) - 1):
            xs = [_mm(x, x) for x in xs]
            ts = [t + _mm(t, x) for t, x in zip(ts, xs)]
        ms = [_mm(t, m) for t, m in zip(ts, low)]
        m2 = [_mm(m, m) for m in ms]
        ns = [eye + m + mm2 + _mm(m, mm2) for m, mm2 in zip(ms, m2)]
        ts = [_mm(nn, t) for nn, t in zip(ns, ts)]
        w1 = each(lambda ch, t: _mm(t, ch['al']), ts)
        w2 = [_mm(t, m) for t, m in zip(ts, akv)]
        s0 = each(lambda ch: s_scr[ch['d'], ch['h']])
        s0b = [s.astype(BF16) for s in s0]
        u = [_mm(a_, s, _NT) + b_ for a_, s, b_ in zip(w1, s0b, w2)]
        uv = each(lambda ch, uu: jnp.concatenate([uu.astype(BF16), ch['v']], axis=0), u)
        ys = each(lambda ch, s, qq, m: _mm(ch['r'], s, _NT) + _mm(qq, m), s0b, bot, uv)
        s_new = each(lambda ch, s, m, kb: (s + _mm(m, kb, _TN)) * ch['p'], s0, uv, bk)
        for ch, s in zip(chains, s_new):
            s_scr[ch['d'], ch['h']] = s
        for ch, y in zip(chains, ys):
            y_s[ch['rows'], ch['h'] * RWKV_HEAD:(ch['h'] + 1) * RWKV_HEAD] += y
        return carry

    lax.fori_loop(0, n_chunks, chunk_step, 0)

    if emit_state:
        so_ref[...] = s_scr[...]

    def pass3(c, carry):
        rows = pl.ds(pl.multiple_of(c * ROWS_A, ROWS_A), ROWS_A)
        y = y_s[rows, :]
        yc = y - seg_sum(y) * (1.0 / RWKV_HEAD)
        var = seg_sum(yc * yc) * (1.0 / RWKV_HEAD)
        yn = yc * lax.rsqrt(var + GN_EPS) * lng_ref[...] + lnb_ref[...]
        o_ref[rows, D_CONV:D_CONV + D_RWKV] = (yn + bon_s[rows, :]) * g_s[rows, :]
        return carry

    lax.fori_loop(0, n_blocks, pass3, 0)


def _pad_rows(w, lo):
    return jnp.pad(w, [(0, 0)] * (w.ndim - 2) + [(lo, D_TAIL - lo - w.shape[-2]), (0, 0)])


def _mix_call(proj, tail, P, l, n, s0=None, emit_state=False):
    T = proj.shape[0]
    B = T // n
    full = lambda shape: pl.BlockSpec(shape, lambda b: (0,) * len(shape))
    once = pl.Buffered(1)
    in_specs = [
        pl.BlockSpec((n, 3 * D_CONV), lambda b: (b, 1), pipeline_mode=once),
        pl.BlockSpec((n, 3 * D_RWKV), lambda b: (b, 2), pipeline_mode=once),
        pl.BlockSpec((n, D_TAIL), lambda b: (b, 0), pipeline_mode=once),
        full((1, 3 * D_RWKV)), full((1, D_TAIL)), full((3, D_CONV)),
        full((2, D_RWKV)), full((2, D_TAIL, D_RWKV)), full((2, D_RWKV)), full((2, D_TAIL, D_RWKV)),
        full((D_TAIL, D_RWKV)), full((1, D_RWKV)), full((1, D_RWKV)), full((1, D_RWKV)),
        full((1, D_RWKV)), full((1, D_RWKV)), full((D_RWKV, D_RWKV)),
    ]
    head = jnp.arange(D_RWKV) // RWKV_HEAD
    seg = (head[:, None] == head[None, :]).astype(BF16)
    mu = P['rwkv_mu'][l]
    row = lambda a: a.reshape(1, -1)
    args = [proj, proj, tail, row(mu[:3 * D_RWKV]), row(mu[3 * D_RWKV:]), P['conv_w'][l],
            P['rwkv_w0'][l], _pad_rows(P['rwkv_w_up'][l], 0), P['rwkv_a0'][l],
            _pad_rows(P['rwkv_a_up'][l], DECAY_RANK), _pad_rows(P['rwkv_g_up'][l], DECAY_RANK + A_RANK),
            row(P['rwkv_k_k'][l]), row(P['rwkv_k_a'][l]), row(P['rwkv_r_k'][l]),
            row(P['rwkv_ln_g'][l]), row(P['rwkv_ln_b'][l]), seg]
    if s0 is not None:
        in_specs.append(pl.BlockSpec((None, None, 2, RWKV_HEADS, RWKV_HEAD, RWKV_HEAD),
                                     lambda b: (b, l, 0, 0, 0, 0)))
        args.append(s0)
    out_specs = [pl.BlockSpec((n, D_CONV + D_RWKV), lambda b: (b, 0))]
    out_shape = [jax.ShapeDtypeStruct((T, D_CONV + D_RWKV), F32)]
    if emit_state:
        out_specs.append(pl.BlockSpec((None, 2, RWKV_HEADS, RWKV_HEAD, RWKV_HEAD), lambda b: (b, 0, 0, 0, 0)))
        out_shape.append(jax.ShapeDtypeStruct((B, 2, RWKV_HEADS, RWKV_HEAD, RWKV_HEAD), F32))
    per_head = lambda: pltpu.VMEM((2, RWKV_HEADS, n, RWKV_HEAD), BF16)
    return pl.pallas_call(
        functools.partial(_mix_kernel, n=n, has_s0=s0 is not None, emit_state=emit_state),
        grid=(B,),
        in_specs=in_specs,
        out_specs=out_specs,
        out_shape=out_shape,
        scratch_shapes=[per_head(), per_head(), per_head(), per_head(),
                        pltpu.VMEM((RWKV_HEADS, n, RWKV_HEAD), BF16),
                        pltpu.VMEM((2, n // CHUNK, D_RWKV), F32),
                        pltpu.VMEM((n, D_RWKV), F32), pltpu.VMEM((n, D_RWKV), F32),
                        pltpu.VMEM((n, D_RWKV), F32),
                        pltpu.VMEM((2, RWKV_HEADS, RWKV_HEAD, RWKV_HEAD), F32)],
        compiler_params=_params(1),
        name="conv_rwkv",
    )(*args)


def _outproj_kernel(at_ref, cr_ref, w_ref, x_ref, ga_ref, shf_ref, scf_ref, g1_ref, g2_ref,
                    x1_ref, h2_ref):
    kt = pl.program_id(1)

    @pl.when(kt == 0)
    def _():
        x1_ref[...] = _dot(at_ref[...], w_ref[...])

    @pl.when((kt > 0) & (kt < 2))
    def _():
        x1_ref[...] += _dot(at_ref[...], w_ref[...])

    @pl.when(kt >= 2)
    def _():
        x1_ref[...] += _dot(cr_ref[...], w_ref[...])

    @pl.when(kt == pl.num_programs(1) - 1)
    def _():
        x1 = x_ref[...] + ga_ref[...] * _rms(x1_ref[...], g1_ref[...])
        x1_ref[...] = x1
        h2_ref[...] = (_rms(x1, g2_ref[...]) * (1.0 + scf_ref[...]) + shf_ref[...]).astype(BF16)


def _outproj_call(attn, cr, x, mod3, ng3, w_out, l, row_fn, tm):
    T = x.shape[0]
    tk = 512
    return pl.pallas_call(
        _outproj_kernel,
        grid=(T // tm, D_MODEL // tk),
        in_specs=[
            pl.BlockSpec((tm, tk), lambda i, k: (i, jnp.minimum(k, 1))),
            pl.BlockSpec((tm, tk), lambda i, k: (i, jnp.maximum(k - 2, 0))),
            pl.BlockSpec((None, tk, D_MODEL), lambda i, k: (l, k, 0)),
            pl.BlockSpec((tm, D_MODEL), lambda i, k: (i, 0), pipeline_mode=pl.Buffered(1)),
            _mod_spec(row_fn, 2), _mod_spec(row_fn, 3), _mod_spec(row_fn, 4),
            _ng_spec(l, 1), _ng_spec(l, 2),
        ],
        out_specs=[
            pl.BlockSpec((tm, D_MODEL), lambda i, k: (i, 0)),
            pl.BlockSpec((tm, D_MODEL), lambda i, k: (i, 0)),
        ],
        out_shape=[jax.ShapeDtypeStruct((T, D_MODEL), F32), jax.ShapeDtypeStruct((T, D_MODEL), BF16)],
        compiler_params=_params(2),
        name="out_proj",
    )(attn, cr, w_out, x, mod3, mod3, mod3, ng3, ng3)


def _ffn_kernel(h_ref, x1_ref, wg_ref, wu_ref, wd_ref, gf_ref, g3_ref, o_ref):
    f = pl.program_id(1)
    h = h_ref[...]
    gate = _dot(h, wg_ref[...])
    up = _dot(h, wu_ref[...])
    part = _dot(gate * jax.nn.sigmoid(gate) * up, wd_ref[...])

    @pl.when(f == 0)
    def _():
        o_ref[...] = part

    @pl.when(f > 0)
    def _():
        o_ref[...] += part

    @pl.when(f == pl.num_programs(1) - 1)
    def _():
        o_ref[...] = x1_ref[...] + gf_ref[...] * _rms(o_ref[...], g3_ref[...])


def _ffn_call(h2, x1, mod3, ng3, w_gate, w_up, w_down, l, row_fn, tm):
    T = x1.shape[0]
    tf = 256
    return pl.pallas_call(
        _ffn_kernel,
        grid=(T // tm, D_FF // tf),
        in_specs=[
            pl.BlockSpec((tm, D_MODEL), lambda i, f: (i, 0)),
            pl.BlockSpec((tm, D_MODEL), lambda i, f: (i, 0), pipeline_mode=pl.Buffered(1)),
            pl.BlockSpec((None, D_MODEL, tf), lambda i, f: (l, 0, f)),
            pl.BlockSpec((None, D_MODEL, tf), lambda i, f: (l, 0, f)),
            pl.BlockSpec((None, tf, D_MODEL), lambda i, f: (l, f, 0)),
            _mod_spec(row_fn, 5),
            _ng_spec(l, 3),
        ],
        out_specs=pl.BlockSpec((tm, D_MODEL), lambda i, f: (i, 0)),
        out_shape=jax.ShapeDtypeStruct((T, D_MODEL), F32),
        compiler_params=_params(2),
        name="ffn",
    )(h2, x1, w_gate, w_up, w_down, mod3, ng3)


def _cast_kernel(w_ref, o_ref):
    o_ref[...] = w_ref[...].astype(BF16)


def _cast_call(w, tk):
    L, K, N = w.shape
    spec = pl.BlockSpec((None, tk, N), lambda l, k: (l, k, 0))
    return pl.pallas_call(
        _cast_kernel,
        grid=(L, K // tk),
        in_specs=[spec],
        out_specs=spec,
        out_shape=jax.ShapeDtypeStruct(w.shape, BF16),
        compiler_params=_params(2),
        name="cast_bf16",
    )(w)


def _rope_tables(n):
    rows = n // GRID_W
    row = jnp.repeat(jnp.arange(rows), GRID_W).astype(F32)
    col = jnp.tile(jnp.arange(GRID_W), rows).astype(F32)
    half = HEAD_DIM // 2
    inv = 1.0 / (ROPE_THETA ** (jnp.arange(0, half, 2, dtype=F32) / half))
    ar = row[:, None] * inv
    ac = col[:, None] * inv
    ang = jnp.concatenate([ar, ar, ac, ac], axis=-1)
    return jnp.cos(ang), jnp.sin(ang)


def _layer(x, mod3, ng3, P, l, n, row_fn_of_tm, rope_tabs=None, cache=None, s0=None, emit=False):
    w_tail = P['w_in'][l][:, D_MAIN:]
    proj, tail = _inproj_call(x, mod3, ng3, P['w_in'], w_tail, l, row_fn_of_tm(1024), 1024)
    attn_out = _attn_call(proj, P['q_norm'][l].reshape(1, -1), P['k_norm'][l].reshape(1, -1), l, n,
                          rope_tabs=rope_tabs, cache=cache, emit_kv=emit)
    mix_out = _mix_call(proj, tail, P, l, n, s0=s0, emit_state=emit)
    attn = attn_out[0]
    cr = mix_out[0]
    x1, h2 = _outproj_call(attn, cr, x, mod3, ng3, P['w_out'], l, row_fn_of_tm(1024), 1024)
    x2 = _ffn_call(h2, x1, mod3, ng3, P['w_gate'], P['w_up'], P['w_down'], l, row_fn_of_tm(1024), 1024)
    if emit:
        return x2, attn_out[1], attn_out[2], mix_out[1]
    return x2


def kernel(x_prompt, x_sample, cache_k, cache_v, state_rwkv, c, c_ctx, w_ada, b_ada, norm_g, w_in, q_norm, k_norm, conv_w, rwkv_mu, rwkv_w0, rwkv_w_up, rwkv_a0, rwkv_a_up, rwkv_g_up, rwkv_k_k, rwkv_k_a, rwkv_r_k, rwkv_ln_g, rwkv_ln_b, w_out, w_gate, w_up, w_down):
    P = {'w_in': w_in, 'q_norm': q_norm, 'k_norm': k_norm, 'conv_w': conv_w, 'rwkv_mu': rwkv_mu,
         'rwkv_w0': rwkv_w0, 'rwkv_w_up': rwkv_w_up, 'rwkv_a0': rwkv_a0, 'rwkv_a_up': rwkv_a_up,
         'rwkv_g_up': rwkv_g_up, 'rwkv_k_k': rwkv_k_k, 'rwkv_k_a': rwkv_k_a, 'rwkv_r_k': rwkv_r_k,
         'rwkv_ln_g': rwkv_ln_g, 'rwkv_ln_b': rwkv_ln_b, 'w_out': _cast_call(w_out, 512),
         'w_gate': _cast_call(w_gate, 256), 'w_up': _cast_call(w_up, 256),
         'w_down': _cast_call(w_down, 512)}
    P['w_in'] = _cast_call(w_in, 256)
    batch, seq, _ = x_prompt.shape
    dec_batch, dec_seq, _ = x_sample.shape

    cc = jnp.zeros((8, D_MODEL), F32).at[0].set(c_ctx).at[1:1 + dec_batch].set(c)
    mod3 = _ada_call(cc, w_ada, b_ada).reshape(DEPTH * 8, 1, N_MOD * D_MODEL)
    ng3 = norm_g.reshape(DEPTH * 4, 1, D_MODEL)

    ck = cache_k.reshape(dec_batch, DEPTH, PAST_LEN, D_KV)
    cv = cache_v.reshape(dec_batch, DEPTH, PAST_LEN, D_KV)
    rope_tabs = _rope_tables(dec_seq)

    xp = x_prompt.reshape(batch * seq, D_MODEL)
    xs = x_sample.reshape(dec_batch * dec_seq, D_MODEL)
    ks, vs, ss = [], [], []
    for l in range(DEPTH):
        ctx_row = lambda tm, l=l: (lambda i: l * 8)
        smp_row = lambda tm, l=l: (lambda i: l * 8 + 1 + (i * tm) // dec_seq)
        xp, k_l, v_l, s_l = _layer(xp, mod3, ng3, P, l, seq, ctx_row, emit=True)
        ks.append(k_l.reshape(batch, seq, N_KV_HEADS, HEAD_DIM))
        vs.append(v_l.reshape(batch, seq, N_KV_HEADS, HEAD_DIM))
        ss.append(s_l)
        xs = _layer(xs, mod3, ng3, P, l, dec_seq, smp_row, rope_tabs=rope_tabs, cache=(ck, cv),
                    s0=state_rwkv)
    return (xp.reshape(batch, seq, D_MODEL), xs.reshape(dec_batch, dec_seq, D_MODEL),
            jnp.stack(ks, axis=1), jnp.stack(vs, axis=1), jnp.stack(ss, axis=1))
```

```python
import functools
import math

import jax
import jax.numpy as jnp
from jax import lax
from jax.experimental import pallas as pl
from jax.experimental.pallas import tpu as pltpu

D_MODEL = 2048
DEPTH = 2
GRID_W = 64
HEAD_DIM = 128
N_HEADS = 8
N_KV_HEADS = 2
GROUP = N_HEADS // N_KV_HEADS
D_ATTN = N_HEADS * HEAD_DIM
D_KV = N_KV_HEADS * HEAD_DIM
ROPE_THETA = 10000.0
D_CONV = 512
D_RWKV = 512
RWKV_HEAD = 64
RWKV_HEADS = D_RWKV // RWKV_HEAD
DECAY_RANK = 32
A_RANK = 32
GATE_RANK = 96
D_TAIL = DECAY_RANK + A_RANK + GATE_RANK
D_IN = D_ATTN + 2 * D_KV + 3 * D_CONV + 3 * D_RWKV + D_TAIL
D_MAIN = D_IN - D_TAIL
D_FF = 5632
N_MOD = 6
EPS = 1e-6
GN_EPS = 64e-5
PAST_LEN = 256

VMEM_LIMIT_BYTES = 56 * 1024 * 1024
CHUNK = 64
BF16 = jnp.bfloat16
F32 = jnp.float32


def _params(n_grid, **kw):
    sem = ("parallel",) + ("arbitrary",) * (n_grid - 1)
    return pltpu.CompilerParams(dimension_semantics=sem, vmem_limit_bytes=VMEM_LIMIT_BYTES, **kw)


def _rms(x, g):
    return x * lax.rsqrt(jnp.mean(x * x, axis=-1, keepdims=True) + EPS) * g


def _dot(a, b):
    return jnp.dot(a.astype(BF16), b.astype(BF16), preferred_element_type=F32)


_NN = (((1,), (0,)), ((), ()))
_NT = (((1,), (1,)), ((), ()))
_TN = (((0,), (0,)), ((), ()))


def _mm(a, b, dims=_NN):
    return lax.dot_general(a.astype(BF16), b.astype(BF16), dims, preferred_element_type=F32)


def _ada_kernel(c_ref, w_ref, b_ref, o_ref):
    c = c_ref[...]
    s = c * jax.nn.sigmoid(c)
    o_ref[...] = _dot(s, w_ref[...]) + b_ref[...]


def _ada_call(cc, w_ada, b_ada):
    tn = 512
    n_out = N_MOD * D_MODEL
    return pl.pallas_call(
        _ada_kernel,
        grid=(DEPTH, n_out // tn),
        in_specs=[
            pl.BlockSpec((8, D_MODEL), lambda l, n: (0, 0)),
            pl.BlockSpec((None, D_MODEL, tn), lambda l, n: (l, 0, n)),
            pl.BlockSpec((None, 1, tn), lambda l, n: (l, 0, n)),
        ],
        out_specs=pl.BlockSpec((None, 8, tn), lambda l, n: (l, 0, n)),
        out_shape=jax.ShapeDtypeStruct((DEPTH, 8, n_out), F32),
        compiler_params=_params(2),
        name="adaln",
    )(cc, w_ada, b_ada.reshape(DEPTH, 1, n_out))


def _mod_spec(row_fn, chunk):
    return pl.BlockSpec((None, 1, D_MODEL), lambda i, *_: (row_fn(i), 0, chunk))


def _ng_spec(l, j):
    return pl.BlockSpec((None, 1, D_MODEL), lambda i, *_: (l * 4 + j, 0, 0))


def _inproj_kernel(x_ref, sh_ref, sc_ref, g_ref, w_ref, wt_ref, o_ref, ot_ref, h_scr):
    @pl.when(pl.program_id(1) == 0)
    def _():
        h = _rms(x_ref[...], g_ref[...]) * (1.0 + sc_ref[...]) + sh_ref[...]
        h_scr[...] = h.astype(BF16)
        ot_ref[...] = _mm(h_scr[...], wt_ref[...], _NT)

    o_ref[...] = _mm(h_scr[...], w_ref[...], _NT)


def _inproj_call(x, mod3, ng3, w_in_t, w_tail_t, l, row_fn, tm):
    T = x.shape[0]
    tn = 512
    return pl.pallas_call(
        _inproj_kernel,
        grid=(T // tm, D_MAIN // tn),
        in_specs=[
            pl.BlockSpec((tm, D_MODEL), lambda i, n: (i, 0)),
            _mod_spec(row_fn, 0),
            _mod_spec(row_fn, 1),
            _ng_spec(l, 0),
            pl.BlockSpec((None, tn, D_MODEL), lambda i, n: (l, n, 0)),
            pl.BlockSpec((None, D_TAIL, D_MODEL), lambda i, n: (l, 0, 0)),
        ],
        out_specs=[
            pl.BlockSpec((tm, tn), lambda i, n: (i, n)),
            pl.BlockSpec((tm, D_TAIL), lambda i, n: (i, 0)),
        ],
        out_shape=[
            jax.ShapeDtypeStruct((T, D_MAIN), F32),
            jax.ShapeDtypeStruct((T, D_TAIL), F32),
        ],
        scratch_shapes=[pltpu.VMEM((tm, D_MODEL), BF16)],
        compiler_params=_params(2),
        name="in_proj",
    )(x, mod3, mod3, ng3, w_in_t, w_tail_t)


def _rope(x, cos, sin):
    lane = lax.broadcasted_iota(jnp.int32, x.shape, 1)
    first = (lane % (HEAD_DIM // 2)) < (HEAD_DIM // 4)
    rot = jnp.where(first, -pltpu.roll(x, HEAD_DIM - HEAD_DIM // 4, 1), pltpu.roll(x, HEAD_DIM // 4, 1))
    return x * cos + rot * sin


def _attn_kernel(*refs, n, past, rope, emit_kv, qb):
    it = iter(refs)
    q_ref, k_ref, v_ref, qn_ref, kn_ref = (next(it) for _ in range(5))
    cos_ref = sin_ref = ck_ref = cv_ref = None
    if rope:
        cos_ref, sin_ref = next(it), next(it)
    if past:
        ck_ref, cv_ref = next(it), next(it)
    o_ref = next(it)
    if emit_kv:
        ko_ref, vo_ref = next(it), next(it)
    k_scr, v_scr = next(it), next(it)

    k = _rms(k_ref[...], kn_ref[...])
    v = v_ref[...]
    if emit_kv:
        ko_ref[...] = k
        vo_ref[...] = v
    if rope:
        k = _rope(k, cos_ref[...], sin_ref[...])
    k_scr[0:n, :] = k.astype(BF16)
    v_scr[0:n, :] = v.astype(BF16)
    if past:
        k_scr[n:n + past, :] = ck_ref[...].astype(BF16)
        v_scr[n:n + past, :] = cv_ref[...].astype(BF16)
    scale = HEAD_DIM ** -0.5

    def block(b, carry):
        rows = pl.ds(pl.multiple_of(b * qb, qb), qb)
        for g in range(GROUP):
            cols = slice(g * HEAD_DIM, (g + 1) * HEAD_DIM)
            q = _rms(q_ref[rows, cols], qn_ref[...])
            if rope:
                q = _rope(q, cos_ref[rows, :], sin_ref[rows, :])
            s = lax.dot_general(q.astype(BF16), k_scr[...], _NT, preferred_element_type=F32) * scale
            p = jnp.exp(s - jnp.max(s, axis=-1, keepdims=True))
            den = jnp.sum(p, axis=-1, keepdims=True)
            o = jnp.dot(p.astype(BF16), v_scr[...], preferred_element_type=F32)
            o_ref[rows, cols] = (o / den).astype(BF16)
        return carry

    lax.fori_loop(0, n // qb, block, 0)


def _attn_call(proj, qn, kn, l, n, rope_tabs=None, cache=None, emit_kv=False):
    T = proj.shape[0]
    B = T // n
    past = PAST_LEN if cache is not None else 0
    qw = GROUP * HEAD_DIM
    k_blk = D_ATTN // HEAD_DIM
    v_blk = (D_ATTN + D_KV) // HEAD_DIM
    in_specs = [
        pl.BlockSpec((n, qw), lambda b, h: (b, h)),
        pl.BlockSpec((n, HEAD_DIM), lambda b, h: (b, k_blk + h)),
        pl.BlockSpec((n, HEAD_DIM), lambda b, h: (b, v_blk + h)),
        pl.BlockSpec((1, HEAD_DIM), lambda b, h: (0, 0)),
        pl.BlockSpec((1, HEAD_DIM), lambda b, h: (0, 0)),
    ]
    args = [proj, proj, proj, qn, kn]
    if rope_tabs is not None:
        in_specs += [pl.BlockSpec((n, HEAD_DIM), lambda b, h: (0, 0))] * 2
        args += list(rope_tabs)
    if cache is not None:
        in_specs += [pl.BlockSpec((None, None, PAST_LEN, HEAD_DIM), lambda b, h: (b, l, 0, h))] * 2
        args += list(cache)
    out_specs = [pl.BlockSpec((n, qw), lambda b, h: (b, h))]
    out_shape = [jax.ShapeDtypeStruct((T, D_ATTN), BF16)]
    if emit_kv:
        out_specs += [pl.BlockSpec((None, n, HEAD_DIM), lambda b, h: (b, 0, h))] * 2
        out_shape += [jax.ShapeDtypeStruct((B, n, D_KV), F32)] * 2
    return pl.pallas_call(
        functools.partial(_attn_kernel, n=n, past=past, rope=rope_tabs is not None,
                          emit_kv=emit_kv, qb=256),
        grid=(B, N_KV_HEADS),
        in_specs=in_specs,
        out_specs=out_specs,
        out_shape=out_shape,
        scratch_shapes=[pltpu.VMEM((n + past, HEAD_DIM), BF16)] * 2,
        compiler_params=_params(2),
        name="attention",
    )(*args)


ROWS_A = 256
SUB = 16
assert CHUNK // SUB == 4


def _split2(x):
    hi = x.astype(BF16)
    return hi, (x - hi.astype(F32)).astype(BF16)


def _dot_x3(a, b):
    ah, al = _split2(a)
    bh, bl = _split2(b)
    mm = lambda p, q: jnp.dot(p, q, preferred_element_type=F32)
    return mm(ah, bh) + mm(ah, bl) + mm(al, bh)


def _split_sum(x, pieces, dot_piece):
    acc = None
    rest = x
    for _ in range(pieces):
        part = rest.astype(BF16)
        rest = rest - part.astype(F32)
        term = dot_piece(part)
        acc = term if acc is None else acc + term
    return acc


def _mix_kernel(*refs, n, nseq, has_s0, emit_state):
    it = iter(refs)
    cv_ref, um_ref, ut_ref = next(it), next(it), next(it)
    (mum_ref, mut_ref, cw_ref, w0_ref, wup_ref, a0_ref, aup_ref, gup_ref, kk_ref, ka_ref,
     rk_ref, lng_ref, lnb_ref, seg_ref) = (next(it) for _ in range(14))
    s0_ref = next(it) if has_s0 else None
    o_ref = next(it)
    so_ref = next(it) if emit_state else None
    al_s, be_s, kd_s, r_s, v_s, p_s, bon_s, g_s, y_s, s_scr = (next(it) for _ in range(10))
    blocks_per_seq = n // ROWS_A
    n_blocks = nseq * blocks_per_seq
    n_chunks = n // CHUNK
    per_block = ROWS_A // CHUNK

    def seg_sum(x):
        return _split_sum(x, 2, lambda part: jnp.dot(part, seg_ref[...], preferred_element_type=F32))

    def heads(x):
        return [x[:, h * RWKV_HEAD:(h + 1) * RWKV_HEAD] for h in range(RWKV_HEADS)]

    bi = lax.broadcasted_iota(jnp.int32, (ROWS_A, ROWS_A), 0)
    bj = lax.broadcasted_iota(jnp.int32, (ROWS_A, ROWS_A), 1)
    same_chunk = (bi // CHUNK) == (bj // CHUNK)
    cum_mat = [(same_chunk & (bj <= bi)).astype(BF16), (same_chunk & (bj >= bi)).astype(BF16)]

    def pass1(c, carry):
        c0 = pl.multiple_of(c * ROWS_A, ROWS_A)
        rows = pl.ds(c0, ROWS_A)
        prev_row = pl.ds(jnp.maximum(c0 - 1, 0), 1)
        next_row = pl.ds(jnp.minimum(c0 + ROWS_A, nseq * n - 1), 1)
        seq_first = (c % blocks_per_seq) == 0
        seq_last = (c % blocks_per_seq) == blocks_per_seq - 1

        def neighbours(load):
            cur = load(rows)
            rid = lax.broadcasted_iota(jnp.int32, cur.shape, 0)
            before = jnp.where(seq_first, 0.0, load(prev_row))
            after = jnp.where(seq_last, 0.0, load(next_row))
            prev = jnp.where(rid == 0, before, pltpu.roll(cur, 1, 0))
            nxt = jnp.where(rid == ROWS_A - 1, after, pltpu.roll(cur, ROWS_A - 1, 0))
            return cur, prev, nxt

        z, zp, zn = neighbours(lambda rs: cv_ref[rs, D_CONV:2 * D_CONV] * cv_ref[rs, 2 * D_CONV:3 * D_CONV])
        cw = cw_ref[...]
        conv = cv_ref[rows, 0:D_CONV] * (zp * cw[0:1, :] + z * cw[1:2, :] + zn * cw[2:3, :])
        o_ref[rows, 0:D_CONV] = conv.astype(BF16)

        u, up, un = neighbours(lambda rs: um_ref[rs, :])
        u = u + mum_ref[...] * (0.5 * (up + un) - u)
        t, tp, tn = neighbours(lambda rs: ut_ref[rs, :])
        ut = t + mut_ref[...] * (0.5 * (tp + tn) - t)
        r = u[:, 0:D_RWKV]
        k = u[:, D_RWKV:2 * D_RWKV]
        v = u[:, 2 * D_RWKV:3 * D_RWKV]
        kk = k * kk_ref[...]
        kk = kk * lax.rsqrt(seg_sum(kk * kk) + 1e-12)
        for h, v_h in enumerate(heads(v)):
            v_s[h, rows, :] = v_h.astype(BF16)
        tw = jnp.tanh(ut)
        a_sum = jnp.zeros((ROWS_A, D_RWKV), F32)
        for d in range(2):
            zz = w0_ref[d:d + 1, :] + _dot_x3(tw, wup_ref[d])
            lw = -math.exp(-0.5) * jax.nn.sigmoid(zz)
            a = jax.nn.sigmoid(a0_ref[d:d + 1, :] + _dot_x3(ut, aup_ref[d]))
            a_sum = a_sum + a
            cum = _split_sum(lw, 3, lambda part: jnp.dot(cum_mat[d], part, preferred_element_type=F32))
            e_out = jnp.exp(-cum)
            kd = k * (1.0 + (a - 1.0) * ka_ref[...])
            scaled = ((al_s, -kk * jnp.exp(cum - lw)), (be_s, kk * a * e_out), (kd_s, kd * e_out),
                      (r_s, r * jnp.exp(cum)))
            for ref, val in scaled:
                for h, val_h in enumerate(heads(val)):
                    ref[d, h, rows, :] = val_h.astype(BF16)
            for j in range(per_block):
                last = j * CHUNK + (CHUNK - 1 if d == 0 else 0)
                p_s[d, pl.ds(c * per_block + j, 1), :] = jnp.exp(cum[last:last + 1, :])
        kd_sum = k * (2.0 + (a_sum - 2.0) * ka_ref[...])
        bon_s[rows, :] = seg_sum(r * kd_sum * rk_ref[...]) * v
        g_s[rows, :] = _dot_x3(jax.nn.sigmoid(ut), gup_ref[...])
        y_s[rows, :] = jnp.zeros((ROWS_A, D_RWKV), F32)
        return carry

    lax.fori_loop(0, n_blocks, pass1, 0)

    if has_s0:
        s_scr[0] = s0_ref[...]
    else:
        s_scr[...] = jnp.zeros(s_scr.shape, F32)

    ri = lax.broadcasted_iota(jnp.int32, (CHUNK, 2 * CHUNK), 0)
    ci = lax.broadcasted_iota(jnp.int32, (CHUNK, 2 * CHUNK), 1) % CHUNK
    ei = lax.broadcasted_iota(jnp.int32, (CHUNK, CHUNK), 0)
    ej = lax.broadcasted_iota(jnp.int32, (CHUNK, CHUNK), 1)
    eye = (ei == ej).astype(F32)
    diag_blk = (ei // SUB) == (ej // SUB)
    zeros_b = jnp.zeros((CHUNK, RWKV_HEAD), BF16)

    def chunk_step(c, carry):
        chains = []
        for s, d in ((s, d) for s in range(nseq) for d in range(2)):
            cc = s * n_chunks + (c if d == 0 else n_chunks - 1 - c)
            rows = pl.ds(pl.multiple_of(cc * CHUNK, CHUNK), CHUNK)
            p_heads = heads(p_s[d, pl.ds(cc, 1), :])
            for h in range(RWKV_HEADS):
                chains.append(dict(
                    s=s, d=d, h=h, rows=rows, p=p_heads[h],
                    strict=(ci < ri) if d == 0 else (ci > ri), incl=(ci <= ri) if d == 0 else (ci >= ri),
                    al=al_s[d, h, rows, :], be=be_s[d, h, rows, :], kd=kd_s[d, h, rows, :],
                    r=r_s[d, h, rows, :], v=v_s[h, rows, :]))

        def each(fn, *lists):
            return [fn(*args) for args in zip(chains, *lists)]

        bk = each(lambda ch: jnp.concatenate([ch['be'], ch['kd']], axis=0))
        g = each(lambda ch, m: _mm(jnp.concatenate([ch['al'], ch['r']], axis=0), m, _NT), bk)
        top = each(lambda ch, m: jnp.where(ch['strict'], m[0:CHUNK], 0.0), g)
        bot = each(lambda ch, m: jnp.where(ch['incl'], m[CHUNK:2 * CHUNK], 0.0).astype(BF16), g)
        akv = each(lambda ch, m: _mm(m, jnp.concatenate([zeros_b, ch['v']], axis=0)), top)
        a = [m[:, 0:CHUNK] for m in top]
        xs = [jnp.where(diag_blk, m, 0.0) for m in a]
        low = [jnp.where(diag_blk, 0.0, m) for m in a]
        ts = [eye + x for x in xs]
        for _ in range(int(math.log2(SUB)) - 1):
            xs = [_mm(x, x) for x in xs]
            ts = [t + _mm(t, x) for t, x in zip(ts, xs)]
        ms = [_mm(t, m) for t, m in zip(ts, low)]
        m2 = [_mm(m, m) for m in ms]
        ns = [eye + m + mm2 + _mm(m, mm2) for m, mm2 in zip(ms, m2)]
        ts = [_mm(nn, t) for nn, t in zip(ns, ts)]
        w1 = each(lambda ch, t: _mm(t, ch['al']), ts)
        w2 = [_mm(t, m) for t, m in zip(ts, akv)]
        s0 = each(lambda ch: s_scr[ch['s'], ch['d'], ch['h']])
        s0b = [s.astype(BF16) for s in s0]
        u = [_mm(a_, s, _NT) + b_ for a_, s, b_ in zip(w1, s0b, w2)]
        uv = each(lambda ch, uu: jnp.concatenate([uu.astype(BF16), ch['v']], axis=0), u)
        ys = each(lambda ch, s, qq, m: _mm(ch['r'], s, _NT) + _mm(qq, m), s0b, bot, uv)
        s_new = each(lambda ch, s, m, kb: (s + _mm(m, kb, _TN)) * ch['p'], s0, uv, bk)
        for ch, s in zip(chains, s_new):
            s_scr[ch['s'], ch['d'], ch['h']] = s
        for ch, y in zip(chains, ys):
            y_s[ch['rows'], ch['h'] * RWKV_HEAD:(ch['h'] + 1) * RWKV_HEAD] += y
        return carry

    lax.fori_loop(0, n_chunks, chunk_step, 0)

    if emit_state:
        so_ref[...] = s_scr[...]

    def pass3(c, carry):
        rows = pl.ds(pl.multiple_of(c * ROWS_A, ROWS_A), ROWS_A)
        y = y_s[rows, :]
        yc = y - seg_sum(y) * (1.0 / RWKV_HEAD)
        var = seg_sum(yc * yc) * (1.0 / RWKV_HEAD)
        yn = yc * lax.rsqrt(var + GN_EPS) * lng_ref[...] + lnb_ref[...]
        o_ref[rows, D_CONV:D_CONV + D_RWKV] = ((yn + bon_s[rows, :]) * g_s[rows, :]).astype(BF16)
        return carry

    lax.fori_loop(0, n_blocks, pass3, 0)


def _pad_rows(w, lo):
    return jnp.pad(w, [(0, 0)] * (w.ndim - 2) + [(lo, D_TAIL - lo - w.shape[-2]), (0, 0)])


def _mix_call(proj, tail, P, l, n, nseq, s0=None, emit_state=False):
    T = proj.shape[0]
    B = T // n
    rows = nseq * n
    assert s0 is None or nseq == 1
    full = lambda shape: pl.BlockSpec(shape, lambda b: (0,) * len(shape))
    once = pl.Buffered(1)
    in_specs = [
        pl.BlockSpec((rows, 3 * D_CONV), lambda b: (b, 1), pipeline_mode=once),
        pl.BlockSpec((rows, 3 * D_RWKV), lambda b: (b, 2), pipeline_mode=once),
        pl.BlockSpec((rows, D_TAIL), lambda b: (b, 0), pipeline_mode=once),
        full((1, 3 * D_RWKV)), full((1, D_TAIL)), full((3, D_CONV)),
        full((2, D_RWKV)), full((2, D_TAIL, D_RWKV)), full((2, D_RWKV)), full((2, D_TAIL, D_RWKV)),
        full((D_TAIL, D_RWKV)), full((1, D_RWKV)), full((1, D_RWKV)), full((1, D_RWKV)),
        full((1, D_RWKV)), full((1, D_RWKV)), full((D_RWKV, D_RWKV)),
    ]
    head = jnp.arange(D_RWKV) // RWKV_HEAD
    seg = (head[:, None] == head[None, :]).astype(BF16)
    mu = P['rwkv_mu'][l]
    row = lambda a: a.reshape(1, -1)
    args = [proj, proj, tail, row(mu[:3 * D_RWKV]), row(mu[3 * D_RWKV:]), P['conv_w'][l],
            P['rwkv_w0'][l], _pad_rows(P['rwkv_w_up'][l], 0), P['rwkv_a0'][l],
            _pad_rows(P['rwkv_a_up'][l], DECAY_RANK), _pad_rows(P['rwkv_g_up'][l], DECAY_RANK + A_RANK),
            row(P['rwkv_k_k'][l]), row(P['rwkv_k_a'][l]), row(P['rwkv_r_k'][l]),
            row(P['rwkv_ln_g'][l]), row(P['rwkv_ln_b'][l]), seg]
    if s0 is not None:
        in_specs.append(pl.BlockSpec((None, None, 2, RWKV_HEADS, RWKV_HEAD, RWKV_HEAD),
                                     lambda b: (b, l, 0, 0, 0, 0)))
        args.append(s0)
    out_specs = [pl.BlockSpec((rows, D_CONV + D_RWKV), lambda b: (b, 0))]
    out_shape = [jax.ShapeDtypeStruct((T, D_CONV + D_RWKV), BF16)]
    state_shape = (2, RWKV_HEADS, RWKV_HEAD, RWKV_HEAD)
    if emit_state:
        out_specs.append(pl.BlockSpec((nseq,) + state_shape, lambda b: (b, 0, 0, 0, 0)))
        out_shape.append(jax.ShapeDtypeStruct((B,) + state_shape, F32))
    per_head = lambda: pltpu.VMEM((2, RWKV_HEADS, rows, RWKV_HEAD), BF16)
    return pl.pallas_call(
        functools.partial(_mix_kernel, n=n, nseq=nseq, has_s0=s0 is not None, emit_state=emit_state),
        grid=(B // nseq,),
        in_specs=in_specs,
        out_specs=out_specs,
        out_shape=out_shape,
        scratch_shapes=[per_head(), per_head(), per_head(), per_head(),
                        pltpu.VMEM((RWKV_HEADS, rows, RWKV_HEAD), BF16),
                        pltpu.VMEM((2, rows // CHUNK, D_RWKV), F32),
                        pltpu.VMEM((rows, D_RWKV), F32), pltpu.VMEM((rows, D_RWKV), F32),
                        pltpu.VMEM((rows, D_RWKV), F32),
                        pltpu.VMEM((nseq,) + state_shape, F32)],
        compiler_params=_params(1),
        name="conv_rwkv",
    )(*args)


def _outproj_kernel(at_ref, cr_ref, w_ref, x_ref, ga_ref, shf_ref, scf_ref, g1_ref, g2_ref,
                    x1_ref, h2_ref):
    kt = pl.program_id(1)

    @pl.when(kt == 0)
    def _():
        x1_ref[...] = jnp.zeros(x1_ref.shape, F32)

    lhs = jnp.where(kt < 2, at_ref[...], cr_ref[...])
    x1_ref[...] += jnp.dot(lhs, w_ref[...], preferred_element_type=F32)

    @pl.when(kt == pl.num_programs(1) - 1)
    def _():
        x1 = x_ref[...] + ga_ref[...] * _rms(x1_ref[...], g1_ref[...])
        x1_ref[...] = x1
        h2_ref[...] = (_rms(x1, g2_ref[...]) * (1.0 + scf_ref[...]) + shf_ref[...]).astype(BF16)


def _outproj_call(attn, cr, x, mod3, ng3, w_out, l, row_fn, tm):
    T = x.shape[0]
    tk = 512
    return pl.pallas_call(
        _outproj_kernel,
        grid=(T // tm, D_MODEL // tk),
        in_specs=[
            pl.BlockSpec((tm, tk), lambda i, k: (i, jnp.minimum(k, 1))),
            pl.BlockSpec((tm, tk), lambda i, k: (i, jnp.maximum(k - 2, 0))),
            pl.BlockSpec((None, tk, D_MODEL), lambda i, k: (l, k, 0)),
            pl.BlockSpec((tm, D_MODEL), lambda i, k: (i, 0), pipeline_mode=pl.Buffered(1)),
            _mod_spec(row_fn, 2), _mod_spec(row_fn, 3), _mod_spec(row_fn, 4),
            _ng_spec(l, 1), _ng_spec(l, 2),
        ],
        out_specs=[
            pl.BlockSpec((tm, D_MODEL), lambda i, k: (i, 0)),
            pl.BlockSpec((tm, D_MODEL), lambda i, k: (i, 0)),
        ],
        out_shape=[jax.ShapeDtypeStruct((T, D_MODEL), F32), jax.ShapeDtypeStruct((T, D_MODEL), BF16)],
        compiler_params=_params(2),
        name="out_proj",
    )(attn, cr, w_out, x, mod3, mod3, mod3, ng3, ng3)


def _ffn_kernel(h_ref, x1_ref, wg_ref, wu_ref, wd_ref, gf_ref, g3_ref, o_ref):
    f = pl.program_id(1)

    @pl.when(f == 0)
    def _():
        o_ref[...] = jnp.zeros(o_ref.shape, F32)

    h = h_ref[...]
    gate = _dot(h, wg_ref[...])
    up = _dot(h, wu_ref[...])
    o_ref[...] += _dot(gate * jax.nn.sigmoid(gate) * up, wd_ref[...])

    @pl.when(f == pl.num_programs(1) - 1)
    def _():
        o_ref[...] = x1_ref[...] + gf_ref[...] * _rms(o_ref[...], g3_ref[...])


def _ffn_call(h2, x1, mod3, ng3, w_gate, w_up, w_down, l, row_fn, tm):
    T = x1.shape[0]
    tf = 256
    return pl.pallas_call(
        _ffn_kernel,
        grid=(T // tm, D_FF // tf),
        in_specs=[
            pl.BlockSpec((tm, D_MODEL), lambda i, f: (i, 0)),
            pl.BlockSpec((tm, D_MODEL), lambda i, f: (i, 0), pipeline_mode=pl.Buffered(1)),
            pl.BlockSpec((None, D_MODEL, tf), lambda i, f: (l, 0, f)),
            pl.BlockSpec((None, D_MODEL, tf), lambda i, f: (l, 0, f)),
            pl.BlockSpec((None, tf, D_MODEL), lambda i, f: (l, f, 0)),
            _mod_spec(row_fn, 5),
            _ng_spec(l, 3),
        ],
        out_specs=pl.BlockSpec((tm, D_MODEL), lambda i, f: (i, 0)),
        out_shape=jax.ShapeDtypeStruct((T, D_MODEL), F32),
        compiler_params=_params(2),
        name="ffn",
    )(h2, x1, w_gate, w_up, w_down, mod3, ng3)


def _cast_kernel(w_ref, o_ref):
    o_ref[...] = w_ref[...].astype(BF16)


def _cast_call(w, tk, rows=None):
    L, K, N = w.shape
    rows = K if rows is None else rows
    spec = pl.BlockSpec((None, tk, N), lambda l, k: (l, k, 0))
    return pl.pallas_call(
        _cast_kernel,
        grid=(L, rows // tk),
        in_specs=[spec],
        out_specs=spec,
        out_shape=jax.ShapeDtypeStruct((L, rows, N), BF16),
        compiler_params=_params(2),
        name="cast_bf16",
    )(w)


def _rope_tables(n):
    rows = n // GRID_W
    row = jnp.repeat(jnp.arange(rows), GRID_W).astype(F32)
    col = jnp.tile(jnp.arange(GRID_W), rows).astype(F32)
    half = HEAD_DIM // 2
    inv = 1.0 / (ROPE_THETA ** (jnp.arange(0, half, 2, dtype=F32) / half))
    ar = row[:, None] * inv
    ac = col[:, None] * inv
    ang = jnp.concatenate([ar, ar, ac, ac], axis=-1)
    return jnp.cos(ang), jnp.sin(ang)


def _layer(x, mod3, ng3, P, l, n, row_fn_of_tm, rope_tabs=None, cache=None, s0=None, emit=False):
    proj, tail = _inproj_call(x, mod3, ng3, P['w_in_t'], P['w_tail_t'], l, row_fn_of_tm(1024), 1024)
    attn_out = _attn_call(proj, P['q_norm'][l].reshape(1, -1), P['k_norm'][l].reshape(1, -1), l, n,
                          rope_tabs=rope_tabs, cache=cache, emit_kv=emit)
    mix_out = _mix_call(proj, tail, P, l, n, 1 if s0 is not None else 2, s0=s0, emit_state=emit)
    attn = attn_out[0]
    cr = mix_out[0]
    x1, h2 = _outproj_call(attn, cr, x, mod3, ng3, P['w_out'], l, row_fn_of_tm(1024), 1024)
    x2 = _ffn_call(h2, x1, mod3, ng3, P['w_gate'], P['w_up'], P['w_down'], l, row_fn_of_tm(1024), 1024)
    if emit:
        return x2, attn_out[1], attn_out[2], mix_out[1]
    return x2


def kernel(x_prompt, x_sample, cache_k, cache_v, state_rwkv, c, c_ctx, w_ada, b_ada, norm_g, w_in, q_norm, k_norm, conv_w, rwkv_mu, rwkv_w0, rwkv_w_up, rwkv_a0, rwkv_a_up, rwkv_g_up, rwkv_k_k, rwkv_k_a, rwkv_r_k, rwkv_ln_g, rwkv_ln_b, w_out, w_gate, w_up, w_down):
    P = {'w_in': w_in, 'q_norm': q_norm, 'k_norm': k_norm, 'conv_w': conv_w, 'rwkv_mu': rwkv_mu,
         'rwkv_w0': rwkv_w0, 'rwkv_w_up': rwkv_w_up, 'rwkv_a0': rwkv_a0, 'rwkv_a_up': rwkv_a_up,
         'rwkv_g_up': rwkv_g_up, 'rwkv_k_k': rwkv_k_k, 'rwkv_k_a': rwkv_k_a, 'rwkv_r_k': rwkv_r_k,
         'rwkv_ln_g': rwkv_ln_g, 'rwkv_ln_b': rwkv_ln_b, 'w_out': _cast_call(w_out, 512),
         'w_gate': _cast_call(w_gate, 256), 'w_up': _cast_call(w_up, 256),
         'w_down': _cast_call(w_down, 512)}
    w_in_t = jnp.swapaxes(w_in, 1, 2)
    P['w_in_t'] = _cast_call(w_in_t, 512, rows=D_MAIN)
    P['w_tail_t'] = _cast_call(w_in_t[:, D_MAIN:, :], D_TAIL)
    batch, seq, _ = x_prompt.shape
    dec_batch, dec_seq, _ = x_sample.shape

    cc = jnp.zeros((8, D_MODEL), F32).at[0].set(c_ctx).at[1:1 + dec_batch].set(c)
    mod3 = _ada_call(cc, w_ada, b_ada).reshape(DEPTH * 8, 1, N_MOD * D_MODEL)
    ng3 = norm_g.reshape(DEPTH * 4, 1, D_MODEL)

    ck = cache_k.reshape(dec_batch, DEPTH, PAST_LEN, D_KV)
    cv = cache_v.reshape(dec_batch, DEPTH, PAST_LEN, D_KV)
    rope_tabs = _rope_tables(dec_seq)

    xp = x_prompt.reshape(batch * seq, D_MODEL)
    xs = x_sample.reshape(dec_batch * dec_seq, D_MODEL)
    ks, vs, ss = [], [], []
    for l in range(DEPTH):
        ctx_row = lambda tm, l=l: (lambda i: l * 8)
        smp_row = lambda tm, l=l: (lambda i: l * 8 + 1 + (i * tm) // dec_seq)
        xp, k_l, v_l, s_l = _layer(xp, mod3, ng3, P, l, seq, ctx_row, emit=True)
        ks.append(k_l.reshape(batch, seq, N_KV_HEADS, HEAD_DIM))
        vs.append(v_l.reshape(batch, seq, N_KV_HEADS, HEAD_DIM))
        ss.append(s_l)
        xs = _layer(xs, mod3, ng3, P, l, dec_seq, smp_row, rope_tabs=rope_tabs, cache=(ck, cv),
                    s0=state_rwkv)
    return (xp.reshape(batch, seq, D_MODEL), xs.reshape(dec_batch, dec_seq, D_MODEL),
            jnp.stack(ks, axis=1), jnp.stack(vs, axis=1), jnp.stack(ss, axis=1))
```

```python
import functools
import math

import jax
import jax.numpy as jnp
from jax import lax
from jax.experimental import pallas as pl
from jax.experimental.pallas import tpu as pltpu

D_MODEL = 2048
DEPTH = 2
GRID_W = 64
HEAD_DIM = 128
N_HEADS = 8
N_KV_HEADS = 2
GROUP = N_HEADS // N_KV_HEADS
D_ATTN = N_HEADS * HEAD_DIM
D_KV = N_KV_HEADS * HEAD_DIM
ROPE_THETA = 10000.0
D_CONV = 512
D_RWKV = 512
RWKV_HEAD = 64
RWKV_HEADS = D_RWKV // RWKV_HEAD
DECAY_RANK = 32
A_RANK = 32
GATE_RANK = 96
D_TAIL = DECAY_RANK + A_RANK + GATE_RANK
D_IN = D_ATTN + 2 * D_KV + 3 * D_CONV + 3 * D_RWKV + D_TAIL
D_MAIN = D_IN - D_TAIL
D_FF = 5632
N_MOD = 6
EPS = 1e-6
GN_EPS = 64e-5
PAST_LEN = 256

VMEM_LIMIT_BYTES = 56 * 1024 * 1024
CHUNK = 64
BF16 = jnp.bfloat16
F32 = jnp.float32


def _params(n_grid, **kw):
    sem = ("parallel",) + ("arbitrary",) * (n_grid - 1)
    return pltpu.CompilerParams(dimension_semantics=sem, vmem_limit_bytes=VMEM_LIMIT_BYTES, **kw)


def _rms(x, g):
    return x * lax.rsqrt(jnp.mean(x * x, axis=-1, keepdims=True) + EPS) * g


def _dot(a, b):
    return jnp.dot(a.astype(BF16), b.astype(BF16), preferred_element_type=F32)


_NN = (((1,), (0,)), ((), ()))
_NT = (((1,), (1,)), ((), ()))
_TN = (((0,), (0,)), ((), ()))


def _mm(a, b, dims=_NN):
    return lax.dot_general(a.astype(BF16), b.astype(BF16), dims, preferred_element_type=F32)


def _ada_kernel(c_ref, w_ref, b_ref, o_ref):
    c = c_ref[...]
    s = c * jax.nn.sigmoid(c)
    o_ref[...] = _dot(s, w_ref[...]) + b_ref[...]


def _ada_call(cc, w_ada, b_ada):
    tn = 512
    n_out = N_MOD * D_MODEL
    return pl.pallas_call(
        _ada_kernel,
        grid=(DEPTH, n_out // tn),
        in_specs=[
            pl.BlockSpec((8, D_MODEL), lambda l, n: (0, 0)),
            pl.BlockSpec((None, D_MODEL, tn), lambda l, n: (l, 0, n)),
            pl.BlockSpec((None, 1, tn), lambda l, n: (l, 0, n)),
        ],
        out_specs=pl.BlockSpec((None, 8, tn), lambda l, n: (l, 0, n)),
        out_shape=jax.ShapeDtypeStruct((DEPTH, 8, n_out), F32),
        compiler_params=_params(2),
        name="adaln",
    )(cc, w_ada, b_ada.reshape(DEPTH, 1, n_out))


def _mod_spec(row_fn, chunk):
    return pl.BlockSpec((None, 1, D_MODEL), lambda i, *_: (row_fn(i), 0, chunk))


def _ng_spec(l, j):
    return pl.BlockSpec((None, 1, D_MODEL), lambda i, *_: (l * 4 + j, 0, 0))


def _inproj_kernel(x_ref, sh_ref, sc_ref, g_ref, w_ref, wt_ref, o_ref, ot_ref, h_scr):
    @pl.when(pl.program_id(1) == 0)
    def _():
        h = _rms(x_ref[...], g_ref[...]) * (1.0 + sc_ref[...]) + sh_ref[...]
        h_scr[...] = h.astype(BF16)
        ot_ref[...] = _mm(h_scr[...], wt_ref[...], _NT)

    o_ref[...] = _mm(h_scr[...], w_ref[...], _NT)


def _inproj_call(x, mod3, ng3, w_in_t, w_tail_t, l, row_fn, tm):
    T = x.shape[0]
    tn = 512
    return pl.pallas_call(
        _inproj_kernel,
        grid=(T // tm, D_MAIN // tn),
        in_specs=[
            pl.BlockSpec((tm, D_MODEL), lambda i, n: (i, 0)),
            _mod_spec(row_fn, 0),
            _mod_spec(row_fn, 1),
            _ng_spec(l, 0),
            pl.BlockSpec((None, tn, D_MODEL), lambda i, n: (l, n, 0)),
            pl.BlockSpec((None, D_TAIL, D_MODEL), lambda i, n: (l, 0, 0)),
        ],
        out_specs=[
            pl.BlockSpec((tm, tn), lambda i, n: (i, n)),
            pl.BlockSpec((tm, D_TAIL), lambda i, n: (i, 0)),
        ],
        out_shape=[
            jax.ShapeDtypeStruct((T, D_MAIN), F32),
            jax.ShapeDtypeStruct((T, D_TAIL), F32),
        ],
        scratch_shapes=[pltpu.VMEM((tm, D_MODEL), BF16)],
        compiler_params=_params(2),
        name="in_proj",
    )(x, mod3, mod3, ng3, w_in_t, w_tail_t)


def _rope(x, cos, sin):
    lane = lax.broadcasted_iota(jnp.int32, x.shape, 1)
    first = (lane % (HEAD_DIM // 2)) < (HEAD_DIM // 4)
    rot = jnp.where(first, -pltpu.roll(x, HEAD_DIM - HEAD_DIM // 4, 1), pltpu.roll(x, HEAD_DIM // 4, 1))
    return x * cos + rot * sin


def _attn_kernel(*refs, n, past, rope, emit_kv, qb):
    it = iter(refs)
    q_ref, k_ref, v_ref, qn_ref, kn_ref = (next(it) for _ in range(5))
    cos_ref = sin_ref = ck_ref = cv_ref = None
    if rope:
        cos_ref, sin_ref = next(it), next(it)
    if past:
        ck_ref, cv_ref = next(it), next(it)
    o_ref = next(it)
    if emit_kv:
        ko_ref, vo_ref = next(it), next(it)
    k_scr, v_scr = next(it), next(it)

    k = _rms(k_ref[...], kn_ref[...])
    v = v_ref[...]
    if emit_kv:
        ko_ref[...] = k
        vo_ref[...] = v
    if rope:
        k = _rope(k, cos_ref[...], sin_ref[...])
    k_scr[0:n, :] = k.astype(BF16)
    v_scr[0:n, :] = v.astype(BF16)
    if past:
        k_scr[n:n + past, :] = ck_ref[...].astype(BF16)
        v_scr[n:n + past, :] = cv_ref[...].astype(BF16)
    exp2_scale = HEAD_DIM ** -0.5 * math.log2(math.e)

    blocks_per_step = min(n // qb, 4)

    def block(b, carry):
        items = [(pl.ds(pl.multiple_of((b * blocks_per_step + j) * qb, qb), qb),
                  slice(g * HEAD_DIM, (g + 1) * HEAD_DIM))
                 for j in range(blocks_per_step) for g in range(GROUP)]

        def scores(item):
            rows, cols = item
            q = _rms(q_ref[rows, cols], qn_ref[...])
            if rope:
                q = _rope(q, cos_ref[rows, :], sin_ref[rows, :])
            return lax.dot_general(q.astype(BF16), k_scr[...], _NT, preferred_element_type=F32)

        def softmax(s):
            p = jnp.exp2((s - jnp.max(s, axis=-1, keepdims=True)) * exp2_scale)
            return p.astype(BF16), jnp.sum(p, axis=-1, keepdims=True)

        def values(item, pd):
            rows, cols = item
            o = jnp.dot(pd[0], v_scr[...], preferred_element_type=F32)
            o_ref[rows, cols] = (o / pd[1]).astype(BF16)

        if blocks_per_step == 1:
            ps = [softmax(s) for s in [scores(item) for item in items]]
            for item, pd in zip(items, ps):
                values(item, pd)
            return carry
        depth = 2
        ss, ps = {}, {}
        for i in range(len(items) + depth):
            if i < len(items):
                ss[i] = scores(items[i])
            if 1 <= i <= len(items):
                ps[i - 1] = softmax(ss.pop(i - 1))
            if i >= depth:
                values(items[i - depth], ps.pop(i - depth))
        return carry

    lax.fori_loop(0, n // (qb * blocks_per_step), block, 0)


def _attn_call(proj, qn, kn, l, n, rope_tabs=None, cache=None, emit_kv=False):
    T = proj.shape[0]
    B = T // n
    past = PAST_LEN if cache is not None else 0
    qw = GROUP * HEAD_DIM
    k_blk = D_ATTN // HEAD_DIM
    v_blk = (D_ATTN + D_KV) // HEAD_DIM
    in_specs = [
        pl.BlockSpec((n, qw), lambda b, h: (b, h)),
        pl.BlockSpec((n, HEAD_DIM), lambda b, h: (b, k_blk + h)),
        pl.BlockSpec((n, HEAD_DIM), lambda b, h: (b, v_blk + h)),
        pl.BlockSpec((1, HEAD_DIM), lambda b, h: (0, 0)),
        pl.BlockSpec((1, HEAD_DIM), lambda b, h: (0, 0)),
    ]
    args = [proj, proj, proj, qn, kn]
    if rope_tabs is not None:
        in_specs += [pl.BlockSpec((n, HEAD_DIM), lambda b, h: (0, 0))] * 2
        args += list(rope_tabs)
    if cache is not None:
        in_specs += [pl.BlockSpec((None, None, PAST_LEN, HEAD_DIM), lambda b, h: (b, l, 0, h))] * 2
        args += list(cache)
    out_specs = [pl.BlockSpec((n, qw), lambda b, h: (b, h))]
    out_shape = [jax.ShapeDtypeStruct((T, D_ATTN), BF16)]
    if emit_kv:
        out_specs += [pl.BlockSpec((None, n, HEAD_DIM), lambda b, h: (b, 0, h))] * 2
        out_shape += [jax.ShapeDtypeStruct((B, n, D_KV), F32)] * 2
    return pl.pallas_call(
        functools.partial(_attn_kernel, n=n, past=past, rope=rope_tabs is not None,
                          emit_kv=emit_kv, qb=256),
        grid=(B, N_KV_HEADS),
        in_specs=in_specs,
        out_specs=out_specs,
        out_shape=out_shape,
        scratch_shapes=[pltpu.VMEM((n + past, HEAD_DIM), BF16)] * 2,
        compiler_params=_params(2),
        name="attention",
    )(*args)


ROWS_A = 256
SUB = 16
assert CHUNK // SUB == 4


def _split2(x):
    hi = x.astype(BF16)
    return hi, (x - hi.astype(F32)).astype(BF16)


def _dot_x3(a, b):
    ah, al = _split2(a)
    bh, bl = _split2(b)
    mm = lambda p, q: jnp.dot(p, q, preferred_element_type=F32)
    return mm(ah, bh) + mm(ah, bl) + mm(al, bh)


def _split_sum(x, pieces, dot_piece):
    acc = None
    rest = x
    for _ in range(pieces):
        part = rest.astype(BF16)
        rest = rest - part.astype(F32)
        term = dot_piece(part)
        acc = term if acc is None else acc + term
    return acc


def _mix_kernel(*refs, n, nseq, has_s0, emit_state):
    it = iter(refs)
    cv_ref, um_ref, ut_ref = next(it), next(it), next(it)
    (mum_ref, mut_ref, cw_ref, w0_ref, wup_ref, a0_ref, aup_ref, gup_ref, kk_ref, ka_ref,
     rk_ref, lng_ref, lnb_ref, seg_ref) = (next(it) for _ in range(14))
    s0_ref = next(it) if has_s0 else None
    o_ref = next(it)
    so_ref = next(it) if emit_state else None
    al_s, be_s, kd_s, r_s, v_s, p_s, bon_s, g_s, y_s, s_scr = (next(it) for _ in range(10))
    blocks_per_seq = n // ROWS_A
    n_blocks = nseq * blocks_per_seq
    n_chunks = n // CHUNK
    per_block = ROWS_A // CHUNK

    def seg_sum(x):
        return _split_sum(x, 2, lambda part: jnp.dot(part, seg_ref[...], preferred_element_type=F32))

    def heads(x):
        return [x[:, h * RWKV_HEAD:(h + 1) * RWKV_HEAD] for h in range(RWKV_HEADS)]

    bi = lax.broadcasted_iota(jnp.int32, (ROWS_A, ROWS_A), 0)
    bj = lax.broadcasted_iota(jnp.int32, (ROWS_A, ROWS_A), 1)
    same_chunk = (bi // CHUNK) == (bj // CHUNK)
    cum_mat = [(same_chunk & (bj <= bi)).astype(BF16), (same_chunk & (bj >= bi)).astype(BF16)]

    def pass1(c, carry):
        c0 = pl.multiple_of(c * ROWS_A, ROWS_A)
        rows = pl.ds(c0, ROWS_A)
        prev_row = pl.ds(jnp.maximum(c0 - 1, 0), 1)
        next_row = pl.ds(jnp.minimum(c0 + ROWS_A, nseq * n - 1), 1)
        seq_first = (c % blocks_per_seq) == 0
        seq_last = (c % blocks_per_seq) == blocks_per_seq - 1

        def neighbours(load):
            cur = load(rows)
            rid = lax.broadcasted_iota(jnp.int32, cur.shape, 0)
            before = jnp.where(seq_first, 0.0, load(prev_row))
            after = jnp.where(seq_last, 0.0, load(next_row))
            prev = jnp.where(rid == 0, before, pltpu.roll(cur, 1, 0))
            nxt = jnp.where(rid == ROWS_A - 1, after, pltpu.roll(cur, ROWS_A - 1, 0))
            return cur, prev, nxt

        z, zp, zn = neighbours(lambda rs: cv_ref[rs, D_CONV:2 * D_CONV] * cv_ref[rs, 2 * D_CONV:3 * D_CONV])
        cw = cw_ref[...]
        conv = cv_ref[rows, 0:D_CONV] * (zp * cw[0:1, :] + z * cw[1:2, :] + zn * cw[2:3, :])
        o_ref[rows, 0:D_CONV] = conv.astype(BF16)

        u, up, un = neighbours(lambda rs: um_ref[rs, :])
        u = u + mum_ref[...] * (0.5 * (up + un) - u)
        t, tp, tn = neighbours(lambda rs: ut_ref[rs, :])
        ut = t + mut_ref[...] * (0.5 * (tp + tn) - t)
        r = u[:, 0:D_RWKV]
        k = u[:, D_RWKV:2 * D_RWKV]
        v = u[:, 2 * D_RWKV:3 * D_RWKV]
        kk = k * kk_ref[...]
        kk = kk * lax.rsqrt(seg_sum(kk * kk) + 1e-12)
        for h, v_h in enumerate(heads(v)):
            v_s[h, rows, :] = v_h.astype(BF16)
        tw = jnp.tanh(ut)
        a_sum = jnp.zeros((ROWS_A, D_RWKV), F32)
        for d in range(2):
            zz = w0_ref[d:d + 1, :] + _dot_x3(tw, wup_ref[d])
            lw = -math.exp(-0.5) * jax.nn.sigmoid(zz)
            a = jax.nn.sigmoid(a0_ref[d:d + 1, :] + _dot_x3(ut, aup_ref[d]))
            a_sum = a_sum + a
            cum = _split_sum(lw, 3, lambda part: jnp.dot(cum_mat[d], part, preferred_element_type=F32))
            e_out = jnp.exp(-cum)
            kd = k * (1.0 + (a - 1.0) * ka_ref[...])
            scaled = ((al_s, -kk * jnp.exp(cum - lw)), (be_s, kk * a * e_out), (kd_s, kd * e_out),
                      (r_s, r * jnp.exp(cum)))
            for ref, val in scaled:
                for h, val_h in enumerate(heads(val)):
                    ref[d, h, rows, :] = val_h.astype(BF16)
            for j in range(per_block):
                last = j * CHUNK + (CHUNK - 1 if d == 0 else 0)
                p_s[d, pl.ds(c * per_block + j, 1), :] = jnp.exp(cum[last:last + 1, :])
        kd_sum = k * (2.0 + (a_sum - 2.0) * ka_ref[...])
        bon_s[rows, :] = seg_sum(r * kd_sum * rk_ref[...]) * v
        g_s[rows, :] = _dot_x3(jax.nn.sigmoid(ut), gup_ref[...])
        y_s[rows, :] = jnp.zeros((ROWS_A, D_RWKV), F32)
        return carry

    lax.fori_loop(0, n_blocks, pass1, 0)

    if has_s0:
        s_scr[0] = s0_ref[...]
    else:
        s_scr[...] = jnp.zeros(s_scr.shape, F32)

    ri = lax.broadcasted_iota(jnp.int32, (CHUNK, 2 * CHUNK), 0)
    ci = lax.broadcasted_iota(jnp.int32, (CHUNK, 2 * CHUNK), 1) % CHUNK
    ei = lax.broadcasted_iota(jnp.int32, (CHUNK, CHUNK), 0)
    ej = lax.broadcasted_iota(jnp.int32, (CHUNK, CHUNK), 1)
    eye = (ei == ej).astype(F32)
    diag_blk = (ei // SUB) == (ej // SUB)
    zeros_b = jnp.zeros((CHUNK, RWKV_HEAD), BF16)

    def chunk_step(c, carry):
        chains = []
        for s, d in ((s, d) for s in range(nseq) for d in range(2)):
            cc = s * n_chunks + (c if d == 0 else n_chunks - 1 - c)
            rows = pl.ds(pl.multiple_of(cc * CHUNK, CHUNK), CHUNK)
            p_heads = heads(p_s[d, pl.ds(cc, 1), :])
            for h in range(RWKV_HEADS):
                chains.append(dict(
                    s=s, d=d, h=h, rows=rows, p=p_heads[h],
                    strict=(ci < ri) if d == 0 else (ci > ri), incl=(ci <= ri) if d == 0 else (ci >= ri),
                    al=al_s[d, h, rows, :], be=be_s[d, h, rows, :], kd=kd_s[d, h, rows, :],
                    r=r_s[d, h, rows, :], v=v_s[h, rows, :]))

        def each(fn, *lists):
            return [fn(*args) for args in zip(chains, *lists)]

        bk = each(lambda ch: jnp.concatenate([ch['be'], ch['kd']], axis=0))
        g = each(lambda ch, m: _mm(jnp.concatenate([ch['al'], ch['r']], axis=0), m, _NT), bk)
        top = each(lambda ch, m: jnp.where(ch['strict'], m[0:CHUNK], 0.0), g)
        bot = each(lambda ch, m: jnp.where(ch['incl'], m[CHUNK:2 * CHUNK], 0.0).astype(BF16), g)
        akv = each(lambda ch, m: _mm(m, jnp.concatenate([zeros_b, ch['v']], axis=0)), top)
        a = [m[:, 0:CHUNK] for m in top]
        xs = [jnp.where(diag_blk, m, 0.0) for m in a]
        low = [jnp.where(diag_blk, 0.0, m) for m in a]
        ts = [eye + x for x in xs]
        for _ in range(int(math.log2(SUB)) - 1):
            xs = [_mm(x, x) for x in xs]
            ts = [t + _mm(t, x) for t, x in zip(ts, xs)]
        ms = [_mm(t, m) for t, m in zip(ts, low)]
        m2 = [_mm(m, m) for m in ms]
        ns = [eye + m + mm2 + _mm(m, mm2) for m, mm2 in zip(ms, m2)]
        ts = [_mm(nn, t) for nn, t in zip(ns, ts)]
        w1 = each(lambda ch, t: _mm(t, ch['al']), ts)
        w2 = [_mm(t, m) for t, m in zip(ts, akv)]
        s0 = each(lambda ch: s_scr[ch['s'], ch['d'], ch['h']])
        s0b = [s.astype(BF16) for s in s0]
        u = [_mm(a_, s, _NT) + b_ for a_, s, b_ in zip(w1, s0b, w2)]
        uv = each(lambda ch, uu: jnp.concatenate([uu.astype(BF16), ch['v']], axis=0), u)
        ys = each(lambda ch, s, qq, m: _mm(ch['r'], s, _NT) + _mm(qq, m), s0b, bot, uv)
        s_new = each(lambda ch, s, m, kb: (s + _mm(m, kb, _TN)) * ch['p'], s0, uv, bk)
        for ch, s in zip(chains, s_new):
            s_scr[ch['s'], ch['d'], ch['h']] = s
        for ch, y in zip(chains, ys):
            y_s[ch['rows'], ch['h'] * RWKV_HEAD:(ch['h'] + 1) * RWKV_HEAD] += y
        return carry

    lax.fori_loop(0, n_chunks, chunk_step, 0)

    if emit_state:
        so_ref[...] = s_scr[...]

    def pass3(c, carry):
        rows = pl.ds(pl.multiple_of(c * ROWS_A, ROWS_A), ROWS_A)
        y = y_s[rows, :]
        yc = y - seg_sum(y) * (1.0 / RWKV_HEAD)
        var = seg_sum(yc * yc) * (1.0 / RWKV_HEAD)
        yn = yc * lax.rsqrt(var + GN_EPS) * lng_ref[...] + lnb_ref[...]
        o_ref[rows, D_CONV:D_CONV + D_RWKV] = ((yn + bon_s[rows, :]) * g_s[rows, :]).astype(BF16)
        return carry

    lax.fori_loop(0, n_blocks, pass3, 0)


def _pad_rows(w, lo):
    return jnp.pad(w, [(0, 0)] * (w.ndim - 2) + [(lo, D_TAIL - lo - w.shape[-2]), (0, 0)])


def _mix_call(proj, tail, P, l, n, nseq, s0=None, emit_state=False):
    T = proj.shape[0]
    B = T // n
    rows = nseq * n
    assert s0 is None or nseq == 1
    full = lambda shape: pl.BlockSpec(shape, lambda b: (0,) * len(shape))
    once = pl.Buffered(1)
    in_specs = [
        pl.BlockSpec((rows, 3 * D_CONV), lambda b: (b, 1), pipeline_mode=once),
        pl.BlockSpec((rows, 3 * D_RWKV), lambda b: (b, 2), pipeline_mode=once),
        pl.BlockSpec((rows, D_TAIL), lambda b: (b, 0), pipeline_mode=once),
        full((1, 3 * D_RWKV)), full((1, D_TAIL)), full((3, D_CONV)),
        full((2, D_RWKV)), full((2, D_TAIL, D_RWKV)), full((2, D_RWKV)), full((2, D_TAIL, D_RWKV)),
        full((D_TAIL, D_RWKV)), full((1, D_RWKV)), full((1, D_RWKV)), full((1, D_RWKV)),
        full((1, D_RWKV)), full((1, D_RWKV)), full((D_RWKV, D_RWKV)),
    ]
    head = jnp.arange(D_RWKV) // RWKV_HEAD
    seg = (head[:, None] == head[None, :]).astype(BF16)
    mu = P['rwkv_mu'][l]
    row = lambda a: a.reshape(1, -1)
    args = [proj, proj, tail, row(mu[:3 * D_RWKV]), row(mu[3 * D_RWKV:]), P['conv_w'][l],
            P['rwkv_w0'][l], _pad_rows(P['rwkv_w_up'][l], 0), P['rwkv_a0'][l],
            _pad_rows(P['rwkv_a_up'][l], DECAY_RANK), _pad_rows(P['rwkv_g_up'][l], DECAY_RANK + A_RANK),
            row(P['rwkv_k_k'][l]), row(P['rwkv_k_a'][l]), row(P['rwkv_r_k'][l]),
            row(P['rwkv_ln_g'][l]), row(P['rwkv_ln_b'][l]), seg]
    if s0 is not None:
        in_specs.append(pl.BlockSpec((None, None, 2, RWKV_HEADS, RWKV_HEAD, RWKV_HEAD),
                                     lambda b: (b, l, 0, 0, 0, 0)))
        args.append(s0)
    out_specs = [pl.BlockSpec((rows, D_CONV + D_RWKV), lambda b: (b, 0))]
    out_shape = [jax.ShapeDtypeStruct((T, D_CONV + D_RWKV), BF16)]
    state_shape = (2, RWKV_HEADS, RWKV_HEAD, RWKV_HEAD)
    if emit_state:
        out_specs.append(pl.BlockSpec((nseq,) + state_shape, lambda b: (b, 0, 0, 0, 0)))
        out_shape.append(jax.ShapeDtypeStruct((B,) + state_shape, F32))
    per_head = lambda: pltpu.VMEM((2, RWKV_HEADS, rows, RWKV_HEAD), BF16)
    return pl.pallas_call(
        functools.partial(_mix_kernel, n=n, nseq=nseq, has_s0=s0 is not None, emit_state=emit_state),
        grid=(B // nseq,),
        in_specs=in_specs,
        out_specs=out_specs,
        out_shape=out_shape,
        scratch_shapes=[per_head(), per_head(), per_head(), per_head(),
                        pltpu.VMEM((RWKV_HEADS, rows, RWKV_HEAD), BF16),
                        pltpu.VMEM((2, rows // CHUNK, D_RWKV), F32),
                        pltpu.VMEM((rows, D_RWKV), F32), pltpu.VMEM((rows, D_RWKV), F32),
                        pltpu.VMEM((rows, D_RWKV), F32),
                        pltpu.VMEM((nseq,) + state_shape, F32)],
        compiler_params=_params(1),
        name="conv_rwkv",
    )(*args)


def _outproj_kernel(at_ref, cr_ref, w_ref, x_ref, ga_ref, shf_ref, scf_ref, g1_ref, g2_ref,
                    x1_ref, h2_ref):
    kt = pl.program_id(1)

    @pl.when(kt == 0)
    def _():
        x1_ref[...] = jnp.zeros(x1_ref.shape, F32)

    lhs = jnp.where(kt < 2, at_ref[...], cr_ref[...])
    x1_ref[...] += jnp.dot(lhs, w_ref[...].astype(BF16), preferred_element_type=F32)

    @pl.when(kt == pl.num_programs(1) - 1)
    def _():
        x1 = x_ref[...] + ga_ref[...] * _rms(x1_ref[...], g1_ref[...])
        x1_ref[...] = x1
        h2_ref[...] = (_rms(x1, g2_ref[...]) * (1.0 + scf_ref[...]) + shf_ref[...]).astype(BF16)


def _outproj_call(attn, cr, x, mod3, ng3, w_out, l, row_fn, tm):
    T = x.shape[0]
    tk = 512
    return pl.pallas_call(
        _outproj_kernel,
        grid=(T // tm, D_MODEL // tk),
        in_specs=[
            pl.BlockSpec((tm, tk), lambda i, k: (i, jnp.minimum(k, 1))),
            pl.BlockSpec((tm, tk), lambda i, k: (i, jnp.maximum(k - 2, 0))),
            pl.BlockSpec((None, tk, D_MODEL), lambda i, k: (l, k, 0)),
            pl.BlockSpec((tm, D_MODEL), lambda i, k: (i, 0), pipeline_mode=pl.Buffered(1)),
            _mod_spec(row_fn, 2), _mod_spec(row_fn, 3), _mod_spec(row_fn, 4),
            _ng_spec(l, 1), _ng_spec(l, 2),
        ],
        out_specs=[
            pl.BlockSpec((tm, D_MODEL), lambda i, k: (i, 0)),
            pl.BlockSpec((tm, D_MODEL), lambda i, k: (i, 0)),
        ],
        out_shape=[jax.ShapeDtypeStruct((T, D_MODEL), F32), jax.ShapeDtypeStruct((T, D_MODEL), BF16)],
        compiler_params=_params(2),
        name="out_proj",
    )(attn, cr, w_out, x, mod3, mod3, mod3, ng3, ng3)


def _ffn_kernel(h_ref, x1_ref, wg_ref, wu_ref, wd_ref, gf_ref, g3_ref, o_ref):
    f = pl.program_id(1)

    @pl.when(f == 0)
    def _():
        o_ref[...] = jnp.zeros(o_ref.shape, F32)

    h = h_ref[...]
    gate = _dot(h, wg_ref[...])
    up = _dot(h, wu_ref[...])
    o_ref[...] += _dot(gate * jax.nn.sigmoid(gate) * up, wd_ref[...])

    @pl.when(f == pl.num_programs(1) - 1)
    def _():
        o_ref[...] = x1_ref[...] + gf_ref[...] * _rms(o_ref[...], g3_ref[...])


def _ffn_call(h2, x1, mod3, ng3, w_gate, w_up, w_down, l, row_fn, tm):
    T = x1.shape[0]
    tf = 256
    return pl.pallas_call(
        _ffn_kernel,
        grid=(T // tm, D_FF // tf),
        in_specs=[
            pl.BlockSpec((tm, D_MODEL), lambda i, f: (i, 0)),
            pl.BlockSpec((tm, D_MODEL), lambda i, f: (i, 0), pipeline_mode=pl.Buffered(1)),
            pl.BlockSpec((None, D_MODEL, tf), lambda i, f: (l, 0, f)),
            pl.BlockSpec((None, D_MODEL, tf), lambda i, f: (l, 0, f)),
            pl.BlockSpec((None, tf, D_MODEL), lambda i, f: (l, f, 0)),
            _mod_spec(row_fn, 5),
            _ng_spec(l, 3),
        ],
        out_specs=pl.BlockSpec((tm, D_MODEL), lambda i, f: (i, 0)),
        out_shape=jax.ShapeDtypeStruct((T, D_MODEL), F32),
        compiler_params=_params(2),
        name="ffn",
    )(h2, x1, w_gate, w_up, w_down, mod3, ng3)


def _cast_kernel(w_ref, o_ref):
    o_ref[...] = w_ref[...].astype(BF16)


def _cast_call(w, tk, rows=None):
    L, K, N = w.shape
    rows = K if rows is None else rows
    spec = pl.BlockSpec((None, tk, N), lambda l, k: (l, k, 0))
    return pl.pallas_call(
        _cast_kernel,
        grid=(L, rows // tk),
        in_specs=[spec],
        out_specs=spec,
        out_shape=jax.ShapeDtypeStruct((L, rows, N), BF16),
        compiler_params=_params(2),
        name="cast_bf16",
    )(w)


def _rope_tables(n):
    rows = n // GRID_W
    row = jnp.repeat(jnp.arange(rows), GRID_W).astype(F32)
    col = jnp.tile(jnp.arange(GRID_W), rows).astype(F32)
    half = HEAD_DIM // 2
    inv = 1.0 / (ROPE_THETA ** (jnp.arange(0, half, 2, dtype=F32) / half))
    ar = row[:, None] * inv
    ac = col[:, None] * inv
    ang = jnp.concatenate([ar, ar, ac, ac], axis=-1)
    return jnp.cos(ang), jnp.sin(ang)


def _layer(x, mod3, ng3, P, l, n, row_fn_of_tm, rope_tabs=None, cache=None, s0=None, emit=False):
    proj, tail = _inproj_call(x, mod3, ng3, P['w_in_t'], P['w_tail_t'], l, row_fn_of_tm(1024), 1024)
    attn_out = _attn_call(proj, P['q_norm'][l].reshape(1, -1), P['k_norm'][l].reshape(1, -1), l, n,
                          rope_tabs=rope_tabs, cache=cache, emit_kv=emit)
    mix_out = _mix_call(proj, tail, P, l, n, 1 if s0 is not None else 2, s0=s0, emit_state=emit)
    attn = attn_out[0]
    cr = mix_out[0]
    x1, h2 = _outproj_call(attn, cr, x, mod3, ng3, P['w_out'], l, row_fn_of_tm(1024), 1024)
    x2 = _ffn_call(h2, x1, mod3, ng3, P['w_gate'], P['w_up'], P['w_down'], l, row_fn_of_tm(1024), 1024)
    if emit:
        return x2, attn_out[1], attn_out[2], mix_out[1]
    return x2


def kernel(x_prompt, x_sample, cache_k, cache_v, state_rwkv, c, c_ctx, w_ada, b_ada, norm_g, w_in, q_norm, k_norm, conv_w, rwkv_mu, rwkv_w0, rwkv_w_up, rwkv_a0, rwkv_a_up, rwkv_g_up, rwkv_k_k, rwkv_k_a, rwkv_r_k, rwkv_ln_g, rwkv_ln_b, w_out, w_gate, w_up, w_down):
    P = {'w_in': w_in, 'q_norm': q_norm, 'k_norm': k_norm, 'conv_w': conv_w, 'rwkv_mu': rwkv_mu,
         'rwkv_w0': rwkv_w0, 'rwkv_w_up': rwkv_w_up, 'rwkv_a0': rwkv_a0, 'rwkv_a_up': rwkv_a_up,
         'rwkv_g_up': rwkv_g_up, 'rwkv_k_k': rwkv_k_k, 'rwkv_k_a': rwkv_k_a, 'rwkv_r_k': rwkv_r_k,
         'rwkv_ln_g': rwkv_ln_g, 'rwkv_ln_b': rwkv_ln_b, 'w_out': w_out,
         'w_gate': w_gate, 'w_up': w_up, 'w_down': w_down}
    w_in_t = jnp.swapaxes(w_in, 1, 2)
    P['w_in_t'] = w_in_t
    P['w_tail_t'] = w_in_t[:, D_MAIN:, :]
    batch, seq, _ = x_prompt.shape
    dec_batch, dec_seq, _ = x_sample.shape

    cc = jnp.zeros((8, D_MODEL), F32).at[0].set(c_ctx).at[1:1 + dec_batch].set(c)
    mod3 = _ada_call(cc, w_ada, b_ada).reshape(DEPTH * 8, 1, N_MOD * D_MODEL)
    ng3 = norm_g.reshape(DEPTH * 4, 1, D_MODEL)

    ck = cache_k.reshape(dec_batch, DEPTH, PAST_LEN, D_KV)
    cv = cache_v.reshape(dec_batch, DEPTH, PAST_LEN, D_KV)
    rope_tabs = _rope_tables(dec_seq)

    xp = x_prompt.reshape(batch * seq, D_MODEL)
    xs = x_sample.reshape(dec_batch * dec_seq, D_MODEL)
    ks, vs, ss = [], [], []
    for l in range(DEPTH):
        ctx_row = lambda tm, l=l: (lambda i: l * 8)
        smp_row = lambda tm, l=l: (lambda i: l * 8 + 1 + (i * tm) // dec_seq)
        xp, k_l, v_l, s_l = _layer(xp, mod3, ng3, P, l, seq, ctx_row, emit=True)
        ks.append(k_l.reshape(batch, seq, N_KV_HEADS, HEAD_DIM))
        vs.append(v_l.reshape(batch, seq, N_KV_HEADS, HEAD_DIM))
        ss.append(s_l)
        xs = _layer(xs, mod3, ng3, P, l, dec_seq, smp_row, rope_tabs=rope_tabs, cache=(ck, cv),
                    s0=state_rwkv)
    return (xp.reshape(batch, seq, D_MODEL), xs.reshape(dec_batch, dec_seq, D_MODEL),
            jnp.stack(ks, axis=1), jnp.stack(vs, axis=1), jnp.stack(ss, axis=1))
```

```python
import functools
import math

import jax
import jax.numpy as jnp
from jax import lax
from jax.experimental import pallas as pl
from jax.experimental.pallas import tpu as pltpu

D_MODEL = 2048
DEPTH = 2
GRID_W = 64
HEAD_DIM = 128
N_HEADS = 8
N_KV_HEADS = 2
GROUP = N_HEADS // N_KV_HEADS
D_ATTN = N_HEADS * HEAD_DIM
D_KV = N_KV_HEADS * HEAD_DIM
ROPE_THETA = 10000.0
D_CONV = 512
D_RWKV = 512
RWKV_HEAD = 64
RWKV_HEADS = D_RWKV // RWKV_HEAD
DECAY_RANK = 32
A_RANK = 32
GATE_RANK = 96
D_TAIL = DECAY_RANK + A_RANK + GATE_RANK
D_IN = D_ATTN + 2 * D_KV + 3 * D_CONV + 3 * D_RWKV + D_TAIL
D_MAIN = D_IN - D_TAIL
D_FF = 5632
N_MOD = 6
EPS = 1e-6
GN_EPS = 64e-5
PAST_LEN = 256

VMEM_LIMIT_BYTES = 56 * 1024 * 1024
CHUNK = 64
BF16 = jnp.bfloat16
F32 = jnp.float32


def _params(n_grid, **kw):
    sem = ("parallel",) + ("arbitrary",) * (n_grid - 1)
    return pltpu.CompilerParams(dimension_semantics=sem, vmem_limit_bytes=VMEM_LIMIT_BYTES, **kw)


def _rms(x, g):
    return x * lax.rsqrt(jnp.mean(x * x, axis=-1, keepdims=True) + EPS) * g


def _dot(a, b):
    return jnp.dot(a.astype(BF16), b.astype(BF16), preferred_element_type=F32)


_NN = (((1,), (0,)), ((), ()))
_NT = (((1,), (1,)), ((), ()))
_TN = (((0,), (0,)), ((), ()))


def _mm(a, b, dims=_NN):
    return lax.dot_general(a.astype(BF16), b.astype(BF16), dims, preferred_element_type=F32)


def _ada_kernel(c_ref, w_ref, b_ref, o_ref):
    c = c_ref[...]
    s = c * jax.nn.sigmoid(c)
    o_ref[...] = _dot(s, w_ref[...]) + b_ref[...]


def _ada_call(cc, w_ada, b_ada):
    tn = 512
    n_out = N_MOD * D_MODEL
    return pl.pallas_call(
        _ada_kernel,
        grid=(DEPTH, n_out // tn),
        in_specs=[
            pl.BlockSpec((8, D_MODEL), lambda l, n: (0, 0)),
            pl.BlockSpec((None, D_MODEL, tn), lambda l, n: (l, 0, n)),
            pl.BlockSpec((None, 1, tn), lambda l, n: (l, 0, n)),
        ],
        out_specs=pl.BlockSpec((None, 8, tn), lambda l, n: (l, 0, n)),
        out_shape=jax.ShapeDtypeStruct((DEPTH, 8, n_out), F32),
        compiler_params=_params(2),
        name="adaln",
    )(cc, w_ada, b_ada.reshape(DEPTH, 1, n_out))


def _mod_spec(row_fn, chunk):
    return pl.BlockSpec((None, 1, D_MODEL), lambda i, *_: (row_fn(i), 0, chunk))


def _ng_spec(l, j):
    return pl.BlockSpec((None, 1, D_MODEL), lambda i, *_: (l * 4 + j, 0, 0))


def _inproj_kernel(x_ref, sh_ref, sc_ref, g_ref, w_ref, wt_ref, o_ref, ot_ref, h_scr):
    @pl.when(pl.program_id(1) == 0)
    def _():
        h = _rms(x_ref[...], g_ref[...]) * (1.0 + sc_ref[...]) + sh_ref[...]
        h_scr[...] = h.astype(BF16)
        ot_ref[...] = _mm(h_scr[...], wt_ref[...], _NT)

    o_ref[...] = _mm(h_scr[...], w_ref[...], _NT)


def _inproj_call(x, mod3, ng3, w_in_t, w_tail_t, l, row_fn, tm):
    T = x.shape[0]
    tn = 512
    return pl.pallas_call(
        _inproj_kernel,
        grid=(T // tm, D_MAIN // tn),
        in_specs=[
            pl.BlockSpec((tm, D_MODEL), lambda i, n: (i, 0)),
            _mod_spec(row_fn, 0),
            _mod_spec(row_fn, 1),
            _ng_spec(l, 0),
            pl.BlockSpec((None, tn, D_MODEL), lambda i, n: (l, n, 0)),
            pl.BlockSpec((None, D_TAIL, D_MODEL), lambda i, n: (l, 0, 0)),
        ],
        out_specs=[
            pl.BlockSpec((tm, tn), lambda i, n: (i, n)),
            pl.BlockSpec((tm, D_TAIL), lambda i, n: (i, 0)),
        ],
        out_shape=[
            jax.ShapeDtypeStruct((T, D_MAIN), F32),
            jax.ShapeDtypeStruct((T, D_TAIL), F32),
        ],
        scratch_shapes=[pltpu.VMEM((tm, D_MODEL), BF16)],
        compiler_params=_params(2),
        name="in_proj",
    )(x, mod3, mod3, ng3, w_in_t, w_tail_t)


def _rope(x, cos, sin):
    lane = lax.broadcasted_iota(jnp.int32, x.shape, 1)
    first = (lane % (HEAD_DIM // 2)) < (HEAD_DIM // 4)
    rot = jnp.where(first, -pltpu.roll(x, HEAD_DIM - HEAD_DIM // 4, 1), pltpu.roll(x, HEAD_DIM // 4, 1))
    return x * cos + rot * sin


def _attn_kernel(*refs, n, past, rope, emit_kv, qb):
    it = iter(refs)
    q_ref, k_ref, v_ref, qn_ref, kn_ref = (next(it) for _ in range(5))
    cos_ref = sin_ref = ck_ref = cv_ref = None
    if rope:
        cos_ref, sin_ref = next(it), next(it)
    if past:
        ck_ref, cv_ref = next(it), next(it)
    o_ref = next(it)
    if emit_kv:
        ko_ref, vo_ref = next(it), next(it)
    k_scr, v_scr = next(it), next(it)

    k = _rms(k_ref[...], kn_ref[...])
    v = v_ref[...]
    if emit_kv:
        ko_ref[...] = k
        vo_ref[...] = v
    if rope:
        k = _rope(k, cos_ref[...], sin_ref[...])
    k_scr[0:n, :] = k.astype(BF16)
    v_scr[0:n, :] = v.astype(BF16)
    if past:
        k_scr[n:n + past, :] = ck_ref[...].astype(BF16)
        v_scr[n:n + past, :] = cv_ref[...].astype(BF16)
    exp2_scale = HEAD_DIM ** -0.5 * math.log2(math.e)

    blocks_per_step = min(n // qb, 4)

    def block(b, carry):
        items = [(pl.ds(pl.multiple_of((b * blocks_per_step + j) * qb, qb), qb),
                  slice(g * HEAD_DIM, (g + 1) * HEAD_DIM))
                 for j in range(blocks_per_step) for g in range(GROUP)]

        def scores(item):
            rows, cols = item
            q = _rms(q_ref[rows, cols], qn_ref[...])
            if rope:
                q = _rope(q, cos_ref[rows, :], sin_ref[rows, :])
            return lax.dot_general(q.astype(BF16), k_scr[...], _NT, preferred_element_type=F32)

        def softmax(s):
            p = jnp.exp2((s - jnp.max(s, axis=-1, keepdims=True)) * exp2_scale)
            return p.astype(BF16), jnp.sum(p, axis=-1, keepdims=True)

        def values(item, pd):
            rows, cols = item
            o = jnp.dot(pd[0], v_scr[...], preferred_element_type=F32)
            o_ref[rows, cols] = (o / pd[1]).astype(BF16)

        if blocks_per_step == 1:
            ps = [softmax(s) for s in [scores(item) for item in items]]
            for item, pd in zip(items, ps):
                values(item, pd)
            return carry
        depth = 2
        ss, ps = {}, {}
        for i in range(len(items) + depth):
            if i < len(items):
                ss[i] = scores(items[i])
            if 1 <= i <= len(items):
                ps[i - 1] = softmax(ss.pop(i - 1))
            if i >= depth:
                values(items[i - depth], ps.pop(i - depth))
        return carry

    lax.fori_loop(0, n // (qb * blocks_per_step), block, 0)


def _attn_call(proj, qn, kn, l, n, rope_tabs=None, cache=None, emit_kv=False):
    T = proj.shape[0]
    B = T // n
    past = PAST_LEN if cache is not None else 0
    qw = GROUP * HEAD_DIM
    k_blk = D_ATTN // HEAD_DIM
    v_blk = (D_ATTN + D_KV) // HEAD_DIM
    in_specs = [
        pl.BlockSpec((n, qw), lambda b, h: (b, h)),
        pl.BlockSpec((n, HEAD_DIM), lambda b, h: (b, k_blk + h)),
        pl.BlockSpec((n, HEAD_DIM), lambda b, h: (b, v_blk + h)),
        pl.BlockSpec((1, HEAD_DIM), lambda b, h: (0, 0)),
        pl.BlockSpec((1, HEAD_DIM), lambda b, h: (0, 0)),
    ]
    args = [proj, proj, proj, qn, kn]
    if rope_tabs is not None:
        in_specs += [pl.BlockSpec((n, HEAD_DIM), lambda b, h: (0, 0))] * 2
        args += list(rope_tabs)
    if cache is not None:
        in_specs += [pl.BlockSpec((None, None, PAST_LEN, HEAD_DIM), lambda b, h: (b, l, 0, h))] * 2
        args += list(cache)
    out_specs = [pl.BlockSpec((n, qw), lambda b, h: (b, h))]
    out_shape = [jax.ShapeDtypeStruct((T, D_ATTN), BF16)]
    if emit_kv:
        out_specs += [pl.BlockSpec((None, n, HEAD_DIM), lambda b, h: (b, 0, h))] * 2
        out_shape += [jax.ShapeDtypeStruct((B, n, D_KV), F32)] * 2
    return pl.pallas_call(
        functools.partial(_attn_kernel, n=n, past=past, rope=rope_tabs is not None,
                          emit_kv=emit_kv, qb=256),
        grid=(B, N_KV_HEADS),
        in_specs=in_specs,
        out_specs=out_specs,
        out_shape=out_shape,
        scratch_shapes=[pltpu.VMEM((n + past, HEAD_DIM), BF16)] * 2,
        compiler_params=_params(2),
        name="attention",
    )(*args)


ROWS_A = 256
SUB = 16
assert CHUNK // SUB == 4


def _split2(x):
    hi = x.astype(BF16)
    return hi, (x - hi.astype(F32)).astype(BF16)


def _dot_x3(a, b):
    ah, al = _split2(a)
    bh, bl = _split2(b)
    mm = lambda p, q: jnp.dot(p, q, preferred_element_type=F32)
    return mm(ah, bh) + mm(ah, bl) + mm(al, bh)


def _split_sum(x, pieces, dot_piece):
    acc = None
    rest = x
    for _ in range(pieces):
        part = rest.astype(BF16)
        rest = rest - part.astype(F32)
        term = dot_piece(part)
        acc = term if acc is None else acc + term
    return acc


def _mix_kernel(*refs, n, nseq, has_s0, emit_state):
    it = iter(refs)
    cv_ref, um_ref, ut_ref = next(it), next(it), next(it)
    (mum_ref, mut_ref, cw_ref, w0_ref, wup_ref, a0_ref, aup_ref, gup_ref, kk_ref, ka_ref,
     rk_ref, lng_ref, lnb_ref, seg_ref) = (next(it) for _ in range(14))
    s0_ref = next(it) if has_s0 else None
    o_ref = next(it)
    so_ref = next(it) if emit_state else None
    al_s, be_s, kd_s, r_s, v_s, p_s, bon_s, g_s, y_s, s_scr = (next(it) for _ in range(10))
    blocks_per_seq = n // ROWS_A
    n_blocks = nseq * blocks_per_seq
    n_chunks = n // CHUNK
    per_block = ROWS_A // CHUNK

    def seg_sum(x):
        return _split_sum(x, 2, lambda part: jnp.dot(part, seg_ref[...], preferred_element_type=F32))

    PAIR = 2 * RWKV_HEAD

    def per_head_tiles(x, high):
        lane = lax.broadcasted_iota(jnp.int32, x.shape, 1)
        is_even = (lane // RWKV_HEAD) % 2 == 0
        parts = (jnp.where(is_even, x, 0.0), jnp.where(is_even, 0.0, x))
        tiles = []
        for h in range(RWKV_HEADS):
            t = parts[h % 2][:, (h // 2) * PAIR:(h // 2 + 1) * PAIR]
            if (h % 2 == 1) != high:
                t = pltpu.roll(t, RWKV_HEAD, 1)
            tiles.append(t)
        return tiles

    bi = lax.broadcasted_iota(jnp.int32, (ROWS_A, ROWS_A), 0)
    bj = lax.broadcasted_iota(jnp.int32, (ROWS_A, ROWS_A), 1)
    same_chunk = (bi // CHUNK) == (bj // CHUNK)
    cum_mat = [(same_chunk & (bj <= bi)).astype(BF16), (same_chunk & (bj >= bi)).astype(BF16)]

    def pass1(c, carry):
        c0 = pl.multiple_of(c * ROWS_A, ROWS_A)
        rows = pl.ds(c0, ROWS_A)
        prev_row = pl.ds(jnp.maximum(c0 - 1, 0), 1)
        next_row = pl.ds(jnp.minimum(c0 + ROWS_A, nseq * n - 1), 1)
        seq_first = (c % blocks_per_seq) == 0
        seq_last = (c % blocks_per_seq) == blocks_per_seq - 1

        def neighbours(load):
            cur = load(rows)
            rid = lax.broadcasted_iota(jnp.int32, cur.shape, 0)
            before = jnp.where(seq_first, 0.0, load(prev_row))
            after = jnp.where(seq_last, 0.0, load(next_row))
            prev = jnp.where(rid == 0, before, pltpu.roll(cur, 1, 0))
            nxt = jnp.where(rid == ROWS_A - 1, after, pltpu.roll(cur, ROWS_A - 1, 0))
            return cur, prev, nxt

        z, zp, zn = neighbours(lambda rs: cv_ref[rs, D_CONV:2 * D_CONV] * cv_ref[rs, 2 * D_CONV:3 * D_CONV])
        cw = cw_ref[...]
        conv = cv_ref[rows, 0:D_CONV] * (zp * cw[0:1, :] + z * cw[1:2, :] + zn * cw[2:3, :])
        o_ref[rows, 0:D_CONV] = conv.astype(BF16)

        u, up, un = neighbours(lambda rs: um_ref[rs, :])
        u = u + mum_ref[...] * (0.5 * (up + un) - u)
        t, tp, tn = neighbours(lambda rs: ut_ref[rs, :])
        ut = t + mut_ref[...] * (0.5 * (tp + tn) - t)
        r = u[:, 0:D_RWKV]
        k = u[:, D_RWKV:2 * D_RWKV]
        v = u[:, 2 * D_RWKV:3 * D_RWKV]
        kk = k * kk_ref[...]
        kk = kk * lax.rsqrt(seg_sum(kk * kk) + 1e-12)
        for h, v_h in enumerate(per_head_tiles(v, high=False)):
            v_s[h, rows, :] = v_h.astype(BF16)
        tw = jnp.tanh(ut)
        a_sum = jnp.zeros((ROWS_A, D_RWKV), F32)
        for d in range(2):
            zz = w0_ref[d:d + 1, :] + _dot_x3(tw, wup_ref[d])
            lw = -math.exp(-0.5) * jax.nn.sigmoid(zz)
            a = jax.nn.sigmoid(a0_ref[d:d + 1, :] + _dot_x3(ut, aup_ref[d]))
            a_sum = a_sum + a
            cum = _split_sum(lw, 3, lambda part: jnp.dot(cum_mat[d], part, preferred_element_type=F32))
            e_out = jnp.exp(-cum)
            kd = k * (1.0 + (a - 1.0) * ka_ref[...])
            scaled = ((al_s, -kk * jnp.exp(cum - lw)), (be_s, kk * a * e_out), (kd_s, kd * e_out),
                      (r_s, r * jnp.exp(cum)))
            for ref, val in scaled:
                for h, val_h in enumerate(per_head_tiles(val, high=True)):
                    ref[d, h, rows, :] = val_h.astype(BF16)
            half = lax.broadcasted_iota(jnp.int32, (1, PAIR), 1) >= RWKV_HEAD
            for j in range(per_block):
                last = j * CHUNK + (CHUNK - 1 if d == 0 else 0)
                for h, p_h in enumerate(per_head_tiles(jnp.exp(cum[last:last + 1, :]), high=True)):
                    p_s[d, c * per_block + j, h:h + 1, :] = jnp.where(half, p_h, 1.0)
        kd_sum = k * (2.0 + (a_sum - 2.0) * ka_ref[...])
        bon_s[rows, :] = seg_sum(r * kd_sum * rk_ref[...]) * v
        g_s[rows, :] = _dot_x3(jax.nn.sigmoid(ut), gup_ref[...])
        y_s[rows, :] = jnp.zeros((ROWS_A, D_RWKV), F32)
        return carry

    lax.fori_loop(0, n_blocks, pass1, 0)

    eye_low = (lax.broadcasted_iota(jnp.int32, (RWKV_HEAD, PAIR), 0)
               == lax.broadcasted_iota(jnp.int32, (RWKV_HEAD, PAIR), 1)).astype(F32)
    for s in range(nseq):
        for d in range(2):
            for h in range(RWKV_HEADS):
                if has_s0:
                    s_scr[s, d, h] = jnp.concatenate([eye_low[:, 0:RWKV_HEAD], s0_ref[d, h]], axis=1)
                else:
                    s_scr[s, d, h] = eye_low

    ri = lax.broadcasted_iota(jnp.int32, (CHUNK, 2 * CHUNK), 0)
    ci = lax.broadcasted_iota(jnp.int32, (CHUNK, 2 * CHUNK), 1) % CHUNK
    ei = lax.broadcasted_iota(jnp.int32, (CHUNK, CHUNK), 0)
    ej = lax.broadcasted_iota(jnp.int32, (CHUNK, CHUNK), 1)
    eye = (ei == ej).astype(F32)
    diag_blk = (ei // SUB) == (ej // SUB)
    zeros_b = jnp.zeros((CHUNK, PAIR), BF16)
    n_rounds = int(math.log2(SUB)) - 1

    def chunk_step(c, carry):
        chains = []
        for s, d in ((s, d) for s in range(nseq) for d in range(2)):
            cc = s * n_chunks + (c if d == 0 else n_chunks - 1 - c)
            rows = pl.ds(pl.multiple_of(cc * CHUNK, CHUNK), CHUNK)
            p_tile = p_s[d, cc]
            for h in range(RWKV_HEADS):
                chains.append(dict(
                    s=s, d=d, h=h, rows=rows, p=p_tile[h:h + 1, :],
                    strict=(ci < ri) if d == 0 else (ci > ri), incl=(ci <= ri) if d == 0 else (ci >= ri),
                    al=al_s[d, h, rows, :], be=be_s[d, h, rows, :], kd=kd_s[d, h, rows, :],
                    r=r_s[d, h, rows, :], v=v_s[h, rows, :]))

        def each(fn, *lists):
            return [fn(*args) for args in zip(chains, *lists)]

        bk = each(lambda ch: jnp.concatenate([ch['be'], ch['kd']], axis=0))
        g = each(lambda ch, m: _mm(jnp.concatenate([ch['al'], ch['r']], axis=0), m, _NT), bk)
        top = each(lambda ch, m: jnp.where(ch['strict'], m[0:CHUNK], 0.0), g)
        bot = each(lambda ch, m: jnp.where(ch['incl'], m[CHUNK:2 * CHUNK], 0.0).astype(BF16), g)
        rhs = each(lambda ch, m: _mm(m, jnp.concatenate([zeros_b, ch['v']], axis=0))
                   + ch['al'].astype(F32), top)
        a = [m[:, 0:CHUNK] for m in top]
        xs = [jnp.where(diag_blk, m, 0.0) for m in a]
        low = [jnp.where(diag_blk, 0.0, m) for m in a]
        ts = [eye + x for x in xs]
        xs = [_mm(x, x) for x in xs]
        for _ in range(n_rounds - 1):
            z = [_mm(jnp.concatenate([x, t], axis=0), x) for x, t in zip(xs, ts)]
            xs = [m[0:CHUNK] for m in z]
            ts = [t + m[CHUNK:2 * CHUNK] for t, m in zip(ts, z)]
        ts = [t + _mm(t, x) for t, x in zip(ts, xs)]
        ms = [_mm(t, m) for t, m in zip(ts, low)]
        sol = [_mm(t, m) for t, m in zip(ts, rhs)]
        m2 = [_mm(m, m) for m in ms]
        ims = [eye + m for m in ms]
        ns = [im + _mm(im, mm2) for im, mm2 in zip(ims, m2)]
        w = [_mm(nn, x) for nn, x in zip(ns, sol)]
        st = each(lambda ch: s_scr[ch['s'], ch['d'], ch['h']])
        stb = [t.astype(BF16) for t in st]
        u = [_mm(ww, t, _NT) for ww, t in zip(w, stb)]
        uv = each(lambda ch, uu: jnp.concatenate([uu.astype(BF16), ch['v'][:, 0:RWKV_HEAD]], axis=0), u)
        ys = each(lambda ch, t, qq, m: _mm(ch['r'], t, _NT) + _mm(qq, m), stb, bot, uv)
        s_new = each(lambda ch, t, m, kb: (t + _mm(m, kb, _TN)) * ch['p'], st, uv, bk)
        for ch, t in zip(chains, s_new):
            s_scr[ch['s'], ch['d'], ch['h']] = t
        for ch, y in zip(chains, ys):
            y_s[ch['rows'], ch['h'] * RWKV_HEAD:(ch['h'] + 1) * RWKV_HEAD] += y
        return carry

    lax.fori_loop(0, n_chunks, chunk_step, 0)

    if emit_state:
        for s in range(nseq):
            for d in range(2):
                for h in range(RWKV_HEADS):
                    so_ref[s, d, h] = s_scr[s, d, h][:, RWKV_HEAD:PAIR]

    def pass3(c, carry):
        rows = pl.ds(pl.multiple_of(c * ROWS_A, ROWS_A), ROWS_A)
        y = y_s[rows, :]
        yc = y - seg_sum(y) * (1.0 / RWKV_HEAD)
        var = seg_sum(yc * yc) * (1.0 / RWKV_HEAD)
        yn = yc * lax.rsqrt(var + GN_EPS) * lng_ref[...] + lnb_ref[...]
        o_ref[rows, D_CONV:D_CONV + D_RWKV] = ((yn + bon_s[rows, :]) * g_s[rows, :]).astype(BF16)
        return carry

    lax.fori_loop(0, n_blocks, pass3, 0)


def _pad_rows(w, lo):
    return jnp.pad(w, [(0, 0)] * (w.ndim - 2) + [(lo, D_TAIL - lo - w.shape[-2]), (0, 0)])


def _mix_call(proj, tail, P, l, n, nseq, s0=None, emit_state=False):
    T = proj.shape[0]
    B = T // n
    rows = nseq * n
    assert s0 is None or nseq == 1
    full = lambda shape: pl.BlockSpec(shape, lambda b: (0,) * len(shape))
    lanes = lambda w: -(-w // 128) * 128
    in_bytes = rows * (3 * D_CONV + 3 * D_RWKV + lanes(D_TAIL)) * 4
    scratch_bytes = rows * (4 * 2 * RWKV_HEADS * lanes(RWKV_HEAD) * 2 + RWKV_HEADS * lanes(RWKV_HEAD) * 2
                            + 3 * D_RWKV * 4)
    out_bytes = 2 * rows * (D_CONV + D_RWKV) * 2
    temporaries = 6 * 1024 * 1024
    fits = 2 * in_bytes + scratch_bytes + out_bytes + temporaries <= VMEM_LIMIT_BYTES
    mode = {} if fits else {'pipeline_mode': pl.Buffered(1)}
    in_specs = [
        pl.BlockSpec((rows, 3 * D_CONV), lambda b: (b, 1), **mode),
        pl.BlockSpec((rows, 3 * D_RWKV), lambda b: (b, 2), **mode),
        pl.BlockSpec((rows, D_TAIL), lambda b: (b, 0), **mode),
        full((1, 3 * D_RWKV)), full((1, D_TAIL)), full((3, D_CONV)),
        full((2, D_RWKV)), full((2, D_TAIL, D_RWKV)), full((2, D_RWKV)), full((2, D_TAIL, D_RWKV)),
        full((D_TAIL, D_RWKV)), full((1, D_RWKV)), full((1, D_RWKV)), full((1, D_RWKV)),
        full((1, D_RWKV)), full((1, D_RWKV)), full((D_RWKV, D_RWKV)),
    ]
    head = jnp.arange(D_RWKV) // RWKV_HEAD
    seg = (head[:, None] == head[None, :]).astype(BF16)
    mu = P['rwkv_mu'][l]
    row = lambda a: a.reshape(1, -1)
    args = [proj, proj, tail, row(mu[:3 * D_RWKV]), row(mu[3 * D_RWKV:]), P['conv_w'][l],
            P['rwkv_w0'][l], _pad_rows(P['rwkv_w_up'][l], 0), P['rwkv_a0'][l],
            _pad_rows(P['rwkv_a_up'][l], DECAY_RANK), _pad_rows(P['rwkv_g_up'][l], DECAY_RANK + A_RANK),
            row(P['rwkv_k_k'][l]), row(P['rwkv_k_a'][l]), row(P['rwkv_r_k'][l]),
            row(P['rwkv_ln_g'][l]), row(P['rwkv_ln_b'][l]), seg]
    if s0 is not None:
        in_specs.append(pl.BlockSpec((None, None, 2, RWKV_HEADS, RWKV_HEAD, RWKV_HEAD),
                                     lambda b: (b, l, 0, 0, 0, 0)))
        args.append(s0)
    out_specs = [pl.BlockSpec((rows, D_CONV + D_RWKV), lambda b: (b, 0))]
    out_shape = [jax.ShapeDtypeStruct((T, D_CONV + D_RWKV), BF16)]
    state_shape = (2, RWKV_HEADS, RWKV_HEAD, RWKV_HEAD)
    if emit_state:
        out_specs.append(pl.BlockSpec((nseq,) + state_shape, lambda b: (b, 0, 0, 0, 0)))
        out_shape.append(jax.ShapeDtypeStruct((B,) + state_shape, F32))
    pair = 2 * RWKV_HEAD
    per_head = lambda: pltpu.VMEM((2, RWKV_HEADS, rows, pair), BF16)
    return pl.pallas_call(
        functools.partial(_mix_kernel, n=n, nseq=nseq, has_s0=s0 is not None, emit_state=emit_state),
        grid=(B // nseq,),
        in_specs=in_specs,
        out_specs=out_specs,
        out_shape=out_shape,
        scratch_shapes=[per_head(), per_head(), per_head(), per_head(),
                        pltpu.VMEM((RWKV_HEADS, rows, pair), BF16),
                        pltpu.VMEM((2, rows // CHUNK, RWKV_HEADS, pair), F32),
                        pltpu.VMEM((rows, D_RWKV), F32), pltpu.VMEM((rows, D_RWKV), F32),
                        pltpu.VMEM((rows, D_RWKV), F32),
                        pltpu.VMEM((nseq, 2, RWKV_HEADS, RWKV_HEAD, pair), F32)],
        compiler_params=_params(1),
        name="conv_rwkv",
    )(*args)


def _residual_copy(x_hbm, x_buf, sem):
    rows = x_buf.shape[0]
    start = pl.multiple_of(pl.program_id(0) * rows, rows)
    return pltpu.make_async_copy(x_hbm.at[pl.ds(start, rows), :], x_buf, sem)


def _outproj_kernel(at_ref, cr_ref, w_ref, x_hbm, ga_ref, shf_ref, scf_ref, g1_ref, g2_ref,
                    x1_ref, h2_ref, x_ref, x_sem):
    kt = pl.program_id(1)

    @pl.when(kt == 0)
    def _():
        _residual_copy(x_hbm, x_ref, x_sem).start()
        x1_ref[...] = jnp.zeros(x1_ref.shape, F32)

    lhs = jnp.where(kt < 2, at_ref[...], cr_ref[...])
    x1_ref[...] += jnp.dot(lhs, w_ref[...].astype(BF16), preferred_element_type=F32)

    @pl.when(kt == pl.num_programs(1) - 1)
    def _():
        _residual_copy(x_hbm, x_ref, x_sem).wait()
        x1 = x_ref[...] + ga_ref[...] * _rms(x1_ref[...], g1_ref[...])
        x1_ref[...] = x1
        h2_ref[...] = (_rms(x1, g2_ref[...]) * (1.0 + scf_ref[...]) + shf_ref[...]).astype(BF16)


def _outproj_call(attn, cr, x, mod3, ng3, w_out, l, row_fn, tm):
    T = x.shape[0]
    tk = 512
    return pl.pallas_call(
        _outproj_kernel,
        grid=(T // tm, D_MODEL // tk),
        in_specs=[
            pl.BlockSpec((tm, tk), lambda i, k: (i, jnp.minimum(k, 1))),
            pl.BlockSpec((tm, tk), lambda i, k: (i, jnp.maximum(k - 2, 0))),
            pl.BlockSpec((None, tk, D_MODEL), lambda i, k: (l, k, 0)),
            pl.BlockSpec(memory_space=pl.ANY),
            _mod_spec(row_fn, 2), _mod_spec(row_fn, 3), _mod_spec(row_fn, 4),
            _ng_spec(l, 1), _ng_spec(l, 2),
        ],
        out_specs=[
            pl.BlockSpec((tm, D_MODEL), lambda i, k: (i, 0)),
            pl.BlockSpec((tm, D_MODEL), lambda i, k: (i, 0)),
        ],
        out_shape=[jax.ShapeDtypeStruct((T, D_MODEL), F32), jax.ShapeDtypeStruct((T, D_MODEL), BF16)],
        scratch_shapes=[pltpu.VMEM((tm, D_MODEL), F32), pltpu.SemaphoreType.DMA(())],
        compiler_params=_params(2),
        name="out_proj",
    )(attn, cr, w_out, x, mod3, mod3, mod3, ng3, ng3)


def _ffn_kernel(h_ref, x1_hbm, wg_ref, wu_ref, wd_ref, gf_ref, g3_ref, o_ref, x1_ref, x1_sem):
    f = pl.program_id(1)

    @pl.when(f == 0)
    def _():
        _residual_copy(x1_hbm, x1_ref, x1_sem).start()
        o_ref[...] = jnp.zeros(o_ref.shape, F32)

    h = h_ref[...]
    gate = _dot(h, wg_ref[...])
    up = _dot(h, wu_ref[...])
    o_ref[...] += _dot(gate * jax.nn.sigmoid(gate) * up, wd_ref[...])

    @pl.when(f == pl.num_programs(1) - 1)
    def _():
        _residual_copy(x1_hbm, x1_ref, x1_sem).wait()
        o_ref[...] = x1_ref[...] + gf_ref[...] * _rms(o_ref[...], g3_ref[...])


def _ffn_call(h2, x1, mod3, ng3, w_gate, w_up, w_down, l, row_fn, tm):
    T = x1.shape[0]
    tf = 256
    return pl.pallas_call(
        _ffn_kernel,
        grid=(T // tm, D_FF // tf),
        in_specs=[
            pl.BlockSpec((tm, D_MODEL), lambda i, f: (i, 0)),
            pl.BlockSpec(memory_space=pl.ANY),
            pl.BlockSpec((None, D_MODEL, tf), lambda i, f: (l, 0, f)),
            pl.BlockSpec((None, D_MODEL, tf), lambda i, f: (l, 0, f)),
            pl.BlockSpec((None, tf, D_MODEL), lambda i, f: (l, f, 0)),
            _mod_spec(row_fn, 5),
            _ng_spec(l, 3),
        ],
        out_specs=pl.BlockSpec((tm, D_MODEL), lambda i, f: (i, 0)),
        out_shape=jax.ShapeDtypeStruct((T, D_MODEL), F32),
        scratch_shapes=[pltpu.VMEM((tm, D_MODEL), F32), pltpu.SemaphoreType.DMA(())],
        compiler_params=_params(2),
        name="ffn",
    )(h2, x1, w_gate, w_up, w_down, mod3, ng3)


def _rope_tables(n):
    rows = n // GRID_W
    row = jnp.repeat(jnp.arange(rows), GRID_W).astype(F32)
    col = jnp.tile(jnp.arange(GRID_W), rows).astype(F32)
    half = HEAD_DIM // 2
    inv = 1.0 / (ROPE_THETA ** (jnp.arange(0, half, 2, dtype=F32) / half))
    ar = row[:, None] * inv
    ac = col[:, None] * inv
    ang = jnp.concatenate([ar, ar, ac, ac], axis=-1)
    return jnp.cos(ang), jnp.sin(ang)


def _layer(x, mod3, ng3, P, l, n, row_fn_of_tm, rope_tabs=None, cache=None, s0=None, emit=False):
    proj, tail = _inproj_call(x, mod3, ng3, P['w_in_t'], P['w_tail_t'], l, row_fn_of_tm(1024), 1024)
    attn_out = _attn_call(proj, P['q_norm'][l].reshape(1, -1), P['k_norm'][l].reshape(1, -1), l, n,
                          rope_tabs=rope_tabs, cache=cache, emit_kv=emit)
    mix_out = _mix_call(proj, tail, P, l, n, 1 if s0 is not None else 2, s0=s0, emit_state=emit)
    attn = attn_out[0]
    cr = mix_out[0]
    x1, h2 = _outproj_call(attn, cr, x, mod3, ng3, P['w_out'], l, row_fn_of_tm(1024), 1024)
    x2 = _ffn_call(h2, x1, mod3, ng3, P['w_gate'], P['w_up'], P['w_down'], l, row_fn_of_tm(1024), 1024)
    if emit:
        return x2, attn_out[1], attn_out[2], mix_out[1]
    return x2


def kernel(x_prompt, x_sample, cache_k, cache_v, state_rwkv, c, c_ctx, w_ada, b_ada, norm_g, w_in, q_norm, k_norm, conv_w, rwkv_mu, rwkv_w0, rwkv_w_up, rwkv_a0, rwkv_a_up, rwkv_g_up, rwkv_k_k, rwkv_k_a, rwkv_r_k, rwkv_ln_g, rwkv_ln_b, w_out, w_gate, w_up, w_down):
    P = {'q_norm': q_norm, 'k_norm': k_norm, 'conv_w': conv_w, 'rwkv_mu': rwkv_mu,
         'rwkv_w0': rwkv_w0, 'rwkv_w_up': rwkv_w_up, 'rwkv_a0': rwkv_a0, 'rwkv_a_up': rwkv_a_up,
         'rwkv_g_up': rwkv_g_up, 'rwkv_k_k': rwkv_k_k, 'rwkv_k_a': rwkv_k_a, 'rwkv_r_k': rwkv_r_k,
         'rwkv_ln_g': rwkv_ln_g, 'rwkv_ln_b': rwkv_ln_b, 'w_out': w_out,
         'w_gate': w_gate, 'w_up': w_up, 'w_down': w_down}
    w_in_t = jnp.swapaxes(w_in, 1, 2)
    P['w_in_t'] = w_in_t
    P['w_tail_t'] = w_in_t[:, D_MAIN:, :]
    batch, seq, _ = x_prompt.shape
    dec_batch, dec_seq, _ = x_sample.shape

    cc = jnp.zeros((8, D_MODEL), F32).at[0].set(c_ctx).at[1:1 + dec_batch].set(c)
    mod3 = _ada_call(cc, w_ada, b_ada).reshape(DEPTH * 8, 1, N_MOD * D_MODEL)
    ng3 = norm_g.reshape(DEPTH * 4, 1, D_MODEL)

    ck = cache_k.reshape(dec_batch, DEPTH, PAST_LEN, D_KV)
    cv = cache_v.reshape(dec_batch, DEPTH, PAST_LEN, D_KV)
    rope_tabs = _rope_tables(dec_seq)

    xp = x_prompt.reshape(batch * seq, D_MODEL)
    xs = x_sample.reshape(dec_batch * dec_seq, D_MODEL)
    ks, vs, ss = [], [], []
    for l in range(DEPTH):
        ctx_row = lambda tm, l=l: (lambda i: l * 8)
        smp_row = lambda tm, l=l: (lambda i: l * 8 + 1 + (i * tm) // dec_seq)
        xp, k_l, v_l, s_l = _layer(xp, mod3, ng3, P, l, seq, ctx_row, emit=True)
        ks.append(k_l.reshape(batch, seq, N_KV_HEADS, HEAD_DIM))
        vs.append(v_l.reshape(batch, seq, N_KV_HEADS, HEAD_DIM))
        ss.append(s_l)
        xs = _layer(xs, mod3, ng3, P, l, dec_seq, smp_row, rope_tabs=rope_tabs, cache=(ck, cv),
                    s0=state_rwkv)
    return (xp.reshape(batch, seq, D_MODEL), xs.reshape(dec_batch, dec_seq, D_MODEL),
            jnp.stack(ks, axis=1), jnp.stack(vs, axis=1), jnp.stack(ss, axis=1))
```

```python
import functools
import math

import jax
import jax.numpy as jnp
from jax import lax
from jax.experimental import pallas as pl
from jax.experimental.pallas import tpu as pltpu

D_MODEL = 2048
DEPTH = 2
GRID_W = 64
HEAD_DIM = 128
N_HEADS = 8
N_KV_HEADS = 2
GROUP = N_HEADS // N_KV_HEADS
D_ATTN = N_HEADS * HEAD_DIM
D_KV = N_KV_HEADS * HEAD_DIM
ROPE_THETA = 10000.0
D_CONV = 512
D_RWKV = 512
RWKV_HEAD = 64
RWKV_HEADS = D_RWKV // RWKV_HEAD
DECAY_RANK = 32
A_RANK = 32
GATE_RANK = 96
D_TAIL = DECAY_RANK + A_RANK + GATE_RANK
D_IN = D_ATTN + 2 * D_KV + 3 * D_CONV + 3 * D_RWKV + D_TAIL
D_MAIN = D_IN - D_TAIL
D_FF = 5632
N_MOD = 6
EPS = 1e-6
GN_EPS = 64e-5
PAST_LEN = 256

VMEM_LIMIT_BYTES = 56 * 1024 * 1024
CHUNK = 64
BF16 = jnp.bfloat16
F32 = jnp.float32


def _params(n_grid, **kw):
    sem = ("parallel",) + ("arbitrary",) * (n_grid - 1)
    return pltpu.CompilerParams(dimension_semantics=sem, vmem_limit_bytes=VMEM_LIMIT_BYTES, **kw)


def _rms(x, g):
    return x * lax.rsqrt(jnp.mean(x * x, axis=-1, keepdims=True) + EPS) * g


def _dot(a, b):
    return jnp.dot(a.astype(BF16), b.astype(BF16), preferred_element_type=F32)


_NN = (((1,), (0,)), ((), ()))
_NT = (((1,), (1,)), ((), ()))
_TN = (((0,), (0,)), ((), ()))


def _mm(a, b, dims=_NN):
    return lax.dot_general(a.astype(BF16), b.astype(BF16), dims, preferred_element_type=F32)


def _ada_kernel(c_ref, w_ref, b_ref, o_ref):
    c = c_ref[...]
    s = c * jax.nn.sigmoid(c)
    o_ref[...] = _dot(s, w_ref[...]) + b_ref[...]


def _ada_call(cc, w_ada, b_ada):
    tn = 512
    n_out = N_MOD * D_MODEL
    return pl.pallas_call(
        _ada_kernel,
        grid=(DEPTH, n_out // tn),
        in_specs=[
            pl.BlockSpec((8, D_MODEL), lambda l, n: (0, 0)),
            pl.BlockSpec((None, D_MODEL, tn), lambda l, n: (l, 0, n)),
            pl.BlockSpec((None, 1, tn), lambda l, n: (l, 0, n)),
        ],
        out_specs=pl.BlockSpec((None, 8, tn), lambda l, n: (l, 0, n)),
        out_shape=jax.ShapeDtypeStruct((DEPTH, 8, n_out), F32),
        compiler_params=_params(2),
        name="adaln",
    )(cc, w_ada, b_ada.reshape(DEPTH, 1, n_out))


def _mod_spec(row_fn, chunk):
    return pl.BlockSpec((None, 1, D_MODEL), lambda i, *_: (row_fn(i), 0, chunk))


def _ng_spec(l, j):
    return pl.BlockSpec((None, 1, D_MODEL), lambda i, *_: (l * 4 + j, 0, 0))


INPROJ_NORM_ROWS = 256
INPROJ_VMEM_LIMIT_BYTES = 60 * 1024 * 1024


def _row_tile_copy(x_hbm, x_buf, sem, tile):
    rows = x_buf.shape[0]
    return pltpu.make_async_copy(x_hbm.at[pl.ds(pl.multiple_of(tile * rows, rows), rows), :], x_buf, sem)


def _inproj_kernel(x_hbm, sha_ref, sca_ref, shb_ref, scb_ref, g_ref, w_ref, wt_ref, o_ref, ot_ref,
                   x_buf, h_scr, x_sem):
    i = pl.program_id(0)

    @pl.when(pl.program_id(1) == 0)
    def _():
        @pl.when(i == 0)
        def _():
            _row_tile_copy(x_hbm, x_buf, x_sem, 0).start()

        _row_tile_copy(x_hbm, x_buf, x_sem, i).wait()
        half = x_buf.shape[0] // 2
        for r0 in range(0, x_buf.shape[0], INPROJ_NORM_ROWS):
            rs = slice(r0, r0 + INPROJ_NORM_ROWS)
            sh_ref, sc_ref = (sha_ref, sca_ref) if r0 < half else (shb_ref, scb_ref)
            h = _rms(x_buf[rs, :], g_ref[...]) * (1.0 + sc_ref[...]) + sh_ref[...]
            h_scr[rs, :] = h.astype(BF16)

        @pl.when(i + 1 < pl.num_programs(0))
        def _():
            _row_tile_copy(x_hbm, x_buf, x_sem, i + 1).start()

        ot_ref[...] = _mm(h_scr[...], wt_ref[...], _NT)

    o_ref[...] = _mm(h_scr[...], w_ref[...], _NT)


def _inproj_call(x, mod3, ng3, w_in_t, w_tail_t, l, row_fn_half, tm):
    T = x.shape[0]
    tn = 512
    seg = lambda j: (lambda i: row_fn_half(2 * i + j))
    return pl.pallas_call(
        _inproj_kernel,
        grid=(T // tm, D_MAIN // tn),
        in_specs=[
            pl.BlockSpec(memory_space=pl.ANY),
            _mod_spec(seg(0), 0), _mod_spec(seg(0), 1),
            _mod_spec(seg(1), 0), _mod_spec(seg(1), 1),
            _ng_spec(l, 0),
            pl.BlockSpec((None, tn, D_MODEL), lambda i, n: (l, n, 0)),
            pl.BlockSpec((None, D_TAIL, D_MODEL), lambda i, n: (l, 0, 0)),
        ],
        out_specs=[
            pl.BlockSpec((tm, tn), lambda i, n: (i, n)),
            pl.BlockSpec((tm, D_TAIL), lambda i, n: (i, 0)),
        ],
        out_shape=[
            jax.ShapeDtypeStruct((T, D_MAIN), F32),
            jax.ShapeDtypeStruct((T, D_TAIL), F32),
        ],
        scratch_shapes=[pltpu.VMEM((tm, D_MODEL), F32), pltpu.VMEM((tm, D_MODEL), BF16),
                        pltpu.SemaphoreType.DMA(())],
        compiler_params=pltpu.CompilerParams(dimension_semantics=("arbitrary", "arbitrary"),
                                             vmem_limit_bytes=INPROJ_VMEM_LIMIT_BYTES),
        name="in_proj",
    )(x, mod3, mod3, mod3, mod3, ng3, w_in_t, w_tail_t)


def _rope(x, cos, sin):
    lane = lax.broadcasted_iota(jnp.int32, x.shape, 1)
    first = (lane % (HEAD_DIM // 2)) < (HEAD_DIM // 4)
    rot = jnp.where(first, -pltpu.roll(x, HEAD_DIM - HEAD_DIM // 4, 1), pltpu.roll(x, HEAD_DIM // 4, 1))
    return x * cos + rot * sin


def _attn_kernel(*refs, n, past, rope, emit_kv, qb):
    it = iter(refs)
    q_ref, k_ref, v_ref, qn_ref, kn_ref = (next(it) for _ in range(5))
    cos_ref = sin_ref = ck_ref = cv_ref = None
    if rope:
        cos_ref, sin_ref = next(it), next(it)
    if past:
        ck_ref, cv_ref = next(it), next(it)
    o_ref = next(it)
    if emit_kv:
        ko_ref, vo_ref = next(it), next(it)
    k_scr, v_scr = next(it), next(it)

    k = _rms(k_ref[...], kn_ref[...])
    v = v_ref[...]
    if emit_kv:
        ko_ref[...] = k
        vo_ref[...] = v
    if rope:
        k = _rope(k, cos_ref[...], sin_ref[...])
    k_scr[0:n, :] = k.astype(BF16)
    v_scr[0:n, :] = v.astype(BF16)
    if past:
        k_scr[n:n + past, :] = ck_ref[...].astype(BF16)
        v_scr[n:n + past, :] = cv_ref[...].astype(BF16)
    exp2_scale = HEAD_DIM ** -0.5 * math.log2(math.e)

    blocks_per_step = min(n // qb, 4)

    def block(b, carry):
        items = [(pl.ds(pl.multiple_of((b * blocks_per_step + j) * qb, qb), qb),
                  slice(g * HEAD_DIM, (g + 1) * HEAD_DIM))
                 for j in range(blocks_per_step) for g in range(GROUP)]

        def scores(item):
            rows, cols = item
            q = _rms(q_ref[rows, cols], qn_ref[...])
            if rope:
                q = _rope(q, cos_ref[rows, :], sin_ref[rows, :])
            return lax.dot_general(q.astype(BF16), k_scr[...], _NT, preferred_element_type=F32)

        def softmax(s):
            p = jnp.exp2((s - jnp.max(s, axis=-1, keepdims=True)) * exp2_scale)
            return p.astype(BF16), jnp.sum(p, axis=-1, keepdims=True)

        def values(item, pd):
            rows, cols = item
            o = jnp.dot(pd[0], v_scr[...], preferred_element_type=F32)
            o_ref[rows, cols] = (o / pd[1]).astype(BF16)

        if blocks_per_step == 1:
            ps = [softmax(s) for s in [scores(item) for item in items]]
            for item, pd in zip(items, ps):
                values(item, pd)
            return carry
        depth = 2
        ss, ps = {}, {}
        for i in range(len(items) + depth):
            if i < len(items):
                ss[i] = scores(items[i])
            if 1 <= i <= len(items):
                ps[i - 1] = softmax(ss.pop(i - 1))
            if i >= depth:
                values(items[i - depth], ps.pop(i - depth))
        return carry

    lax.fori_loop(0, n // (qb * blocks_per_step), block, 0)


def _attn_call(proj, qn, kn, l, n, rope_tabs=None, cache=None, emit_kv=False):
    T = proj.shape[0]
    B = T // n
    past = PAST_LEN if cache is not None else 0
    qw = GROUP * HEAD_DIM
    k_blk = D_ATTN // HEAD_DIM
    v_blk = (D_ATTN + D_KV) // HEAD_DIM
    in_specs = [
        pl.BlockSpec((n, qw), lambda b, h: (b, h)),
        pl.BlockSpec((n, HEAD_DIM), lambda b, h: (b, k_blk + h)),
        pl.BlockSpec((n, HEAD_DIM), lambda b, h: (b, v_blk + h)),
        pl.BlockSpec((1, HEAD_DIM), lambda b, h: (0, 0)),
        pl.BlockSpec((1, HEAD_DIM), lambda b, h: (0, 0)),
    ]
    args = [proj, proj, proj, qn, kn]
    if rope_tabs is not None:
        in_specs += [pl.BlockSpec((n, HEAD_DIM), lambda b, h: (0, 0))] * 2
        args += list(rope_tabs)
    if cache is not None:
        in_specs += [pl.BlockSpec((None, None, PAST_LEN, HEAD_DIM), lambda b, h: (b, l, 0, h))] * 2
        args += list(cache)
    out_specs = [pl.BlockSpec((n, qw), lambda b, h: (b, h))]
    out_shape = [jax.ShapeDtypeStruct((T, D_ATTN), BF16)]
    if emit_kv:
        out_specs += [pl.BlockSpec((None, n, HEAD_DIM), lambda b, h: (b, 0, h))] * 2
        out_shape += [jax.ShapeDtypeStruct((B, n, D_KV), F32)] * 2
    return pl.pallas_call(
        functools.partial(_attn_kernel, n=n, past=past, rope=rope_tabs is not None,
                          emit_kv=emit_kv, qb=256),
        grid=(B, N_KV_HEADS),
        in_specs=in_specs,
        out_specs=out_specs,
        out_shape=out_shape,
        scratch_shapes=[pltpu.VMEM((n + past, HEAD_DIM), BF16)] * 2,
        compiler_params=_params(2),
        name="attention",
    )(*args)


ROWS_A = 256
SUB = 16
assert CHUNK // SUB == 4


def _split2(x):
    hi = x.astype(BF16)
    return hi, (x - hi.astype(F32)).astype(BF16)


def _dot_x3(a, b):
    ah, al = _split2(a)
    bh, bl = _split2(b)
    mm = lambda p, q: jnp.dot(p, q, preferred_element_type=F32)
    return mm(ah, bh) + mm(ah, bl) + mm(al, bh)


def _split_sum(x, pieces, dot_piece):
    acc = None
    rest = x
    for _ in range(pieces):
        part = rest.astype(BF16)
        rest = rest - part.astype(F32)
        term = dot_piece(part)
        acc = term if acc is None else acc + term
    return acc


def _mix_kernel(*refs, n, nseq, has_s0, emit_state):
    it = iter(refs)
    cv_ref, um_ref, ut_ref = next(it), next(it), next(it)
    (mum_ref, mut_ref, cw_ref, w0_ref, wup_ref, a0_ref, aup_ref, gup_ref, kk_ref, ka_ref,
     rk_ref, lng_ref, lnb_ref, seg_ref) = (next(it) for _ in range(14))
    s0_ref = next(it) if has_s0 else None
    o_ref = next(it)
    so_ref = next(it) if emit_state else None
    al_s, be_s, kd_s, r_s, v_s, p_s, bon_s, g_s, y_s, s_scr = (next(it) for _ in range(10))
    blocks_per_seq = n // ROWS_A
    n_blocks = nseq * blocks_per_seq
    n_chunks = n // CHUNK
    per_block = ROWS_A // CHUNK

    def seg_sum(x):
        return _split_sum(x, 2, lambda part: jnp.dot(part, seg_ref[...], preferred_element_type=F32))

    PAIR = 2 * RWKV_HEAD

    def per_head_tiles(x, high):
        lane = lax.broadcasted_iota(jnp.int32, x.shape, 1)
        is_even = (lane // RWKV_HEAD) % 2 == 0
        parts = (jnp.where(is_even, x, 0.0), jnp.where(is_even, 0.0, x))
        tiles = []
        for h in range(RWKV_HEADS):
            t = parts[h % 2][:, (h // 2) * PAIR:(h // 2 + 1) * PAIR]
            if (h % 2 == 1) != high:
                t = pltpu.roll(t, RWKV_HEAD, 1)
            tiles.append(t)
        return tiles

    bi = lax.broadcasted_iota(jnp.int32, (ROWS_A, ROWS_A), 0)
    bj = lax.broadcasted_iota(jnp.int32, (ROWS_A, ROWS_A), 1)
    same_chunk = (bi // CHUNK) == (bj // CHUNK)
    cum_mat = [(same_chunk & (bj <= bi)).astype(BF16), (same_chunk & (bj >= bi)).astype(BF16)]

    def pass1(c, carry):
        c0 = pl.multiple_of(c * ROWS_A, ROWS_A)
        rows = pl.ds(c0, ROWS_A)
        prev_row = pl.ds(jnp.maximum(c0 - 1, 0), 1)
        next_row = pl.ds(jnp.minimum(c0 + ROWS_A, nseq * n - 1), 1)
        seq_first = (c % blocks_per_seq) == 0
        seq_last = (c % blocks_per_seq) == blocks_per_seq - 1

        def neighbours(load):
            cur = load(rows)
            rid = lax.broadcasted_iota(jnp.int32, cur.shape, 0)
            before = jnp.where(seq_first, 0.0, load(prev_row))
            after = jnp.where(seq_last, 0.0, load(next_row))
            prev = jnp.where(rid == 0, before, pltpu.roll(cur, 1, 0))
            nxt = jnp.where(rid == ROWS_A - 1, after, pltpu.roll(cur, ROWS_A - 1, 0))
            return cur, prev, nxt

        z, zp, zn = neighbours(lambda rs: cv_ref[rs, D_CONV:2 * D_CONV] * cv_ref[rs, 2 * D_CONV:3 * D_CONV])
        cw = cw_ref[...]
        conv = cv_ref[rows, 0:D_CONV] * (zp * cw[0:1, :] + z * cw[1:2, :] + zn * cw[2:3, :])
        o_ref[rows, 0:D_CONV] = conv.astype(BF16)

        u, up, un = neighbours(lambda rs: um_ref[rs, :])
        u = u + mum_ref[...] * (0.5 * (up + un) - u)
        t, tp, tn = neighbours(lambda rs: ut_ref[rs, :])
        ut = t + mut_ref[...] * (0.5 * (tp + tn) - t)
        r = u[:, 0:D_RWKV]
        k = u[:, D_RWKV:2 * D_RWKV]
        v = u[:, 2 * D_RWKV:3 * D_RWKV]
        kk = k * kk_ref[...]
        kk = kk * lax.rsqrt(seg_sum(kk * kk) + 1e-12)
        for h, v_h in enumerate(per_head_tiles(v, high=False)):
            v_s[h, rows, :] = v_h.astype(BF16)
        tw = jnp.tanh(ut)
        a_sum = jnp.zeros((ROWS_A, D_RWKV), F32)
        for d in range(2):
            zz = w0_ref[d:d + 1, :] + _dot_x3(tw, wup_ref[d])
            lw = -math.exp(-0.5) * jax.nn.sigmoid(zz)
            a = jax.nn.sigmoid(a0_ref[d:d + 1, :] + _dot_x3(ut, aup_ref[d]))
            a_sum = a_sum + a
            cum = _split_sum(lw, 3, lambda part: jnp.dot(cum_mat[d], part, preferred_element_type=F32))
            e_out = jnp.exp(-cum)
            kd = k * (1.0 + (a - 1.0) * ka_ref[...])
            scaled = ((al_s, -kk * jnp.exp(cum - lw)), (be_s, kk * a * e_out), (kd_s, kd * e_out),
                      (r_s, r * jnp.exp(cum)))
            for ref, val in scaled:
                for h, val_h in enumerate(per_head_tiles(val, high=True)):
                    ref[d, h, rows, :] = val_h.astype(BF16)
            half = lax.broadcasted_iota(jnp.int32, (1, PAIR), 1) >= RWKV_HEAD
            for j in range(per_block):
                last = j * CHUNK + (CHUNK - 1 if d == 0 else 0)
                for h, p_h in enumerate(per_head_tiles(jnp.exp(cum[last:last + 1, :]), high=True)):
                    p_s[d, c * per_block + j, h:h + 1, :] = jnp.where(half, p_h, 1.0)
        kd_sum = k * (2.0 + (a_sum - 2.0) * ka_ref[...])
        bon_s[rows, :] = seg_sum(r * kd_sum * rk_ref[...]) * v
        g_s[rows, :] = _dot_x3(jax.nn.sigmoid(ut), gup_ref[...])
        y_s[rows, :] = jnp.zeros((ROWS_A, D_RWKV), F32)
        return carry

    lax.fori_loop(0, n_blocks, pass1, 0)

    eye_low = (lax.broadcasted_iota(jnp.int32, (RWKV_HEAD, PAIR), 0)
               == lax.broadcasted_iota(jnp.int32, (RWKV_HEAD, PAIR), 1)).astype(F32)
    for s in range(nseq):
        for d in range(2):
            for h in range(RWKV_HEADS):
                if has_s0:
                    s_scr[s, d, h] = jnp.concatenate([eye_low[:, 0:RWKV_HEAD], s0_ref[d, h]], axis=1)
                else:
                    s_scr[s, d, h] = eye_low

    ri = lax.broadcasted_iota(jnp.int32, (CHUNK, 2 * CHUNK), 0)
    ci = lax.broadcasted_iota(jnp.int32, (CHUNK, 2 * CHUNK), 1) % CHUNK
    ei = lax.broadcasted_iota(jnp.int32, (CHUNK, CHUNK), 0)
    ej = lax.broadcasted_iota(jnp.int32, (CHUNK, CHUNK), 1)
    eye = (ei == ej).astype(F32)
    diag_blk = (ei // SUB) == (ej // SUB)
    zeros_b = jnp.zeros((CHUNK, PAIR), BF16)
    n_rounds = int(math.log2(SUB)) - 1

    def chunk_step(c, carry):
        chains = []
        for s, d in ((s, d) for s in range(nseq) for d in range(2)):
            cc = s * n_chunks + (c if d == 0 else n_chunks - 1 - c)
            rows = pl.ds(pl.multiple_of(cc * CHUNK, CHUNK), CHUNK)
            p_tile = p_s[d, cc]
            for h in range(RWKV_HEADS):
                chains.append(dict(
                    s=s, d=d, h=h, rows=rows, p=p_tile[h:h + 1, :],
                    strict=(ci < ri) if d == 0 else (ci > ri), incl=(ci <= ri) if d == 0 else (ci >= ri),
                    al=al_s[d, h, rows, :], be=be_s[d, h, rows, :], kd=kd_s[d, h, rows, :],
                    r=r_s[d, h, rows, :], v=v_s[h, rows, :]))

        def each(fn, *lists):
            return [fn(*args) for args in zip(chains, *lists)]

        bk = each(lambda ch: jnp.concatenate([ch['be'], ch['kd']], axis=0))
        g = each(lambda ch, m: _mm(jnp.concatenate([ch['al'], ch['r']], axis=0), m, _NT), bk)
        top = each(lambda ch, m: jnp.where(ch['strict'], m[0:CHUNK], 0.0), g)
        bot = each(lambda ch, m: jnp.where(ch['incl'], m[CHUNK:2 * CHUNK], 0.0).astype(BF16), g)
        rhs = each(lambda ch, m: _mm(m, jnp.concatenate([zeros_b, ch['v']], axis=0))
                   + ch['al'].astype(F32), top)
        a = [m[:, 0:CHUNK] for m in top]
        xs = [jnp.where(diag_blk, m, 0.0) for m in a]
        low = [jnp.where(diag_blk, 0.0, m) for m in a]
        ts = [eye + x for x in xs]
        xs = [_mm(x, x) for x in xs]
        for _ in range(n_rounds - 1):
            z = [_mm(jnp.concatenate([x, t], axis=0), x) for x, t in zip(xs, ts)]
            xs = [m[0:CHUNK] for m in z]
            ts = [t + m[CHUNK:2 * CHUNK] for t, m in zip(ts, z)]
        ts = [t + _mm(t, x) for t, x in zip(ts, xs)]
        ms = [_mm(t, m) for t, m in zip(ts, low)]
        sol = [_mm(t, m) for t, m in zip(ts, rhs)]
        m2 = [_mm(m, m) for m in ms]
        ims = [eye + m for m in ms]
        ns = [im + _mm(im, mm2) for im, mm2 in zip(ims, m2)]
        w = [_mm(nn, x) for nn, x in zip(ns, sol)]
        st = each(lambda ch: s_scr[ch['s'], ch['d'], ch['h']])
        stb = [t.astype(BF16) for t in st]
        u = [_mm(ww, t, _NT) for ww, t in zip(w, stb)]
        uv = each(lambda ch, uu: jnp.concatenate([uu.astype(BF16), ch['v'][:, 0:RWKV_HEAD]], axis=0), u)
        ys = each(lambda ch, t, qq, m: _mm(ch['r'], t, _NT) + _mm(qq, m), stb, bot, uv)
        s_new = each(lambda ch, t, m, kb: (t + _mm(m, kb, _TN)) * ch['p'], st, uv, bk)
        for ch, t in zip(chains, s_new):
            s_scr[ch['s'], ch['d'], ch['h']] = t
        for ch, y in zip(chains, ys):
            y_s[ch['rows'], ch['h'] * RWKV_HEAD:(ch['h'] + 1) * RWKV_HEAD] += y
        return carry

    lax.fori_loop(0, n_chunks, chunk_step, 0)

    if emit_state:
        for s in range(nseq):
            for d in range(2):
                for h in range(RWKV_HEADS):
                    so_ref[s, d, h] = s_scr[s, d, h][:, RWKV_HEAD:PAIR]

    def pass3(c, carry):
        rows = pl.ds(pl.multiple_of(c * ROWS_A, ROWS_A), ROWS_A)
        y = y_s[rows, :]
        yc = y - seg_sum(y) * (1.0 / RWKV_HEAD)
        var = seg_sum(yc * yc) * (1.0 / RWKV_HEAD)
        yn = yc * lax.rsqrt(var + GN_EPS) * lng_ref[...] + lnb_ref[...]
        o_ref[rows, D_CONV:D_CONV + D_RWKV] = ((yn + bon_s[rows, :]) * g_s[rows, :]).astype(BF16)
        return carry

    lax.fori_loop(0, n_blocks, pass3, 0)


def _pad_rows(w, lo):
    return jnp.pad(w, [(0, 0)] * (w.ndim - 2) + [(lo, D_TAIL - lo - w.shape[-2]), (0, 0)])


def _mix_call(proj, tail, P, l, n, nseq, s0=None, emit_state=False):
    T = proj.shape[0]
    B = T // n
    rows = nseq * n
    assert s0 is None or nseq == 1
    full = lambda shape: pl.BlockSpec(shape, lambda b: (0,) * len(shape))
    lanes = lambda w: -(-w // 128) * 128
    in_bytes = rows * (3 * D_CONV + 3 * D_RWKV + lanes(D_TAIL)) * 4
    scratch_bytes = rows * (4 * 2 * RWKV_HEADS * lanes(RWKV_HEAD) * 2 + RWKV_HEADS * lanes(RWKV_HEAD) * 2
                            + 3 * D_RWKV * 4)
    out_bytes = 2 * rows * (D_CONV + D_RWKV) * 2
    temporaries = 6 * 1024 * 1024
    fits = 2 * in_bytes + scratch_bytes + out_bytes + temporaries <= VMEM_LIMIT_BYTES
    mode = {} if fits else {'pipeline_mode': pl.Buffered(1)}
    in_specs = [
        pl.BlockSpec((rows, 3 * D_CONV), lambda b: (b, 1), **mode),
        pl.BlockSpec((rows, 3 * D_RWKV), lambda b: (b, 2), **mode),
        pl.BlockSpec((rows, D_TAIL), lambda b: (b, 0), **mode),
        full((1, 3 * D_RWKV)), full((1, D_TAIL)), full((3, D_CONV)),
        full((2, D_RWKV)), full((2, D_TAIL, D_RWKV)), full((2, D_RWKV)), full((2, D_TAIL, D_RWKV)),
        full((D_TAIL, D_RWKV)), full((1, D_RWKV)), full((1, D_RWKV)), full((1, D_RWKV)),
        full((1, D_RWKV)), full((1, D_RWKV)), full((D_RWKV, D_RWKV)),
    ]
    head = jnp.arange(D_RWKV) // RWKV_HEAD
    seg = (head[:, None] == head[None, :]).astype(BF16)
    mu = P['rwkv_mu'][l]
    row = lambda a: a.reshape(1, -1)
    args = [proj, proj, tail, row(mu[:3 * D_RWKV]), row(mu[3 * D_RWKV:]), P['conv_w'][l],
            P['rwkv_w0'][l], _pad_rows(P['rwkv_w_up'][l], 0), P['rwkv_a0'][l],
            _pad_rows(P['rwkv_a_up'][l], DECAY_RANK), _pad_rows(P['rwkv_g_up'][l], DECAY_RANK + A_RANK),
            row(P['rwkv_k_k'][l]), row(P['rwkv_k_a'][l]), row(P['rwkv_r_k'][l]),
            row(P['rwkv_ln_g'][l]), row(P['rwkv_ln_b'][l]), seg]
    if s0 is not None:
        in_specs.append(pl.BlockSpec((None, None, 2, RWKV_HEADS, RWKV_HEAD, RWKV_HEAD),
                                     lambda b: (b, l, 0, 0, 0, 0)))
        args.append(s0)
    out_specs = [pl.BlockSpec((rows, D_CONV + D_RWKV), lambda b: (b, 0))]
    out_shape = [jax.ShapeDtypeStruct((T, D_CONV + D_RWKV), BF16)]
    state_shape = (2, RWKV_HEADS, RWKV_HEAD, RWKV_HEAD)
    if emit_state:
        out_specs.append(pl.BlockSpec((nseq,) + state_shape, lambda b: (b, 0, 0, 0, 0)))
        out_shape.append(jax.ShapeDtypeStruct((B,) + state_shape, F32))
    pair = 2 * RWKV_HEAD
    per_head = lambda: pltpu.VMEM((2, RWKV_HEADS, rows, pair), BF16)
    return pl.pallas_call(
        functools.partial(_mix_kernel, n=n, nseq=nseq, has_s0=s0 is not None, emit_state=emit_state),
        grid=(B // nseq,),
        in_specs=in_specs,
        out_specs=out_specs,
        out_shape=out_shape,
        scratch_shapes=[per_head(), per_head(), per_head(), per_head(),
                        pltpu.VMEM((RWKV_HEADS, rows, pair), BF16),
                        pltpu.VMEM((2, rows // CHUNK, RWKV_HEADS, pair), F32),
                        pltpu.VMEM((rows, D_RWKV), F32), pltpu.VMEM((rows, D_RWKV), F32),
                        pltpu.VMEM((rows, D_RWKV), F32),
                        pltpu.VMEM((nseq, 2, RWKV_HEADS, RWKV_HEAD, pair), F32)],
        compiler_params=_params(1),
        name="conv_rwkv",
    )(*args)


def _residual_copy(x_hbm, x_buf, sem):
    rows = x_buf.shape[0]
    start = pl.multiple_of(pl.program_id(0) * rows, rows)
    return pltpu.make_async_copy(x_hbm.at[pl.ds(start, rows), :], x_buf, sem)


def _outproj_kernel(at_ref, cr_ref, w_ref, x_hbm, ga_ref, shf_ref, scf_ref, g1_ref, g2_ref,
                    x1_ref, h2_ref, x_ref, x_sem):
    kt = pl.program_id(1)

    @pl.when(kt == 0)
    def _():
        _residual_copy(x_hbm, x_ref, x_sem).start()
        x1_ref[...] = jnp.zeros(x1_ref.shape, F32)

    lhs = jnp.where(kt < 2, at_ref[...], cr_ref[...])
    x1_ref[...] += jnp.dot(lhs, w_ref[...].astype(BF16), preferred_element_type=F32)

    @pl.when(kt == pl.num_programs(1) - 1)
    def _():
        _residual_copy(x_hbm, x_ref, x_sem).wait()
        x1 = x_ref[...] + ga_ref[...] * _rms(x1_ref[...], g1_ref[...])
        x1_ref[...] = x1
        h2_ref[...] = (_rms(x1, g2_ref[...]) * (1.0 + scf_ref[...]) + shf_ref[...]).astype(BF16)


def _outproj_call(attn, cr, x, mod3, ng3, w_out, l, row_fn, tm):
    T = x.shape[0]
    tk = 512
    return pl.pallas_call(
        _outproj_kernel,
        grid=(T // tm, D_MODEL // tk),
        in_specs=[
            pl.BlockSpec((tm, tk), lambda i, k: (i, jnp.minimum(k, 1))),
            pl.BlockSpec((tm, tk), lambda i, k: (i, jnp.maximum(k - 2, 0))),
            pl.BlockSpec((None, tk, D_MODEL), lambda i, k: (l, k, 0)),
            pl.BlockSpec(memory_space=pl.ANY),
            _mod_spec(row_fn, 2), _mod_spec(row_fn, 3), _mod_spec(row_fn, 4),
            _ng_spec(l, 1), _ng_spec(l, 2),
        ],
        out_specs=[
            pl.BlockSpec((tm, D_MODEL), lambda i, k: (i, 0)),
            pl.BlockSpec((tm, D_MODEL), lambda i, k: (i, 0)),
        ],
        out_shape=[jax.ShapeDtypeStruct((T, D_MODEL), F32), jax.ShapeDtypeStruct((T, D_MODEL), BF16)],
        scratch_shapes=[pltpu.VMEM((tm, D_MODEL), F32), pltpu.SemaphoreType.DMA(())],
        compiler_params=_params(2),
        name="out_proj",
    )(attn, cr, w_out, x, mod3, mod3, mod3, ng3, ng3)


def _ffn_kernel(h_ref, x1_hbm, wg_ref, wu_ref, wd_ref, gf_ref, g3_ref, o_ref, x1_ref, x1_sem):
    f = pl.program_id(1)

    @pl.when(f == 0)
    def _():
        _residual_copy(x1_hbm, x1_ref, x1_sem).start()
        o_ref[...] = jnp.zeros(o_ref.shape, F32)

    h = h_ref[...]
    gate = _dot(h, wg_ref[...])
    up = _dot(h, wu_ref[...])
    o_ref[...] += _dot(gate * jax.nn.sigmoid(gate) * up, wd_ref[...])

    @pl.when(f == pl.num_programs(1) - 1)
    def _():
        _residual_copy(x1_hbm, x1_ref, x1_sem).wait()
        o_ref[...] = x1_ref[...] + gf_ref[...] * _rms(o_ref[...], g3_ref[...])


def _ffn_call(h2, x1, mod3, ng3, w_gate, w_up, w_down, l, row_fn, tm):
    T = x1.shape[0]
    tf = 256
    return pl.pallas_call(
        _ffn_kernel,
        grid=(T // tm, D_FF // tf),
        in_specs=[
            pl.BlockSpec((tm, D_MODEL), lambda i, f: (i, 0)),
            pl.BlockSpec(memory_space=pl.ANY),
            pl.BlockSpec((None, D_MODEL, tf), lambda i, f: (l, 0, f)),
            pl.BlockSpec((None, D_MODEL, tf), lambda i, f: (l, 0, f)),
            pl.BlockSpec((None, tf, D_MODEL), lambda i, f: (l, f, 0)),
            _mod_spec(row_fn, 5),
            _ng_spec(l, 3),
        ],
        out_specs=pl.BlockSpec((tm, D_MODEL), lambda i, f: (i, 0)),
        out_shape=jax.ShapeDtypeStruct((T, D_MODEL), F32),
        scratch_shapes=[pltpu.VMEM((tm, D_MODEL), F32), pltpu.SemaphoreType.DMA(())],
        compiler_params=_params(2),
        name="ffn",
    )(h2, x1, w_gate, w_up, w_down, mod3, ng3)


def _rope_tables(n):
    rows = n // GRID_W
    row = jnp.repeat(jnp.arange(rows), GRID_W).astype(F32)
    col = jnp.tile(jnp.arange(GRID_W), rows).astype(F32)
    half = HEAD_DIM // 2
    inv = 1.0 / (ROPE_THETA ** (jnp.arange(0, half, 2, dtype=F32) / half))
    ar = row[:, None] * inv
    ac = col[:, None] * inv
    ang = jnp.concatenate([ar, ar, ac, ac], axis=-1)
    return jnp.cos(ang), jnp.sin(ang)


def _layer(x, mod3, ng3, P, l, n, row_fn_of_tm, rope_tabs=None, cache=None, s0=None, emit=False):
    proj, tail = _inproj_call(x, mod3, ng3, P['w_in_t'], P['w_tail_t'], l, row_fn_of_tm(1024), 2048)
    attn_out = _attn_call(proj, P['q_norm'][l].reshape(1, -1), P['k_norm'][l].reshape(1, -1), l, n,
                          rope_tabs=rope_tabs, cache=cache, emit_kv=emit)
    mix_out = _mix_call(proj, tail, P, l, n, 1 if s0 is not None else 2, s0=s0, emit_state=emit)
    attn = attn_out[0]
    cr = mix_out[0]
    x1, h2 = _outproj_call(attn, cr, x, mod3, ng3, P['w_out'], l, row_fn_of_tm(1024), 1024)
    x2 = _ffn_call(h2, x1, mod3, ng3, P['w_gate'], P['w_up'], P['w_down'], l, row_fn_of_tm(1024), 1024)
    if emit:
        return x2, attn_out[1], attn_out[2], mix_out[1]
    return x2


def kernel(x_prompt, x_sample, cache_k, cache_v, state_rwkv, c, c_ctx, w_ada, b_ada, norm_g, w_in, q_norm, k_norm, conv_w, rwkv_mu, rwkv_w0, rwkv_w_up, rwkv_a0, rwkv_a_up, rwkv_g_up, rwkv_k_k, rwkv_k_a, rwkv_r_k, rwkv_ln_g, rwkv_ln_b, w_out, w_gate, w_up, w_down):
    P = {'q_norm': q_norm, 'k_norm': k_norm, 'conv_w': conv_w, 'rwkv_mu': rwkv_mu,
         'rwkv_w0': rwkv_w0, 'rwkv_w_up': rwkv_w_up, 'rwkv_a0': rwkv_a0, 'rwkv_a_up': rwkv_a_up,
         'rwkv_g_up': rwkv_g_up, 'rwkv_k_k': rwkv_k_k, 'rwkv_k_a': rwkv_k_a, 'rwkv_r_k': rwkv_r_k,
         'rwkv_ln_g': rwkv_ln_g, 'rwkv_ln_b': rwkv_ln_b, 'w_out': w_out,
         'w_gate': w_gate, 'w_up': w_up, 'w_down': w_down}
    w_in_t = jnp.swapaxes(w_in, 1, 2)
    P['w_in_t'] = w_in_t
    P['w_tail_t'] = w_in_t[:, D_MAIN:, :]
    batch, seq, _ = x_prompt.shape
    dec_batch, dec_seq, _ = x_sample.shape

    cc = jnp.zeros((8, D_MODEL), F32).at[0].set(c_ctx).at[1:1 + dec_batch].set(c)
    mod3 = _ada_call(cc, w_ada, b_ada).reshape(DEPTH * 8, 1, N_MOD * D_MODEL)
    ng3 = norm_g.reshape(DEPTH * 4, 1, D_MODEL)

    ck = cache_k.reshape(dec_batch, DEPTH, PAST_LEN, D_KV)
    cv = cache_v.reshape(dec_batch, DEPTH, PAST_LEN, D_KV)
    rope_tabs = _rope_tables(dec_seq)

    xp = x_prompt.reshape(batch * seq, D_MODEL)
    xs = x_sample.reshape(dec_batch * dec_seq, D_MODEL)
    ks, vs, ss = [], [], []
    for l in range(DEPTH):
        ctx_row = lambda tm, l=l: (lambda i: l * 8)
        smp_row = lambda tm, l=l: (lambda i: l * 8 + 1 + (i * tm) // dec_seq)
        xp, k_l, v_l, s_l = _layer(xp, mod3, ng3, P, l, seq, ctx_row, emit=True)
        ks.append(k_l.reshape(batch, seq, N_KV_HEADS, HEAD_DIM))
        vs.append(v_l.reshape(batch, seq, N_KV_HEADS, HEAD_DIM))
        ss.append(s_l)
        xs = _layer(xs, mod3, ng3, P, l, dec_seq, smp_row, rope_tabs=rope_tabs, cache=(ck, cv),
                    s0=state_rwkv)
    return (xp.reshape(batch, seq, D_MODEL), xs.reshape(dec_batch, dec_seq, D_MODEL),
            jnp.stack(ks, axis=1), jnp.stack(vs, axis=1), jnp.stack(ss, axis=1))
```

```python
import functools
import math

import jax
import jax.numpy as jnp
from jax import lax
from jax.experimental import pallas as pl
from jax.experimental.pallas import tpu as pltpu

D_MODEL = 2048
DEPTH = 2
GRID_W = 64
HEAD_DIM = 128
N_HEADS = 8
N_KV_HEADS = 2
GROUP = N_HEADS // N_KV_HEADS
D_ATTN = N_HEADS * HEAD_DIM
D_KV = N_KV_HEADS * HEAD_DIM
ROPE_THETA = 10000.0
D_CONV = 512
D_RWKV = 512
RWKV_HEAD = 64
RWKV_HEADS = D_RWKV // RWKV_HEAD
DECAY_RANK = 32
A_RANK = 32
GATE_RANK = 96
D_TAIL = DECAY_RANK + A_RANK + GATE_RANK
D_IN = D_ATTN + 2 * D_KV + 3 * D_CONV + 3 * D_RWKV + D_TAIL
D_MAIN = D_IN - D_TAIL
D_FF = 5632
N_MOD = 6
EPS = 1e-6
GN_EPS = 64e-5
PAST_LEN = 256

VMEM_LIMIT_BYTES = 56 * 1024 * 1024
CHUNK = 64
BF16 = jnp.bfloat16
F32 = jnp.float32


def _params(n_grid, **kw):
    sem = ("parallel",) + ("arbitrary",) * (n_grid - 1)
    return pltpu.CompilerParams(dimension_semantics=sem, vmem_limit_bytes=VMEM_LIMIT_BYTES, **kw)


def _rms(x, g):
    return x * lax.rsqrt(jnp.mean(x * x, axis=-1, keepdims=True) + EPS) * g


def _dot(a, b):
    return jnp.dot(a.astype(BF16), b.astype(BF16), preferred_element_type=F32)


_NN = (((1,), (0,)), ((), ()))
_NT = (((1,), (1,)), ((), ()))
_TN = (((0,), (0,)), ((), ()))


def _mm(a, b, dims=_NN):
    return lax.dot_general(a.astype(BF16), b.astype(BF16), dims, preferred_element_type=F32)


def _ada_kernel(c_ref, w_ref, b_ref, o_ref):
    c = c_ref[...]
    s = c * jax.nn.sigmoid(c)
    o_ref[...] = _dot(s, w_ref[...]) + b_ref[...]


def _ada_call(cc, w_ada, b_ada):
    tn = 512
    n_out = N_MOD * D_MODEL
    return pl.pallas_call(
        _ada_kernel,
        grid=(DEPTH, n_out // tn),
        in_specs=[
            pl.BlockSpec((8, D_MODEL), lambda l, n: (0, 0)),
            pl.BlockSpec((None, D_MODEL, tn), lambda l, n: (l, 0, n)),
            pl.BlockSpec((None, 1, tn), lambda l, n: (l, 0, n)),
        ],
        out_specs=pl.BlockSpec((None, 8, tn), lambda l, n: (l, 0, n)),
        out_shape=jax.ShapeDtypeStruct((DEPTH, 8, n_out), F32),
        compiler_params=_params(2),
        name="adaln",
    )(cc, w_ada, b_ada.reshape(DEPTH, 1, n_out))


def _mod_spec(row_fn, chunk):
    return pl.BlockSpec((None, 1, D_MODEL), lambda i, *_: (row_fn(i), 0, chunk))


def _ng_spec(l, j):
    return pl.BlockSpec((None, 1, D_MODEL), lambda i, *_: (l * 4 + j, 0, 0))


INPROJ_NORM_ROWS = 256
INPROJ_VMEM_LIMIT_BYTES = 60 * 1024 * 1024


def _row_tile_copy(x_hbm, x_buf, sem, tile):
    rows = x_buf.shape[0]
    return pltpu.make_async_copy(x_hbm.at[pl.ds(pl.multiple_of(tile * rows, rows), rows), :], x_buf, sem)


def _inproj_kernel(x_hbm, sha_ref, sca_ref, shb_ref, scb_ref, g_ref, w_ref, wt_ref, o_ref, ot_ref,
                   x_buf, h_scr, x_sem):
    i = pl.program_id(0)

    @pl.when(pl.program_id(1) == 0)
    def _():
        @pl.when(i == 0)
        def _():
            _row_tile_copy(x_hbm, x_buf, x_sem, 0).start()

        _row_tile_copy(x_hbm, x_buf, x_sem, i).wait()
        half = x_buf.shape[0] // 2
        for r0 in range(0, x_buf.shape[0], INPROJ_NORM_ROWS):
            rs = slice(r0, r0 + INPROJ_NORM_ROWS)
            sh_ref, sc_ref = (sha_ref, sca_ref) if r0 < half else (shb_ref, scb_ref)
            h = _rms(x_buf[rs, :], g_ref[...]) * (1.0 + sc_ref[...]) + sh_ref[...]
            h_scr[rs, :] = h.astype(BF16)

        @pl.when(i + 1 < pl.num_programs(0))
        def _():
            _row_tile_copy(x_hbm, x_buf, x_sem, i + 1).start()

        ot_ref[...] = _mm(h_scr[...], wt_ref[...], _NT)

    o_ref[...] = _mm(h_scr[...], w_ref[...], _NT)


def _inproj_call(x, mod3, ng3, w_in_t, w_tail_t, l, row_fn_half, tm):
    T = x.shape[0]
    tn = 512
    seg = lambda j: (lambda i: row_fn_half(2 * i + j))
    return pl.pallas_call(
        _inproj_kernel,
        grid=(T // tm, D_MAIN // tn),
        in_specs=[
            pl.BlockSpec(memory_space=pl.ANY),
            _mod_spec(seg(0), 0), _mod_spec(seg(0), 1),
            _mod_spec(seg(1), 0), _mod_spec(seg(1), 1),
            _ng_spec(l, 0),
            pl.BlockSpec((None, tn, D_MODEL), lambda i, n: (l, n, 0)),
            pl.BlockSpec((None, D_TAIL, D_MODEL), lambda i, n: (l, 0, 0)),
        ],
        out_specs=[
            pl.BlockSpec((tm, tn), lambda i, n: (i, n)),
            pl.BlockSpec((tm, D_TAIL), lambda i, n: (i, 0)),
        ],
        out_shape=[
            jax.ShapeDtypeStruct((T, D_MAIN), F32),
            jax.ShapeDtypeStruct((T, D_TAIL), F32),
        ],
        scratch_shapes=[pltpu.VMEM((tm, D_MODEL), F32), pltpu.VMEM((tm, D_MODEL), BF16),
                        pltpu.SemaphoreType.DMA(())],
        compiler_params=pltpu.CompilerParams(dimension_semantics=("arbitrary", "arbitrary"),
                                             vmem_limit_bytes=INPROJ_VMEM_LIMIT_BYTES),
        name="in_proj",
    )(x, mod3, mod3, mod3, mod3, ng3, w_in_t, w_tail_t)


def _rope(x, cos, sin):
    lane = lax.broadcasted_iota(jnp.int32, x.shape, 1)
    first = (lane % (HEAD_DIM // 2)) < (HEAD_DIM // 4)
    rot = jnp.where(first, -pltpu.roll(x, HEAD_DIM - HEAD_DIM // 4, 1), pltpu.roll(x, HEAD_DIM // 4, 1))
    return x * cos + rot * sin


def _attn_kernel(*refs, n, past, rope, emit_kv, qb):
    it = iter(refs)
    q_ref, k_ref, v_ref, qn_ref, kn_ref = (next(it) for _ in range(5))
    cos_ref = sin_ref = ck_ref = cv_ref = None
    if rope:
        cos_ref, sin_ref = next(it), next(it)
    if past:
        ck_ref, cv_ref = next(it), next(it)
    o_ref = next(it)
    if emit_kv:
        ko_ref, vo_ref = next(it), next(it)
    k_scr, v_scr = next(it), next(it)

    k = _rms(k_ref[...], kn_ref[...])
    v = v_ref[...]
    if emit_kv:
        ko_ref[...] = k
        vo_ref[...] = v
    if rope:
        k = _rope(k, cos_ref[...], sin_ref[...])
    k_scr[0:n, :] = k.astype(BF16)
    v_scr[0:n, :] = v.astype(BF16)
    if past:
        k_scr[n:n + past, :] = ck_ref[...].astype(BF16)
        v_scr[n:n + past, :] = cv_ref[...].astype(BF16)
    exp2_scale = HEAD_DIM ** -0.5 * math.log2(math.e)

    blocks_per_step = min(n // qb, 4)

    def block(b, carry):
        items = [(pl.ds(pl.multiple_of((b * blocks_per_step + j) * qb, qb), qb),
                  slice(g * HEAD_DIM, (g + 1) * HEAD_DIM))
                 for j in range(blocks_per_step) for g in range(GROUP)]

        def scores(item):
            rows, cols = item
            q = _rms(q_ref[rows, cols], qn_ref[...])
            if rope:
                q = _rope(q, cos_ref[rows, :], sin_ref[rows, :])
            return lax.dot_general(q.astype(BF16), k_scr[...], _NT, preferred_element_type=F32)

        def softmax(s):
            p = jnp.exp2((s - jnp.max(s, axis=-1, keepdims=True)) * exp2_scale)
            return p.astype(BF16), jnp.sum(p, axis=-1, keepdims=True)

        def values(item, pd):
            rows, cols = item
            o = jnp.dot(pd[0], v_scr[...], preferred_element_type=F32)
            o_ref[rows, cols] = (o / pd[1]).astype(BF16)

        if blocks_per_step == 1:
            ps = [softmax(s) for s in [scores(item) for item in items]]
            for item, pd in zip(items, ps):
                values(item, pd)
            return carry
        depth = 2
        ss, ps = {}, {}
        for i in range(len(items) + depth):
            if i < len(items):
                ss[i] = scores(items[i])
            if 1 <= i <= len(items):
                ps[i - 1] = softmax(ss.pop(i - 1))
            if i >= depth:
                values(items[i - depth], ps.pop(i - depth))
        return carry

    lax.fori_loop(0, n // (qb * blocks_per_step), block, 0)


def _attn_call(proj, qn, kn, l, n, rope_tabs=None, cache=None, emit_kv=False):
    T = proj.shape[0]
    B = T // n
    past = PAST_LEN if cache is not None else 0
    qw = GROUP * HEAD_DIM
    k_blk = D_ATTN // HEAD_DIM
    v_blk = (D_ATTN + D_KV) // HEAD_DIM
    in_specs = [
        pl.BlockSpec((n, qw), lambda b, h: (b, h)),
        pl.BlockSpec((n, HEAD_DIM), lambda b, h: (b, k_blk + h)),
        pl.BlockSpec((n, HEAD_DIM), lambda b, h: (b, v_blk + h)),
        pl.BlockSpec((1, HEAD_DIM), lambda b, h: (0, 0)),
        pl.BlockSpec((1, HEAD_DIM), lambda b, h: (0, 0)),
    ]
    args = [proj, proj, proj, qn, kn]
    if rope_tabs is not None:
        in_specs += [pl.BlockSpec((n, HEAD_DIM), lambda b, h: (0, 0))] * 2
        args += list(rope_tabs)
    if cache is not None:
        in_specs += [pl.BlockSpec((None, None, PAST_LEN, HEAD_DIM), lambda b, h: (b, l, 0, h))] * 2
        args += list(cache)
    out_specs = [pl.BlockSpec((n, qw), lambda b, h: (b, h))]
    out_shape = [jax.ShapeDtypeStruct((T, D_ATTN), BF16)]
    if emit_kv:
        out_specs += [pl.BlockSpec((None, n, HEAD_DIM), lambda b, h: (b, 0, h))] * 2
        out_shape += [jax.ShapeDtypeStruct((B, n, D_KV), F32)] * 2
    return pl.pallas_call(
        functools.partial(_attn_kernel, n=n, past=past, rope=rope_tabs is not None,
                          emit_kv=emit_kv, qb=256),
        grid=(B, N_KV_HEADS),
        in_specs=in_specs,
        out_specs=out_specs,
        out_shape=out_shape,
        scratch_shapes=[pltpu.VMEM((n + past, HEAD_DIM), BF16)] * 2,
        compiler_params=_params(2),
        name="attention",
    )(*args)


CONV_COL = D_ATTN + 2 * D_KV
RWKV_COL = CONV_COL + 3 * D_CONV
ROWS_A = 256
SUB = 16
assert CHUNK // SUB == 4


def _split2(x):
    hi = x.astype(BF16)
    return hi, (x - hi.astype(F32)).astype(BF16)


def _dot_x3(a, b):
    ah, al = _split2(a)
    bh, bl = _split2(b)
    mm = lambda p, q: jnp.dot(p, q, preferred_element_type=F32)
    return mm(ah, bh) + mm(ah, bl) + mm(al, bh)


def _split_sum(x, pieces, dot_piece):
    acc = None
    rest = x
    for _ in range(pieces):
        part = rest.astype(BF16)
        rest = rest - part.astype(F32)
        term = dot_piece(part)
        acc = term if acc is None else acc + term
    return acc


def _mix_kernel(*refs, n, nseq, has_s0, emit_state, manual_fetch):
    it = iter(refs)
    if manual_fetch:
        proj_hbm, tail_hbm = next(it), next(it)
    else:
        cv_ref, um_ref, ut_ref = next(it), next(it), next(it)
    (mum_ref, mut_ref, cw_ref, w0_ref, wup_ref, a0_ref, aup_ref, gup_ref, kk_ref, ka_ref,
     rk_ref, lng_ref, lnb_ref, seg_ref) = (next(it) for _ in range(14))
    s0_ref = next(it) if has_s0 else None
    o_ref = next(it)
    so_ref = next(it) if emit_state else None
    al_s, be_s, kd_s, r_s, v_s, p_s, bon_s, g_s, y_s, s_scr = (next(it) for _ in range(10))
    if manual_fetch:
        cv_ref, um_ref, ut_ref, in_sem = next(it), next(it), next(it), next(it)

        def input_copies(step):
            rows = pl.ds(pl.multiple_of(step * (nseq * n), nseq * n), nseq * n)
            return (
                pltpu.make_async_copy(proj_hbm.at[rows, pl.ds(CONV_COL, 3 * D_CONV)], cv_ref, in_sem.at[0]),
                pltpu.make_async_copy(proj_hbm.at[rows, pl.ds(RWKV_COL, 3 * D_RWKV)], um_ref, in_sem.at[1]),
                pltpu.make_async_copy(tail_hbm.at[rows, :], ut_ref, in_sem.at[2]))

        step = pl.program_id(0)

        @pl.when(step == 0)
        def _():
            for cp in input_copies(0):
                cp.start()

        for cp in input_copies(step):
            cp.wait()
    blocks_per_seq = n // ROWS_A
    n_blocks = nseq * blocks_per_seq
    n_chunks = n // CHUNK
    per_block = ROWS_A // CHUNK

    def seg_sum(x):
        return _split_sum(x, 2, lambda part: jnp.dot(part, seg_ref[...], preferred_element_type=F32))

    PAIR = 2 * RWKV_HEAD

    def per_head_tiles(x, high):
        lane = lax.broadcasted_iota(jnp.int32, x.shape, 1)
        is_even = (lane // RWKV_HEAD) % 2 == 0
        parts = (jnp.where(is_even, x, 0.0), jnp.where(is_even, 0.0, x))
        tiles = []
        for h in range(RWKV_HEADS):
            t = parts[h % 2][:, (h // 2) * PAIR:(h // 2 + 1) * PAIR]
            if (h % 2 == 1) != high:
                t = pltpu.roll(t, RWKV_HEAD, 1)
            tiles.append(t)
        return tiles

    bi = lax.broadcasted_iota(jnp.int32, (ROWS_A, ROWS_A), 0)
    bj = lax.broadcasted_iota(jnp.int32, (ROWS_A, ROWS_A), 1)
    same_chunk = (bi // CHUNK) == (bj // CHUNK)
    cum_mat = [(same_chunk & (bj <= bi)).astype(BF16), (same_chunk & (bj >= bi)).astype(BF16)]

    def pass1(c, carry):
        c0 = pl.multiple_of(c * ROWS_A, ROWS_A)
        rows = pl.ds(c0, ROWS_A)
        prev_row = pl.ds(jnp.maximum(c0 - 1, 0), 1)
        next_row = pl.ds(jnp.minimum(c0 + ROWS_A, nseq * n - 1), 1)
        seq_first = (c % blocks_per_seq) == 0
        seq_last = (c % blocks_per_seq) == blocks_per_seq - 1

        def neighbours(load):
            cur = load(rows)
            rid = lax.broadcasted_iota(jnp.int32, cur.shape, 0)
            before = jnp.where(seq_first, 0.0, load(prev_row))
            after = jnp.where(seq_last, 0.0, load(next_row))
            prev = jnp.where(rid == 0, before, pltpu.roll(cur, 1, 0))
            nxt = jnp.where(rid == ROWS_A - 1, after, pltpu.roll(cur, ROWS_A - 1, 0))
            return cur, prev, nxt

        z, zp, zn = neighbours(lambda rs: cv_ref[rs, D_CONV:2 * D_CONV] * cv_ref[rs, 2 * D_CONV:3 * D_CONV])
        cw = cw_ref[...]
        conv = cv_ref[rows, 0:D_CONV] * (zp * cw[0:1, :] + z * cw[1:2, :] + zn * cw[2:3, :])
        o_ref[rows, 0:D_CONV] = conv.astype(BF16)

        u, up, un = neighbours(lambda rs: um_ref[rs, :])
        u = u + mum_ref[...] * (0.5 * (up + un) - u)
        t, tp, tn = neighbours(lambda rs: ut_ref[rs, :])
        ut = t + mut_ref[...] * (0.5 * (tp + tn) - t)
        r = u[:, 0:D_RWKV]
        k = u[:, D_RWKV:2 * D_RWKV]
        v = u[:, 2 * D_RWKV:3 * D_RWKV]
        kk = k * kk_ref[...]
        kk = kk * lax.rsqrt(seg_sum(kk * kk) + 1e-12)
        for h, v_h in enumerate(per_head_tiles(v, high=False)):
            v_s[h, rows, :] = v_h.astype(BF16)
        tw = jnp.tanh(ut)
        a_sum = jnp.zeros((ROWS_A, D_RWKV), F32)
        for d in range(2):
            zz = w0_ref[d:d + 1, :] + _dot_x3(tw, wup_ref[d])
            lw = -math.exp(-0.5) * jax.nn.sigmoid(zz)
            a = jax.nn.sigmoid(a0_ref[d:d + 1, :] + _dot_x3(ut, aup_ref[d]))
            a_sum = a_sum + a
            cum = _split_sum(lw, 3, lambda part: jnp.dot(cum_mat[d], part, preferred_element_type=F32))
            e_out = jnp.exp(-cum)
            kd = k * (1.0 + (a - 1.0) * ka_ref[...])
            scaled = ((al_s, -kk * jnp.exp(cum - lw)), (be_s, kk * a * e_out), (kd_s, kd * e_out),
                      (r_s, r * jnp.exp(cum)))
            for ref, val in scaled:
                for h, val_h in enumerate(per_head_tiles(val, high=True)):
                    ref[d, h, rows, :] = val_h.astype(BF16)
            half = lax.broadcasted_iota(jnp.int32, (1, PAIR), 1) >= RWKV_HEAD
            for j in range(per_block):
                last = j * CHUNK + (CHUNK - 1 if d == 0 else 0)
                for h, p_h in enumerate(per_head_tiles(jnp.exp(cum[last:last + 1, :]), high=True)):
                    p_s[d, c * per_block + j, h:h + 1, :] = jnp.where(half, p_h, 1.0)
        kd_sum = k * (2.0 + (a_sum - 2.0) * ka_ref[...])
        bon_s[rows, :] = seg_sum(r * kd_sum * rk_ref[...]) * v
        g_s[rows, :] = _dot_x3(jax.nn.sigmoid(ut), gup_ref[...])
        y_s[rows, :] = jnp.zeros((ROWS_A, D_RWKV), F32)
        return carry

    lax.fori_loop(0, n_blocks, pass1, 0)

    if manual_fetch:
        @pl.when(step + 1 < pl.num_programs(0))
        def _():
            for cp in input_copies(step + 1):
                cp.start()

    eye_low = (lax.broadcasted_iota(jnp.int32, (RWKV_HEAD, PAIR), 0)
               == lax.broadcasted_iota(jnp.int32, (RWKV_HEAD, PAIR), 1)).astype(F32)
    for s in range(nseq):
        for d in range(2):
            for h in range(RWKV_HEADS):
                if has_s0:
                    s_scr[s, d, h] = jnp.concatenate([eye_low[:, 0:RWKV_HEAD], s0_ref[d, h]], axis=1)
                else:
                    s_scr[s, d, h] = eye_low

    ri = lax.broadcasted_iota(jnp.int32, (CHUNK, 2 * CHUNK), 0)
    ci = lax.broadcasted_iota(jnp.int32, (CHUNK, 2 * CHUNK), 1) % CHUNK
    ei = lax.broadcasted_iota(jnp.int32, (CHUNK, CHUNK), 0)
    ej = lax.broadcasted_iota(jnp.int32, (CHUNK, CHUNK), 1)
    eye = (ei == ej).astype(F32)
    diag_blk = (ei // SUB) == (ej // SUB)
    zeros_b = jnp.zeros((CHUNK, PAIR), BF16)
    n_rounds = int(math.log2(SUB)) - 1

    def chunk_step(c, carry):
        chains = []
        for s, d in ((s, d) for s in range(nseq) for d in range(2)):
            cc = s * n_chunks + (c if d == 0 else n_chunks - 1 - c)
            rows = pl.ds(pl.multiple_of(cc * CHUNK, CHUNK), CHUNK)
            p_tile = p_s[d, cc]
            for h in range(RWKV_HEADS):
                chains.append(dict(
                    s=s, d=d, h=h, rows=rows, p=p_tile[h:h + 1, :],
                    strict=(ci < ri) if d == 0 else (ci > ri), incl=(ci <= ri) if d == 0 else (ci >= ri),
                    al=al_s[d, h, rows, :], be=be_s[d, h, rows, :], kd=kd_s[d, h, rows, :],
                    r=r_s[d, h, rows, :], v=v_s[h, rows, :]))

        def each(fn, *lists):
            return [fn(*args) for args in zip(chains, *lists)]

        bk = each(lambda ch: jnp.concatenate([ch['be'], ch['kd']], axis=0))
        g = each(lambda ch, m: _mm(jnp.concatenate([ch['al'], ch['r']], axis=0), m, _NT), bk)
        top = each(lambda ch, m: jnp.where(ch['strict'], m[0:CHUNK], 0.0), g)
        bot = each(lambda ch, m: jnp.where(ch['incl'], m[CHUNK:2 * CHUNK], 0.0).astype(BF16), g)
        rhs = each(lambda ch, m: _mm(m, jnp.concatenate([zeros_b, ch['v']], axis=0))
                   + ch['al'].astype(F32), top)
        a = [m[:, 0:CHUNK] for m in top]
        xs = [jnp.where(diag_blk, m, 0.0) for m in a]
        low = [jnp.where(diag_blk, 0.0, m) for m in a]
        ts = [eye + x for x in xs]
        xs = [_mm(x, x) for x in xs]
        for _ in range(n_rounds - 1):
            z = [_mm(jnp.concatenate([x, t], axis=0), x) for x, t in zip(xs, ts)]
            xs = [m[0:CHUNK] for m in z]
            ts = [t + m[CHUNK:2 * CHUNK] for t, m in zip(ts, z)]
        ts = [t + _mm(t, x) for t, x in zip(ts, xs)]
        ms = [_mm(t, m) for t, m in zip(ts, low)]
        sol = [_mm(t, m) for t, m in zip(ts, rhs)]
        m2 = [_mm(m, m) for m in ms]
        ims = [eye + m for m in ms]
        ns = [im + _mm(im, mm2) for im, mm2 in zip(ims, m2)]
        w = [_mm(nn, x) for nn, x in zip(ns, sol)]
        st = each(lambda ch: s_scr[ch['s'], ch['d'], ch['h']])
        stb = [t.astype(BF16) for t in st]
        u = [_mm(ww, t, _NT) for ww, t in zip(w, stb)]
        uv = each(lambda ch, uu: jnp.concatenate([uu.astype(BF16), ch['v'][:, 0:RWKV_HEAD]], axis=0), u)
        ys = each(lambda ch, t, qq, m: _mm(ch['r'], t, _NT) + _mm(qq, m), stb, bot, uv)
        s_new = each(lambda ch, t, m, kb: (t + _mm(m, kb, _TN)) * ch['p'], st, uv, bk)
        for ch, t in zip(chains, s_new):
            s_scr[ch['s'], ch['d'], ch['h']] = t
        for ch, y in zip(chains, ys):
            y_s[ch['rows'], ch['h'] * RWKV_HEAD:(ch['h'] + 1) * RWKV_HEAD] += y
        return carry

    lax.fori_loop(0, n_chunks, chunk_step, 0)

    if emit_state:
        for s in range(nseq):
            for d in range(2):
                for h in range(RWKV_HEADS):
                    so_ref[s, d, h] = s_scr[s, d, h][:, RWKV_HEAD:PAIR]

    def pass3(c, carry):
        rows = pl.ds(pl.multiple_of(c * ROWS_A, ROWS_A), ROWS_A)
        y = y_s[rows, :]
        yc = y - seg_sum(y) * (1.0 / RWKV_HEAD)
        var = seg_sum(yc * yc) * (1.0 / RWKV_HEAD)
        yn = yc * lax.rsqrt(var + GN_EPS) * lng_ref[...] + lnb_ref[...]
        o_ref[rows, D_CONV:D_CONV + D_RWKV] = ((yn + bon_s[rows, :]) * g_s[rows, :]).astype(BF16)
        return carry

    lax.fori_loop(0, n_blocks, pass3, 0)


def _pad_rows(w, lo):
    return jnp.pad(w, [(0, 0)] * (w.ndim - 2) + [(lo, D_TAIL - lo - w.shape[-2]), (0, 0)])


def _mix_call(proj, tail, P, l, n, nseq, s0=None, emit_state=False):
    T = proj.shape[0]
    B = T // n
    rows = nseq * n
    assert s0 is None or nseq == 1
    full = lambda shape: pl.BlockSpec(shape, lambda b: (0,) * len(shape))
    lanes = lambda w: -(-w // 128) * 128
    in_bytes = rows * (3 * D_CONV + 3 * D_RWKV + lanes(D_TAIL)) * 4
    scratch_bytes = rows * (4 * 2 * RWKV_HEADS * lanes(RWKV_HEAD) * 2 + RWKV_HEADS * lanes(RWKV_HEAD) * 2
                            + 3 * D_RWKV * 4)
    out_bytes = 2 * rows * (D_CONV + D_RWKV) * 2
    temporaries = 6 * 1024 * 1024
    manual_fetch = 2 * in_bytes + scratch_bytes + out_bytes + temporaries > VMEM_LIMIT_BYTES
    if manual_fetch:
        in_specs = [pl.BlockSpec(memory_space=pl.ANY), pl.BlockSpec(memory_space=pl.ANY)]
        token_args = [proj, tail]
    else:
        in_specs = [
            pl.BlockSpec((rows, 3 * D_CONV), lambda b: (b, CONV_COL // (3 * D_CONV))),
            pl.BlockSpec((rows, 3 * D_RWKV), lambda b: (b, RWKV_COL // (3 * D_RWKV))),
            pl.BlockSpec((rows, D_TAIL), lambda b: (b, 0)),
        ]
        token_args = [proj, proj, tail]
    in_specs += [
        full((1, 3 * D_RWKV)), full((1, D_TAIL)), full((3, D_CONV)),
        full((2, D_RWKV)), full((2, D_TAIL, D_RWKV)), full((2, D_RWKV)), full((2, D_TAIL, D_RWKV)),
        full((D_TAIL, D_RWKV)), full((1, D_RWKV)), full((1, D_RWKV)), full((1, D_RWKV)),
        full((1, D_RWKV)), full((1, D_RWKV)), full((D_RWKV, D_RWKV)),
    ]
    head = jnp.arange(D_RWKV) // RWKV_HEAD
    seg = (head[:, None] == head[None, :]).astype(BF16)
    mu = P['rwkv_mu'][l]
    row = lambda a: a.reshape(1, -1)
    args = token_args + [row(mu[:3 * D_RWKV]), row(mu[3 * D_RWKV:]), P['conv_w'][l],
            P['rwkv_w0'][l], _pad_rows(P['rwkv_w_up'][l], 0), P['rwkv_a0'][l],
            _pad_rows(P['rwkv_a_up'][l], DECAY_RANK), _pad_rows(P['rwkv_g_up'][l], DECAY_RANK + A_RANK),
            row(P['rwkv_k_k'][l]), row(P['rwkv_k_a'][l]), row(P['rwkv_r_k'][l]),
            row(P['rwkv_ln_g'][l]), row(P['rwkv_ln_b'][l]), seg]
    if s0 is not None:
        in_specs.append(pl.BlockSpec((None, None, 2, RWKV_HEADS, RWKV_HEAD, RWKV_HEAD),
                                     lambda b: (b, l, 0, 0, 0, 0)))
        args.append(s0)
    out_specs = [pl.BlockSpec((rows, D_CONV + D_RWKV), lambda b: (b, 0))]
    out_shape = [jax.ShapeDtypeStruct((T, D_CONV + D_RWKV), BF16)]
    state_shape = (2, RWKV_HEADS, RWKV_HEAD, RWKV_HEAD)
    if emit_state:
        out_specs.append(pl.BlockSpec((nseq,) + state_shape, lambda b: (b, 0, 0, 0, 0)))
        out_shape.append(jax.ShapeDtypeStruct((B,) + state_shape, F32))
    pair = 2 * RWKV_HEAD
    per_head = lambda: pltpu.VMEM((2, RWKV_HEADS, rows, pair), BF16)
    scratch_shapes = [per_head(), per_head(), per_head(), per_head(),
                      pltpu.VMEM((RWKV_HEADS, rows, pair), BF16),
                      pltpu.VMEM((2, rows // CHUNK, RWKV_HEADS, pair), F32),
                      pltpu.VMEM((rows, D_RWKV), F32), pltpu.VMEM((rows, D_RWKV), F32),
                      pltpu.VMEM((rows, D_RWKV), F32),
                      pltpu.VMEM((nseq, 2, RWKV_HEADS, RWKV_HEAD, pair), F32)]
    if manual_fetch:
        scratch_shapes += [pltpu.VMEM((rows, 3 * D_CONV), F32), pltpu.VMEM((rows, 3 * D_RWKV), F32),
                           pltpu.VMEM((rows, D_TAIL), F32), pltpu.SemaphoreType.DMA((3,))]
    semantics = ("arbitrary",) if manual_fetch else ("parallel",)
    return pl.pallas_call(
        functools.partial(_mix_kernel, n=n, nseq=nseq, has_s0=s0 is not None, emit_state=emit_state,
                          manual_fetch=manual_fetch),
        grid=(B // nseq,),
        in_specs=in_specs,
        out_specs=out_specs,
        out_shape=out_shape,
        scratch_shapes=scratch_shapes,
        compiler_params=pltpu.CompilerParams(dimension_semantics=semantics,
                                             vmem_limit_bytes=VMEM_LIMIT_BYTES),
        name="conv_rwkv",
    )(*args)


def _residual_copy(x_hbm, x_buf, sem):
    rows = x_buf.shape[0]
    start = pl.multiple_of(pl.program_id(0) * rows, rows)
    return pltpu.make_async_copy(x_hbm.at[pl.ds(start, rows), :], x_buf, sem)


def _outproj_kernel(at_ref, cr_ref, w_ref, x_hbm, ga_ref, shf_ref, scf_ref, g1_ref, g2_ref,
                    x1_ref, h2_ref, x_ref, x_sem):
    kt = pl.program_id(1)

    @pl.when(kt == 0)
    def _():
        _residual_copy(x_hbm, x_ref, x_sem).start()
        x1_ref[...] = jnp.zeros(x1_ref.shape, F32)

    lhs = jnp.where(kt < 2, at_ref[...], cr_ref[...])
    x1_ref[...] += jnp.dot(lhs, w_ref[...].astype(BF16), preferred_element_type=F32)

    @pl.when(kt == pl.num_programs(1) - 1)
    def _():
        _residual_copy(x_hbm, x_ref, x_sem).wait()
        x1 = x_ref[...] + ga_ref[...] * _rms(x1_ref[...], g1_ref[...])
        x1_ref[...] = x1
        h2_ref[...] = (_rms(x1, g2_ref[...]) * (1.0 + scf_ref[...]) + shf_ref[...]).astype(BF16)


def _outproj_call(attn, cr, x, mod3, ng3, w_out, l, row_fn, tm):
    T = x.shape[0]
    tk = 512
    return pl.pallas_call(
        _outproj_kernel,
        grid=(T // tm, D_MODEL // tk),
        in_specs=[
            pl.BlockSpec((tm, tk), lambda i, k: (i, jnp.minimum(k, 1))),
            pl.BlockSpec((tm, tk), lambda i, k: (i, jnp.maximum(k - 2, 0))),
            pl.BlockSpec((None, tk, D_MODEL), lambda i, k: (l, k, 0)),
            pl.BlockSpec(memory_space=pl.ANY),
            _mod_spec(row_fn, 2), _mod_spec(row_fn, 3), _mod_spec(row_fn, 4),
            _ng_spec(l, 1), _ng_spec(l, 2),
        ],
        out_specs=[
            pl.BlockSpec((tm, D_MODEL), lambda i, k: (i, 0)),
            pl.BlockSpec((tm, D_MODEL), lambda i, k: (i, 0)),
        ],
        out_shape=[jax.ShapeDtypeStruct((T, D_MODEL), F32), jax.ShapeDtypeStruct((T, D_MODEL), BF16)],
        scratch_shapes=[pltpu.VMEM((tm, D_MODEL), F32), pltpu.SemaphoreType.DMA(())],
        compiler_params=_params(2),
        name="out_proj",
    )(attn, cr, w_out, x, mod3, mod3, mod3, ng3, ng3)


def _ffn_kernel(h_ref, x1_hbm, wg_ref, wu_ref, wd_ref, gf_ref, g3_ref, o_ref, x1_ref, x1_sem):
    f = pl.program_id(1)

    @pl.when(f == 0)
    def _():
        _residual_copy(x1_hbm, x1_ref, x1_sem).start()
        o_ref[...] = jnp.zeros(o_ref.shape, F32)

    h = h_ref[...]
    gate = _dot(h, wg_ref[...])
    up = _dot(h, wu_ref[...])
    o_ref[...] += _dot(gate * jax.nn.sigmoid(gate) * up, wd_ref[...])

    @pl.when(f == pl.num_programs(1) - 1)
    def _():
        _residual_copy(x1_hbm, x1_ref, x1_sem).wait()
        o_ref[...] = x1_ref[...] + gf_ref[...] * _rms(o_ref[...], g3_ref[...])


def _ffn_call(h2, x1, mod3, ng3, w_gate, w_up, w_down, l, row_fn, tm):
    T = x1.shape[0]
    tf = 256
    return pl.pallas_call(
        _ffn_kernel,
        grid=(T // tm, D_FF // tf),
        in_specs=[
            pl.BlockSpec((tm, D_MODEL), lambda i, f: (i, 0)),
            pl.BlockSpec(memory_space=pl.ANY),
            pl.BlockSpec((None, D_MODEL, tf), lambda i, f: (l, 0, f)),
            pl.BlockSpec((None, D_MODEL, tf), lambda i, f: (l, 0, f)),
            pl.BlockSpec((None, tf, D_MODEL), lambda i, f: (l, f, 0)),
            _mod_spec(row_fn, 5),
            _ng_spec(l, 3),
        ],
        out_specs=pl.BlockSpec((tm, D_MODEL), lambda i, f: (i, 0)),
        out_shape=jax.ShapeDtypeStruct((T, D_MODEL), F32),
        scratch_shapes=[pltpu.VMEM((tm, D_MODEL), F32), pltpu.SemaphoreType.DMA(())],
        compiler_params=_params(2),
        name="ffn",
    )(h2, x1, w_gate, w_up, w_down, mod3, ng3)


def _rope_tables(n):
    rows = n // GRID_W
    row = jnp.repeat(jnp.arange(rows), GRID_W).astype(F32)
    col = jnp.tile(jnp.arange(GRID_W), rows).astype(F32)
    half = HEAD_DIM // 2
    inv = 1.0 / (ROPE_THETA ** (jnp.arange(0, half, 2, dtype=F32) / half))
    ar = row[:, None] * inv
    ac = col[:, None] * inv
    ang = jnp.concatenate([ar, ar, ac, ac], axis=-1)
    return jnp.cos(ang), jnp.sin(ang)


def _layer(x, mod3, ng3, P, l, n, row_fn_of_tm, rope_tabs=None, cache=None, s0=None, emit=False):
    proj, tail = _inproj_call(x, mod3, ng3, P['w_in_t'], P['w_tail_t'], l, row_fn_of_tm(1024), 2048)
    attn_out = _attn_call(proj, P['q_norm'][l].reshape(1, -1), P['k_norm'][l].reshape(1, -1), l, n,
                          rope_tabs=rope_tabs, cache=cache, emit_kv=emit)
    mix_out = _mix_call(proj, tail, P, l, n, 1 if s0 is not None else 2, s0=s0, emit_state=emit)
    attn = attn_out[0]
    cr = mix_out[0]
    x1, h2 = _outproj_call(attn, cr, x, mod3, ng3, P['w_out'], l, row_fn_of_tm(1024), 1024)
    x2 = _ffn_call(h2, x1, mod3, ng3, P['w_gate'], P['w_up'], P['w_down'], l, row_fn_of_tm(1024), 1024)
    if emit:
        return x2, attn_out[1], attn_out[2], mix_out[1]
    return x2


def kernel(x_prompt, x_sample, cache_k, cache_v, state_rwkv, c, c_ctx, w_ada, b_ada, norm_g, w_in, q_norm, k_norm, conv_w, rwkv_mu, rwkv_w0, rwkv_w_up, rwkv_a0, rwkv_a_up, rwkv_g_up, rwkv_k_k, rwkv_k_a, rwkv_r_k, rwkv_ln_g, rwkv_ln_b, w_out, w_gate, w_up, w_down):
    P = {'q_norm': q_norm, 'k_norm': k_norm, 'conv_w': conv_w, 'rwkv_mu': rwkv_mu,
         'rwkv_w0': rwkv_w0, 'rwkv_w_up': rwkv_w_up, 'rwkv_a0': rwkv_a0, 'rwkv_a_up': rwkv_a_up,
         'rwkv_g_up': rwkv_g_up, 'rwkv_k_k': rwkv_k_k, 'rwkv_k_a': rwkv_k_a, 'rwkv_r_k': rwkv_r_k,
         'rwkv_ln_g': rwkv_ln_g, 'rwkv_ln_b': rwkv_ln_b, 'w_out': w_out,
         'w_gate': w_gate, 'w_up': w_up, 'w_down': w_down}
    w_in_t = jnp.swapaxes(w_in, 1, 2)
    P['w_in_t'] = w_in_t
    P['w_tail_t'] = w_in_t[:, D_MAIN:, :]
    batch, seq, _ = x_prompt.shape
    dec_batch, dec_seq, _ = x_sample.shape

    cc = jnp.zeros((8, D_MODEL), F32).at[0].set(c_ctx).at[1:1 + dec_batch].set(c)
    mod3 = _ada_call(cc, w_ada, b_ada).reshape(DEPTH * 8, 1, N_MOD * D_MODEL)
    ng3 = norm_g.reshape(DEPTH * 4, 1, D_MODEL)

    ck = cache_k.reshape(dec_batch, DEPTH, PAST_LEN, D_KV)
    cv = cache_v.reshape(dec_batch, DEPTH, PAST_LEN, D_KV)
    rope_tabs = _rope_tables(dec_seq)

    xp = x_prompt.reshape(batch * seq, D_MODEL)
    xs = x_sample.reshape(dec_batch * dec_seq, D_MODEL)
    ks, vs, ss = [], [], []
    for l in range(DEPTH):
        ctx_row = lambda tm, l=l: (lambda i: l * 8)
        smp_row = lambda tm, l=l: (lambda i: l * 8 + 1 + (i * tm) // dec_seq)
        xp, k_l, v_l, s_l = _layer(xp, mod3, ng3, P, l, seq, ctx_row, emit=True)
        ks.append(k_l.reshape(batch, seq, N_KV_HEADS, HEAD_DIM))
        vs.append(v_l.reshape(batch, seq, N_KV_HEADS, HEAD_DIM))
        ss.append(s_l)
        xs = _layer(xs, mod3, ng3, P, l, dec_seq, smp_row, rope_tabs=rope_tabs, cache=(ck, cv),
                    s0=state_rwkv)
    return (xp.reshape(batch, seq, D_MODEL), xs.reshape(dec_batch, dec_seq, D_MODEL),
            jnp.stack(ks, axis=1), jnp.stack(vs, axis=1), jnp.stack(ss, axis=1))
```

```python
import functools
import math

import jax
import jax.numpy as jnp
from jax import lax
from jax.experimental import pallas as pl
from jax.experimental.pallas import tpu as pltpu

D_MODEL = 2048
DEPTH = 2
GRID_W = 64
HEAD_DIM = 128
N_HEADS = 8
N_KV_HEADS = 2
GROUP = N_HEADS // N_KV_HEADS
D_ATTN = N_HEADS * HEAD_DIM
D_KV = N_KV_HEADS * HEAD_DIM
ROPE_THETA = 10000.0
D_CONV = 512
D_RWKV = 512
RWKV_HEAD = 64
RWKV_HEADS = D_RWKV // RWKV_HEAD
DECAY_RANK = 32
A_RANK = 32
GATE_RANK = 96
D_TAIL = DECAY_RANK + A_RANK + GATE_RANK
D_IN = D_ATTN + 2 * D_KV + 3 * D_CONV + 3 * D_RWKV + D_TAIL
D_MAIN = D_IN - D_TAIL
D_FF = 5632
N_MOD = 6
EPS = 1e-6
GN_EPS = 64e-5
PAST_LEN = 256

VMEM_LIMIT_BYTES = 56 * 1024 * 1024
CHUNK = 64
BF16 = jnp.bfloat16
F32 = jnp.float32


def _params(n_grid, **kw):
    sem = ("parallel",) + ("arbitrary",) * (n_grid - 1)
    return pltpu.CompilerParams(dimension_semantics=sem, vmem_limit_bytes=VMEM_LIMIT_BYTES, **kw)


def _rms(x, g):
    return x * lax.rsqrt(jnp.mean(x * x, axis=-1, keepdims=True) + EPS) * g


def _dot(a, b):
    return jnp.dot(a.astype(BF16), b.astype(BF16), preferred_element_type=F32)


_NN = (((1,), (0,)), ((), ()))
_NT = (((1,), (1,)), ((), ()))
_TN = (((0,), (0,)), ((), ()))


def _mm(a, b, dims=_NN):
    return lax.dot_general(a.astype(BF16), b.astype(BF16), dims, preferred_element_type=F32)


def _ada_kernel(c_ref, w_ref, b_ref, o_ref):
    c = c_ref[...]
    s = c * jax.nn.sigmoid(c)
    o_ref[...] = _dot(s, w_ref[...]) + b_ref[...]


def _ada_call(cc, w_ada, b_ada):
    tn = 512
    n_out = N_MOD * D_MODEL
    return pl.pallas_call(
        _ada_kernel,
        grid=(DEPTH, n_out // tn),
        in_specs=[
            pl.BlockSpec((8, D_MODEL), lambda l, n: (0, 0)),
            pl.BlockSpec((None, D_MODEL, tn), lambda l, n: (l, 0, n)),
            pl.BlockSpec((None, 1, tn), lambda l, n: (l, 0, n)),
        ],
        out_specs=pl.BlockSpec((None, 8, tn), lambda l, n: (l, 0, n)),
        out_shape=jax.ShapeDtypeStruct((DEPTH, 8, n_out), F32),
        compiler_params=_params(2),
        name="adaln",
    )(cc, w_ada, b_ada.reshape(DEPTH, 1, n_out))


def _mod_spec(row_fn, chunk):
    return pl.BlockSpec((None, 1, D_MODEL), lambda i, *_: (row_fn(i), 0, chunk))


def _ng_spec(l, j):
    return pl.BlockSpec((None, 1, D_MODEL), lambda i, *_: (l * 4 + j, 0, 0))


INPROJ_NORM_ROWS = 256
INPROJ_VMEM_LIMIT_BYTES = 60 * 1024 * 1024


def _row_tile_copy(x_hbm, x_buf, sem, tile):
    rows = x_buf.shape[0]
    return pltpu.make_async_copy(x_hbm.at[pl.ds(pl.multiple_of(tile * rows, rows), rows), :], x_buf, sem)


def _inproj_kernel(x_hbm, sha_ref, sca_ref, shb_ref, scb_ref, g_ref, w_ref, wt_ref, o_ref, ot_ref,
                   x_buf, h_scr, x_sem):
    i = pl.program_id(0)

    @pl.when(pl.program_id(1) == 0)
    def _():
        @pl.when(i == 0)
        def _():
            _row_tile_copy(x_hbm, x_buf, x_sem, 0).start()

        _row_tile_copy(x_hbm, x_buf, x_sem, i).wait()
        half = x_buf.shape[0] // 2
        for r0 in range(0, x_buf.shape[0], INPROJ_NORM_ROWS):
            rs = slice(r0, r0 + INPROJ_NORM_ROWS)
            sh_ref, sc_ref = (sha_ref, sca_ref) if r0 < half else (shb_ref, scb_ref)
            h = _rms(x_buf[rs, :], g_ref[...]) * (1.0 + sc_ref[...]) + sh_ref[...]
            h_scr[rs, :] = h.astype(BF16)

        @pl.when(i + 1 < pl.num_programs(0))
        def _():
            _row_tile_copy(x_hbm, x_buf, x_sem, i + 1).start()

        ot_ref[...] = _mm(h_scr[...], wt_ref[...], _NT)

    o_ref[...] = _mm(h_scr[...], w_ref[...], _NT)


def _inproj_call(x, mod3, ng3, w_in_t, w_tail_t, l, row_fn_half, tm):
    T = x.shape[0]
    tn = 512
    seg = lambda j: (lambda i: row_fn_half(2 * i + j))
    return pl.pallas_call(
        _inproj_kernel,
        grid=(T // tm, D_MAIN // tn),
        in_specs=[
            pl.BlockSpec(memory_space=pl.ANY),
            _mod_spec(seg(0), 0), _mod_spec(seg(0), 1),
            _mod_spec(seg(1), 0), _mod_spec(seg(1), 1),
            _ng_spec(l, 0),
            pl.BlockSpec((None, tn, D_MODEL), lambda i, n: (l, n, 0)),
            pl.BlockSpec((None, D_TAIL, D_MODEL), lambda i, n: (l, 0, 0)),
        ],
        out_specs=[
            pl.BlockSpec((tm, tn), lambda i, n: (i, n)),
            pl.BlockSpec((tm, D_TAIL), lambda i, n: (i, 0)),
        ],
        out_shape=[
            jax.ShapeDtypeStruct((T, D_MAIN), F32),
            jax.ShapeDtypeStruct((T, D_TAIL), F32),
        ],
        scratch_shapes=[pltpu.VMEM((tm, D_MODEL), F32), pltpu.VMEM((tm, D_MODEL), BF16),
                        pltpu.SemaphoreType.DMA(())],
        compiler_params=pltpu.CompilerParams(dimension_semantics=("arbitrary", "arbitrary"),
                                             vmem_limit_bytes=INPROJ_VMEM_LIMIT_BYTES),
        name="in_proj",
    )(x, mod3, mod3, mod3, mod3, ng3, w_in_t, w_tail_t)


def _rope(x, cos, sin):
    lane = lax.broadcasted_iota(jnp.int32, x.shape, 1)
    first = (lane % (HEAD_DIM // 2)) < (HEAD_DIM // 4)
    rot = jnp.where(first, -pltpu.roll(x, HEAD_DIM - HEAD_DIM // 4, 1), pltpu.roll(x, HEAD_DIM // 4, 1))
    return x * cos + rot * sin


def _attn_kernel(*refs, n, past, rope, emit_kv, qb):
    it = iter(refs)
    q_ref, k_ref, v_ref, qn_ref, kn_ref = (next(it) for _ in range(5))
    cos_ref = sin_ref = ck_ref = cv_ref = None
    if rope:
        cos_ref, sin_ref = next(it), next(it)
    if past:
        ck_ref, cv_ref = next(it), next(it)
    o_ref = next(it)
    if emit_kv:
        ko_ref, vo_ref = next(it), next(it)
    k_scr, v_scr = next(it), next(it)

    k = _rms(k_ref[...], kn_ref[...])
    v = v_ref[...]
    if emit_kv:
        ko_ref[...] = k
        vo_ref[...] = v
    if rope:
        k = _rope(k, cos_ref[...], sin_ref[...])
    k_scr[0:n, :] = k.astype(BF16)
    v_scr[0:n, :] = v.astype(BF16)
    if past:
        k_scr[n:n + past, :] = ck_ref[...].astype(BF16)
        v_scr[n:n + past, :] = cv_ref[...].astype(BF16)
    exp2_scale = HEAD_DIM ** -0.5 * math.log2(math.e)

    blocks_per_step = min(n // qb, 4)

    def block(b, carry):
        items = [(pl.ds(pl.multiple_of((b * blocks_per_step + j) * qb, qb), qb),
                  slice(g * HEAD_DIM, (g + 1) * HEAD_DIM))
                 for j in range(blocks_per_step) for g in range(GROUP)]

        def scores(item):
            rows, cols = item
            q = _rms(q_ref[rows, cols], qn_ref[...])
            if rope:
                q = _rope(q, cos_ref[rows, :], sin_ref[rows, :])
            return lax.dot_general(q.astype(BF16), k_scr[...], _NT, preferred_element_type=F32)

        def softmax(s):
            p = jnp.exp2((s - jnp.max(s, axis=-1, keepdims=True)) * exp2_scale)
            return p.astype(BF16), jnp.sum(p, axis=-1, keepdims=True)

        def values(item, pd):
            rows, cols = item
            o = jnp.dot(pd[0], v_scr[...], preferred_element_type=F32)
            o_ref[rows, cols] = (o / pd[1]).astype(BF16)

        if blocks_per_step == 1:
            ps = [softmax(s) for s in [scores(item) for item in items]]
            for item, pd in zip(items, ps):
                values(item, pd)
            return carry
        depth = 2
        ss, ps = {}, {}
        for i in range(len(items) + depth):
            if i < len(items):
                ss[i] = scores(items[i])
            if 1 <= i <= len(items):
                ps[i - 1] = softmax(ss.pop(i - 1))
            if i >= depth:
                values(items[i - depth], ps.pop(i - depth))
        return carry

    lax.fori_loop(0, n // (qb * blocks_per_step), block, 0)


def _attn_call(proj, qn, kn, l, n, rope_tabs=None, cache=None, emit_kv=False):
    T = proj.shape[0]
    B = T // n
    past = PAST_LEN if cache is not None else 0
    qw = GROUP * HEAD_DIM
    k_blk = D_ATTN // HEAD_DIM
    v_blk = (D_ATTN + D_KV) // HEAD_DIM
    in_specs = [
        pl.BlockSpec((n, qw), lambda b, h: (b, h)),
        pl.BlockSpec((n, HEAD_DIM), lambda b, h: (b, k_blk + h)),
        pl.BlockSpec((n, HEAD_DIM), lambda b, h: (b, v_blk + h)),
        pl.BlockSpec((1, HEAD_DIM), lambda b, h: (0, 0)),
        pl.BlockSpec((1, HEAD_DIM), lambda b, h: (0, 0)),
    ]
    args = [proj, proj, proj, qn, kn]
    if rope_tabs is not None:
        in_specs += [pl.BlockSpec((n, HEAD_DIM), lambda b, h: (0, 0))] * 2
        args += list(rope_tabs)
    if cache is not None:
        in_specs += [pl.BlockSpec((None, None, PAST_LEN, HEAD_DIM), lambda b, h: (b, l, 0, h))] * 2
        args += list(cache)
    out_specs = [pl.BlockSpec((n, qw), lambda b, h: (b, h))]
    out_shape = [jax.ShapeDtypeStruct((T, D_ATTN), BF16)]
    if emit_kv:
        out_specs += [pl.BlockSpec((None, n, HEAD_DIM), lambda b, h: (b, 0, h))] * 2
        out_shape += [jax.ShapeDtypeStruct((B, n, D_KV), F32)] * 2
    return pl.pallas_call(
        functools.partial(_attn_kernel, n=n, past=past, rope=rope_tabs is not None,
                          emit_kv=emit_kv, qb=256),
        grid=(B, N_KV_HEADS),
        in_specs=in_specs,
        out_specs=out_specs,
        out_shape=out_shape,
        scratch_shapes=[pltpu.VMEM((n + past, HEAD_DIM), BF16)] * 2,
        compiler_params=_params(2),
        name="attention",
    )(*args)


CONV_COL = D_ATTN + 2 * D_KV
RWKV_COL = CONV_COL + 3 * D_CONV
ROWS_A = 256
SUB = 16
assert CHUNK // SUB == 4


def _split2(x):
    hi = x.astype(BF16)
    return hi, (x - hi.astype(F32)).astype(BF16)


def _dot_x3(a, b):
    ah, al = _split2(a)
    bh, bl = _split2(b)
    mm = lambda p, q: jnp.dot(p, q, preferred_element_type=F32)
    return mm(ah, bh) + mm(ah, bl) + mm(al, bh)


def _split_sum(x, pieces, dot_piece):
    acc = None
    rest = x
    for _ in range(pieces):
        part = rest.astype(BF16)
        rest = rest - part.astype(F32)
        term = dot_piece(part)
        acc = term if acc is None else acc + term
    return acc


def _mix_kernel(*refs, n, nseq, has_s0, emit_state, manual_fetch):
    it = iter(refs)
    if manual_fetch:
        proj_hbm, tail_hbm = next(it), next(it)
    else:
        cv_ref, um_ref, ut_ref = next(it), next(it), next(it)
    (mum_ref, mut_ref, cw_ref, w0_ref, wup_ref, a0_ref, aup_ref, gup_ref, kk_ref, ka_ref,
     rk_ref, lng_ref, lnb_ref, seg_ref) = (next(it) for _ in range(14))
    s0_ref = next(it) if has_s0 else None
    o_ref = next(it)
    so_ref = next(it) if emit_state else None
    al_s, be_s, kd_s, r_s, v_s, p_s, bon_s, g_s, y_s, s_scr = (next(it) for _ in range(10))
    if manual_fetch:
        cv_ref, um_ref, ut_ref, in_sem = next(it), next(it), next(it), next(it)

        def input_copies(step):
            rows = pl.ds(pl.multiple_of(step * (nseq * n), nseq * n), nseq * n)
            return (
                pltpu.make_async_copy(proj_hbm.at[rows, pl.ds(CONV_COL, 3 * D_CONV)], cv_ref, in_sem.at[0]),
                pltpu.make_async_copy(proj_hbm.at[rows, pl.ds(RWKV_COL, 3 * D_RWKV)], um_ref, in_sem.at[1]),
                pltpu.make_async_copy(tail_hbm.at[rows, :], ut_ref, in_sem.at[2]))

        step = pl.program_id(0)

        @pl.when(step == 0)
        def _():
            for cp in input_copies(0):
                cp.start()

        for cp in input_copies(step):
            cp.wait()
    blocks_per_seq = n // ROWS_A
    n_blocks = nseq * blocks_per_seq
    n_chunks = n // CHUNK
    per_block = ROWS_A // CHUNK

    def seg_sum(x):
        return _split_sum(x, 2, lambda part: jnp.dot(part, seg_ref[...], preferred_element_type=F32))

    PAIR = 2 * RWKV_HEAD
    N_PAIRS = RWKV_HEADS // 2

    def pair_tiles(x):
        return [x[:, p * PAIR:(p + 1) * PAIR] for p in range(N_PAIRS)]

    bi = lax.broadcasted_iota(jnp.int32, (ROWS_A, ROWS_A), 0)
    bj = lax.broadcasted_iota(jnp.int32, (ROWS_A, ROWS_A), 1)
    same_chunk = (bi // CHUNK) == (bj // CHUNK)
    cum_mat = [(same_chunk & (bj <= bi)).astype(BF16), (same_chunk & (bj >= bi)).astype(BF16)]

    def pass1(c, carry):
        c0 = pl.multiple_of(c * ROWS_A, ROWS_A)
        rows = pl.ds(c0, ROWS_A)
        prev_row = pl.ds(jnp.maximum(c0 - 1, 0), 1)
        next_row = pl.ds(jnp.minimum(c0 + ROWS_A, nseq * n - 1), 1)
        seq_first = (c % blocks_per_seq) == 0
        seq_last = (c % blocks_per_seq) == blocks_per_seq - 1

        def neighbours(load):
            cur = load(rows)
            rid = lax.broadcasted_iota(jnp.int32, cur.shape, 0)
            before = jnp.where(seq_first, 0.0, load(prev_row))
            after = jnp.where(seq_last, 0.0, load(next_row))
            prev = jnp.where(rid == 0, before, pltpu.roll(cur, 1, 0))
            nxt = jnp.where(rid == ROWS_A - 1, after, pltpu.roll(cur, ROWS_A - 1, 0))
            return cur, prev, nxt

        z, zp, zn = neighbours(lambda rs: cv_ref[rs, D_CONV:2 * D_CONV] * cv_ref[rs, 2 * D_CONV:3 * D_CONV])
        cw = cw_ref[...]
        conv = cv_ref[rows, 0:D_CONV] * (zp * cw[0:1, :] + z * cw[1:2, :] + zn * cw[2:3, :])
        o_ref[rows, 0:D_CONV] = conv.astype(BF16)

        u, up, un = neighbours(lambda rs: um_ref[rs, :])
        u = u + mum_ref[...] * (0.5 * (up + un) - u)
        t, tp, tn = neighbours(lambda rs: ut_ref[rs, :])
        ut = t + mut_ref[...] * (0.5 * (tp + tn) - t)
        r = u[:, 0:D_RWKV]
        k = u[:, D_RWKV:2 * D_RWKV]
        v = u[:, 2 * D_RWKV:3 * D_RWKV]
        kk = k * kk_ref[...]
        kk = kk * lax.rsqrt(seg_sum(kk * kk) + 1e-12)
        for p, v_p in enumerate(pair_tiles(v)):
            v_s[p, rows, :] = v_p.astype(BF16)
        tw = jnp.tanh(ut)
        a_sum = jnp.zeros((ROWS_A, D_RWKV), F32)
        for d in range(2):
            zz = w0_ref[d:d + 1, :] + _dot_x3(tw, wup_ref[d])
            lw = -math.exp(-0.5) * jax.nn.sigmoid(zz)
            a = jax.nn.sigmoid(a0_ref[d:d + 1, :] + _dot_x3(ut, aup_ref[d]))
            a_sum = a_sum + a
            cum = _split_sum(lw, 3, lambda part: jnp.dot(cum_mat[d], part, preferred_element_type=F32))
            e_out = jnp.exp(-cum)
            kd = k * (1.0 + (a - 1.0) * ka_ref[...])
            scaled = ((al_s, -kk * jnp.exp(cum - lw)), (be_s, kk * a * e_out), (kd_s, kd * e_out),
                      (r_s, r * jnp.exp(cum)))
            for ref, val in scaled:
                for p, val_p in enumerate(pair_tiles(val)):
                    ref[d, p, rows, :] = val_p.astype(BF16)
            for j in range(per_block):
                last = j * CHUNK + (CHUNK - 1 if d == 0 else 0)
                for p, dec_p in enumerate(pair_tiles(jnp.exp(cum[last:last + 1, :]))):
                    p_s[d, c * per_block + j, p:p + 1, :] = dec_p
        kd_sum = k * (2.0 + (a_sum - 2.0) * ka_ref[...])
        bon_s[rows, :] = seg_sum(r * kd_sum * rk_ref[...]) * v
        g_s[rows, :] = _dot_x3(jax.nn.sigmoid(ut), gup_ref[...])
        y_s[rows, :] = jnp.zeros((ROWS_A, D_RWKV), F32)
        return carry

    lax.fori_loop(0, n_blocks, pass1, 0)

    if manual_fetch:
        @pl.when(step + 1 < pl.num_programs(0))
        def _():
            for cp in input_copies(step + 1):
                cp.start()

    for s in range(nseq):
        for d in range(2):
            for p in range(N_PAIRS):
                if has_s0:
                    s_scr[s, d, p] = jnp.concatenate([s0_ref[d, 2 * p], s0_ref[d, 2 * p + 1]], axis=1)
                else:
                    s_scr[s, d, p] = jnp.zeros((RWKV_HEAD, PAIR), F32)

    ri = lax.broadcasted_iota(jnp.int32, (CHUNK, PAIR), 0)
    lane = lax.broadcasted_iota(jnp.int32, (CHUNK, PAIR), 1)
    ci = lane % CHUNK
    low_half = lane < RWKV_HEAD
    eye = (ci == ri).astype(F32)
    diag_blk = (ri // SUB) == (ci // SUB)
    n_rounds = int(math.log2(SUB)) - 1

    def bd(x):
        zero = jnp.zeros_like(x)
        return jnp.concatenate([jnp.where(low_half, x, zero), jnp.where(low_half, zero, x)], axis=0)

    def pair_mm(a, b):
        return _mm(a, bd(b.astype(BF16)))

    def pair_mm_nt(a, b):
        return _mm(a, bd(b.astype(BF16)), _NT)

    def chunk_step(c, carry):
        chains = []
        for s, d in ((s, d) for s in range(nseq) for d in range(2)):
            cc = s * n_chunks + (c if d == 0 else n_chunks - 1 - c)
            rows = pl.ds(pl.multiple_of(cc * CHUNK, CHUNK), CHUNK)
            p_tile = p_s[d, cc]
            for p in range(N_PAIRS):
                chains.append(dict(
                    s=s, d=d, p=p, rows=rows, dec=p_tile[p:p + 1, :],
                    strict=(ci < ri) if d == 0 else (ci > ri), incl=(ci <= ri) if d == 0 else (ci >= ri),
                    al=al_s[d, p, rows, :], be=be_s[d, p, rows, :], kd=kd_s[d, p, rows, :],
                    r=r_s[d, p, rows, :], v=v_s[p, rows, :]))

        def each(fn, *lists):
            return [fn(*args) for args in zip(chains, *lists)]

        alr = each(lambda ch: jnp.concatenate([ch['al'], ch['r']], axis=0))
        g_be = each(lambda ch, m: pair_mm_nt(m, ch['be']), alr)
        g_kd = each(lambda ch, m: pair_mm_nt(m, ch['kd']), alr)
        a = each(lambda ch, m: jnp.where(ch['strict'], m[0:CHUNK], 0.0), g_be)
        a_ak = each(lambda ch, m: jnp.where(ch['strict'], m[0:CHUNK], 0.0), g_kd)
        q = each(lambda ch, mb, mk: jnp.concatenate(
            [jnp.where(ch['incl'], mb[CHUNK:2 * CHUNK], 0.0), jnp.where(ch['incl'], mk[CHUNK:2 * CHUNK], 0.0)],
            axis=1).astype(BF16), g_be, g_kd)
        akv = each(lambda ch, m: pair_mm(m, ch['v']), a_ak)
        xs = [jnp.where(diag_blk, m, 0.0) for m in a]
        low = [jnp.where(diag_blk, 0.0, m) for m in a]
        ts = [eye + x for x in xs]
        xs = [pair_mm(x, x) for x in xs]
        for _ in range(n_rounds - 1):
            z = [pair_mm(jnp.concatenate([x, t], axis=0), x) for x, t in zip(xs, ts)]
            xs = [m[0:CHUNK] for m in z]
            ts = [t + m[CHUNK:2 * CHUNK] for t, m in zip(ts, z)]
        ts = [t + pair_mm(t, x) for t, x in zip(ts, xs)]
        ms = [pair_mm(t, m) for t, m in zip(ts, low)]
        sol = each(lambda ch, t, m: _mm(t, jnp.concatenate([bd(m.astype(BF16)), bd(ch['al'])], axis=1)),
                   ts, akv)
        m2 = [pair_mm(m, m) for m in ms]
        ims = [eye + m for m in ms]
        ns = [im + pair_mm(im, mm2) for im, mm2 in zip(ims, m2)]
        w = [_mm(nn, jnp.concatenate([bd(x[:, 0:PAIR].astype(BF16)), bd(x[:, PAIR:2 * PAIR].astype(BF16))],
                                     axis=1)) for nn, x in zip(ns, sol)]
        st = each(lambda ch: s_scr[ch['s'], ch['d'], ch['p']])
        st_bd = [bd(t.astype(BF16)) for t in st]
        u = [x[:, 0:PAIR] + _mm(x[:, PAIR:2 * PAIR], t, _NT) for x, t in zip(w, st_bd)]
        ub = [x.astype(BF16) for x in u]
        ys = each(lambda ch, t, qq, uu: _mm(ch['r'], t, _NT)
                  + _mm(qq, jnp.concatenate([bd(uu), bd(ch['v'])], axis=0)), st_bd, q, ub)
        full = each(lambda ch, uu: _mm(jnp.concatenate([uu, ch['v']], axis=0),
                                       jnp.concatenate([ch['be'], ch['kd']], axis=0), _TN), ub)
        s_new = each(lambda ch, t, m: (t + jnp.where(low_half, m[0:RWKV_HEAD], m[RWKV_HEAD:PAIR])) * ch['dec'],
                     st, full)
        for ch, t in zip(chains, s_new):
            s_scr[ch['s'], ch['d'], ch['p']] = t
        for ch, y in zip(chains, ys):
            y_s[ch['rows'], ch['p'] * PAIR:(ch['p'] + 1) * PAIR] += y
        return carry

    lax.fori_loop(0, n_chunks, chunk_step, 0)

    if emit_state:
        for s in range(nseq):
            for d in range(2):
                for p in range(N_PAIRS):
                    so_ref[s, d, 2 * p] = s_scr[s, d, p][:, 0:RWKV_HEAD]
                    so_ref[s, d, 2 * p + 1] = s_scr[s, d, p][:, RWKV_HEAD:PAIR]

    def pass3(c, carry):
        rows = pl.ds(pl.multiple_of(c * ROWS_A, ROWS_A), ROWS_A)
        y = y_s[rows, :]
        yc = y - seg_sum(y) * (1.0 / RWKV_HEAD)
        var = seg_sum(yc * yc) * (1.0 / RWKV_HEAD)
        yn = yc * lax.rsqrt(var + GN_EPS) * lng_ref[...] + lnb_ref[...]
        o_ref[rows, D_CONV:D_CONV + D_RWKV] = ((yn + bon_s[rows, :]) * g_s[rows, :]).astype(BF16)
        return carry

    lax.fori_loop(0, n_blocks, pass3, 0)


def _pad_rows(w, lo):
    return jnp.pad(w, [(0, 0)] * (w.ndim - 2) + [(lo, D_TAIL - lo - w.shape[-2]), (0, 0)])


def _mix_call(proj, tail, P, l, n, nseq, s0=None, emit_state=False):
    T = proj.shape[0]
    B = T // n
    rows = nseq * n
    assert s0 is None or nseq == 1
    full = lambda shape: pl.BlockSpec(shape, lambda b: (0,) * len(shape))
    lanes = lambda w: -(-w // 128) * 128
    in_bytes = rows * (3 * D_CONV + 3 * D_RWKV + lanes(D_TAIL)) * 4
    scratch_bytes = rows * (4 * 2 * D_RWKV * 2 + D_RWKV * 2 + 3 * D_RWKV * 4)
    out_bytes = 2 * rows * (D_CONV + D_RWKV) * 2
    temporaries = 12 * 1024 * 1024
    manual_fetch = 2 * in_bytes + scratch_bytes + out_bytes + temporaries > VMEM_LIMIT_BYTES
    if manual_fetch:
        in_specs = [pl.BlockSpec(memory_space=pl.ANY), pl.BlockSpec(memory_space=pl.ANY)]
        token_args = [proj, tail]
    else:
        in_specs = [
            pl.BlockSpec((rows, 3 * D_CONV), lambda b: (b, CONV_COL // (3 * D_CONV))),
            pl.BlockSpec((rows, 3 * D_RWKV), lambda b: (b, RWKV_COL // (3 * D_RWKV))),
            pl.BlockSpec((rows, D_TAIL), lambda b: (b, 0)),
        ]
        token_args = [proj, proj, tail]
    in_specs += [
        full((1, 3 * D_RWKV)), full((1, D_TAIL)), full((3, D_CONV)),
        full((2, D_RWKV)), full((2, D_TAIL, D_RWKV)), full((2, D_RWKV)), full((2, D_TAIL, D_RWKV)),
        full((D_TAIL, D_RWKV)), full((1, D_RWKV)), full((1, D_RWKV)), full((1, D_RWKV)),
        full((1, D_RWKV)), full((1, D_RWKV)), full((D_RWKV, D_RWKV)),
    ]
    head = jnp.arange(D_RWKV) // RWKV_HEAD
    seg = (head[:, None] == head[None, :]).astype(BF16)
    mu = P['rwkv_mu'][l]
    row = lambda a: a.reshape(1, -1)
    args = token_args + [row(mu[:3 * D_RWKV]), row(mu[3 * D_RWKV:]), P['conv_w'][l],
            P['rwkv_w0'][l], _pad_rows(P['rwkv_w_up'][l], 0), P['rwkv_a0'][l],
            _pad_rows(P['rwkv_a_up'][l], DECAY_RANK), _pad_rows(P['rwkv_g_up'][l], DECAY_RANK + A_RANK),
            row(P['rwkv_k_k'][l]), row(P['rwkv_k_a'][l]), row(P['rwkv_r_k'][l]),
            row(P['rwkv_ln_g'][l]), row(P['rwkv_ln_b'][l]), seg]
    if s0 is not None:
        in_specs.append(pl.BlockSpec((None, None, 2, RWKV_HEADS, RWKV_HEAD, RWKV_HEAD),
                                     lambda b: (b, l, 0, 0, 0, 0)))
        args.append(s0)
    out_specs = [pl.BlockSpec((rows, D_CONV + D_RWKV), lambda b: (b, 0))]
    out_shape = [jax.ShapeDtypeStruct((T, D_CONV + D_RWKV), BF16)]
    state_shape = (2, RWKV_HEADS, RWKV_HEAD, RWKV_HEAD)
    if emit_state:
        out_specs.append(pl.BlockSpec((nseq,) + state_shape, lambda b: (b, 0, 0, 0, 0)))
        out_shape.append(jax.ShapeDtypeStruct((B,) + state_shape, F32))
    pair = 2 * RWKV_HEAD
    n_pairs = RWKV_HEADS // 2
    per_pair = lambda: pltpu.VMEM((2, n_pairs, rows, pair), BF16)
    scratch_shapes = [per_pair(), per_pair(), per_pair(), per_pair(),
                      pltpu.VMEM((n_pairs, rows, pair), BF16),
                      pltpu.VMEM((2, rows // CHUNK, n_pairs, pair), F32),
                      pltpu.VMEM((rows, D_RWKV), F32), pltpu.VMEM((rows, D_RWKV), F32),
                      pltpu.VMEM((rows, D_RWKV), F32),
                      pltpu.VMEM((nseq, 2, n_pairs, RWKV_HEAD, pair), F32)]
    if manual_fetch:
        scratch_shapes += [pltpu.VMEM((rows, 3 * D_CONV), F32), pltpu.VMEM((rows, 3 * D_RWKV), F32),
                           pltpu.VMEM((rows, D_TAIL), F32), pltpu.SemaphoreType.DMA((3,))]
    semantics = ("arbitrary",) if manual_fetch else ("parallel",)
    return pl.pallas_call(
        functools.partial(_mix_kernel, n=n, nseq=nseq, has_s0=s0 is not None, emit_state=emit_state,
                          manual_fetch=manual_fetch),
        grid=(B // nseq,),
        in_specs=in_specs,
        out_specs=out_specs,
        out_shape=out_shape,
        scratch_shapes=scratch_shapes,
        compiler_params=pltpu.CompilerParams(dimension_semantics=semantics,
                                             vmem_limit_bytes=VMEM_LIMIT_BYTES),
        name="conv_rwkv",
    )(*args)


def _residual_copy(x_hbm, x_buf, sem):
    rows = x_buf.shape[0]
    start = pl.multiple_of(pl.program_id(0) * rows, rows)
    return pltpu.make_async_copy(x_hbm.at[pl.ds(start, rows), :], x_buf, sem)


def _outproj_kernel(at_ref, cr_ref, w_ref, x_hbm, ga_ref, shf_ref, scf_ref, g1_ref, g2_ref,
                    x1_ref, h2_ref, x_ref, x_sem):
    kt = pl.program_id(1)

    @pl.when(kt == 0)
    def _():
        _residual_copy(x_hbm, x_ref, x_sem).start()
        x1_ref[...] = jnp.zeros(x1_ref.shape, F32)

    lhs = jnp.where(kt < 2, at_ref[...], cr_ref[...])
    x1_ref[...] += jnp.dot(lhs, w_ref[...].astype(BF16), preferred_element_type=F32)

    @pl.when(kt == pl.num_programs(1) - 1)
    def _():
        _residual_copy(x_hbm, x_ref, x_sem).wait()
        x1 = x_ref[...] + ga_ref[...] * _rms(x1_ref[...], g1_ref[...])
        x1_ref[...] = x1
        h2_ref[...] = (_rms(x1, g2_ref[...]) * (1.0 + scf_ref[...]) + shf_ref[...]).astype(BF16)


def _outproj_call(attn, cr, x, mod3, ng3, w_out, l, row_fn, tm):
    T = x.shape[0]
    tk = 512
    return pl.pallas_call(
        _outproj_kernel,
        grid=(T // tm, D_MODEL // tk),
        in_specs=[
            pl.BlockSpec((tm, tk), lambda i, k: (i, jnp.minimum(k, 1))),
            pl.BlockSpec((tm, tk), lambda i, k: (i, jnp.maximum(k - 2, 0))),
            pl.BlockSpec((None, tk, D_MODEL), lambda i, k: (l, k, 0)),
            pl.BlockSpec(memory_space=pl.ANY),
            _mod_spec(row_fn, 2), _mod_spec(row_fn, 3), _mod_spec(row_fn, 4),
            _ng_spec(l, 1), _ng_spec(l, 2),
        ],
        out_specs=[
            pl.BlockSpec((tm, D_MODEL), lambda i, k: (i, 0)),
            pl.BlockSpec((tm, D_MODEL), lambda i, k: (i, 0)),
        ],
        out_shape=[jax.ShapeDtypeStruct((T, D_MODEL), F32), jax.ShapeDtypeStruct((T, D_MODEL), BF16)],
        scratch_shapes=[pltpu.VMEM((tm, D_MODEL), F32), pltpu.SemaphoreType.DMA(())],
        compiler_params=_params(2),
        name="out_proj",
    )(attn, cr, w_out, x, mod3, mod3, mod3, ng3, ng3)


def _ffn_kernel(h_ref, x1_hbm, wg_ref, wu_ref, wd_ref, gf_ref, g3_ref, o_ref, x1_ref, x1_sem):
    f = pl.program_id(1)

    @pl.when(f == 0)
    def _():
        _residual_copy(x1_hbm, x1_ref, x1_sem).start()
        o_ref[...] = jnp.zeros(o_ref.shape, F32)

    h = h_ref[...]
    gate = _dot(h, wg_ref[...])
    up = _dot(h, wu_ref[...])
    o_ref[...] += _dot(gate * jax.nn.sigmoid(gate) * up, wd_ref[...])

    @pl.when(f == pl.num_programs(1) - 1)
    def _():
        _residual_copy(x1_hbm, x1_ref, x1_sem).wait()
        o_ref[...] = x1_ref[...] + gf_ref[...] * _rms(o_ref[...], g3_ref[...])


def _ffn_call(h2, x1, mod3, ng3, w_gate, w_up, w_down, l, row_fn, tm):
    T = x1.shape[0]
    tf = 256
    return pl.pallas_call(
        _ffn_kernel,
        grid=(T // tm, D_FF // tf),
        in_specs=[
            pl.BlockSpec((tm, D_MODEL), lambda i, f: (i, 0)),
            pl.BlockSpec(memory_space=pl.ANY),
            pl.BlockSpec((None, D_MODEL, tf), lambda i, f: (l, 0, f)),
            pl.BlockSpec((None, D_MODEL, tf), lambda i, f: (l, 0, f)),
            pl.BlockSpec((None, tf, D_MODEL), lambda i, f: (l, f, 0)),
            _mod_spec(row_fn, 5),
            _ng_spec(l, 3),
        ],
        out_specs=pl.BlockSpec((tm, D_MODEL), lambda i, f: (i, 0)),
        out_shape=jax.ShapeDtypeStruct((T, D_MODEL), F32),
        scratch_shapes=[pltpu.VMEM((tm, D_MODEL), F32), pltpu.SemaphoreType.DMA(())],
        compiler_params=_params(2),
        name="ffn",
    )(h2, x1, w_gate, w_up, w_down, mod3, ng3)


def _rope_tables(n):
    rows = n // GRID_W
    row = jnp.repeat(jnp.arange(rows), GRID_W).astype(F32)
    col = jnp.tile(jnp.arange(GRID_W), rows).astype(F32)
    half = HEAD_DIM // 2
    inv = 1.0 / (ROPE_THETA ** (jnp.arange(0, half, 2, dtype=F32) / half))
    ar = row[:, None] * inv
    ac = col[:, None] * inv
    ang = jnp.concatenate([ar, ar, ac, ac], axis=-1)
    return jnp.cos(ang), jnp.sin(ang)


def _layer(x, mod3, ng3, P, l, n, row_fn_of_tm, rope_tabs=None, cache=None, s0=None, emit=False):
    proj, tail = _inproj_call(x, mod3, ng3, P['w_in_t'], P['w_tail_t'], l, row_fn_of_tm(1024), 2048)
    attn_out = _attn_call(proj, P['q_norm'][l].reshape(1, -1), P['k_norm'][l].reshape(1, -1), l, n,
                          rope_tabs=rope_tabs, cache=cache, emit_kv=emit)
    mix_out = _mix_call(proj, tail, P, l, n, 1 if s0 is not None else 2, s0=s0, emit_state=emit)
    attn = attn_out[0]
    cr = mix_out[0]
    x1, h2 = _outproj_call(attn, cr, x, mod3, ng3, P['w_out'], l, row_fn_of_tm(1024), 1024)
    x2 = _ffn_call(h2, x1, mod3, ng3, P['w_gate'], P['w_up'], P['w_down'], l, row_fn_of_tm(1024), 1024)
    if emit:
        return x2, attn_out[1], attn_out[2], mix_out[1]
    return x2


def kernel(x_prompt, x_sample, cache_k, cache_v, state_rwkv, c, c_ctx, w_ada, b_ada, norm_g, w_in, q_norm, k_norm, conv_w, rwkv_mu, rwkv_w0, rwkv_w_up, rwkv_a0, rwkv_a_up, rwkv_g_up, rwkv_k_k, rwkv_k_a, rwkv_r_k, rwkv_ln_g, rwkv_ln_b, w_out, w_gate, w_up, w_down):
    P = {'q_norm': q_norm, 'k_norm': k_norm, 'conv_w': conv_w, 'rwkv_mu': rwkv_mu,
         'rwkv_w0': rwkv_w0, 'rwkv_w_up': rwkv_w_up, 'rwkv_a0': rwkv_a0, 'rwkv_a_up': rwkv_a_up,
         'rwkv_g_up': rwkv_g_up, 'rwkv_k_k': rwkv_k_k, 'rwkv_k_a': rwkv_k_a, 'rwkv_r_k': rwkv_r_k,
         'rwkv_ln_g': rwkv_ln_g, 'rwkv_ln_b': rwkv_ln_b, 'w_out': w_out,
         'w_gate': w_gate, 'w_up': w_up, 'w_down': w_down}
    w_in_t = jnp.swapaxes(w_in, 1, 2)
    P['w_in_t'] = w_in_t
    P['w_tail_t'] = w_in_t[:, D_MAIN:, :]
    batch, seq, _ = x_prompt.shape
    dec_batch, dec_seq, _ = x_sample.shape

    cc = jnp.zeros((8, D_MODEL), F32).at[0].set(c_ctx).at[1:1 + dec_batch].set(c)
    mod3 = _ada_call(cc, w_ada, b_ada).reshape(DEPTH * 8, 1, N_MOD * D_MODEL)
    ng3 = norm_g.reshape(DEPTH * 4, 1, D_MODEL)

    ck = cache_k.reshape(dec_batch, DEPTH, PAST_LEN, D_KV)
    cv = cache_v.reshape(dec_batch, DEPTH, PAST_LEN, D_KV)
    rope_tabs = _rope_tables(dec_seq)

    xp = x_prompt.reshape(batch * seq, D_MODEL)
    xs = x_sample.reshape(dec_batch * dec_seq, D_MODEL)
    ks, vs, ss = [], [], []
    for l in range(DEPTH):
        ctx_row = lambda tm, l=l: (lambda i: l * 8)
        smp_row = lambda tm, l=l: (lambda i: l * 8 + 1 + (i * tm) // dec_seq)
        xp, k_l, v_l, s_l = _layer(xp, mod3, ng3, P, l, seq, ctx_row, emit=True)
        ks.append(k_l.reshape(batch, seq, N_KV_HEADS, HEAD_DIM))
        vs.append(v_l.reshape(batch, seq, N_KV_HEADS, HEAD_DIM))
        ss.append(s_l)
        xs = _layer(xs, mod3, ng3, P, l, dec_seq, smp_row, rope_tabs=rope_tabs, cache=(ck, cv),
                    s0=state_rwkv)
    return (xp.reshape(batch, seq, D_MODEL), xs.reshape(dec_batch, dec_seq, D_MODEL),
            jnp.stack(ks, axis=1), jnp.stack(vs, axis=1), jnp.stack(ss, axis=1))
```

```python
import functools
import math

import jax
import jax.numpy as jnp
from jax import lax
from jax.experimental import pallas as pl
from jax.experimental.pallas import tpu as pltpu

D_MODEL = 2048
DEPTH = 2
GRID_W = 64
HEAD_DIM = 128
N_HEADS = 8
N_KV_HEADS = 2
GROUP = N_HEADS // N_KV_HEADS
D_ATTN = N_HEADS * HEAD_DIM
D_KV = N_KV_HEADS * HEAD_DIM
ROPE_THETA = 10000.0
D_CONV = 512
D_RWKV = 512
RWKV_HEAD = 64
RWKV_HEADS = D_RWKV // RWKV_HEAD
DECAY_RANK = 32
A_RANK = 32
GATE_RANK = 96
D_TAIL = DECAY_RANK + A_RANK + GATE_RANK
D_IN = D_ATTN + 2 * D_KV + 3 * D_CONV + 3 * D_RWKV + D_TAIL
D_MAIN = D_IN - D_TAIL
D_FF = 5632
N_MOD = 6
EPS = 1e-6
GN_EPS = 64e-5
PAST_LEN = 256

VMEM_LIMIT_BYTES = 56 * 1024 * 1024
CHUNK = 64
BF16 = jnp.bfloat16
F32 = jnp.float32


def _params(n_grid, **kw):
    sem = ("parallel",) + ("arbitrary",) * (n_grid - 1)
    return pltpu.CompilerParams(dimension_semantics=sem, vmem_limit_bytes=VMEM_LIMIT_BYTES, **kw)


def _rms(x, g):
    return x * lax.rsqrt(jnp.mean(x * x, axis=-1, keepdims=True) + EPS) * g


def _dot(a, b):
    return jnp.dot(a.astype(BF16), b.astype(BF16), preferred_element_type=F32)


_NN = (((1,), (0,)), ((), ()))
_NT = (((1,), (1,)), ((), ()))
_TN = (((0,), (0,)), ((), ()))


def _mm(a, b, dims=_NN):
    return lax.dot_general(a.astype(BF16), b.astype(BF16), dims, preferred_element_type=F32)


def _ada_kernel(c_ref, w_ref, b_ref, o_ref):
    c = c_ref[...]
    s = c * jax.nn.sigmoid(c)
    o_ref[...] = _dot(s, w_ref[...]) + b_ref[...]


def _ada_call(cc, w_ada, b_ada):
    tn = 512
    n_out = N_MOD * D_MODEL
    return pl.pallas_call(
        _ada_kernel,
        grid=(DEPTH, n_out // tn),
        in_specs=[
            pl.BlockSpec((8, D_MODEL), lambda l, n: (0, 0)),
            pl.BlockSpec((None, D_MODEL, tn), lambda l, n: (l, 0, n)),
            pl.BlockSpec((None, 1, tn), lambda l, n: (l, 0, n)),
        ],
        out_specs=pl.BlockSpec((None, 8, tn), lambda l, n: (l, 0, n)),
        out_shape=jax.ShapeDtypeStruct((DEPTH, 8, n_out), F32),
        compiler_params=_params(2),
        name="adaln",
    )(cc, w_ada, b_ada.reshape(DEPTH, 1, n_out))


def _mod_spec(row_fn, chunk):
    return pl.BlockSpec((None, 1, D_MODEL), lambda i, *_: (row_fn(i), 0, chunk))


def _ng_spec(l, j):
    return pl.BlockSpec((None, 1, D_MODEL), lambda i, *_: (l * 4 + j, 0, 0))


INPROJ_NORM_ROWS = 256
INPROJ_VMEM_LIMIT_BYTES = 60 * 1024 * 1024


def _row_tile_copy(x_hbm, x_buf, sem, tile):
    rows = x_buf.shape[0]
    return pltpu.make_async_copy(x_hbm.at[pl.ds(pl.multiple_of(tile * rows, rows), rows), :], x_buf, sem)


def _inproj_kernel(x_hbm, sha_ref, sca_ref, shb_ref, scb_ref, g_ref, w_ref, wt_ref, o_ref, ot_ref,
                   x_buf, h_scr, x_sem):
    i = pl.program_id(0)

    @pl.when(pl.program_id(1) == 0)
    def _():
        @pl.when(i == 0)
        def _():
            _row_tile_copy(x_hbm, x_buf, x_sem, 0).start()

        _row_tile_copy(x_hbm, x_buf, x_sem, i).wait()
        half = x_buf.shape[0] // 2
        for r0 in range(0, x_buf.shape[0], INPROJ_NORM_ROWS):
            rs = slice(r0, r0 + INPROJ_NORM_ROWS)
            sh_ref, sc_ref = (sha_ref, sca_ref) if r0 < half else (shb_ref, scb_ref)
            h = _rms(x_buf[rs, :], g_ref[...]) * (1.0 + sc_ref[...]) + sh_ref[...]
            h_scr[rs, :] = h.astype(BF16)

        @pl.when(i + 1 < pl.num_programs(0))
        def _():
            _row_tile_copy(x_hbm, x_buf, x_sem, i + 1).start()

        ot_ref[...] = _mm(h_scr[...], wt_ref[...], _NT)

    o_ref[...] = _mm(h_scr[...], w_ref[...], _NT)


def _inproj_call(x, mod3, ng3, w_in_t, w_tail_t, l, row_fn_half, tm):
    T = x.shape[0]
    tn = 512
    seg = lambda j: (lambda i: row_fn_half(2 * i + j))
    return pl.pallas_call(
        _inproj_kernel,
        grid=(T // tm, D_MAIN // tn),
        in_specs=[
            pl.BlockSpec(memory_space=pl.ANY),
            _mod_spec(seg(0), 0), _mod_spec(seg(0), 1),
            _mod_spec(seg(1), 0), _mod_spec(seg(1), 1),
            _ng_spec(l, 0),
            pl.BlockSpec((None, tn, D_MODEL), lambda i, n: (l, n, 0)),
            pl.BlockSpec((None, D_TAIL, D_MODEL), lambda i, n: (l, 0, 0)),
        ],
        out_specs=[
            pl.BlockSpec((tm, tn), lambda i, n: (i, n)),
            pl.BlockSpec((tm, D_TAIL), lambda i, n: (i, 0)),
        ],
        out_shape=[
            jax.ShapeDtypeStruct((T, D_MAIN), F32),
            jax.ShapeDtypeStruct((T, D_TAIL), F32),
        ],
        scratch_shapes=[pltpu.VMEM((tm, D_MODEL), F32), pltpu.VMEM((tm, D_MODEL), BF16),
                        pltpu.SemaphoreType.DMA(())],
        compiler_params=pltpu.CompilerParams(dimension_semantics=("arbitrary", "arbitrary"),
                                             vmem_limit_bytes=INPROJ_VMEM_LIMIT_BYTES),
        name="in_proj",
    )(x, mod3, mod3, mod3, mod3, ng3, w_in_t, w_tail_t)


def _rope(x, cos, sin):
    lane = lax.broadcasted_iota(jnp.int32, x.shape, 1)
    first = (lane % (HEAD_DIM // 2)) < (HEAD_DIM // 4)
    rot = jnp.where(first, -pltpu.roll(x, HEAD_DIM - HEAD_DIM // 4, 1), pltpu.roll(x, HEAD_DIM // 4, 1))
    return x * cos + rot * sin


def _attn_kernel(*refs, n, past, rope, emit_kv, qb):
    it = iter(refs)
    q_ref, k_ref, v_ref, qn_ref, kn_ref = (next(it) for _ in range(5))
    cos_ref = sin_ref = ck_ref = cv_ref = None
    if rope:
        cos_ref, sin_ref = next(it), next(it)
    if past:
        ck_ref, cv_ref = next(it), next(it)
    o_ref = next(it)
    if emit_kv:
        ko_ref, vo_ref = next(it), next(it)
    k_scr, v_scr = next(it), next(it)

    k = _rms(k_ref[...], kn_ref[...])
    v = v_ref[...]
    if emit_kv:
        ko_ref[...] = k
        vo_ref[...] = v
    if rope:
        k = _rope(k, cos_ref[...], sin_ref[...])
    k_scr[0:n, :] = k.astype(BF16)
    v_scr[0:n, :] = v.astype(BF16)
    if past:
        k_scr[n:n + past, :] = ck_ref[...].astype(BF16)
        v_scr[n:n + past, :] = cv_ref[...].astype(BF16)
    exp2_scale = HEAD_DIM ** -0.5 * math.log2(math.e)

    blocks_per_step = min(n // qb, 4)

    def block(b, carry):
        items = [(pl.ds(pl.multiple_of((b * blocks_per_step + j) * qb, qb), qb),
                  slice(g * HEAD_DIM, (g + 1) * HEAD_DIM))
                 for j in range(blocks_per_step) for g in range(GROUP)]

        def scores(item):
            rows, cols = item
            q = _rms(q_ref[rows, cols], qn_ref[...])
            if rope:
                q = _rope(q, cos_ref[rows, :], sin_ref[rows, :])
            return lax.dot_general(q.astype(BF16), k_scr[...], _NT, preferred_element_type=F32)

        def softmax(s):
            p = jnp.exp2((s - jnp.max(s, axis=-1, keepdims=True)) * exp2_scale)
            return p.astype(BF16), jnp.sum(p, axis=-1, keepdims=True)

        def values(item, pd):
            rows, cols = item
            o = jnp.dot(pd[0], v_scr[...], preferred_element_type=F32)
            o_ref[rows, cols] = (o / pd[1]).astype(BF16)

        if blocks_per_step == 1:
            ps = [softmax(s) for s in [scores(item) for item in items]]
            for item, pd in zip(items, ps):
                values(item, pd)
            return carry
        depth = 2
        ss, ps = {}, {}
        for i in range(len(items) + depth):
            if i < len(items):
                ss[i] = scores(items[i])
            if 1 <= i <= len(items):
                ps[i - 1] = softmax(ss.pop(i - 1))
            if i >= depth:
                values(items[i - depth], ps.pop(i - depth))
        return carry

    lax.fori_loop(0, n // (qb * blocks_per_step), block, 0)


def _attn_call(proj, qn, kn, l, n, rope_tabs=None, cache=None, emit_kv=False):
    T = proj.shape[0]
    B = T // n
    past = PAST_LEN if cache is not None else 0
    qw = GROUP * HEAD_DIM
    k_blk = D_ATTN // HEAD_DIM
    v_blk = (D_ATTN + D_KV) // HEAD_DIM
    in_specs = [
        pl.BlockSpec((n, qw), lambda b, h: (b, h)),
        pl.BlockSpec((n, HEAD_DIM), lambda b, h: (b, k_blk + h)),
        pl.BlockSpec((n, HEAD_DIM), lambda b, h: (b, v_blk + h)),
        pl.BlockSpec((1, HEAD_DIM), lambda b, h: (0, 0)),
        pl.BlockSpec((1, HEAD_DIM), lambda b, h: (0, 0)),
    ]
    args = [proj, proj, proj, qn, kn]
    if rope_tabs is not None:
        in_specs += [pl.BlockSpec((n, HEAD_DIM), lambda b, h: (0, 0))] * 2
        args += list(rope_tabs)
    if cache is not None:
        in_specs += [pl.BlockSpec((None, None, PAST_LEN, HEAD_DIM), lambda b, h: (b, l, 0, h))] * 2
        args += list(cache)
    out_specs = [pl.BlockSpec((n, qw), lambda b, h: (b, h))]
    out_shape = [jax.ShapeDtypeStruct((T, D_ATTN), BF16)]
    if emit_kv:
        out_specs += [pl.BlockSpec((None, n, HEAD_DIM), lambda b, h: (b, 0, h))] * 2
        out_shape += [jax.ShapeDtypeStruct((B, n, D_KV), F32)] * 2
    return pl.pallas_call(
        functools.partial(_attn_kernel, n=n, past=past, rope=rope_tabs is not None,
                          emit_kv=emit_kv, qb=256),
        grid=(B, N_KV_HEADS),
        in_specs=in_specs,
        out_specs=out_specs,
        out_shape=out_shape,
        scratch_shapes=[pltpu.VMEM((n + past, HEAD_DIM), BF16)] * 2,
        compiler_params=_params(2),
        name="attention",
    )(*args)


CONV_COL = D_ATTN + 2 * D_KV
RWKV_COL = CONV_COL + 3 * D_CONV
ROWS_A = 256
SUB = 16
CHUNKS_A = 2
assert CHUNK // SUB == 4


def _split2(x):
    hi = x.astype(BF16)
    return hi, (x - hi.astype(F32)).astype(BF16)


def _dot_x3(a, b):
    ah, al = _split2(a)
    bh, bl = _split2(b)
    mm = lambda p, q: jnp.dot(p, q, preferred_element_type=F32)
    return mm(ah, bh) + mm(ah, bl) + mm(al, bh)


def _split_sum(x, pieces, dot_piece):
    acc = None
    rest = x
    for _ in range(pieces):
        part = rest.astype(BF16)
        rest = rest - part.astype(F32)
        term = dot_piece(part)
        acc = term if acc is None else acc + term
    return acc


def _mix_kernel(*refs, n, nseq, has_s0, emit_state, manual_fetch):
    it = iter(refs)
    if manual_fetch:
        proj_hbm, tail_hbm = next(it), next(it)
    else:
        cv_ref, um_ref, ut_ref = next(it), next(it), next(it)
    (mum_ref, mut_ref, cw_ref, w0_ref, wup_ref, a0_ref, aup_ref, gup_ref, kk_ref, ka_ref,
     rk_ref, lng_ref, lnb_ref, seg_ref) = (next(it) for _ in range(14))
    s0_ref = next(it) if has_s0 else None
    o_ref = next(it)
    so_ref = next(it) if emit_state else None
    al_s, be_s, kd_s, r_s, v_s, p_s, bon_s, g_s, y_s, s_scr = (next(it) for _ in range(10))
    w2_s, w1_s, q_s = next(it), next(it), next(it)
    if manual_fetch:
        cv_ref, um_ref, ut_ref, in_sem = next(it), next(it), next(it), next(it)

        def input_copies(step):
            rows = pl.ds(pl.multiple_of(step * (nseq * n), nseq * n), nseq * n)
            return (
                pltpu.make_async_copy(proj_hbm.at[rows, pl.ds(CONV_COL, 3 * D_CONV)], cv_ref, in_sem.at[0]),
                pltpu.make_async_copy(proj_hbm.at[rows, pl.ds(RWKV_COL, 3 * D_RWKV)], um_ref, in_sem.at[1]),
                pltpu.make_async_copy(tail_hbm.at[rows, :], ut_ref, in_sem.at[2]))

        step = pl.program_id(0)

        @pl.when(step == 0)
        def _():
            for cp in input_copies(0):
                cp.start()

        for cp in input_copies(step):
            cp.wait()
    blocks_per_seq = n // ROWS_A
    n_blocks = nseq * blocks_per_seq
    n_chunks = n // CHUNK
    per_block = ROWS_A // CHUNK

    def seg_sum(x):
        return _split_sum(x, 2, lambda part: jnp.dot(part, seg_ref[...], preferred_element_type=F32))

    PAIR = 2 * RWKV_HEAD
    N_PAIRS = RWKV_HEADS // 2

    def pair_tiles(x):
        return [x[:, p * PAIR:(p + 1) * PAIR] for p in range(N_PAIRS)]

    bi = lax.broadcasted_iota(jnp.int32, (ROWS_A, ROWS_A), 0)
    bj = lax.broadcasted_iota(jnp.int32, (ROWS_A, ROWS_A), 1)
    same_chunk = (bi // CHUNK) == (bj // CHUNK)
    cum_mat = [(same_chunk & (bj <= bi)).astype(BF16), (same_chunk & (bj >= bi)).astype(BF16)]

    def pass1(c, carry):
        c0 = pl.multiple_of(c * ROWS_A, ROWS_A)
        rows = pl.ds(c0, ROWS_A)
        prev_row = pl.ds(jnp.maximum(c0 - 1, 0), 1)
        next_row = pl.ds(jnp.minimum(c0 + ROWS_A, nseq * n - 1), 1)
        seq_first = (c % blocks_per_seq) == 0
        seq_last = (c % blocks_per_seq) == blocks_per_seq - 1

        def neighbours(load):
            cur = load(rows)
            rid = lax.broadcasted_iota(jnp.int32, cur.shape, 0)
            before = jnp.where(seq_first, 0.0, load(prev_row))
            after = jnp.where(seq_last, 0.0, load(next_row))
            prev = jnp.where(rid == 0, before, pltpu.roll(cur, 1, 0))
            nxt = jnp.where(rid == ROWS_A - 1, after, pltpu.roll(cur, ROWS_A - 1, 0))
            return cur, prev, nxt

        z, zp, zn = neighbours(lambda rs: cv_ref[rs, D_CONV:2 * D_CONV] * cv_ref[rs, 2 * D_CONV:3 * D_CONV])
        cw = cw_ref[...]
        conv = cv_ref[rows, 0:D_CONV] * (zp * cw[0:1, :] + z * cw[1:2, :] + zn * cw[2:3, :])
        o_ref[rows, 0:D_CONV] = conv.astype(BF16)

        u, up, un = neighbours(lambda rs: um_ref[rs, :])
        u = u + mum_ref[...] * (0.5 * (up + un) - u)
        t, tp, tn = neighbours(lambda rs: ut_ref[rs, :])
        ut = t + mut_ref[...] * (0.5 * (tp + tn) - t)
        r = u[:, 0:D_RWKV]
        k = u[:, D_RWKV:2 * D_RWKV]
        v = u[:, 2 * D_RWKV:3 * D_RWKV]
        kk = k * kk_ref[...]
        kk = kk * lax.rsqrt(seg_sum(kk * kk) + 1e-12)
        for p, v_p in enumerate(pair_tiles(v)):
            v_s[p, rows, :] = v_p.astype(BF16)
        tw = jnp.tanh(ut)
        a_sum = jnp.zeros((ROWS_A, D_RWKV), F32)
        for d in range(2):
            zz = w0_ref[d:d + 1, :] + _dot_x3(tw, wup_ref[d])
            lw = -math.exp(-0.5) * jax.nn.sigmoid(zz)
            a = jax.nn.sigmoid(a0_ref[d:d + 1, :] + _dot_x3(ut, aup_ref[d]))
            a_sum = a_sum + a
            cum = _split_sum(lw, 3, lambda part: jnp.dot(cum_mat[d], part, preferred_element_type=F32))
            e_out = jnp.exp(-cum)
            kd = k * (1.0 + (a - 1.0) * ka_ref[...])
            scaled = ((al_s, -kk * jnp.exp(cum - lw)), (be_s, kk * a * e_out), (kd_s, kd * e_out),
                      (r_s, r * jnp.exp(cum)))
            for ref, val in scaled:
                for p, val_p in enumerate(pair_tiles(val)):
                    ref[d, p, rows, :] = val_p.astype(BF16)
            for j in range(per_block):
                last = j * CHUNK + (CHUNK - 1 if d == 0 else 0)
                for p, dec_p in enumerate(pair_tiles(jnp.exp(cum[last:last + 1, :]))):
                    p_s[d, c * per_block + j, p:p + 1, :] = dec_p
        kd_sum = k * (2.0 + (a_sum - 2.0) * ka_ref[...])
        bon_s[rows, :] = seg_sum(r * kd_sum * rk_ref[...]) * v
        g_s[rows, :] = _dot_x3(jax.nn.sigmoid(ut), gup_ref[...])
        y_s[rows, :] = jnp.zeros((ROWS_A, D_RWKV), F32)
        return carry

    lax.fori_loop(0, n_blocks, pass1, 0)

    if manual_fetch:
        @pl.when(step + 1 < pl.num_programs(0))
        def _():
            for cp in input_copies(step + 1):
                cp.start()

    for s in range(nseq):
        for d in range(2):
            for p in range(N_PAIRS):
                if has_s0:
                    s_scr[s, d, p] = jnp.concatenate([s0_ref[d, 2 * p], s0_ref[d, 2 * p + 1]], axis=1)
                else:
                    s_scr[s, d, p] = jnp.zeros((RWKV_HEAD, PAIR), F32)

    ri = lax.broadcasted_iota(jnp.int32, (CHUNK, PAIR), 0)
    lane = lax.broadcasted_iota(jnp.int32, (CHUNK, PAIR), 1)
    ci = lane % CHUNK
    low_half = lane < RWKV_HEAD
    eye = (ci == ri).astype(F32)
    diag_blk = (ri // SUB) == (ci // SUB)
    n_rounds = int(math.log2(SUB)) - 1

    def bd(x):
        zero = jnp.zeros_like(x)
        return jnp.concatenate([jnp.where(low_half, x, zero), jnp.where(low_half, zero, x)], axis=0)

    def pair_mm(a, b):
        return _mm(a, bd(b.astype(BF16)))

    def pair_mm_nt(a, b):
        return _mm(a, bd(b.astype(BF16)), _NT)

    def solve_step(c, carry):
        chains = []
        for s, d, j in ((s, d, j) for s in range(nseq) for d in range(2) for j in range(CHUNKS_A)):
            cc = s * n_chunks + c * CHUNKS_A + j
            rows = pl.ds(pl.multiple_of(cc * CHUNK, CHUNK), CHUNK)
            for p in range(N_PAIRS):
                chains.append(dict(
                    d=d, p=p, rows=rows,
                    strict=(ci < ri) if d == 0 else (ci > ri), incl=(ci <= ri) if d == 0 else (ci >= ri),
                    al=al_s[d, p, rows, :], be=be_s[d, p, rows, :], kd=kd_s[d, p, rows, :],
                    r=r_s[d, p, rows, :], v=v_s[p, rows, :]))

        def each(fn, *lists):
            return [fn(*args) for args in zip(chains, *lists)]

        alr = each(lambda ch: jnp.concatenate([ch['al'], ch['r']], axis=0))
        g_be = each(lambda ch, m: pair_mm_nt(m, ch['be']), alr)
        g_kd = each(lambda ch, m: pair_mm_nt(m, ch['kd']), alr)
        a = each(lambda ch, m: jnp.where(ch['strict'], m[0:CHUNK], 0.0), g_be)
        a_ak = each(lambda ch, m: jnp.where(ch['strict'], m[0:CHUNK], 0.0), g_kd)
        q = each(lambda ch, mb, mk: jnp.concatenate(
            [jnp.where(ch['incl'], mb[CHUNK:2 * CHUNK], 0.0), jnp.where(ch['incl'], mk[CHUNK:2 * CHUNK], 0.0)],
            axis=1).astype(BF16), g_be, g_kd)
        akv = each(lambda ch, m: pair_mm(m, ch['v']), a_ak)
        xs = [jnp.where(diag_blk, m, 0.0) for m in a]
        low = [jnp.where(diag_blk, 0.0, m) for m in a]
        ts = [eye + x for x in xs]
        xs = [pair_mm(x, x) for x in xs]
        for _ in range(n_rounds - 1):
            z = [pair_mm(jnp.concatenate([x, t], axis=0), x) for x, t in zip(xs, ts)]
            xs = [m[0:CHUNK] for m in z]
            ts = [t + m[CHUNK:2 * CHUNK] for t, m in zip(ts, z)]
        ts = [t + pair_mm(t, x) for t, x in zip(ts, xs)]
        ms = [pair_mm(t, m) for t, m in zip(ts, low)]
        sol = each(lambda ch, t, m: _mm(t, jnp.concatenate([bd(m.astype(BF16)), bd(ch['al'])], axis=1)),
                   ts, akv)
        m2 = [pair_mm(m, m) for m in ms]
        ims = [eye + m for m in ms]
        ns = [im + pair_mm(im, mm2) for im, mm2 in zip(ims, m2)]
        w = [_mm(nn, jnp.concatenate([bd(x[:, 0:PAIR].astype(BF16)), bd(x[:, PAIR:2 * PAIR].astype(BF16))],
                                     axis=1)) for nn, x in zip(ns, sol)]
        for ch, x, qq in zip(chains, w, q):
            w2_s[ch['d'], ch['p'], ch['rows'], :] = x[:, 0:PAIR]
            w1_s[ch['d'], ch['p'], ch['rows'], :] = x[:, PAIR:2 * PAIR].astype(BF16)
            q_s[ch['d'], ch['p'], ch['rows'], :] = qq
        return carry

    lax.fori_loop(0, n_chunks // CHUNKS_A, solve_step, 0)

    def chunk_step(c, carry):
        chains = []
        for s, d in ((s, d) for s in range(nseq) for d in range(2)):
            cc = s * n_chunks + (c if d == 0 else n_chunks - 1 - c)
            rows = pl.ds(pl.multiple_of(cc * CHUNK, CHUNK), CHUNK)
            p_tile = p_s[d, cc]
            for p in range(N_PAIRS):
                chains.append(dict(
                    s=s, d=d, p=p, rows=rows, dec=p_tile[p:p + 1, :],
                    be=be_s[d, p, rows, :], kd=kd_s[d, p, rows, :], r=r_s[d, p, rows, :],
                    v=v_s[p, rows, :], w2=w2_s[d, p, rows, :], w1=w1_s[d, p, rows, :],
                    q=q_s[d, p, rows, :]))

        def each(fn, *lists):
            return [fn(*args) for args in zip(chains, *lists)]

        st = each(lambda ch: s_scr[ch['s'], ch['d'], ch['p']])
        st_bd = [bd(t.astype(BF16)) for t in st]
        u = each(lambda ch, t: ch['w2'] + _mm(ch['w1'], t, _NT), st_bd)
        ub = [x.astype(BF16) for x in u]
        ys = each(lambda ch, t, uu: _mm(ch['r'], t, _NT)
                  + _mm(ch['q'], jnp.concatenate([bd(uu), bd(ch['v'])], axis=0)), st_bd, ub)
        full = each(lambda ch, uu: _mm(jnp.concatenate([uu, ch['v']], axis=0),
                                       jnp.concatenate([ch['be'], ch['kd']], axis=0), _TN), ub)
        s_new = each(lambda ch, t, m: (t + jnp.where(low_half, m[0:RWKV_HEAD], m[RWKV_HEAD:PAIR])) * ch['dec'],
                     st, full)
        for ch, t in zip(chains, s_new):
            s_scr[ch['s'], ch['d'], ch['p']] = t
        for ch, y in zip(chains, ys):
            y_s[ch['rows'], ch['p'] * PAIR:(ch['p'] + 1) * PAIR] += y
        return carry

    lax.fori_loop(0, n_chunks, chunk_step, 0)

    if emit_state:
        for s in range(nseq):
            for d in range(2):
                for p in range(N_PAIRS):
                    so_ref[s, d, 2 * p] = s_scr[s, d, p][:, 0:RWKV_HEAD]
                    so_ref[s, d, 2 * p + 1] = s_scr[s, d, p][:, RWKV_HEAD:PAIR]

    def pass3(c, carry):
        rows = pl.ds(pl.multiple_of(c * ROWS_A, ROWS_A), ROWS_A)
        y = y_s[rows, :]
        yc = y - seg_sum(y) * (1.0 / RWKV_HEAD)
        var = seg_sum(yc * yc) * (1.0 / RWKV_HEAD)
        yn = yc * lax.rsqrt(var + GN_EPS) * lng_ref[...] + lnb_ref[...]
        o_ref[rows, D_CONV:D_CONV + D_RWKV] = ((yn + bon_s[rows, :]) * g_s[rows, :]).astype(BF16)
        return carry

    lax.fori_loop(0, n_blocks, pass3, 0)


def _pad_rows(w, lo):
    return jnp.pad(w, [(0, 0)] * (w.ndim - 2) + [(lo, D_TAIL - lo - w.shape[-2]), (0, 0)])


def _mix_call(proj, tail, P, l, n, nseq, s0=None, emit_state=False):
    T = proj.shape[0]
    B = T // n
    rows = nseq * n
    assert s0 is None or nseq == 1
    full = lambda shape: pl.BlockSpec(shape, lambda b: (0,) * len(shape))
    lanes = lambda w: -(-w // 128) * 128
    in_bytes = rows * (3 * D_CONV + 3 * D_RWKV + lanes(D_TAIL)) * 4
    scratch_bytes = rows * (4 * 2 * D_RWKV * 2 + D_RWKV * 2 + 3 * D_RWKV * 4
                            + 2 * D_RWKV * (4 + 2 + 2 * 2))
    out_bytes = 2 * rows * (D_CONV + D_RWKV) * 2
    temporaries = 12 * 1024 * 1024
    manual_fetch = 2 * in_bytes + scratch_bytes + out_bytes + temporaries > VMEM_LIMIT_BYTES
    if manual_fetch:
        in_specs = [pl.BlockSpec(memory_space=pl.ANY), pl.BlockSpec(memory_space=pl.ANY)]
        token_args = [proj, tail]
    else:
        in_specs = [
            pl.BlockSpec((rows, 3 * D_CONV), lambda b: (b, CONV_COL // (3 * D_CONV))),
            pl.BlockSpec((rows, 3 * D_RWKV), lambda b: (b, RWKV_COL // (3 * D_RWKV))),
            pl.BlockSpec((rows, D_TAIL), lambda b: (b, 0)),
        ]
        token_args = [proj, proj, tail]
    in_specs += [
        full((1, 3 * D_RWKV)), full((1, D_TAIL)), full((3, D_CONV)),
        full((2, D_RWKV)), full((2, D_TAIL, D_RWKV)), full((2, D_RWKV)), full((2, D_TAIL, D_RWKV)),
        full((D_TAIL, D_RWKV)), full((1, D_RWKV)), full((1, D_RWKV)), full((1, D_RWKV)),
        full((1, D_RWKV)), full((1, D_RWKV)), full((D_RWKV, D_RWKV)),
    ]
    head = jnp.arange(D_RWKV) // RWKV_HEAD
    seg = (head[:, None] == head[None, :]).astype(BF16)
    mu = P['rwkv_mu'][l]
    row = lambda a: a.reshape(1, -1)
    args = token_args + [row(mu[:3 * D_RWKV]), row(mu[3 * D_RWKV:]), P['conv_w'][l],
            P['rwkv_w0'][l], _pad_rows(P['rwkv_w_up'][l], 0), P['rwkv_a0'][l],
            _pad_rows(P['rwkv_a_up'][l], DECAY_RANK), _pad_rows(P['rwkv_g_up'][l], DECAY_RANK + A_RANK),
            row(P['rwkv_k_k'][l]), row(P['rwkv_k_a'][l]), row(P['rwkv_r_k'][l]),
            row(P['rwkv_ln_g'][l]), row(P['rwkv_ln_b'][l]), seg]
    if s0 is not None:
        in_specs.append(pl.BlockSpec((None, None, 2, RWKV_HEADS, RWKV_HEAD, RWKV_HEAD),
                                     lambda b: (b, l, 0, 0, 0, 0)))
        args.append(s0)
    out_specs = [pl.BlockSpec((rows, D_CONV + D_RWKV), lambda b: (b, 0))]
    out_shape = [jax.ShapeDtypeStruct((T, D_CONV + D_RWKV), BF16)]
    state_shape = (2, RWKV_HEADS, RWKV_HEAD, RWKV_HEAD)
    if emit_state:
        out_specs.append(pl.BlockSpec((nseq,) + state_shape, lambda b: (b, 0, 0, 0, 0)))
        out_shape.append(jax.ShapeDtypeStruct((B,) + state_shape, F32))
    pair = 2 * RWKV_HEAD
    n_pairs = RWKV_HEADS // 2
    per_pair = lambda: pltpu.VMEM((2, n_pairs, rows, pair), BF16)
    scratch_shapes = [per_pair(), per_pair(), per_pair(), per_pair(),
                      pltpu.VMEM((n_pairs, rows, pair), BF16),
                      pltpu.VMEM((2, rows // CHUNK, n_pairs, pair), F32),
                      pltpu.VMEM((rows, D_RWKV), F32), pltpu.VMEM((rows, D_RWKV), F32),
                      pltpu.VMEM((rows, D_RWKV), F32),
                      pltpu.VMEM((nseq, 2, n_pairs, RWKV_HEAD, pair), F32),
                      pltpu.VMEM((2, n_pairs, rows, pair), F32), pltpu.VMEM((2, n_pairs, rows, pair), BF16),
                      pltpu.VMEM((2, n_pairs, rows, 2 * pair), BF16)]
    if manual_fetch:
        scratch_shapes += [pltpu.VMEM((rows, 3 * D_CONV), F32), pltpu.VMEM((rows, 3 * D_RWKV), F32),
                           pltpu.VMEM((rows, D_TAIL), F32), pltpu.SemaphoreType.DMA((3,))]
    semantics = ("arbitrary",) if manual_fetch else ("parallel",)
    return pl.pallas_call(
        functools.partial(_mix_kernel, n=n, nseq=nseq, has_s0=s0 is not None, emit_state=emit_state,
                          manual_fetch=manual_fetch),
        grid=(B // nseq,),
        in_specs=in_specs,
        out_specs=out_specs,
        out_shape=out_shape,
        scratch_shapes=scratch_shapes,
        compiler_params=pltpu.CompilerParams(dimension_semantics=semantics,
                                             vmem_limit_bytes=VMEM_LIMIT_BYTES),
        name="conv_rwkv",
    )(*args)


def _residual_copy(x_hbm, x_buf, sem):
    rows = x_buf.shape[0]
    start = pl.multiple_of(pl.program_id(0) * rows, rows)
    return pltpu.make_async_copy(x_hbm.at[pl.ds(start, rows), :], x_buf, sem)


def _outproj_kernel(at_ref, cr_ref, w_ref, x_hbm, ga_ref, shf_ref, scf_ref, g1_ref, g2_ref,
                    x1_ref, h2_ref, x_ref, x_sem):
    kt = pl.program_id(1)

    @pl.when(kt == 0)
    def _():
        _residual_copy(x_hbm, x_ref, x_sem).start()
        x1_ref[...] = jnp.zeros(x1_ref.shape, F32)

    lhs = jnp.where(kt < 2, at_ref[...], cr_ref[...])
    x1_ref[...] += jnp.dot(lhs, w_ref[...].astype(BF16), preferred_element_type=F32)

    @pl.when(kt == pl.num_programs(1) - 1)
    def _():
        _residual_copy(x_hbm, x_ref, x_sem).wait()
        x1 = x_ref[...] + ga_ref[...] * _rms(x1_ref[...], g1_ref[...])
        x1_ref[...] = x1
        h2_ref[...] = (_rms(x1, g2_ref[...]) * (1.0 + scf_ref[...]) + shf_ref[...]).astype(BF16)


def _outproj_call(attn, cr, x, mod3, ng3, w_out, l, row_fn, tm):
    T = x.shape[0]
    tk = 512
    return pl.pallas_call(
        _outproj_kernel,
        grid=(T // tm, D_MODEL // tk),
        in_specs=[
            pl.BlockSpec((tm, tk), lambda i, k: (i, jnp.minimum(k, 1))),
            pl.BlockSpec((tm, tk), lambda i, k: (i, jnp.maximum(k - 2, 0))),
            pl.BlockSpec((None, tk, D_MODEL), lambda i, k: (l, k, 0)),
            pl.BlockSpec(memory_space=pl.ANY),
            _mod_spec(row_fn, 2), _mod_spec(row_fn, 3), _mod_spec(row_fn, 4),
            _ng_spec(l, 1), _ng_spec(l, 2),
        ],
        out_specs=[
            pl.BlockSpec((tm, D_MODEL), lambda i, k: (i, 0)),
            pl.BlockSpec((tm, D_MODEL), lambda i, k: (i, 0)),
        ],
        out_shape=[jax.ShapeDtypeStruct((T, D_MODEL), F32), jax.ShapeDtypeStruct((T, D_MODEL), BF16)],
        scratch_shapes=[pltpu.VMEM((tm, D_MODEL), F32), pltpu.SemaphoreType.DMA(())],
        compiler_params=_params(2),
        name="out_proj",
    )(attn, cr, w_out, x, mod3, mod3, mod3, ng3, ng3)


def _ffn_kernel(h_ref, x1_hbm, wg_ref, wu_ref, wd_ref, gf_ref, g3_ref, o_ref, x1_ref, x1_sem):
    f = pl.program_id(1)

    @pl.when(f == 0)
    def _():
        _residual_copy(x1_hbm, x1_ref, x1_sem).start()
        o_ref[...] = jnp.zeros(o_ref.shape, F32)

    h = h_ref[...]
    gate = _dot(h, wg_ref[...])
    up = _dot(h, wu_ref[...])
    o_ref[...] += _dot(gate * jax.nn.sigmoid(gate) * up, wd_ref[...])

    @pl.when(f == pl.num_programs(1) - 1)
    def _():
        _residual_copy(x1_hbm, x1_ref, x1_sem).wait()
        o_ref[...] = x1_ref[...] + gf_ref[...] * _rms(o_ref[...], g3_ref[...])


def _ffn_call(h2, x1, mod3, ng3, w_gate, w_up, w_down, l, row_fn, tm):
    T = x1.shape[0]
    tf = 256
    return pl.pallas_call(
        _ffn_kernel,
        grid=(T // tm, D_FF // tf),
        in_specs=[
            pl.BlockSpec((tm, D_MODEL), lambda i, f: (i, 0)),
            pl.BlockSpec(memory_space=pl.ANY),
            pl.BlockSpec((None, D_MODEL, tf), lambda i, f: (l, 0, f)),
            pl.BlockSpec((None, D_MODEL, tf), lambda i, f: (l, 0, f)),
            pl.BlockSpec((None, tf, D_MODEL), lambda i, f: (l, f, 0)),
            _mod_spec(row_fn, 5),
            _ng_spec(l, 3),
        ],
        out_specs=pl.BlockSpec((tm, D_MODEL), lambda i, f: (i, 0)),
        out_shape=jax.ShapeDtypeStruct((T, D_MODEL), F32),
        scratch_shapes=[pltpu.VMEM((tm, D_MODEL), F32), pltpu.SemaphoreType.DMA(())],
        compiler_params=_params(2),
        name="ffn",
    )(h2, x1, w_gate, w_up, w_down, mod3, ng3)


def _rope_tables(n):
    rows = n // GRID_W
    row = jnp.repeat(jnp.arange(rows), GRID_W).astype(F32)
    col = jnp.tile(jnp.arange(GRID_W), rows).astype(F32)
    half = HEAD_DIM // 2
    inv = 1.0 / (ROPE_THETA ** (jnp.arange(0, half, 2, dtype=F32) / half))
    ar = row[:, None] * inv
    ac = col[:, None] * inv
    ang = jnp.concatenate([ar, ar, ac, ac], axis=-1)
    return jnp.cos(ang), jnp.sin(ang)


def _layer(x, mod3, ng3, P, l, n, row_fn_of_tm, rope_tabs=None, cache=None, s0=None, emit=False):
    proj, tail = _inproj_call(x, mod3, ng3, P['w_in_t'], P['w_tail_t'], l, row_fn_of_tm(1024), 2048)
    attn_out = _attn_call(proj, P['q_norm'][l].reshape(1, -1), P['k_norm'][l].reshape(1, -1), l, n,
                          rope_tabs=rope_tabs, cache=cache, emit_kv=emit)
    mix_out = _mix_call(proj, tail, P, l, n, 1 if s0 is not None else 2, s0=s0, emit_state=emit)
    attn = attn_out[0]
    cr = mix_out[0]
    x1, h2 = _outproj_call(attn, cr, x, mod3, ng3, P['w_out'], l, row_fn_of_tm(1024), 1024)
    x2 = _ffn_call(h2, x1, mod3, ng3, P['w_gate'], P['w_up'], P['w_down'], l, row_fn_of_tm(1024), 1024)
    if emit:
        return x2, attn_out[1], attn_out[2], mix_out[1]
    return x2


def kernel(x_prompt, x_sample, cache_k, cache_v, state_rwkv, c, c_ctx, w_ada, b_ada, norm_g, w_in, q_norm, k_norm, conv_w, rwkv_mu, rwkv_w0, rwkv_w_up, rwkv_a0, rwkv_a_up, rwkv_g_up, rwkv_k_k, rwkv_k_a, rwkv_r_k, rwkv_ln_g, rwkv_ln_b, w_out, w_gate, w_up, w_down):
    P = {'q_norm': q_norm, 'k_norm': k_norm, 'conv_w': conv_w, 'rwkv_mu': rwkv_mu,
         'rwkv_w0': rwkv_w0, 'rwkv_w_up': rwkv_w_up, 'rwkv_a0': rwkv_a0, 'rwkv_a_up': rwkv_a_up,
         'rwkv_g_up': rwkv_g_up, 'rwkv_k_k': rwkv_k_k, 'rwkv_k_a': rwkv_k_a, 'rwkv_r_k': rwkv_r_k,
         'rwkv_ln_g': rwkv_ln_g, 'rwkv_ln_b': rwkv_ln_b, 'w_out': w_out,
         'w_gate': w_gate, 'w_up': w_up, 'w_down': w_down}
    w_in_t = jnp.swapaxes(w_in, 1, 2)
    P['w_in_t'] = w_in_t
    P['w_tail_t'] = w_in_t[:, D_MAIN:, :]
    batch, seq, _ = x_prompt.shape
    dec_batch, dec_seq, _ = x_sample.shape

    cc = jnp.zeros((8, D_MODEL), F32).at[0].set(c_ctx).at[1:1 + dec_batch].set(c)
    mod3 = _ada_call(cc, w_ada, b_ada).reshape(DEPTH * 8, 1, N_MOD * D_MODEL)
    ng3 = norm_g.reshape(DEPTH * 4, 1, D_MODEL)

    ck = cache_k.reshape(dec_batch, DEPTH, PAST_LEN, D_KV)
    cv = cache_v.reshape(dec_batch, DEPTH, PAST_LEN, D_KV)
    rope_tabs = _rope_tables(dec_seq)

    xp = x_prompt.reshape(batch * seq, D_MODEL)
    xs = x_sample.reshape(dec_batch * dec_seq, D_MODEL)
    ks, vs, ss = [], [], []
    for l in range(DEPTH):
        ctx_row = lambda tm, l=l: (lambda i: l * 8)
        smp_row = lambda tm, l=l: (lambda i: l * 8 + 1 + (i * tm) // dec_seq)
        xp, k_l, v_l, s_l = _layer(xp, mod3, ng3, P, l, seq, ctx_row, emit=True)
        ks.append(k_l.reshape(batch, seq, N_KV_HEADS, HEAD_DIM))
        vs.append(v_l.reshape(batch, seq, N_KV_HEADS, HEAD_DIM))
        ss.append(s_l)
        xs = _layer(xs, mod3, ng3, P, l, dec_seq, smp_row, rope_tabs=rope_tabs, cache=(ck, cv),
                    s0=state_rwkv)
    return (xp.reshape(batch, seq, D_MODEL), xs.reshape(dec_batch, dec_seq, D_MODEL),
            jnp.stack(ks, axis=1), jnp.stack(vs, axis=1), jnp.stack(ss, axis=1))
```

```python
import functools
import math

import jax
import jax.numpy as jnp
from jax import lax
from jax.experimental import pallas as pl
from jax.experimental.pallas import tpu as pltpu

D_MODEL = 2048
DEPTH = 2
GRID_W = 64
HEAD_DIM = 128
N_HEADS = 8
N_KV_HEADS = 2
GROUP = N_HEADS // N_KV_HEADS
D_ATTN = N_HEADS * HEAD_DIM
D_KV = N_KV_HEADS * HEAD_DIM
ROPE_THETA = 10000.0
D_CONV = 512
D_RWKV = 512
RWKV_HEAD = 64
RWKV_HEADS = D_RWKV // RWKV_HEAD
DECAY_RANK = 32
A_RANK = 32
GATE_RANK = 96
D_TAIL = DECAY_RANK + A_RANK + GATE_RANK
D_IN = D_ATTN + 2 * D_KV + 3 * D_CONV + 3 * D_RWKV + D_TAIL
D_MAIN = D_IN - D_TAIL
D_FF = 5632
N_MOD = 6
EPS = 1e-6
GN_EPS = 64e-5
PAST_LEN = 256

VMEM_LIMIT_BYTES = 56 * 1024 * 1024
CHUNK = 64
BF16 = jnp.bfloat16
F32 = jnp.float32


def _params(n_grid, **kw):
    sem = ("parallel",) + ("arbitrary",) * (n_grid - 1)
    return pltpu.CompilerParams(dimension_semantics=sem, vmem_limit_bytes=VMEM_LIMIT_BYTES, **kw)


def _rms(x, g):
    return x * lax.rsqrt(jnp.mean(x * x, axis=-1, keepdims=True) + EPS) * g


def _dot(a, b):
    return jnp.dot(a.astype(BF16), b.astype(BF16), preferred_element_type=F32)


_NN = (((1,), (0,)), ((), ()))
_NT = (((1,), (1,)), ((), ()))
_TN = (((0,), (0,)), ((), ()))


def _mm(a, b, dims=_NN):
    return lax.dot_general(a.astype(BF16), b.astype(BF16), dims, preferred_element_type=F32)


def _ada_kernel(c_ref, w_ref, b_ref, o_ref):
    c = c_ref[...]
    s = c * jax.nn.sigmoid(c)
    o_ref[...] = _dot(s, w_ref[...]) + b_ref[...]


def _ada_call(cc, w_ada, b_ada):
    tn = 512
    n_out = N_MOD * D_MODEL
    return pl.pallas_call(
        _ada_kernel,
        grid=(DEPTH, n_out // tn),
        in_specs=[
            pl.BlockSpec((8, D_MODEL), lambda l, n: (0, 0)),
            pl.BlockSpec((None, D_MODEL, tn), lambda l, n: (l, 0, n)),
            pl.BlockSpec((None, 1, tn), lambda l, n: (l, 0, n)),
        ],
        out_specs=pl.BlockSpec((None, 8, tn), lambda l, n: (l, 0, n)),
        out_shape=jax.ShapeDtypeStruct((DEPTH, 8, n_out), F32),
        compiler_params=_params(2),
        name="adaln",
    )(cc, w_ada, b_ada.reshape(DEPTH, 1, n_out))


def _mod_spec(row_fn, chunk):
    return pl.BlockSpec((None, 1, D_MODEL), lambda i, *_: (row_fn(i), 0, chunk))


def _ng_spec(l, j):
    return pl.BlockSpec((None, 1, D_MODEL), lambda i, *_: (l * 4 + j, 0, 0))


INPROJ_NORM_ROWS = 256
INPROJ_VMEM_LIMIT_BYTES = 60 * 1024 * 1024


def _row_tile_copy(x_hbm, x_buf, sem, tile):
    rows = x_buf.shape[0]
    return pltpu.make_async_copy(x_hbm.at[pl.ds(pl.multiple_of(tile * rows, rows), rows), :], x_buf, sem)


def _inproj_kernel(x_hbm, sha_ref, sca_ref, shb_ref, scb_ref, g_ref, w_ref, wt_ref, o_ref, ot_ref,
                   x_buf, h_scr, x_sem):
    i = pl.program_id(0)

    @pl.when(pl.program_id(1) == 0)
    def _():
        @pl.when(i == 0)
        def _():
            _row_tile_copy(x_hbm, x_buf, x_sem, 0).start()

        _row_tile_copy(x_hbm, x_buf, x_sem, i).wait()
        half = x_buf.shape[0] // 2
        for r0 in range(0, x_buf.shape[0], INPROJ_NORM_ROWS):
            rs = slice(r0, r0 + INPROJ_NORM_ROWS)
            sh_ref, sc_ref = (sha_ref, sca_ref) if r0 < half else (shb_ref, scb_ref)
            h = _rms(x_buf[rs, :], g_ref[...]) * (1.0 + sc_ref[...]) + sh_ref[...]
            h_scr[rs, :] = h.astype(BF16)

        @pl.when(i + 1 < pl.num_programs(0))
        def _():
            _row_tile_copy(x_hbm, x_buf, x_sem, i + 1).start()

        ot_ref[...] = _mm(h_scr[...], wt_ref[...], _NT)

    o_ref[...] = _mm(h_scr[...], w_ref[...], _NT)


def _inproj_call(x, mod3, ng3, w_in_t, w_tail_t, l, row_fn_half, tm):
    T = x.shape[0]
    tn = 512
    seg = lambda j: (lambda i: row_fn_half(2 * i + j))
    return pl.pallas_call(
        _inproj_kernel,
        grid=(T // tm, D_MAIN // tn),
        in_specs=[
            pl.BlockSpec(memory_space=pl.ANY),
            _mod_spec(seg(0), 0), _mod_spec(seg(0), 1),
            _mod_spec(seg(1), 0), _mod_spec(seg(1), 1),
            _ng_spec(l, 0),
            pl.BlockSpec((None, tn, D_MODEL), lambda i, n: (l, n, 0)),
            pl.BlockSpec((None, D_TAIL, D_MODEL), lambda i, n: (l, 0, 0)),
        ],
        out_specs=[
            pl.BlockSpec((tm, tn), lambda i, n: (i, n)),
            pl.BlockSpec((tm, D_TAIL), lambda i, n: (i, 0)),
        ],
        out_shape=[
            jax.ShapeDtypeStruct((T, D_MAIN), F32),
            jax.ShapeDtypeStruct((T, D_TAIL), F32),
        ],
        scratch_shapes=[pltpu.VMEM((tm, D_MODEL), F32), pltpu.VMEM((tm, D_MODEL), BF16),
                        pltpu.SemaphoreType.DMA(())],
        compiler_params=pltpu.CompilerParams(dimension_semantics=("arbitrary", "arbitrary"),
                                             vmem_limit_bytes=INPROJ_VMEM_LIMIT_BYTES),
        name="in_proj",
    )(x, mod3, mod3, mod3, mod3, ng3, w_in_t, w_tail_t)


def _rope(x, cos, sin):
    lane = lax.broadcasted_iota(jnp.int32, x.shape, 1)
    first = (lane % (HEAD_DIM // 2)) < (HEAD_DIM // 4)
    rot = jnp.where(first, -pltpu.roll(x, HEAD_DIM - HEAD_DIM // 4, 1), pltpu.roll(x, HEAD_DIM // 4, 1))
    return x * cos + rot * sin


def _attn_kernel(*refs, n, past, rope, emit_kv, qb):
    it = iter(refs)
    q_ref, k_ref, v_ref, qn_ref, kn_ref = (next(it) for _ in range(5))
    cos_ref = sin_ref = ck_ref = cv_ref = None
    if rope:
        cos_ref, sin_ref = next(it), next(it)
    if past:
        ck_ref, cv_ref = next(it), next(it)
    o_ref = next(it)
    if emit_kv:
        ko_ref, vo_ref = next(it), next(it)
    k_scr, v_scr = next(it), next(it)

    k = _rms(k_ref[...], kn_ref[...])
    v = v_ref[...]
    if emit_kv:
        ko_ref[...] = k
        vo_ref[...] = v
    if rope:
        k = _rope(k, cos_ref[...], sin_ref[...])
    k_scr[0:n, :] = k.astype(BF16)
    v_scr[0:n, :] = v.astype(BF16)
    if past:
        k_scr[n:n + past, :] = ck_ref[...].astype(BF16)
        v_scr[n:n + past, :] = cv_ref[...].astype(BF16)
    exp2_scale = HEAD_DIM ** -0.5 * math.log2(math.e)

    blocks_per_step = min(n // qb, 4)

    def block(b, carry):
        items = [(pl.ds(pl.multiple_of((b * blocks_per_step + j) * qb, qb), qb),
                  slice(g * HEAD_DIM, (g + 1) * HEAD_DIM))
                 for j in range(blocks_per_step) for g in range(GROUP)]

        def scores(item):
            rows, cols = item
            q = _rms(q_ref[rows, cols], qn_ref[...])
            if rope:
                q = _rope(q, cos_ref[rows, :], sin_ref[rows, :])
            return lax.dot_general(q.astype(BF16), k_scr[...], _NT, preferred_element_type=F32)

        def softmax(s):
            p = jnp.exp2((s - jnp.max(s, axis=-1, keepdims=True)) * exp2_scale)
            return p.astype(BF16), jnp.sum(p, axis=-1, keepdims=True)

        def values(item, pd):
            rows, cols = item
            o = jnp.dot(pd[0], v_scr[...], preferred_element_type=F32)
            o_ref[rows, cols] = (o / pd[1]).astype(BF16)

        if blocks_per_step == 1:
            ps = [softmax(s) for s in [scores(item) for item in items]]
            for item, pd in zip(items, ps):
                values(item, pd)
            return carry
        depth = 2
        ss, ps = {}, {}
        for i in range(len(items) + depth):
            if i < len(items):
                ss[i] = scores(items[i])
            if 1 <= i <= len(items):
                ps[i - 1] = softmax(ss.pop(i - 1))
            if i >= depth:
                values(items[i - depth], ps.pop(i - depth))
        return carry

    lax.fori_loop(0, n // (qb * blocks_per_step), block, 0)


def _attn_call(proj, qn, kn, l, n, rope_tabs=None, cache=None, emit_kv=False):
    T = proj.shape[0]
    B = T // n
    past = PAST_LEN if cache is not None else 0
    qw = GROUP * HEAD_DIM
    k_blk = D_ATTN // HEAD_DIM
    v_blk = (D_ATTN + D_KV) // HEAD_DIM
    in_specs = [
        pl.BlockSpec((n, qw), lambda b, h: (b, h)),
        pl.BlockSpec((n, HEAD_DIM), lambda b, h: (b, k_blk + h)),
        pl.BlockSpec((n, HEAD_DIM), lambda b, h: (b, v_blk + h)),
        pl.BlockSpec((1, HEAD_DIM), lambda b, h: (0, 0)),
        pl.BlockSpec((1, HEAD_DIM), lambda b, h: (0, 0)),
    ]
    args = [proj, proj, proj, qn, kn]
    if rope_tabs is not None:
        in_specs += [pl.BlockSpec((n, HEAD_DIM), lambda b, h: (0, 0))] * 2
        args += list(rope_tabs)
    if cache is not None:
        in_specs += [pl.BlockSpec((None, None, PAST_LEN, HEAD_DIM), lambda b, h: (b, l, 0, h))] * 2
        args += list(cache)
    out_specs = [pl.BlockSpec((n, qw), lambda b, h: (b, h))]
    out_shape = [jax.ShapeDtypeStruct((T, D_ATTN), BF16)]
    if emit_kv:
        out_specs += [pl.BlockSpec((None, n, HEAD_DIM), lambda b, h: (b, 0, h))] * 2
        out_shape += [jax.ShapeDtypeStruct((B, n, D_KV), F32)] * 2
    return pl.pallas_call(
        functools.partial(_attn_kernel, n=n, past=past, rope=rope_tabs is not None,
                          emit_kv=emit_kv, qb=256),
        grid=(B, N_KV_HEADS),
        in_specs=in_specs,
        out_specs=out_specs,
        out_shape=out_shape,
        scratch_shapes=[pltpu.VMEM((n + past, HEAD_DIM), BF16)] * 2,
        compiler_params=_params(2),
        name="attention",
    )(*args)


CONV_COL = D_ATTN + 2 * D_KV
RWKV_COL = CONV_COL + 3 * D_CONV
ROWS_A = 256
SUB = 16
CHUNKS_A = 2
assert CHUNK // SUB == 4


def _split2(x):
    hi = x.astype(BF16)
    return hi, (x - hi.astype(F32)).astype(BF16)


def _dot_x3(a, b):
    ah, al = _split2(a)
    bh, bl = _split2(b)
    mm = lambda p, q: jnp.dot(p, q, preferred_element_type=F32)
    return mm(ah, bh) + mm(ah, bl) + mm(al, bh)


def _split_sum(x, pieces, dot_piece):
    acc = None
    rest = x
    for _ in range(pieces):
        part = rest.astype(BF16)
        rest = rest - part.astype(F32)
        term = dot_piece(part)
        acc = term if acc is None else acc + term
    return acc


def _mix_kernel(*refs, n, nseq, has_s0, emit_state, manual_fetch):
    it = iter(refs)
    if manual_fetch:
        proj_hbm, tail_hbm = next(it), next(it)
    else:
        cv_ref, um_ref, ut_ref = next(it), next(it), next(it)
    (mum_ref, mut_ref, cw_ref, w0_ref, wup_ref, a0_ref, aup_ref, gup_ref, kk_ref, ka_ref,
     rk_ref, lng_ref, lnb_ref, seg_ref) = (next(it) for _ in range(14))
    s0_ref = next(it) if has_s0 else None
    o_ref = next(it)
    so_ref = next(it) if emit_state else None
    al_s, be_s, kd_s, r_s, v_s, p_s, bon_s, g_s, y_s, s_scr = (next(it) for _ in range(10))
    w2_s, w1_s, q_s = next(it), next(it), next(it)
    if manual_fetch:
        cv_ref, um_ref, ut_ref, in_sem = next(it), next(it), next(it), next(it)

        def input_copies(step):
            rows = pl.ds(pl.multiple_of(step * (nseq * n), nseq * n), nseq * n)
            return (
                pltpu.make_async_copy(proj_hbm.at[rows, pl.ds(CONV_COL, 3 * D_CONV)], cv_ref, in_sem.at[0]),
                pltpu.make_async_copy(proj_hbm.at[rows, pl.ds(RWKV_COL, 3 * D_RWKV)], um_ref, in_sem.at[1]),
                pltpu.make_async_copy(tail_hbm.at[rows, :], ut_ref, in_sem.at[2]))

        step = pl.program_id(0)

        @pl.when(step == 0)
        def _():
            for cp in input_copies(0):
                cp.start()

        for cp in input_copies(step):
            cp.wait()
    blocks_per_seq = n // ROWS_A
    n_blocks = nseq * blocks_per_seq
    n_chunks = n // CHUNK
    per_block = ROWS_A // CHUNK

    def seg_sum(x):
        return _split_sum(x, 2, lambda part: jnp.dot(part, seg_ref[...], preferred_element_type=F32))

    PAIR = 2 * RWKV_HEAD
    N_PAIRS = RWKV_HEADS // 2

    def pair_tiles(x):
        return [x[:, p * PAIR:(p + 1) * PAIR] for p in range(N_PAIRS)]

    bi = lax.broadcasted_iota(jnp.int32, (ROWS_A, ROWS_A), 0)
    bj = lax.broadcasted_iota(jnp.int32, (ROWS_A, ROWS_A), 1)
    same_chunk = (bi // CHUNK) == (bj // CHUNK)
    cum_mat = [(same_chunk & (bj <= bi)).astype(BF16), (same_chunk & (bj >= bi)).astype(BF16)]

    def pass1(c, carry):
        c0 = pl.multiple_of(c * ROWS_A, ROWS_A)
        rows = pl.ds(c0, ROWS_A)
        prev_row = pl.ds(jnp.maximum(c0 - 1, 0), 1)
        next_row = pl.ds(jnp.minimum(c0 + ROWS_A, nseq * n - 1), 1)
        seq_first = (c % blocks_per_seq) == 0
        seq_last = (c % blocks_per_seq) == blocks_per_seq - 1

        def neighbours(load):
            cur = load(rows)
            rid = lax.broadcasted_iota(jnp.int32, cur.shape, 0)
            before = jnp.where(seq_first, 0.0, load(prev_row))
            after = jnp.where(seq_last, 0.0, load(next_row))
            prev = jnp.where(rid == 0, before, pltpu.roll(cur, 1, 0))
            nxt = jnp.where(rid == ROWS_A - 1, after, pltpu.roll(cur, ROWS_A - 1, 0))
            return cur, prev, nxt

        z, zp, zn = neighbours(lambda rs: cv_ref[rs, D_CONV:2 * D_CONV] * cv_ref[rs, 2 * D_CONV:3 * D_CONV])
        cw = cw_ref[...]
        conv = cv_ref[rows, 0:D_CONV] * (zp * cw[0:1, :] + z * cw[1:2, :] + zn * cw[2:3, :])
        o_ref[rows, 0:D_CONV] = conv.astype(BF16)

        u, up, un = neighbours(lambda rs: um_ref[rs, :])
        u = u + mum_ref[...] * (0.5 * (up + un) - u)
        t, tp, tn = neighbours(lambda rs: ut_ref[rs, :])
        ut = t + mut_ref[...] * (0.5 * (tp + tn) - t)
        r = u[:, 0:D_RWKV]
        k = u[:, D_RWKV:2 * D_RWKV]
        v = u[:, 2 * D_RWKV:3 * D_RWKV]
        kk = k * kk_ref[...]
        kk = kk * lax.rsqrt(seg_sum(kk * kk) + 1e-12)
        for p, v_p in enumerate(pair_tiles(v)):
            v_s[p, rows, :] = v_p.astype(BF16)
        tw = jnp.tanh(ut)
        a_sum = jnp.zeros((ROWS_A, D_RWKV), F32)
        for d in range(2):
            zz = w0_ref[d:d + 1, :] + _dot_x3(tw, wup_ref[d])
            lw = -math.exp(-0.5) * jax.nn.sigmoid(zz)
            a = jax.nn.sigmoid(a0_ref[d:d + 1, :] + _dot_x3(ut, aup_ref[d]))
            a_sum = a_sum + a
            cum = _split_sum(lw, 3, lambda part: jnp.dot(cum_mat[d], part, preferred_element_type=F32))
            e_out = jnp.exp(-cum)
            kd = k * (1.0 + (a - 1.0) * ka_ref[...])
            scaled = ((al_s, -kk * jnp.exp(cum - lw)), (be_s, kk * a * e_out), (kd_s, kd * e_out),
                      (r_s, r * jnp.exp(cum)))
            for ref, val in scaled:
                for p, val_p in enumerate(pair_tiles(val)):
                    ref[d, p, rows, :] = val_p.astype(BF16)
            for j in range(per_block):
                last = j * CHUNK + (CHUNK - 1 if d == 0 else 0)
                for p, dec_p in enumerate(pair_tiles(jnp.exp(cum[last:last + 1, :]))):
                    p_s[d, c * per_block + j, p:p + 1, :] = dec_p
        kd_sum = k * (2.0 + (a_sum - 2.0) * ka_ref[...])
        bon_s[rows, :] = seg_sum(r * kd_sum * rk_ref[...]) * v
        g_s[rows, :] = _dot_x3(jax.nn.sigmoid(ut), gup_ref[...])
        y_s[rows, :] = jnp.zeros((ROWS_A, D_RWKV), F32)
        return carry

    lax.fori_loop(0, n_blocks, pass1, 0)

    if manual_fetch:
        @pl.when(step + 1 < pl.num_programs(0))
        def _():
            for cp in input_copies(step + 1):
                cp.start()

    for s in range(nseq):
        for d in range(2):
            for p in range(N_PAIRS):
                if has_s0:
                    s_scr[s, d, p] = jnp.concatenate([s0_ref[d, 2 * p], s0_ref[d, 2 * p + 1]], axis=1)
                else:
                    s_scr[s, d, p] = jnp.zeros((RWKV_HEAD, PAIR), F32)

    ri = lax.broadcasted_iota(jnp.int32, (CHUNK, PAIR), 0)
    lane = lax.broadcasted_iota(jnp.int32, (CHUNK, PAIR), 1)
    ci = lane % CHUNK
    low_half = lane < RWKV_HEAD
    eye = (ci == ri).astype(F32)
    diag_blk = (ri // SUB) == (ci // SUB)
    n_rounds = int(math.log2(SUB)) - 1

    def bd(x):
        zero = jnp.zeros_like(x)
        return jnp.concatenate([jnp.where(low_half, x, zero), jnp.where(low_half, zero, x)], axis=0)

    def pair_mm(a, b):
        return _mm(a, bd(b.astype(BF16)))

    def pair_mm_nt(a, b):
        return _mm(a, bd(b.astype(BF16)), _NT)

    def solve_step(c, carry):
        chains = []
        for s, d, j in ((s, d, j) for s in range(nseq) for d in range(2) for j in range(CHUNKS_A)):
            cc = s * n_chunks + c * CHUNKS_A + j
            rows = pl.ds(pl.multiple_of(cc * CHUNK, CHUNK), CHUNK)
            for p in range(N_PAIRS):
                chains.append(dict(
                    d=d, p=p, rows=rows,
                    strict=(ci < ri) if d == 0 else (ci > ri), incl=(ci <= ri) if d == 0 else (ci >= ri),
                    al=al_s[d, p, rows, :], be=be_s[d, p, rows, :], kd=kd_s[d, p, rows, :],
                    r=r_s[d, p, rows, :], v=v_s[p, rows, :]))

        def each(fn, *lists):
            return [fn(*args) for args in zip(chains, *lists)]

        alr = each(lambda ch: jnp.concatenate([ch['al'], ch['r']], axis=0))
        g_be = each(lambda ch, m: pair_mm_nt(m, ch['be']), alr)
        g_kd = each(lambda ch, m: pair_mm_nt(m, ch['kd']), alr)
        a = each(lambda ch, m: jnp.where(ch['strict'], m[0:CHUNK], 0.0), g_be)
        a_ak = each(lambda ch, m: jnp.where(ch['strict'], m[0:CHUNK], 0.0), g_kd)
        q = each(lambda ch, mb, mk: jnp.concatenate(
            [jnp.where(ch['incl'], mb[CHUNK:2 * CHUNK], 0.0), jnp.where(ch['incl'], mk[CHUNK:2 * CHUNK], 0.0)],
            axis=1).astype(BF16), g_be, g_kd)
        akv = each(lambda ch, m: pair_mm(m, ch['v']), a_ak)
        xs = [jnp.where(diag_blk, m, 0.0) for m in a]
        low = [jnp.where(diag_blk, 0.0, m) for m in a]
        ts = [eye + x for x in xs]
        xs = [pair_mm(x, x) for x in xs]
        for _ in range(n_rounds - 1):
            z = [pair_mm(jnp.concatenate([x, t], axis=0), x) for x, t in zip(xs, ts)]
            xs = [m[0:CHUNK] for m in z]
            ts = [t + m[CHUNK:2 * CHUNK] for t, m in zip(ts, z)]
        ts = [t + pair_mm(t, x) for t, x in zip(ts, xs)]
        ms = [pair_mm(t, m) for t, m in zip(ts, low)]
        sol = each(lambda ch, t, m: _mm(t, jnp.concatenate([bd(m.astype(BF16)), bd(ch['al'])], axis=1)),
                   ts, akv)
        m2 = [pair_mm(m, m) for m in ms]
        ims = [eye + m for m in ms]
        ns = [im + pair_mm(im, mm2) for im, mm2 in zip(ims, m2)]
        w = [_mm(nn, jnp.concatenate([bd(x[:, 0:PAIR].astype(BF16)), bd(x[:, PAIR:2 * PAIR].astype(BF16))],
                                     axis=1)) for nn, x in zip(ns, sol)]
        for ch, x, qq in zip(chains, w, q):
            w2_s[ch['d'], ch['p'], ch['rows'], :] = x[:, 0:PAIR]
            w1_s[ch['d'], ch['p'], ch['rows'], :] = x[:, PAIR:2 * PAIR].astype(BF16)
            q_s[ch['d'], ch['p'], ch['rows'], :] = qq
        return carry

    lax.fori_loop(0, n_chunks // CHUNKS_A, solve_step, 0)

    def chunk_step(c, carry):
        chains = []
        for s, d in ((s, d) for s in range(nseq) for d in range(2)):
            cc = s * n_chunks + (c if d == 0 else n_chunks - 1 - c)
            rows = pl.ds(pl.multiple_of(cc * CHUNK, CHUNK), CHUNK)
            p_tile = p_s[d, cc]
            for p in range(N_PAIRS):
                chains.append(dict(
                    s=s, d=d, p=p, rows=rows, dec=p_tile[p:p + 1, :],
                    be=be_s[d, p, rows, :], kd=kd_s[d, p, rows, :], r=r_s[d, p, rows, :],
                    v=v_s[p, rows, :], w2=w2_s[d, p, rows, :], w1=w1_s[d, p, rows, :],
                    q=q_s[d, p, rows, :]))

        def each(fn, *lists):
            return [fn(*args) for args in zip(chains, *lists)]

        st = each(lambda ch: s_scr[ch['s'], ch['d'], ch['p']])
        st_bd = [bd(t.astype(BF16)) for t in st]
        u = each(lambda ch, t: ch['w2'] + _mm(ch['w1'], t, _NT), st_bd)
        ub = [x.astype(BF16) for x in u]
        ys = each(lambda ch, t, uu: _mm(ch['r'], t, _NT)
                  + _mm(ch['q'], jnp.concatenate([bd(uu), bd(ch['v'])], axis=0)), st_bd, ub)
        full = each(lambda ch, uu: _mm(jnp.concatenate([uu, ch['v']], axis=0),
                                       jnp.concatenate([ch['be'], ch['kd']], axis=0), _TN), ub)
        s_new = each(lambda ch, t, m: (t + jnp.where(low_half, m[0:RWKV_HEAD], m[RWKV_HEAD:PAIR])) * ch['dec'],
                     st, full)
        for ch, t in zip(chains, s_new):
            s_scr[ch['s'], ch['d'], ch['p']] = t
        for ch, y in zip(chains, ys):
            y_s[ch['rows'], ch['p'] * PAIR:(ch['p'] + 1) * PAIR] += y
        return carry

    lax.fori_loop(0, n_chunks, chunk_step, 0)

    if emit_state:
        for s in range(nseq):
            for d in range(2):
                for p in range(N_PAIRS):
                    so_ref[s, d, 2 * p] = s_scr[s, d, p][:, 0:RWKV_HEAD]
                    so_ref[s, d, 2 * p + 1] = s_scr[s, d, p][:, RWKV_HEAD:PAIR]

    def pass3(c, carry):
        rows = pl.ds(pl.multiple_of(c * ROWS_A, ROWS_A), ROWS_A)
        y = y_s[rows, :]
        yc = y - seg_sum(y) * (1.0 / RWKV_HEAD)
        var = seg_sum(yc * yc) * (1.0 / RWKV_HEAD)
        yn = yc * lax.rsqrt(var + GN_EPS) * lng_ref[...] + lnb_ref[...]
        o_ref[rows, D_CONV:D_CONV + D_RWKV] = ((yn + bon_s[rows, :]) * g_s[rows, :]).astype(BF16)
        return carry

    lax.fori_loop(0, n_blocks, pass3, 0)


def _pad_rows(w, lo):
    return jnp.pad(w, [(0, 0)] * (w.ndim - 2) + [(lo, D_TAIL - lo - w.shape[-2]), (0, 0)])


def _mix_call(proj, tail, P, l, n, nseq, s0=None, emit_state=False):
    T = proj.shape[0]
    B = T // n
    rows = nseq * n
    assert s0 is None or nseq == 1
    full = lambda shape: pl.BlockSpec(shape, lambda b: (0,) * len(shape))
    lanes = lambda w: -(-w // 128) * 128
    in_bytes = rows * (3 * D_CONV + 3 * D_RWKV + lanes(D_TAIL)) * 4
    scratch_bytes = rows * (4 * 2 * D_RWKV * 2 + D_RWKV * 2 + 3 * D_RWKV * 4
                            + 2 * D_RWKV * (4 + 2 + 2 * 2))
    out_bytes = 2 * rows * (D_CONV + D_RWKV) * 2
    temporaries = 12 * 1024 * 1024
    manual_fetch = 2 * in_bytes + scratch_bytes + out_bytes + temporaries > VMEM_LIMIT_BYTES
    if manual_fetch:
        in_specs = [pl.BlockSpec(memory_space=pl.ANY), pl.BlockSpec(memory_space=pl.ANY)]
        token_args = [proj, tail]
    else:
        in_specs = [
            pl.BlockSpec((rows, 3 * D_CONV), lambda b: (b, CONV_COL // (3 * D_CONV))),
            pl.BlockSpec((rows, 3 * D_RWKV), lambda b: (b, RWKV_COL // (3 * D_RWKV))),
            pl.BlockSpec((rows, D_TAIL), lambda b: (b, 0)),
        ]
        token_args = [proj, proj, tail]
    in_specs += [
        full((1, 3 * D_RWKV)), full((1, D_TAIL)), full((3, D_CONV)),
        full((2, D_RWKV)), full((2, D_TAIL, D_RWKV)), full((2, D_RWKV)), full((2, D_TAIL, D_RWKV)),
        full((D_TAIL, D_RWKV)), full((1, D_RWKV)), full((1, D_RWKV)), full((1, D_RWKV)),
        full((1, D_RWKV)), full((1, D_RWKV)), full((D_RWKV, D_RWKV)),
    ]
    head = jnp.arange(D_RWKV) // RWKV_HEAD
    seg = (head[:, None] == head[None, :]).astype(BF16)
    mu = P['rwkv_mu'][l]
    row = lambda a: a.reshape(1, -1)
    args = token_args + [row(mu[:3 * D_RWKV]), row(mu[3 * D_RWKV:]), P['conv_w'][l],
            P['rwkv_w0'][l], _pad_rows(P['rwkv_w_up'][l], 0), P['rwkv_a0'][l],
            _pad_rows(P['rwkv_a_up'][l], DECAY_RANK), _pad_rows(P['rwkv_g_up'][l], DECAY_RANK + A_RANK),
            row(P['rwkv_k_k'][l]), row(P['rwkv_k_a'][l]), row(P['rwkv_r_k'][l]),
            row(P['rwkv_ln_g'][l]), row(P['rwkv_ln_b'][l]), seg]
    if s0 is not None:
        in_specs.append(pl.BlockSpec((None, None, 2, RWKV_HEADS, RWKV_HEAD, RWKV_HEAD),
                                     lambda b: (b, l, 0, 0, 0, 0)))
        args.append(s0)
    out_specs = [pl.BlockSpec((rows, D_CONV + D_RWKV), lambda b: (b, 0))]
    out_shape = [jax.ShapeDtypeStruct((T, D_CONV + D_RWKV), BF16)]
    state_shape = (2, RWKV_HEADS, RWKV_HEAD, RWKV_HEAD)
    if emit_state:
        out_specs.append(pl.BlockSpec((nseq,) + state_shape, lambda b: (b, 0, 0, 0, 0)))
        out_shape.append(jax.ShapeDtypeStruct((B,) + state_shape, F32))
    pair = 2 * RWKV_HEAD
    n_pairs = RWKV_HEADS // 2
    per_pair = lambda: pltpu.VMEM((2, n_pairs, rows, pair), BF16)
    scratch_shapes = [per_pair(), per_pair(), per_pair(), per_pair(),
                      pltpu.VMEM((n_pairs, rows, pair), BF16),
                      pltpu.VMEM((2, rows // CHUNK, n_pairs, pair), F32),
                      pltpu.VMEM((rows, D_RWKV), F32), pltpu.VMEM((rows, D_RWKV), F32),
                      pltpu.VMEM((rows, D_RWKV), F32),
                      pltpu.VMEM((nseq, 2, n_pairs, RWKV_HEAD, pair), F32),
                      pltpu.VMEM((2, n_pairs, rows, pair), F32), pltpu.VMEM((2, n_pairs, rows, pair), BF16),
                      pltpu.VMEM((2, n_pairs, rows, 2 * pair), BF16)]
    if manual_fetch:
        scratch_shapes += [pltpu.VMEM((rows, 3 * D_CONV), F32), pltpu.VMEM((rows, 3 * D_RWKV), F32),
                           pltpu.VMEM((rows, D_TAIL), F32), pltpu.SemaphoreType.DMA((3,))]
    semantics = ("arbitrary",) if manual_fetch else ("parallel",)
    return pl.pallas_call(
        functools.partial(_mix_kernel, n=n, nseq=nseq, has_s0=s0 is not None, emit_state=emit_state,
                          manual_fetch=manual_fetch),
        grid=(B // nseq,),
        in_specs=in_specs,
        out_specs=out_specs,
        out_shape=out_shape,
        scratch_shapes=scratch_shapes,
        compiler_params=pltpu.CompilerParams(dimension_semantics=semantics,
                                             vmem_limit_bytes=VMEM_LIMIT_BYTES),
        name="conv_rwkv",
    )(*args)


def _residual_copy(x_hbm, x_buf, sem):
    rows = x_buf.shape[0]
    start = pl.multiple_of(pl.program_id(0) * rows, rows)
    return pltpu.make_async_copy(x_hbm.at[pl.ds(start, rows), :], x_buf, sem)


def _outproj_kernel(at_ref, cr_ref, w_ref, x_hbm, ga_ref, shf_ref, scf_ref, g1_ref, g2_ref,
                    x1_ref, h2_ref, x_ref, x_sem):
    kt = pl.program_id(1)

    @pl.when(kt == 0)
    def _():
        _residual_copy(x_hbm, x_ref, x_sem).start()
        x1_ref[...] = jnp.zeros(x1_ref.shape, F32)

    lhs = jnp.where(kt < 2, at_ref[...], cr_ref[...])
    x1_ref[...] += jnp.dot(lhs, w_ref[...], preferred_element_type=F32)

    @pl.when(kt == pl.num_programs(1) - 1)
    def _():
        _residual_copy(x_hbm, x_ref, x_sem).wait()
        x1 = x_ref[...] + ga_ref[...] * _rms(x1_ref[...], g1_ref[...])
        x1_ref[...] = x1
        h2_ref[...] = (_rms(x1, g2_ref[...]) * (1.0 + scf_ref[...]) + shf_ref[...]).astype(BF16)


def _cast_kernel(w_ref, o_ref):
    o_ref[...] = w_ref[...].astype(BF16)


def _cast_call(w):
    L, K, N = w.shape
    tk = 512
    spec = pl.BlockSpec((None, tk, N), lambda l, k: (l, k, 0))
    return pl.pallas_call(
        _cast_kernel,
        grid=(L, K // tk),
        in_specs=[spec],
        out_specs=spec,
        out_shape=jax.ShapeDtypeStruct(w.shape, BF16),
        compiler_params=_params(2),
        name="cast_bf16",
    )(w)


def _outproj_call(attn, cr, x, mod3, ng3, w_out, l, row_fn, tm):
    T = x.shape[0]
    tk = 512
    return pl.pallas_call(
        _outproj_kernel,
        grid=(T // tm, D_MODEL // tk),
        in_specs=[
            pl.BlockSpec((tm, tk), lambda i, k: (i, jnp.minimum(k, 1))),
            pl.BlockSpec((tm, tk), lambda i, k: (i, jnp.maximum(k - 2, 0))),
            pl.BlockSpec((None, tk, D_MODEL), lambda i, k: (l, k, 0)),
            pl.BlockSpec(memory_space=pl.ANY),
            _mod_spec(row_fn, 2), _mod_spec(row_fn, 3), _mod_spec(row_fn, 4),
            _ng_spec(l, 1), _ng_spec(l, 2),
        ],
        out_specs=[
            pl.BlockSpec((tm, D_MODEL), lambda i, k: (i, 0)),
            pl.BlockSpec((tm, D_MODEL), lambda i, k: (i, 0)),
        ],
        out_shape=[jax.ShapeDtypeStruct((T, D_MODEL), F32), jax.ShapeDtypeStruct((T, D_MODEL), BF16)],
        scratch_shapes=[pltpu.VMEM((tm, D_MODEL), F32), pltpu.SemaphoreType.DMA(())],
        compiler_params=_params(2),
        name="out_proj",
    )(attn, cr, w_out, x, mod3, mod3, mod3, ng3, ng3)


def _ffn_kernel(h_ref, x1_hbm, wg_ref, wu_ref, wd_ref, gf_ref, g3_ref, o_ref, x1_ref, x1_sem):
    f = pl.program_id(1)

    @pl.when(f == 0)
    def _():
        _residual_copy(x1_hbm, x1_ref, x1_sem).start()
        o_ref[...] = jnp.zeros(o_ref.shape, F32)

    h = h_ref[...]
    gate = _dot(h, wg_ref[...])
    up = _dot(h, wu_ref[...])
    o_ref[...] += _dot(gate * jax.nn.sigmoid(gate) * up, wd_ref[...])

    @pl.when(f == pl.num_programs(1) - 1)
    def _():
        _residual_copy(x1_hbm, x1_ref, x1_sem).wait()
        o_ref[...] = x1_ref[...] + gf_ref[...] * _rms(o_ref[...], g3_ref[...])


def _ffn_call(h2, x1, mod3, ng3, w_gate, w_up, w_down, l, row_fn, tm):
    T = x1.shape[0]
    tf = 256
    return pl.pallas_call(
        _ffn_kernel,
        grid=(T // tm, D_FF // tf),
        in_specs=[
            pl.BlockSpec((tm, D_MODEL), lambda i, f: (i, 0)),
            pl.BlockSpec(memory_space=pl.ANY),
            pl.BlockSpec((None, D_MODEL, tf), lambda i, f: (l, 0, f)),
            pl.BlockSpec((None, D_MODEL, tf), lambda i, f: (l, 0, f)),
            pl.BlockSpec((None, tf, D_MODEL), lambda i, f: (l, f, 0)),
            _mod_spec(row_fn, 5),
            _ng_spec(l, 3),
        ],
        out_specs=pl.BlockSpec((tm, D_MODEL), lambda i, f: (i, 0)),
        out_shape=jax.ShapeDtypeStruct((T, D_MODEL), F32),
        scratch_shapes=[pltpu.VMEM((tm, D_MODEL), F32), pltpu.SemaphoreType.DMA(())],
        compiler_params=_params(2),
        name="ffn",
    )(h2, x1, w_gate, w_up, w_down, mod3, ng3)


def _rope_tables(n):
    rows = n // GRID_W
    row = jnp.repeat(jnp.arange(rows), GRID_W).astype(F32)
    col = jnp.tile(jnp.arange(GRID_W), rows).astype(F32)
    half = HEAD_DIM // 2
    inv = 1.0 / (ROPE_THETA ** (jnp.arange(0, half, 2, dtype=F32) / half))
    ar = row[:, None] * inv
    ac = col[:, None] * inv
    ang = jnp.concatenate([ar, ar, ac, ac], axis=-1)
    return jnp.cos(ang), jnp.sin(ang)


ROW_TILE = 1024
INPROJ_ROW_TILE = 2048


def _layer(x, mod3, ng3, P, l, n, mix_seqs, row_fn_of_tm, rope_tabs=None, cache=None, s0=None, emit=False):
    proj, tail = _inproj_call(x, mod3, ng3, P['w_in_t'], P['w_tail_t'], l,
                              row_fn_of_tm(INPROJ_ROW_TILE // 2), INPROJ_ROW_TILE)
    attn_out = _attn_call(proj, P['q_norm'][l].reshape(1, -1), P['k_norm'][l].reshape(1, -1), l, n,
                          rope_tabs=rope_tabs, cache=cache, emit_kv=emit)
    mix_out = _mix_call(proj, tail, P, l, n, mix_seqs, s0=s0, emit_state=emit)
    attn = attn_out[0]
    cr = mix_out[0]
    x1, h2 = _outproj_call(attn, cr, x, mod3, ng3, P['w_out'], l, row_fn_of_tm(ROW_TILE), ROW_TILE)
    x2 = _ffn_call(h2, x1, mod3, ng3, P['w_gate'], P['w_up'], P['w_down'], l, row_fn_of_tm(ROW_TILE),
                   ROW_TILE)
    if emit:
        return x2, attn_out[1], attn_out[2], mix_out[1]
    return x2


def kernel(x_prompt, x_sample, cache_k, cache_v, state_rwkv, c, c_ctx, w_ada, b_ada, norm_g, w_in, q_norm, k_norm, conv_w, rwkv_mu, rwkv_w0, rwkv_w_up, rwkv_a0, rwkv_a_up, rwkv_g_up, rwkv_k_k, rwkv_k_a, rwkv_r_k, rwkv_ln_g, rwkv_ln_b, w_out, w_gate, w_up, w_down):
    P = {'q_norm': q_norm, 'k_norm': k_norm, 'conv_w': conv_w, 'rwkv_mu': rwkv_mu,
         'rwkv_w0': rwkv_w0, 'rwkv_w_up': rwkv_w_up, 'rwkv_a0': rwkv_a0, 'rwkv_a_up': rwkv_a_up,
         'rwkv_g_up': rwkv_g_up, 'rwkv_k_k': rwkv_k_k, 'rwkv_k_a': rwkv_k_a, 'rwkv_r_k': rwkv_r_k,
         'rwkv_ln_g': rwkv_ln_g, 'rwkv_ln_b': rwkv_ln_b, 'w_out': _cast_call(w_out),
         'w_gate': w_gate, 'w_up': w_up, 'w_down': w_down}
    w_in_t = jnp.swapaxes(w_in, 1, 2)
    P['w_in_t'] = w_in_t
    P['w_tail_t'] = w_in_t[:, D_MAIN:, :]
    batch, seq, _ = x_prompt.shape
    dec_batch, dec_seq, _ = x_sample.shape

    cc = jnp.zeros((8, D_MODEL), F32).at[0].set(c_ctx).at[1:1 + dec_batch].set(c)
    mod3 = _ada_call(cc, w_ada, b_ada).reshape(DEPTH * 8, 1, N_MOD * D_MODEL)
    ng3 = norm_g.reshape(DEPTH * 4, 1, D_MODEL)

    ck = cache_k.reshape(dec_batch, DEPTH, PAST_LEN, D_KV)
    cv = cache_v.reshape(dec_batch, DEPTH, PAST_LEN, D_KV)
    rope_tabs = _rope_tables(dec_seq)

    xp = x_prompt.reshape(batch * seq, D_MODEL)
    xs = x_sample.reshape(dec_batch * dec_seq, D_MODEL)
    ks, vs, ss = [], [], []
    for l in range(DEPTH):
        ctx_row = lambda tm, l=l: (lambda i: l * 8)
        smp_row = lambda tm, l=l: (lambda i: l * 8 + 1 + (i * tm) // dec_seq)
        xp, k_l, v_l, s_l = _layer(xp, mod3, ng3, P, l, seq, 2, ctx_row, emit=True)
        ks.append(k_l.reshape(batch, seq, N_KV_HEADS, HEAD_DIM))
        vs.append(v_l.reshape(batch, seq, N_KV_HEADS, HEAD_DIM))
        ss.append(s_l)
        xs = _layer(xs, mod3, ng3, P, l, dec_seq, 1, smp_row, rope_tabs=rope_tabs, cache=(ck, cv),
                    s0=state_rwkv)
    return (xp.reshape(batch, seq, D_MODEL), xs.reshape(dec_batch, dec_seq, D_MODEL),
            jnp.stack(ks, axis=1), jnp.stack(vs, axis=1), jnp.stack(ss, axis=1))
```

```python
import functools
import math

import jax
import jax.numpy as jnp
from jax import lax
from jax.experimental import pallas as pl
from jax.experimental.pallas import tpu as pltpu

D_MODEL = 2048
DEPTH = 2
GRID_W = 64
HEAD_DIM = 128
N_HEADS = 8
N_KV_HEADS = 2
GROUP = N_HEADS // N_KV_HEADS
D_ATTN = N_HEADS * HEAD_DIM
D_KV = N_KV_HEADS * HEAD_DIM
ROPE_THETA = 10000.0
D_CONV = 512
D_RWKV = 512
RWKV_HEAD = 64
RWKV_HEADS = D_RWKV // RWKV_HEAD
DECAY_RANK = 32
A_RANK = 32
GATE_RANK = 96
D_TAIL = DECAY_RANK + A_RANK + GATE_RANK
D_IN = D_ATTN + 2 * D_KV + 3 * D_CONV + 3 * D_RWKV + D_TAIL
D_MAIN = D_IN - D_TAIL
D_FF = 5632
N_MOD = 6
EPS = 1e-6
GN_EPS = 64e-5
PAST_LEN = 256

VMEM_LIMIT_BYTES = 56 * 1024 * 1024
CHUNK = 64
BF16 = jnp.bfloat16
F32 = jnp.float32


def _params(n_grid, **kw):
    sem = ("parallel",) + ("arbitrary",) * (n_grid - 1)
    return pltpu.CompilerParams(dimension_semantics=sem, vmem_limit_bytes=VMEM_LIMIT_BYTES, **kw)


def _rms(x, g):
    return x * lax.rsqrt(jnp.mean(x * x, axis=-1, keepdims=True) + EPS) * g


def _dot(a, b):
    return jnp.dot(a.astype(BF16), b.astype(BF16), preferred_element_type=F32)


_NN = (((1,), (0,)), ((), ()))
_NT = (((1,), (1,)), ((), ()))
_TN = (((0,), (0,)), ((), ()))


def _mm(a, b, dims=_NN):
    return lax.dot_general(a.astype(BF16), b.astype(BF16), dims, preferred_element_type=F32)


def _ada_kernel(c_ref, w_ref, b_ref, o_ref):
    c = c_ref[...]
    s = c * jax.nn.sigmoid(c)
    o_ref[...] = _dot(s, w_ref[...]) + b_ref[...]


def _ada_call(cc, w_ada, b_ada):
    tn = 512
    n_out = N_MOD * D_MODEL
    return pl.pallas_call(
        _ada_kernel,
        grid=(DEPTH, n_out // tn),
        in_specs=[
            pl.BlockSpec((8, D_MODEL), lambda l, n: (0, 0)),
            pl.BlockSpec((None, D_MODEL, tn), lambda l, n: (l, 0, n)),
            pl.BlockSpec((None, 1, tn), lambda l, n: (l, 0, n)),
        ],
        out_specs=pl.BlockSpec((None, 8, tn), lambda l, n: (l, 0, n)),
        out_shape=jax.ShapeDtypeStruct((DEPTH, 8, n_out), F32),
        compiler_params=_params(2),
        name="adaln",
    )(cc, w_ada, b_ada.reshape(DEPTH, 1, n_out))


def _mod_spec(row_fn, chunk):
    return pl.BlockSpec((None, 1, D_MODEL), lambda i, *_: (row_fn(i), 0, chunk))


def _ng_spec(l, j):
    return pl.BlockSpec((None, 1, D_MODEL), lambda i, *_: (l * 4 + j, 0, 0))


INPROJ_NORM_ROWS = 256
INPROJ_VMEM_LIMIT_BYTES = 60 * 1024 * 1024


def _row_tile_copy(x_hbm, x_buf, sem, tile):
    rows = x_buf.shape[0]
    return pltpu.make_async_copy(x_hbm.at[pl.ds(pl.multiple_of(tile * rows, rows), rows), :], x_buf, sem)


def _inproj_kernel(x_hbm, sha_ref, sca_ref, shb_ref, scb_ref, g_ref, w_ref, wt_ref, o_ref, ot_ref,
                   x_buf, h_scr, x_sem):
    i = pl.program_id(0)

    @pl.when(pl.program_id(1) == 0)
    def _():
        @pl.when(i == 0)
        def _():
            _row_tile_copy(x_hbm, x_buf, x_sem, 0).start()

        _row_tile_copy(x_hbm, x_buf, x_sem, i).wait()
        half = x_buf.shape[0] // 2
        for r0 in range(0, x_buf.shape[0], INPROJ_NORM_ROWS):
            rs = slice(r0, r0 + INPROJ_NORM_ROWS)
            sh_ref, sc_ref = (sha_ref, sca_ref) if r0 < half else (shb_ref, scb_ref)
            h = _rms(x_buf[rs, :], g_ref[...]) * (1.0 + sc_ref[...]) + sh_ref[...]
            h_scr[rs, :] = h.astype(BF16)

        @pl.when(i + 1 < pl.num_programs(0))
        def _():
            _row_tile_copy(x_hbm, x_buf, x_sem, i + 1).start()

        ot_ref[...] = _mm(h_scr[...], wt_ref[...], _NT)

    o_ref[...] = _mm(h_scr[...], w_ref[...], _NT)


def _inproj_call(x, mod3, ng3, w_in_t, w_tail_t, l, row_fn_half, tm):
    T = x.shape[0]
    tn = 512
    seg = lambda j: (lambda i: row_fn_half(2 * i + j))
    return pl.pallas_call(
        _inproj_kernel,
        grid=(T // tm, D_MAIN // tn),
        in_specs=[
            pl.BlockSpec(memory_space=pl.ANY),
            _mod_spec(seg(0), 0), _mod_spec(seg(0), 1),
            _mod_spec(seg(1), 0), _mod_spec(seg(1), 1),
            _ng_spec(l, 0),
            pl.BlockSpec((None, tn, D_MODEL), lambda i, n: (l, n, 0)),
            pl.BlockSpec((None, D_TAIL, D_MODEL), lambda i, n: (l, 0, 0)),
        ],
        out_specs=[
            pl.BlockSpec((tm, tn), lambda i, n: (i, n)),
            pl.BlockSpec((tm, D_TAIL), lambda i, n: (i, 0)),
        ],
        out_shape=[
            jax.ShapeDtypeStruct((T, D_MAIN), F32),
            jax.ShapeDtypeStruct((T, D_TAIL), F32),
        ],
        scratch_shapes=[pltpu.VMEM((tm, D_MODEL), F32), pltpu.VMEM((tm, D_MODEL), BF16),
                        pltpu.SemaphoreType.DMA(())],
        compiler_params=pltpu.CompilerParams(dimension_semantics=("arbitrary", "arbitrary"),
                                             vmem_limit_bytes=INPROJ_VMEM_LIMIT_BYTES),
        name="in_proj",
    )(x, mod3, mod3, mod3, mod3, ng3, w_in_t, w_tail_t)


def _rope(x, cos, sin):
    lane = lax.broadcasted_iota(jnp.int32, x.shape, 1)
    first = (lane % (HEAD_DIM // 2)) < (HEAD_DIM // 4)
    rot = jnp.where(first, -pltpu.roll(x, HEAD_DIM - HEAD_DIM // 4, 1), pltpu.roll(x, HEAD_DIM // 4, 1))
    return x * cos + rot * sin


def _attn_kernel(*refs, n, past, rope, emit_kv, qb):
    it = iter(refs)
    q_ref, k_ref, v_ref, qn_ref, kn_ref = (next(it) for _ in range(5))
    cos_ref = sin_ref = ck_ref = cv_ref = None
    if rope:
        cos_ref, sin_ref = next(it), next(it)
    if past:
        ck_ref, cv_ref = next(it), next(it)
    o_ref = next(it)
    if emit_kv:
        ko_ref, vo_ref = next(it), next(it)
    k_scr, v_scr = next(it), next(it)

    k = _rms(k_ref[...], kn_ref[...])
    v = v_ref[...]
    if emit_kv:
        ko_ref[...] = k
        vo_ref[...] = v
    if rope:
        k = _rope(k, cos_ref[...], sin_ref[...])
    k_scr[0:n, :] = k.astype(BF16)
    v_scr[0:n, :] = v.astype(BF16)
    if past:
        k_scr[n:n + past, :] = ck_ref[...].astype(BF16)
        v_scr[n:n + past, :] = cv_ref[...].astype(BF16)
    exp2_scale = HEAD_DIM ** -0.5 * math.log2(math.e)

    blocks_per_step = min(n // qb, 4)

    def block(b, carry):
        items = [(pl.ds(pl.multiple_of((b * blocks_per_step + j) * qb, qb), qb),
                  slice(g * HEAD_DIM, (g + 1) * HEAD_DIM))
                 for j in range(blocks_per_step) for g in range(GROUP)]

        def scores(item):
            rows, cols = item
            q = _rms(q_ref[rows, cols], qn_ref[...])
            if rope:
                q = _rope(q, cos_ref[rows, :], sin_ref[rows, :])
            return lax.dot_general(q.astype(BF16), k_scr[...], _NT, preferred_element_type=F32)

        def softmax(s):
            p = jnp.exp2((s - jnp.max(s, axis=-1, keepdims=True)) * exp2_scale)
            return p.astype(BF16), jnp.sum(p, axis=-1, keepdims=True)

        def values(item, pd):
            rows, cols = item
            o = jnp.dot(pd[0], v_scr[...], preferred_element_type=F32)
            o_ref[rows, cols] = (o / pd[1]).astype(BF16)

        if blocks_per_step == 1:
            ps = [softmax(s) for s in [scores(item) for item in items]]
            for item, pd in zip(items, ps):
                values(item, pd)
            return carry
        depth = 2
        ss, ps = {}, {}
        for i in range(len(items) + depth):
            if i < len(items):
                ss[i] = scores(items[i])
            if 1 <= i <= len(items):
                ps[i - 1] = softmax(ss.pop(i - 1))
            if i >= depth:
                values(items[i - depth], ps.pop(i - depth))
        return carry

    lax.fori_loop(0, n // (qb * blocks_per_step), block, 0)


def _attn_call(proj, qn, kn, l, n, rope_tabs=None, cache=None, emit_kv=False):
    T = proj.shape[0]
    B = T // n
    past = PAST_LEN if cache is not None else 0
    qw = GROUP * HEAD_DIM
    k_blk = D_ATTN // HEAD_DIM
    v_blk = (D_ATTN + D_KV) // HEAD_DIM
    in_specs = [
        pl.BlockSpec((n, qw), lambda b, h: (b, h)),
        pl.BlockSpec((n, HEAD_DIM), lambda b, h: (b, k_blk + h)),
        pl.BlockSpec((n, HEAD_DIM), lambda b, h: (b, v_blk + h)),
        pl.BlockSpec((1, HEAD_DIM), lambda b, h: (0, 0)),
        pl.BlockSpec((1, HEAD_DIM), lambda b, h: (0, 0)),
    ]
    args = [proj, proj, proj, qn, kn]
    if rope_tabs is not None:
        in_specs += [pl.BlockSpec((n, HEAD_DIM), lambda b, h: (0, 0))] * 2
        args += list(rope_tabs)
    if cache is not None:
        in_specs += [pl.BlockSpec((None, None, PAST_LEN, HEAD_DIM), lambda b, h: (b, l, 0, h))] * 2
        args += list(cache)
    out_specs = [pl.BlockSpec((n, qw), lambda b, h: (b, h))]
    out_shape = [jax.ShapeDtypeStruct((T, D_ATTN), BF16)]
    if emit_kv:
        out_specs += [pl.BlockSpec((None, n, HEAD_DIM), lambda b, h: (b, 0, h))] * 2
        out_shape += [jax.ShapeDtypeStruct((B, n, D_KV), F32)] * 2
    return pl.pallas_call(
        functools.partial(_attn_kernel, n=n, past=past, rope=rope_tabs is not None,
                          emit_kv=emit_kv, qb=256),
        grid=(B, N_KV_HEADS),
        in_specs=in_specs,
        out_specs=out_specs,
        out_shape=out_shape,
        scratch_shapes=[pltpu.VMEM((n + past, HEAD_DIM), BF16)] * 2,
        compiler_params=_params(2),
        name="attention",
    )(*args)


CONV_COL = D_ATTN + 2 * D_KV
RWKV_COL = CONV_COL + 3 * D_CONV
ROWS_A = 256
SUB = 16
CHUNKS_A = 2
assert CHUNK // SUB == 4


def _split2(x):
    hi = x.astype(BF16)
    return hi, (x - hi.astype(F32)).astype(BF16)


def _dot_x3(a, b):
    ah, al = _split2(a)
    bh, bl = _split2(b)
    mm = lambda p, q: jnp.dot(p, q, preferred_element_type=F32)
    return mm(ah, bh) + mm(ah, bl) + mm(al, bh)


def _split_sum(x, pieces, dot_piece):
    acc = None
    rest = x
    for _ in range(pieces):
        part = rest.astype(BF16)
        rest = rest - part.astype(F32)
        term = dot_piece(part)
        acc = term if acc is None else acc + term
    return acc


def _mix_kernel(*refs, n, nseq, has_s0, emit_state, manual_fetch):
    it = iter(refs)
    if manual_fetch:
        proj_hbm, tail_hbm = next(it), next(it)
    else:
        cv_ref, um_ref, ut_ref = next(it), next(it), next(it)
    (mum_ref, mut_ref, cw_ref, w0_ref, wup_ref, a0_ref, aup_ref, gup_ref, kk_ref, ka_ref,
     rk_ref, lng_ref, lnb_ref, seg_ref) = (next(it) for _ in range(14))
    s0_ref = next(it) if has_s0 else None
    o_ref = next(it)
    so_ref = next(it) if emit_state else None
    al_s, be_s, kd_s, r_s, v_s, p_s, bon_s, g_s, y_s, s_scr = (next(it) for _ in range(10))
    w2_s, w1_s, q_s = next(it), next(it), next(it)
    if manual_fetch:
        cv_ref, um_ref, ut_ref, in_sem = next(it), next(it), next(it), next(it)

        def input_copies(step):
            rows = pl.ds(pl.multiple_of(step * (nseq * n), nseq * n), nseq * n)
            return (
                pltpu.make_async_copy(proj_hbm.at[rows, pl.ds(CONV_COL, 3 * D_CONV)], cv_ref, in_sem.at[0]),
                pltpu.make_async_copy(proj_hbm.at[rows, pl.ds(RWKV_COL, 3 * D_RWKV)], um_ref, in_sem.at[1]),
                pltpu.make_async_copy(tail_hbm.at[rows, :], ut_ref, in_sem.at[2]))

        step = pl.program_id(0)

        @pl.when(step == 0)
        def _():
            for cp in input_copies(0):
                cp.start()

        for cp in input_copies(step):
            cp.wait()
    blocks_per_seq = n // ROWS_A
    n_blocks = nseq * blocks_per_seq
    n_chunks = n // CHUNK
    per_block = ROWS_A // CHUNK

    def seg_sum(x):
        return _split_sum(x, 2, lambda part: jnp.dot(part, seg_ref[...], preferred_element_type=F32))

    PAIR = 2 * RWKV_HEAD
    N_PAIRS = RWKV_HEADS // 2

    def pair_tiles(x):
        return [x[:, p * PAIR:(p + 1) * PAIR] for p in range(N_PAIRS)]

    bi = lax.broadcasted_iota(jnp.int32, (ROWS_A, ROWS_A), 0)
    bj = lax.broadcasted_iota(jnp.int32, (ROWS_A, ROWS_A), 1)
    same_chunk = (bi // CHUNK) == (bj // CHUNK)
    cum_mat = [(same_chunk & (bj <= bi)).astype(BF16), (same_chunk & (bj >= bi)).astype(BF16)]

    def pass1(c, carry):
        c0 = pl.multiple_of(c * ROWS_A, ROWS_A)
        rows = pl.ds(c0, ROWS_A)
        prev_row = pl.ds(jnp.maximum(c0 - 1, 0), 1)
        next_row = pl.ds(jnp.minimum(c0 + ROWS_A, nseq * n - 1), 1)
        seq_first = (c % blocks_per_seq) == 0
        seq_last = (c % blocks_per_seq) == blocks_per_seq - 1

        def neighbours(load):
            cur = load(rows)
            rid = lax.broadcasted_iota(jnp.int32, cur.shape, 0)
            before = jnp.where(seq_first, 0.0, load(prev_row))
            after = jnp.where(seq_last, 0.0, load(next_row))
            prev = jnp.where(rid == 0, before, pltpu.roll(cur, 1, 0))
            nxt = jnp.where(rid == ROWS_A - 1, after, pltpu.roll(cur, ROWS_A - 1, 0))
            return cur, prev, nxt

        z, zp, zn = neighbours(lambda rs: cv_ref[rs, D_CONV:2 * D_CONV] * cv_ref[rs, 2 * D_CONV:3 * D_CONV])
        cw = cw_ref[...]
        conv = cv_ref[rows, 0:D_CONV] * (zp * cw[0:1, :] + z * cw[1:2, :] + zn * cw[2:3, :])
        o_ref[rows, 0:D_CONV] = conv.astype(BF16)

        u, up, un = neighbours(lambda rs: um_ref[rs, :])
        u = u + mum_ref[...] * (0.5 * (up + un) - u)
        t, tp, tn = neighbours(lambda rs: ut_ref[rs, :])
        ut = t + mut_ref[...] * (0.5 * (tp + tn) - t)
        r = u[:, 0:D_RWKV]
        k = u[:, D_RWKV:2 * D_RWKV]
        v = u[:, 2 * D_RWKV:3 * D_RWKV]
        kk = k * kk_ref[...]
        kk = kk * lax.rsqrt(seg_sum(kk * kk) + 1e-12)
        for p, v_p in enumerate(pair_tiles(v)):
            v_s[p, rows, :] = v_p.astype(BF16)
        tw = jnp.tanh(ut)
        a_sum = jnp.zeros((ROWS_A, D_RWKV), F32)
        for d in range(2):
            zz = w0_ref[d:d + 1, :] + _dot_x3(tw, wup_ref[d])
            lw = -math.exp(-0.5) * jax.nn.sigmoid(zz)
            a = jax.nn.sigmoid(a0_ref[d:d + 1, :] + _dot_x3(ut, aup_ref[d]))
            a_sum = a_sum + a
            cum = _split_sum(lw, 3, lambda part: jnp.dot(cum_mat[d], part, preferred_element_type=F32))
            e_out = jnp.exp(-cum)
            kd = k * (1.0 + (a - 1.0) * ka_ref[...])
            scaled = ((al_s, -kk * jnp.exp(cum - lw)), (be_s, kk * a * e_out), (kd_s, kd * e_out),
                      (r_s, r * jnp.exp(cum)))
            for ref, val in scaled:
                for p, val_p in enumerate(pair_tiles(val)):
                    ref[d, p, rows, :] = val_p.astype(BF16)
            for j in range(per_block):
                last = j * CHUNK + (CHUNK - 1 if d == 0 else 0)
                for p, dec_p in enumerate(pair_tiles(jnp.exp(cum[last:last + 1, :]))):
                    p_s[d, c * per_block + j, p:p + 1, :] = dec_p
        kd_sum = k * (2.0 + (a_sum - 2.0) * ka_ref[...])
        bon_s[rows, :] = seg_sum(r * kd_sum * rk_ref[...]) * v
        g_s[rows, :] = _dot_x3(jax.nn.sigmoid(ut), gup_ref[...])
        y_s[rows, :] = jnp.zeros((ROWS_A, D_RWKV), F32)
        return carry

    lax.fori_loop(0, n_blocks, pass1, 0)

    if manual_fetch:
        @pl.when(step + 1 < pl.num_programs(0))
        def _():
            for cp in input_copies(step + 1):
                cp.start()

    for s in range(nseq):
        for d in range(2):
            for p in range(N_PAIRS):
                if has_s0:
                    s_scr[s, d, p] = jnp.concatenate([s0_ref[d, 2 * p], s0_ref[d, 2 * p + 1]], axis=1)
                else:
                    s_scr[s, d, p] = jnp.zeros((RWKV_HEAD, PAIR), F32)

    ri = lax.broadcasted_iota(jnp.int32, (CHUNK, PAIR), 0)
    lane = lax.broadcasted_iota(jnp.int32, (CHUNK, PAIR), 1)
    ci = lane % CHUNK
    low_half = lane < RWKV_HEAD
    eye = (ci == ri).astype(F32)
    diag_blk = (ri // SUB) == (ci // SUB)
    n_rounds = int(math.log2(SUB)) - 1

    def bd(x):
        zero = jnp.zeros_like(x)
        return jnp.concatenate([jnp.where(low_half, x, zero), jnp.where(low_half, zero, x)], axis=0)

    def pair_mm(a, b):
        return _mm(a, bd(b.astype(BF16)))

    def pair_mm_nt(a, b):
        return _mm(a, bd(b.astype(BF16)), _NT)

    def solve_step(c, carry):
        chains = []
        for s, d, j in ((s, d, j) for s in range(nseq) for d in range(2) for j in range(CHUNKS_A)):
            cc = s * n_chunks + c * CHUNKS_A + j
            rows = pl.ds(pl.multiple_of(cc * CHUNK, CHUNK), CHUNK)
            for p in range(N_PAIRS):
                chains.append(dict(
                    d=d, p=p, rows=rows,
                    strict=(ci < ri) if d == 0 else (ci > ri), incl=(ci <= ri) if d == 0 else (ci >= ri),
                    al=al_s[d, p, rows, :], be=be_s[d, p, rows, :], kd=kd_s[d, p, rows, :],
                    r=r_s[d, p, rows, :], v=v_s[p, rows, :]))

        def each(fn, *lists):
            return [fn(*args) for args in zip(chains, *lists)]

        alr = each(lambda ch: jnp.concatenate([ch['al'], ch['r']], axis=0))
        g_be = each(lambda ch, m: pair_mm_nt(m, ch['be']), alr)
        g_kd = each(lambda ch, m: pair_mm_nt(m, ch['kd']), alr)
        a = each(lambda ch, m: jnp.where(ch['strict'], m[0:CHUNK], 0.0), g_be)
        a_ak = each(lambda ch, m: jnp.where(ch['strict'], m[0:CHUNK], 0.0), g_kd)
        q = each(lambda ch, mb, mk: jnp.concatenate(
            [jnp.where(ch['incl'], mb[CHUNK:2 * CHUNK], 0.0), jnp.where(ch['incl'], mk[CHUNK:2 * CHUNK], 0.0)],
            axis=1).astype(BF16), g_be, g_kd)
        akv = each(lambda ch, m: pair_mm(m, ch['v']), a_ak)
        xs = [jnp.where(diag_blk, m, 0.0) for m in a]
        low = [jnp.where(diag_blk, 0.0, m) for m in a]
        ts = [eye + x for x in xs]
        xs = [pair_mm(x, x) for x in xs]
        for _ in range(n_rounds - 1):
            z = [pair_mm(jnp.concatenate([x, t], axis=0), x) for x, t in zip(xs, ts)]
            xs = [m[0:CHUNK] for m in z]
            ts = [t + m[CHUNK:2 * CHUNK] for t, m in zip(ts, z)]
        ts = [t + pair_mm(t, x) for t, x in zip(ts, xs)]
        ms = [pair_mm(t, m) for t, m in zip(ts, low)]
        sol = each(lambda ch, t, m: _mm(t, jnp.concatenate([bd(m.astype(BF16)), bd(ch['al'])], axis=1)),
                   ts, akv)
        m2 = [pair_mm(m, m) for m in ms]
        ims = [eye + m for m in ms]
        ns = [im + pair_mm(im, mm2) for im, mm2 in zip(ims, m2)]
        w = [_mm(nn, jnp.concatenate([bd(x[:, 0:PAIR].astype(BF16)), bd(x[:, PAIR:2 * PAIR].astype(BF16))],
                                     axis=1)) for nn, x in zip(ns, sol)]
        for ch, x, qq in zip(chains, w, q):
            w2_s[ch['d'], ch['p'], ch['rows'], :] = x[:, 0:PAIR]
            w1_s[ch['d'], ch['p'], ch['rows'], :] = x[:, PAIR:2 * PAIR].astype(BF16)
            q_s[ch['d'], ch['p'], ch['rows'], :] = qq
        return carry

    lax.fori_loop(0, n_chunks // CHUNKS_A, solve_step, 0)

    def chunk_step(c, carry):
        chains = []
        for s, d in ((s, d) for s in range(nseq) for d in range(2)):
            cc = s * n_chunks + (c if d == 0 else n_chunks - 1 - c)
            rows = pl.ds(pl.multiple_of(cc * CHUNK, CHUNK), CHUNK)
            p_tile = p_s[d, cc]
            for p in range(N_PAIRS):
                chains.append(dict(
                    s=s, d=d, p=p, rows=rows, dec=p_tile[p:p + 1, :],
                    be=be_s[d, p, rows, :], kd=kd_s[d, p, rows, :], r=r_s[d, p, rows, :],
                    v=v_s[p, rows, :], w2=w2_s[d, p, rows, :], w1=w1_s[d, p, rows, :],
                    q=q_s[d, p, rows, :]))

        def each(fn, *lists):
            return [fn(*args) for args in zip(chains, *lists)]

        st = each(lambda ch: s_scr[ch['s'], ch['d'], ch['p']])
        st_bd = [bd(t.astype(BF16)) for t in st]
        u = each(lambda ch, t: ch['w2'] + _mm(ch['w1'], t, _NT), st_bd)
        ub = [x.astype(BF16) for x in u]
        ys = each(lambda ch, t, uu: _mm(ch['r'], t, _NT)
                  + _mm(ch['q'], jnp.concatenate([bd(uu), bd(ch['v'])], axis=0)), st_bd, ub)
        full = each(lambda ch, uu: _mm(jnp.concatenate([uu, ch['v']], axis=0),
                                       jnp.concatenate([ch['be'], ch['kd']], axis=0), _TN), ub)
        s_new = each(lambda ch, t, m: (t + jnp.where(low_half, m[0:RWKV_HEAD], m[RWKV_HEAD:PAIR])) * ch['dec'],
                     st, full)
        for ch, t in zip(chains, s_new):
            s_scr[ch['s'], ch['d'], ch['p']] = t
        for ch, y in zip(chains, ys):
            y_s[ch['rows'], ch['p'] * PAIR:(ch['p'] + 1) * PAIR] += y
        return carry

    lax.fori_loop(0, n_chunks, chunk_step, 0)

    if emit_state:
        for s in range(nseq):
            for d in range(2):
                for p in range(N_PAIRS):
                    so_ref[s, d, 2 * p] = s_scr[s, d, p][:, 0:RWKV_HEAD]
                    so_ref[s, d, 2 * p + 1] = s_scr[s, d, p][:, RWKV_HEAD:PAIR]

    def pass3(c, carry):
        rows = pl.ds(pl.multiple_of(c * ROWS_A, ROWS_A), ROWS_A)
        y = y_s[rows, :]
        yc = y - seg_sum(y) * (1.0 / RWKV_HEAD)
        var = seg_sum(yc * yc) * (1.0 / RWKV_HEAD)
        yn = yc * lax.rsqrt(var + GN_EPS) * lng_ref[...] + lnb_ref[...]
        o_ref[rows, D_CONV:D_CONV + D_RWKV] = ((yn + bon_s[rows, :]) * g_s[rows, :]).astype(BF16)
        return carry

    lax.fori_loop(0, n_blocks, pass3, 0)


def _pad_rows(w, lo):
    return jnp.pad(w, [(0, 0)] * (w.ndim - 2) + [(lo, D_TAIL - lo - w.shape[-2]), (0, 0)])


def _mix_call(proj, tail, P, l, n, nseq, s0=None, emit_state=False):
    T = proj.shape[0]
    B = T // n
    rows = nseq * n
    assert s0 is None or nseq == 1
    full = lambda shape: pl.BlockSpec(shape, lambda b: (0,) * len(shape))
    lanes = lambda w: -(-w // 128) * 128
    in_bytes = rows * (3 * D_CONV + 3 * D_RWKV + lanes(D_TAIL)) * 4
    scratch_bytes = rows * (4 * 2 * D_RWKV * 2 + D_RWKV * 2 + 3 * D_RWKV * 4
                            + 2 * D_RWKV * (4 + 2 + 2 * 2))
    out_bytes = 2 * rows * (D_CONV + D_RWKV) * 2
    temporaries = 12 * 1024 * 1024
    manual_fetch = 2 * in_bytes + scratch_bytes + out_bytes + temporaries > VMEM_LIMIT_BYTES
    if manual_fetch:
        in_specs = [pl.BlockSpec(memory_space=pl.ANY), pl.BlockSpec(memory_space=pl.ANY)]
        token_args = [proj, tail]
    else:
        in_specs = [
            pl.BlockSpec((rows, 3 * D_CONV), lambda b: (b, CONV_COL // (3 * D_CONV))),
            pl.BlockSpec((rows, 3 * D_RWKV), lambda b: (b, RWKV_COL // (3 * D_RWKV))),
            pl.BlockSpec((rows, D_TAIL), lambda b: (b, 0)),
        ]
        token_args = [proj, proj, tail]
    in_specs += [
        full((1, 3 * D_RWKV)), full((1, D_TAIL)), full((3, D_CONV)),
        full((2, D_RWKV)), full((2, D_TAIL, D_RWKV)), full((2, D_RWKV)), full((2, D_TAIL, D_RWKV)),
        full((D_TAIL, D_RWKV)), full((1, D_RWKV)), full((1, D_RWKV)), full((1, D_RWKV)),
        full((1, D_RWKV)), full((1, D_RWKV)), full((D_RWKV, D_RWKV)),
    ]
    head = jnp.arange(D_RWKV) // RWKV_HEAD
    seg = (head[:, None] == head[None, :]).astype(BF16)
    mu = P['rwkv_mu'][l]
    row = lambda a: a.reshape(1, -1)
    args = token_args + [row(mu[:3 * D_RWKV]), row(mu[3 * D_RWKV:]), P['conv_w'][l],
            P['rwkv_w0'][l], _pad_rows(P['rwkv_w_up'][l], 0), P['rwkv_a0'][l],
            _pad_rows(P['rwkv_a_up'][l], DECAY_RANK), _pad_rows(P['rwkv_g_up'][l], DECAY_RANK + A_RANK),
            row(P['rwkv_k_k'][l]), row(P['rwkv_k_a'][l]), row(P['rwkv_r_k'][l]),
            row(P['rwkv_ln_g'][l]), row(P['rwkv_ln_b'][l]), seg]
    if s0 is not None:
        in_specs.append(pl.BlockSpec((None, None, 2, RWKV_HEADS, RWKV_HEAD, RWKV_HEAD),
                                     lambda b: (b, l, 0, 0, 0, 0)))
        args.append(s0)
    out_specs = [pl.BlockSpec((rows, D_CONV + D_RWKV), lambda b: (b, 0))]
    out_shape = [jax.ShapeDtypeStruct((T, D_CONV + D_RWKV), BF16)]
    state_shape = (2, RWKV_HEADS, RWKV_HEAD, RWKV_HEAD)
    if emit_state:
        out_specs.append(pl.BlockSpec((nseq,) + state_shape, lambda b: (b, 0, 0, 0, 0)))
        out_shape.append(jax.ShapeDtypeStruct((B,) + state_shape, F32))
    pair = 2 * RWKV_HEAD
    n_pairs = RWKV_HEADS // 2
    per_pair = lambda: pltpu.VMEM((2, n_pairs, rows, pair), BF16)
    scratch_shapes = [per_pair(), per_pair(), per_pair(), per_pair(),
                      pltpu.VMEM((n_pairs, rows, pair), BF16),
                      pltpu.VMEM((2, rows // CHUNK, n_pairs, pair), F32),
                      pltpu.VMEM((rows, D_RWKV), F32), pltpu.VMEM((rows, D_RWKV), F32),
                      pltpu.VMEM((rows, D_RWKV), F32),
                      pltpu.VMEM((nseq, 2, n_pairs, RWKV_HEAD, pair), F32),
                      pltpu.VMEM((2, n_pairs, rows, pair), F32), pltpu.VMEM((2, n_pairs, rows, pair), BF16),
                      pltpu.VMEM((2, n_pairs, rows, 2 * pair), BF16)]
    if manual_fetch:
        scratch_shapes += [pltpu.VMEM((rows, 3 * D_CONV), F32), pltpu.VMEM((rows, 3 * D_RWKV), F32),
                           pltpu.VMEM((rows, D_TAIL), F32), pltpu.SemaphoreType.DMA((3,))]
    semantics = ("arbitrary",) if manual_fetch else ("parallel",)
    return pl.pallas_call(
        functools.partial(_mix_kernel, n=n, nseq=nseq, has_s0=s0 is not None, emit_state=emit_state,
                          manual_fetch=manual_fetch),
        grid=(B // nseq,),
        in_specs=in_specs,
        out_specs=out_specs,
        out_shape=out_shape,
        scratch_shapes=scratch_shapes,
        compiler_params=pltpu.CompilerParams(dimension_semantics=semantics,
                                             vmem_limit_bytes=VMEM_LIMIT_BYTES),
        name="conv_rwkv",
    )(*args)


def _residual_copy(x_hbm, x_buf, sem):
    rows = x_buf.shape[0]
    start = pl.multiple_of(pl.program_id(0) * rows, rows)
    return pltpu.make_async_copy(x_hbm.at[pl.ds(start, rows), :], x_buf, sem)


def _outproj_kernel(at_ref, cr_ref, w_ref, x_hbm, ga_ref, shf_ref, scf_ref, g1_ref, g2_ref,
                    x1_ref, h2_ref, x_ref, x_sem):
    kt = pl.program_id(1)

    @pl.when(kt == 0)
    def _():
        _residual_copy(x_hbm, x_ref, x_sem).start()
        x1_ref[...] = jnp.zeros(x1_ref.shape, F32)

    lhs = jnp.where(kt < 2, at_ref[...], cr_ref[...])
    x1_ref[...] += jnp.dot(lhs, w_ref[...].astype(BF16), preferred_element_type=F32)

    @pl.when(kt == pl.num_programs(1) - 1)
    def _():
        _residual_copy(x_hbm, x_ref, x_sem).wait()
        x1 = x_ref[...] + ga_ref[...] * _rms(x1_ref[...], g1_ref[...])
        x1_ref[...] = x1
        h2_ref[...] = (_rms(x1, g2_ref[...]) * (1.0 + scf_ref[...]) + shf_ref[...]).astype(BF16)


def _outproj_call(attn, cr, x, mod3, ng3, w_out, l, row_fn, tm):
    T = x.shape[0]
    tk = 512
    return pl.pallas_call(
        _outproj_kernel,
        grid=(T // tm, D_MODEL // tk),
        in_specs=[
            pl.BlockSpec((tm, tk), lambda i, k: (i, jnp.minimum(k, 1))),
            pl.BlockSpec((tm, tk), lambda i, k: (i, jnp.maximum(k - 2, 0))),
            pl.BlockSpec((None, tk, D_MODEL), lambda i, k: (l, k, 0)),
            pl.BlockSpec(memory_space=pl.ANY),
            _mod_spec(row_fn, 2), _mod_spec(row_fn, 3), _mod_spec(row_fn, 4),
            _ng_spec(l, 1), _ng_spec(l, 2),
        ],
        out_specs=[
            pl.BlockSpec((tm, D_MODEL), lambda i, k: (i, 0)),
            pl.BlockSpec((tm, D_MODEL), lambda i, k: (i, 0)),
        ],
        out_shape=[jax.ShapeDtypeStruct((T, D_MODEL), F32), jax.ShapeDtypeStruct((T, D_MODEL), BF16)],
        scratch_shapes=[pltpu.VMEM((tm, D_MODEL), F32), pltpu.SemaphoreType.DMA(())],
        compiler_params=_params(2),
        name="out_proj",
    )(attn, cr, w_out, x, mod3, mod3, mod3, ng3, ng3)


def _ffn_kernel(h_ref, x1_hbm, wg_ref, wu_ref, wd_ref, gf_ref, g3_ref, o_ref, x1_ref, x1_sem):
    f = pl.program_id(1)

    @pl.when(f == 0)
    def _():
        _residual_copy(x1_hbm, x1_ref, x1_sem).start()
        o_ref[...] = jnp.zeros(o_ref.shape, F32)

    h = h_ref[...]
    gate = _dot(h, wg_ref[...])
    up = _dot(h, wu_ref[...])
    o_ref[...] += _dot(gate * jax.nn.sigmoid(gate) * up, wd_ref[...])

    @pl.when(f == pl.num_programs(1) - 1)
    def _():
        _residual_copy(x1_hbm, x1_ref, x1_sem).wait()
        o_ref[...] = x1_ref[...] + gf_ref[...] * _rms(o_ref[...], g3_ref[...])


def _ffn_call(h2, x1, mod3, ng3, w_gate, w_up, w_down, l, row_fn, tm):
    T = x1.shape[0]
    tf = 256
    return pl.pallas_call(
        _ffn_kernel,
        grid=(T // tm, D_FF // tf),
        in_specs=[
            pl.BlockSpec((tm, D_MODEL), lambda i, f: (i, 0)),
            pl.BlockSpec(memory_space=pl.ANY),
            pl.BlockSpec((None, D_MODEL, tf), lambda i, f: (l, 0, f)),
            pl.BlockSpec((None, D_MODEL, tf), lambda i, f: (l, 0, f)),
            pl.BlockSpec((None, tf, D_MODEL), lambda i, f: (l, f, 0)),
            _mod_spec(row_fn, 5),
            _ng_spec(l, 3),
        ],
        out_specs=pl.BlockSpec((tm, D_MODEL), lambda i, f: (i, 0)),
        out_shape=jax.ShapeDtypeStruct((T, D_MODEL), F32),
        scratch_shapes=[pltpu.VMEM((tm, D_MODEL), F32), pltpu.SemaphoreType.DMA(())],
        compiler_params=_params(2),
        name="ffn",
    )(h2, x1, w_gate, w_up, w_down, mod3, ng3)


def _rope_tables(n):
    rows = n // GRID_W
    row = jnp.repeat(jnp.arange(rows), GRID_W).astype(F32)
    col = jnp.tile(jnp.arange(GRID_W), rows).astype(F32)
    half = HEAD_DIM // 2
    inv = 1.0 / (ROPE_THETA ** (jnp.arange(0, half, 2, dtype=F32) / half))
    ar = row[:, None] * inv
    ac = col[:, None] * inv
    ang = jnp.concatenate([ar, ar, ac, ac], axis=-1)
    return jnp.cos(ang), jnp.sin(ang)


ROW_TILE = 1024
INPROJ_ROW_TILE = 2048


def _layer(x, mod3, ng3, P, l, n, mix_seqs, row_fn_of_tm, rope_tabs=None, cache=None, s0=None, emit=False):
    proj, tail = _inproj_call(x, mod3, ng3, P['w_in_t'], P['w_tail_t'], l,
                              row_fn_of_tm(INPROJ_ROW_TILE // 2), INPROJ_ROW_TILE)
    attn_out = _attn_call(proj, P['q_norm'][l].reshape(1, -1), P['k_norm'][l].reshape(1, -1), l, n,
                          rope_tabs=rope_tabs, cache=cache, emit_kv=emit)
    mix_out = _mix_call(proj, tail, P, l, n, mix_seqs, s0=s0, emit_state=emit)
    attn = attn_out[0]
    cr = mix_out[0]
    x1, h2 = _outproj_call(attn, cr, x, mod3, ng3, P['w_out'], l, row_fn_of_tm(ROW_TILE), ROW_TILE)
    x2 = _ffn_call(h2, x1, mod3, ng3, P['w_gate'], P['w_up'], P['w_down'], l, row_fn_of_tm(ROW_TILE),
                   ROW_TILE)
    if emit:
        return x2, attn_out[1], attn_out[2], mix_out[1]
    return x2


def kernel(x_prompt, x_sample, cache_k, cache_v, state_rwkv, c, c_ctx, w_ada, b_ada, norm_g, w_in, q_norm, k_norm, conv_w, rwkv_mu, rwkv_w0, rwkv_w_up, rwkv_a0, rwkv_a_up, rwkv_g_up, rwkv_k_k, rwkv_k_a, rwkv_r_k, rwkv_ln_g, rwkv_ln_b, w_out, w_gate, w_up, w_down):
    P = {'q_norm': q_norm, 'k_norm': k_norm, 'conv_w': conv_w, 'rwkv_mu': rwkv_mu,
         'rwkv_w0': rwkv_w0, 'rwkv_w_up': rwkv_w_up, 'rwkv_a0': rwkv_a0, 'rwkv_a_up': rwkv_a_up,
         'rwkv_g_up': rwkv_g_up, 'rwkv_k_k': rwkv_k_k, 'rwkv_k_a': rwkv_k_a, 'rwkv_r_k': rwkv_r_k,
         'rwkv_ln_g': rwkv_ln_g, 'rwkv_ln_b': rwkv_ln_b, 'w_out': w_out,
         'w_gate': w_gate, 'w_up': w_up, 'w_down': w_down}
    w_in_t = jnp.swapaxes(w_in, 1, 2)
    P['w_in_t'] = w_in_t
    P['w_tail_t'] = w_in_t[:, D_MAIN:, :]
    batch, seq, _ = x_prompt.shape
    dec_batch, dec_seq, _ = x_sample.shape

    cc = jnp.zeros((8, D_MODEL), F32).at[0].set(c_ctx).at[1:1 + dec_batch].set(c)
    mod3 = _ada_call(cc, w_ada, b_ada).reshape(DEPTH * 8, 1, N_MOD * D_MODEL)
    ng3 = norm_g.reshape(DEPTH * 4, 1, D_MODEL)

    ck = cache_k.reshape(dec_batch, DEPTH, PAST_LEN, D_KV)
    cv = cache_v.reshape(dec_batch, DEPTH, PAST_LEN, D_KV)
    rope_tabs = _rope_tables(dec_seq)

    xp = x_prompt.reshape(batch * seq, D_MODEL)
    xs = x_sample.reshape(dec_batch * dec_seq, D_MODEL)
    ks, vs, ss = [], [], []
    for l in range(DEPTH):
        ctx_row = lambda tm, l=l: (lambda i: l * 8)
        smp_row = lambda tm, l=l: (lambda i: l * 8 + 1 + (i * tm) // dec_seq)
        xp, k_l, v_l, s_l = _layer(xp, mod3, ng3, P, l, seq, 2, ctx_row, emit=True)
        ks.append(k_l.reshape(batch, seq, N_KV_HEADS, HEAD_DIM))
        vs.append(v_l.reshape(batch, seq, N_KV_HEADS, HEAD_DIM))
        ss.append(s_l)
        xs = _layer(xs, mod3, ng3, P, l, dec_seq, 1, smp_row, rope_tabs=rope_tabs, cache=(ck, cv),
                    s0=state_rwkv)
    return (xp.reshape(batch, seq, D_MODEL), xs.reshape(dec_batch, dec_seq, D_MODEL),
            jnp.stack(ks, axis=1), jnp.stack(vs, axis=1), jnp.stack(ss, axis=1))
```

```python
import functools
import math

import jax
import jax.numpy as jnp
from jax import lax
from jax.experimental import pallas as pl
from jax.experimental.pallas import tpu as pltpu

D_MODEL = 2048
DEPTH = 2
GRID_W = 64
HEAD_DIM = 128
N_HEADS = 8
N_KV_HEADS = 2
GROUP = N_HEADS // N_KV_HEADS
D_ATTN = N_HEADS * HEAD_DIM
D_KV = N_KV_HEADS * HEAD_DIM
ROPE_THETA = 10000.0
D_CONV = 512
D_RWKV = 512
RWKV_HEAD = 64
RWKV_HEADS = D_RWKV // RWKV_HEAD
DECAY_RANK = 32
A_RANK = 32
GATE_RANK = 96
D_TAIL = DECAY_RANK + A_RANK + GATE_RANK
D_IN = D_ATTN + 2 * D_KV + 3 * D_CONV + 3 * D_RWKV + D_TAIL
D_MAIN = D_IN - D_TAIL
D_FF = 5632
N_MOD = 6
EPS = 1e-6
GN_EPS = 64e-5
PAST_LEN = 256

VMEM_LIMIT_BYTES = 56 * 1024 * 1024
CHUNK = 64
BF16 = jnp.bfloat16
F32 = jnp.float32


def _params(n_grid, **kw):
    sem = ("parallel",) + ("arbitrary",) * (n_grid - 1)
    return pltpu.CompilerParams(dimension_semantics=sem, vmem_limit_bytes=VMEM_LIMIT_BYTES, **kw)


def _rms(x, g):
    return x * lax.rsqrt(jnp.mean(x * x, axis=-1, keepdims=True) + EPS) * g


def _dot(a, b):
    return jnp.dot(a.astype(BF16), b.astype(BF16), preferred_element_type=F32)


_NN = (((1,), (0,)), ((), ()))
_NT = (((1,), (1,)), ((), ()))
_TN = (((0,), (0,)), ((), ()))


def _mm(a, b, dims=_NN):
    return lax.dot_general(a.astype(BF16), b.astype(BF16), dims, preferred_element_type=F32)


def _ada_kernel(c_ref, w_ref, b_ref, o_ref):
    c = c_ref[...]
    s = c * jax.nn.sigmoid(c)
    o_ref[...] = _dot(s, w_ref[...]) + b_ref[...]


def _ada_call(cc, w_ada, b_ada):
    tn = 512
    n_out = N_MOD * D_MODEL
    return pl.pallas_call(
        _ada_kernel,
        grid=(DEPTH, n_out // tn),
        in_specs=[
            pl.BlockSpec((8, D_MODEL), lambda l, n: (0, 0)),
            pl.BlockSpec((None, D_MODEL, tn), lambda l, n: (l, 0, n)),
            pl.BlockSpec((None, 1, tn), lambda l, n: (l, 0, n)),
        ],
        out_specs=pl.BlockSpec((None, 8, tn), lambda l, n: (l, 0, n)),
        out_shape=jax.ShapeDtypeStruct((DEPTH, 8, n_out), F32),
        compiler_params=_params(2),
        name="adaln",
    )(cc, w_ada, b_ada.reshape(DEPTH, 1, n_out))


def _mod_spec(row_fn, chunk):
    return pl.BlockSpec((None, 1, D_MODEL), lambda i, *_: (row_fn(i), 0, chunk))


def _ng_spec(l, j):
    return pl.BlockSpec((None, 1, D_MODEL), lambda i, *_: (l * 4 + j, 0, 0))


INPROJ_NORM_ROWS = 256
INPROJ_VMEM_LIMIT_BYTES = 60 * 1024 * 1024


def _row_tile_copy(x_hbm, x_buf, sem, tile):
    rows = x_buf.shape[0]
    return pltpu.make_async_copy(x_hbm.at[pl.ds(pl.multiple_of(tile * rows, rows), rows), :], x_buf, sem)


def _inproj_kernel(x_hbm, sha_ref, sca_ref, shb_ref, scb_ref, g_ref, w_ref, wt_ref, o_ref, ot_ref,
                   x_buf, h_scr, x_sem):
    i = pl.program_id(0)

    @pl.when(pl.program_id(1) == 0)
    def _():
        @pl.when(i == 0)
        def _():
            _row_tile_copy(x_hbm, x_buf, x_sem, 0).start()

        _row_tile_copy(x_hbm, x_buf, x_sem, i).wait()
        half = x_buf.shape[0] // 2
        for r0 in range(0, x_buf.shape[0], INPROJ_NORM_ROWS):
            rs = slice(r0, r0 + INPROJ_NORM_ROWS)
            sh_ref, sc_ref = (sha_ref, sca_ref) if r0 < half else (shb_ref, scb_ref)
            h = _rms(x_buf[rs, :], g_ref[...]) * (1.0 + sc_ref[...]) + sh_ref[...]
            h_scr[rs, :] = h.astype(BF16)

        @pl.when(i + 1 < pl.num_programs(0))
        def _():
            _row_tile_copy(x_hbm, x_buf, x_sem, i + 1).start()

        ot_ref[...] = _mm(h_scr[...], wt_ref[...], _NT)

    o_ref[...] = _mm(h_scr[...], w_ref[...], _NT)


def _inproj_call(x, mod3, ng3, w_in_t, w_tail_t, l, row_fn_half, tm):
    T = x.shape[0]
    tn = 512
    seg = lambda j: (lambda i: row_fn_half(2 * i + j))
    return pl.pallas_call(
        _inproj_kernel,
        grid=(T // tm, D_MAIN // tn),
        in_specs=[
            pl.BlockSpec(memory_space=pl.ANY),
            _mod_spec(seg(0), 0), _mod_spec(seg(0), 1),
            _mod_spec(seg(1), 0), _mod_spec(seg(1), 1),
            _ng_spec(l, 0),
            pl.BlockSpec((None, tn, D_MODEL), lambda i, n: (l, n, 0)),
            pl.BlockSpec((None, D_TAIL, D_MODEL), lambda i, n: (l, 0, 0)),
        ],
        out_specs=[
            pl.BlockSpec((tm, tn), lambda i, n: (i, n)),
            pl.BlockSpec((tm, D_TAIL), lambda i, n: (i, 0)),
        ],
        out_shape=[
            jax.ShapeDtypeStruct((T, D_MAIN), F32),
            jax.ShapeDtypeStruct((T, D_TAIL), F32),
        ],
        scratch_shapes=[pltpu.VMEM((tm, D_MODEL), F32), pltpu.VMEM((tm, D_MODEL), BF16),
                        pltpu.SemaphoreType.DMA(())],
        compiler_params=pltpu.CompilerParams(dimension_semantics=("arbitrary", "arbitrary"),
                                             vmem_limit_bytes=INPROJ_VMEM_LIMIT_BYTES),
        name="in_proj",
    )(x, mod3, mod3, mod3, mod3, ng3, w_in_t, w_tail_t)


def _rope(x, cos, sin):
    lane = lax.broadcasted_iota(jnp.int32, x.shape, 1)
    first = (lane % (HEAD_DIM // 2)) < (HEAD_DIM // 4)
    rot = jnp.where(first, -pltpu.roll(x, HEAD_DIM - HEAD_DIM // 4, 1), pltpu.roll(x, HEAD_DIM // 4, 1))
    return x * cos + rot * sin


def _attn_kernel(*refs, n, past, rope, emit_kv, qb):
    it = iter(refs)
    q_ref, k_ref, v_ref, qn_ref, kn_ref = (next(it) for _ in range(5))
    cos_ref = sin_ref = ck_ref = cv_ref = None
    if rope:
        cos_ref, sin_ref = next(it), next(it)
    if past:
        ck_ref, cv_ref = next(it), next(it)
    o_ref = next(it)
    if emit_kv:
        ko_ref, vo_ref = next(it), next(it)
    k_scr, v_scr = next(it), next(it)

    k = _rms(k_ref[...], kn_ref[...])
    v = v_ref[...]
    if emit_kv:
        ko_ref[...] = k
        vo_ref[...] = v
    if rope:
        k = _rope(k, cos_ref[...], sin_ref[...])
    k_scr[0:n, :] = k.astype(BF16)
    v_scr[0:n, :] = v.astype(BF16)
    if past:
        k_scr[n:n + past, :] = ck_ref[...].astype(BF16)
        v_scr[n:n + past, :] = cv_ref[...].astype(BF16)
    exp2_scale = HEAD_DIM ** -0.5 * math.log2(math.e)

    blocks_per_step = min(n // qb, 4)

    def block(b, carry):
        items = [(pl.ds(pl.multiple_of((b * blocks_per_step + j) * qb, qb), qb),
                  slice(g * HEAD_DIM, (g + 1) * HEAD_DIM))
                 for j in range(blocks_per_step) for g in range(GROUP)]

        def scores(item):
            rows, cols = item
            q = _rms(q_ref[rows, cols], qn_ref[...])
            if rope:
                q = _rope(q, cos_ref[rows, :], sin_ref[rows, :])
            return lax.dot_general(q.astype(BF16), k_scr[...], _NT, preferred_element_type=F32)

        def softmax(s):
            p = jnp.exp2((s - jnp.max(s, axis=-1, keepdims=True)) * exp2_scale)
            return p.astype(BF16), jnp.sum(p, axis=-1, keepdims=True)

        def values(item, pd):
            rows, cols = item
            o = jnp.dot(pd[0], v_scr[...], preferred_element_type=F32)
            o_ref[rows, cols] = (o / pd[1]).astype(BF16)

        if blocks_per_step == 1:
            ps = [softmax(s) for s in [scores(item) for item in items]]
            for item, pd in zip(items, ps):
                values(item, pd)
            return carry
        depth = 2
        ss, ps = {}, {}
        for i in range(len(items) + depth):
            if i < len(items):
                ss[i] = scores(items[i])
            if 1 <= i <= len(items):
                ps[i - 1] = softmax(ss.pop(i - 1))
            if i >= depth:
                values(items[i - depth], ps.pop(i - depth))
        return carry

    lax.fori_loop(0, n // (qb * blocks_per_step), block, 0)


def _attn_call(proj, qn, kn, l, n, rope_tabs=None, cache=None, emit_kv=False):
    T = proj.shape[0]
    B = T // n
    past = PAST_LEN if cache is not None else 0
    qw = GROUP * HEAD_DIM
    k_blk = D_ATTN // HEAD_DIM
    v_blk = (D_ATTN + D_KV) // HEAD_DIM
    in_specs = [
        pl.BlockSpec((n, qw), lambda b, h: (b, h)),
        pl.BlockSpec((n, HEAD_DIM), lambda b, h: (b, k_blk + h)),
        pl.BlockSpec((n, HEAD_DIM), lambda b, h: (b, v_blk + h)),
        pl.BlockSpec((1, HEAD_DIM), lambda b, h: (0, 0)),
        pl.BlockSpec((1, HEAD_DIM), lambda b, h: (0, 0)),
    ]
    args = [proj, proj, proj, qn, kn]
    if rope_tabs is not None:
        in_specs += [pl.BlockSpec((n, HEAD_DIM), lambda b, h: (0, 0))] * 2
        args += list(rope_tabs)
    if cache is not None:
        in_specs += [pl.BlockSpec((None, None, PAST_LEN, HEAD_DIM), lambda b, h: (b, l, 0, h))] * 2
        args += list(cache)
    out_specs = [pl.BlockSpec((n, qw), lambda b, h: (b, h))]
    out_shape = [jax.ShapeDtypeStruct((T, D_ATTN), BF16)]
    if emit_kv:
        out_specs += [pl.BlockSpec((None, n, HEAD_DIM), lambda b, h: (b, 0, h))] * 2
        out_shape += [jax.ShapeDtypeStruct((B, n, D_KV), F32)] * 2
    return pl.pallas_call(
        functools.partial(_attn_kernel, n=n, past=past, rope=rope_tabs is not None,
                          emit_kv=emit_kv, qb=256),
        grid=(B, N_KV_HEADS),
        in_specs=in_specs,
        out_specs=out_specs,
        out_shape=out_shape,
        scratch_shapes=[pltpu.VMEM((n + past, HEAD_DIM), BF16)] * 2,
        compiler_params=_params(2),
        name="attention",
    )(*args)


CONV_COL = D_ATTN + 2 * D_KV
RWKV_COL = CONV_COL + 3 * D_CONV
ROWS_A = 256
SUB = 16
CHUNKS_A = 2
assert CHUNK // SUB == 4


def _split2(x):
    hi = x.astype(BF16)
    return hi, (x - hi.astype(F32)).astype(BF16)


def _dot_x3(a, b):
    ah, al = _split2(a)
    bh, bl = _split2(b)
    mm = lambda p, q: jnp.dot(p, q, preferred_element_type=F32)
    return mm(ah, bh) + mm(ah, bl) + mm(al, bh)


def _split_sum(x, pieces, dot_piece):
    acc = None
    rest = x
    for _ in range(pieces):
        part = rest.astype(BF16)
        rest = rest - part.astype(F32)
        term = dot_piece(part)
        acc = term if acc is None else acc + term
    return acc


def _mix_kernel(*refs, n, nseq, has_s0, emit_state, manual_fetch):
    it = iter(refs)
    if manual_fetch:
        proj_hbm, tail_hbm = next(it), next(it)
    else:
        cv_ref, um_ref, ut_ref = next(it), next(it), next(it)
    (mum_ref, mut_ref, cw_ref, w0_ref, wup_ref, a0_ref, aup_ref, gup_ref, kk_ref, ka_ref,
     rk_ref, lng_ref, lnb_ref, seg_ref) = (next(it) for _ in range(14))
    s0_ref = next(it) if has_s0 else None
    o_ref = next(it)
    so_ref = next(it) if emit_state else None
    al_s, be_s, kd_s, r_s, v_s, p_s, bon_s, g_s, y_s, s_scr = (next(it) for _ in range(10))
    w2_s, w1_s, q_s = next(it), next(it), next(it)
    if manual_fetch:
        cv_ref, um_ref, ut_ref, in_sem = next(it), next(it), next(it), next(it)

        def input_copies(step):
            rows = pl.ds(pl.multiple_of(step * (nseq * n), nseq * n), nseq * n)
            return (
                pltpu.make_async_copy(proj_hbm.at[rows, pl.ds(CONV_COL, 3 * D_CONV)], cv_ref, in_sem.at[0]),
                pltpu.make_async_copy(proj_hbm.at[rows, pl.ds(RWKV_COL, 3 * D_RWKV)], um_ref, in_sem.at[1]),
                pltpu.make_async_copy(tail_hbm.at[rows, :], ut_ref, in_sem.at[2]))

        step = pl.program_id(0)

        @pl.when(step == 0)
        def _():
            for cp in input_copies(0):
                cp.start()

        for cp in input_copies(step):
            cp.wait()
    blocks_per_seq = n // ROWS_A
    n_blocks = nseq * blocks_per_seq
    n_chunks = n // CHUNK
    per_block = ROWS_A // CHUNK

    def seg_sum(x):
        return _split_sum(x, 2, lambda part: jnp.dot(part, seg_ref[...], preferred_element_type=F32))

    PAIR = 2 * RWKV_HEAD
    N_PAIRS = RWKV_HEADS // 2

    def pair_tiles(x):
        return [x[:, p * PAIR:(p + 1) * PAIR] for p in range(N_PAIRS)]

    bi = lax.broadcasted_iota(jnp.int32, (ROWS_A, ROWS_A), 0)
    bj = lax.broadcasted_iota(jnp.int32, (ROWS_A, ROWS_A), 1)
    same_chunk = (bi // CHUNK) == (bj // CHUNK)
    cum_mat = [(same_chunk & (bj <= bi)).astype(BF16), (same_chunk & (bj >= bi)).astype(BF16)]

    def pass1(c, carry):
        c0 = pl.multiple_of(c * ROWS_A, ROWS_A)
        rows = pl.ds(c0, ROWS_A)
        prev_row = pl.ds(jnp.maximum(c0 - 1, 0), 1)
        next_row = pl.ds(jnp.minimum(c0 + ROWS_A, nseq * n - 1), 1)
        seq_first = (c % blocks_per_seq) == 0
        seq_last = (c % blocks_per_seq) == blocks_per_seq - 1

        def neighbours(load):
            cur = load(rows)
            rid = lax.broadcasted_iota(jnp.int32, cur.shape, 0)
            before = jnp.where(seq_first, 0.0, load(prev_row))
            after = jnp.where(seq_last, 0.0, load(next_row))
            prev = jnp.where(rid == 0, before, pltpu.roll(cur, 1, 0))
            nxt = jnp.where(rid == ROWS_A - 1, after, pltpu.roll(cur, ROWS_A - 1, 0))
            return cur, prev, nxt

        z, zp, zn = neighbours(lambda rs: cv_ref[rs, D_CONV:2 * D_CONV] * cv_ref[rs, 2 * D_CONV:3 * D_CONV])
        cw = cw_ref[...]
        conv = cv_ref[rows, 0:D_CONV] * (zp * cw[0:1, :] + z * cw[1:2, :] + zn * cw[2:3, :])
        o_ref[rows, 0:D_CONV] = conv.astype(BF16)

        u, up, un = neighbours(lambda rs: um_ref[rs, :])
        u = u + mum_ref[...] * (0.5 * (up + un) - u)
        t, tp, tn = neighbours(lambda rs: ut_ref[rs, :])
        ut = t + mut_ref[...] * (0.5 * (tp + tn) - t)
        r = u[:, 0:D_RWKV]
        k = u[:, D_RWKV:2 * D_RWKV]
        v = u[:, 2 * D_RWKV:3 * D_RWKV]
        kk = k * kk_ref[...]
        kk = kk * lax.rsqrt(seg_sum(kk * kk) + 1e-12)
        for p, v_p in enumerate(pair_tiles(v)):
            v_s[p, rows, :] = v_p.astype(BF16)
        tw = jnp.tanh(ut)
        a_sum = jnp.zeros((ROWS_A, D_RWKV), F32)
        for d in range(2):
            zz = w0_ref[d:d + 1, :] + _dot_x3(tw, wup_ref[d])
            lw = -math.exp(-0.5) * jax.nn.sigmoid(zz)
            a = jax.nn.sigmoid(a0_ref[d:d + 1, :] + _dot_x3(ut, aup_ref[d]))
            a_sum = a_sum + a
            cum = _split_sum(lw, 3, lambda part: jnp.dot(cum_mat[d], part, preferred_element_type=F32))
            e_out = jnp.exp(-cum)
            kd = k * (1.0 + (a - 1.0) * ka_ref[...])
            scaled = ((al_s, -kk * jnp.exp(cum - lw)), (be_s, kk * a * e_out), (kd_s, kd * e_out),
                      (r_s, r * jnp.exp(cum)))
            for ref, val in scaled:
                for p, val_p in enumerate(pair_tiles(val)):
                    ref[d, p, rows, :] = val_p.astype(BF16)
            for j in range(per_block):
                last = j * CHUNK + (CHUNK - 1 if d == 0 else 0)
                for p, dec_p in enumerate(pair_tiles(jnp.exp(cum[last:last + 1, :]))):
                    p_s[d, c * per_block + j, p:p + 1, :] = dec_p
        kd_sum = k * (2.0 + (a_sum - 2.0) * ka_ref[...])
        bon_s[rows, :] = seg_sum(r * kd_sum * rk_ref[...]) * v
        g_s[rows, :] = _dot_x3(jax.nn.sigmoid(ut), gup_ref[...])
        y_s[rows, :] = jnp.zeros((ROWS_A, D_RWKV), F32)
        return carry

    lax.fori_loop(0, n_blocks, pass1, 0)

    if manual_fetch:
        @pl.when(step + 1 < pl.num_programs(0))
        def _():
            for cp in input_copies(step + 1):
                cp.start()

    for s in range(nseq):
        for d in range(2):
            for p in range(N_PAIRS):
                if has_s0:
                    s_scr[s, d, p] = jnp.concatenate([s0_ref[d, 2 * p], s0_ref[d, 2 * p + 1]], axis=1)
                else:
                    s_scr[s, d, p] = jnp.zeros((RWKV_HEAD, PAIR), F32)

    ri = lax.broadcasted_iota(jnp.int32, (CHUNK, PAIR), 0)
    lane = lax.broadcasted_iota(jnp.int32, (CHUNK, PAIR), 1)
    ci = lane % CHUNK
    low_half = lane < RWKV_HEAD
    eye = (ci == ri).astype(F32)
    diag_blk = (ri // SUB) == (ci // SUB)
    n_rounds = int(math.log2(SUB)) - 1

    def bd(x):
        zero = jnp.zeros_like(x)
        return jnp.concatenate([jnp.where(low_half, x, zero), jnp.where(low_half, zero, x)], axis=0)

    def pair_mm(a, b):
        return _mm(a, bd(b.astype(BF16)))

    def pair_mm_nt(a, b):
        return _mm(a, bd(b.astype(BF16)), _NT)

    def solve_step(c, carry):
        chains = []
        for s, d, j in ((s, d, j) for s in range(nseq) for d in range(2) for j in range(CHUNKS_A)):
            cc = s * n_chunks + c * CHUNKS_A + j
            rows = pl.ds(pl.multiple_of(cc * CHUNK, CHUNK), CHUNK)
            for p in range(N_PAIRS):
                chains.append(dict(
                    d=d, p=p, rows=rows,
                    strict=(ci < ri) if d == 0 else (ci > ri), incl=(ci <= ri) if d == 0 else (ci >= ri),
                    al=al_s[d, p, rows, :], be=be_s[d, p, rows, :], kd=kd_s[d, p, rows, :],
                    r=r_s[d, p, rows, :], v=v_s[p, rows, :]))

        def each(fn, *lists):
            return [fn(*args) for args in zip(chains, *lists)]

        alr = each(lambda ch: jnp.concatenate([ch['al'], ch['r']], axis=0))
        g = each(lambda ch, m: _mm(m, jnp.concatenate([bd(ch['be']), bd(ch['kd'])], axis=0), _NT), alr)
        a = each(lambda ch, m: jnp.where(ch['strict'], m[0:CHUNK, 0:PAIR], 0.0), g)
        a_ak = each(lambda ch, m: jnp.where(ch['strict'], m[0:CHUNK, PAIR:2 * PAIR], 0.0), g)
        q = each(lambda ch, m: jnp.concatenate(
            [jnp.where(ch['incl'], m[CHUNK:2 * CHUNK, 0:PAIR], 0.0),
             jnp.where(ch['incl'], m[CHUNK:2 * CHUNK, PAIR:2 * PAIR], 0.0)], axis=1).astype(BF16), g)
        akv = each(lambda ch, m: pair_mm(m, ch['v']), a_ak)
        xs = [jnp.where(diag_blk, m, 0.0) for m in a]
        low = [jnp.where(diag_blk, 0.0, m) for m in a]
        ts = [eye + x for x in xs]
        xs = [pair_mm(x, x) for x in xs]
        for _ in range(n_rounds - 1):
            z = [pair_mm(jnp.concatenate([x, t], axis=0), x) for x, t in zip(xs, ts)]
            xs = [m[0:CHUNK] for m in z]
            ts = [t + m[CHUNK:2 * CHUNK] for t, m in zip(ts, z)]
        ts = [t + pair_mm(t, x) for t, x in zip(ts, xs)]
        tl = each(lambda ch, t, lo, m: _mm(t, jnp.concatenate(
            [bd(lo.astype(BF16)), bd(m.astype(BF16)), bd(ch['al'])], axis=1)), ts, low, akv)
        ms = [x[:, 0:PAIR] for x in tl]
        sol = [x[:, PAIR:3 * PAIR] for x in tl]
        m2 = [pair_mm(m, m) for m in ms]
        ims = [eye + m for m in ms]
        ns = [im + pair_mm(im, mm2) for im, mm2 in zip(ims, m2)]
        w = [_mm(nn, jnp.concatenate([bd(x[:, 0:PAIR].astype(BF16)), bd(x[:, PAIR:2 * PAIR].astype(BF16))],
                                     axis=1)) for nn, x in zip(ns, sol)]
        for ch, x, qq in zip(chains, w, q):
            w2_s[ch['d'], ch['p'], ch['rows'], :] = x[:, 0:PAIR]
            w1_s[ch['d'], ch['p'], ch['rows'], :] = x[:, PAIR:2 * PAIR].astype(BF16)
            q_s[ch['d'], ch['p'], ch['rows'], :] = qq
        return carry

    lax.fori_loop(0, n_chunks // CHUNKS_A, solve_step, 0)

    def chunk_step(c, carry):
        chains = []
        for s, d in ((s, d) for s in range(nseq) for d in range(2)):
            cc = s * n_chunks + (c if d == 0 else n_chunks - 1 - c)
            rows = pl.ds(pl.multiple_of(cc * CHUNK, CHUNK), CHUNK)
            p_tile = p_s[d, cc]
            for p in range(N_PAIRS):
                chains.append(dict(
                    s=s, d=d, p=p, rows=rows, dec=p_tile[p:p + 1, :],
                    be=be_s[d, p, rows, :], kd=kd_s[d, p, rows, :], r=r_s[d, p, rows, :],
                    v=v_s[p, rows, :], w2=w2_s[d, p, rows, :], w1=w1_s[d, p, rows, :],
                    q=q_s[d, p, rows, :]))

        def each(fn, *lists):
            return [fn(*args) for args in zip(chains, *lists)]

        st = each(lambda ch: s_scr[ch['s'], ch['d'], ch['p']])
        st_bd = [bd(t.astype(BF16)) for t in st]
        u = each(lambda ch, t: ch['w2'] + _mm(ch['w1'], t, _NT), st_bd)
        ub = [x.astype(BF16) for x in u]
        ys = each(lambda ch, t, uu: _mm(ch['r'], t, _NT)
                  + _mm(ch['q'], jnp.concatenate([bd(uu), bd(ch['v'])], axis=0)), st_bd, ub)
        full = each(lambda ch, uu: _mm(jnp.concatenate([uu, ch['v']], axis=0),
                                       jnp.concatenate([ch['be'], ch['kd']], axis=0), _TN), ub)
        s_new = each(lambda ch, t, m: (t + jnp.where(low_half, m[0:RWKV_HEAD], m[RWKV_HEAD:PAIR])) * ch['dec'],
                     st, full)
        for ch, t in zip(chains, s_new):
            s_scr[ch['s'], ch['d'], ch['p']] = t
        for ch, y in zip(chains, ys):
            y_s[ch['rows'], ch['p'] * PAIR:(ch['p'] + 1) * PAIR] += y
        return carry

    lax.fori_loop(0, n_chunks, chunk_step, 0)

    if emit_state:
        for s in range(nseq):
            for d in range(2):
                for p in range(N_PAIRS):
                    so_ref[s, d, 2 * p] = s_scr[s, d, p][:, 0:RWKV_HEAD]
                    so_ref[s, d, 2 * p + 1] = s_scr[s, d, p][:, RWKV_HEAD:PAIR]

    def pass3(c, carry):
        rows = pl.ds(pl.multiple_of(c * ROWS_A, ROWS_A), ROWS_A)
        y = y_s[rows, :]
        yc = y - seg_sum(y) * (1.0 / RWKV_HEAD)
        var = seg_sum(yc * yc) * (1.0 / RWKV_HEAD)
        yn = yc * lax.rsqrt(var + GN_EPS) * lng_ref[...] + lnb_ref[...]
        o_ref[rows, D_CONV:D_CONV + D_RWKV] = ((yn + bon_s[rows, :]) * g_s[rows, :]).astype(BF16)
        return carry

    lax.fori_loop(0, n_blocks, pass3, 0)


def _pad_rows(w, lo):
    return jnp.pad(w, [(0, 0)] * (w.ndim - 2) + [(lo, D_TAIL - lo - w.shape[-2]), (0, 0)])


def _mix_call(proj, tail, P, l, n, nseq, s0=None, emit_state=False):
    T = proj.shape[0]
    B = T // n
    rows = nseq * n
    assert s0 is None or nseq == 1
    full = lambda shape: pl.BlockSpec(shape, lambda b: (0,) * len(shape))
    lanes = lambda w: -(-w // 128) * 128
    in_bytes = rows * (3 * D_CONV + 3 * D_RWKV + lanes(D_TAIL)) * 4
    scratch_bytes = rows * (4 * 2 * D_RWKV * 2 + D_RWKV * 2 + 3 * D_RWKV * 4
                            + 2 * D_RWKV * (4 + 2 + 2 * 2))
    out_bytes = 2 * rows * (D_CONV + D_RWKV) * 2
    temporaries = 12 * 1024 * 1024
    manual_fetch = 2 * in_bytes + scratch_bytes + out_bytes + temporaries > VMEM_LIMIT_BYTES
    if manual_fetch:
        in_specs = [pl.BlockSpec(memory_space=pl.ANY), pl.BlockSpec(memory_space=pl.ANY)]
        token_args = [proj, tail]
    else:
        in_specs = [
            pl.BlockSpec((rows, 3 * D_CONV), lambda b: (b, CONV_COL // (3 * D_CONV))),
            pl.BlockSpec((rows, 3 * D_RWKV), lambda b: (b, RWKV_COL // (3 * D_RWKV))),
            pl.BlockSpec((rows, D_TAIL), lambda b: (b, 0)),
        ]
        token_args = [proj, proj, tail]
    in_specs += [
        full((1, 3 * D_RWKV)), full((1, D_TAIL)), full((3, D_CONV)),
        full((2, D_RWKV)), full((2, D_TAIL, D_RWKV)), full((2, D_RWKV)), full((2, D_TAIL, D_RWKV)),
        full((D_TAIL, D_RWKV)), full((1, D_RWKV)), full((1, D_RWKV)), full((1, D_RWKV)),
        full((1, D_RWKV)), full((1, D_RWKV)), full((D_RWKV, D_RWKV)),
    ]
    head = jnp.arange(D_RWKV) // RWKV_HEAD
    seg = (head[:, None] == head[None, :]).astype(BF16)
    mu = P['rwkv_mu'][l]
    row = lambda a: a.reshape(1, -1)
    args = token_args + [row(mu[:3 * D_RWKV]), row(mu[3 * D_RWKV:]), P['conv_w'][l],
            P['rwkv_w0'][l], _pad_rows(P['rwkv_w_up'][l], 0), P['rwkv_a0'][l],
            _pad_rows(P['rwkv_a_up'][l], DECAY_RANK), _pad_rows(P['rwkv_g_up'][l], DECAY_RANK + A_RANK),
            row(P['rwkv_k_k'][l]), row(P['rwkv_k_a'][l]), row(P['rwkv_r_k'][l]),
            row(P['rwkv_ln_g'][l]), row(P['rwkv_ln_b'][l]), seg]
    if s0 is not None:
        in_specs.append(pl.BlockSpec((None, None, 2, RWKV_HEADS, RWKV_HEAD, RWKV_HEAD),
                                     lambda b: (b, l, 0, 0, 0, 0)))
        args.append(s0)
    out_specs = [pl.BlockSpec((rows, D_CONV + D_RWKV), lambda b: (b, 0))]
    out_shape = [jax.ShapeDtypeStruct((T, D_CONV + D_RWKV), BF16)]
    state_shape = (2, RWKV_HEADS, RWKV_HEAD, RWKV_HEAD)
    if emit_state:
        out_specs.append(pl.BlockSpec((nseq,) + state_shape, lambda b: (b, 0, 0, 0, 0)))
        out_shape.append(jax.ShapeDtypeStruct((B,) + state_shape, F32))
    pair = 2 * RWKV_HEAD
    n_pairs = RWKV_HEADS // 2
    per_pair = lambda: pltpu.VMEM((2, n_pairs, rows, pair), BF16)
    scratch_shapes = [per_pair(), per_pair(), per_pair(), per_pair(),
                      pltpu.VMEM((n_pairs, rows, pair), BF16),
                      pltpu.VMEM((2, rows // CHUNK, n_pairs, pair), F32),
                      pltpu.VMEM((rows, D_RWKV), F32), pltpu.VMEM((rows, D_RWKV), F32),
                      pltpu.VMEM((rows, D_RWKV), F32),
                      pltpu.VMEM((nseq, 2, n_pairs, RWKV_HEAD, pair), F32),
                      pltpu.VMEM((2, n_pairs, rows, pair), F32), pltpu.VMEM((2, n_pairs, rows, pair), BF16),
                      pltpu.VMEM((2, n_pairs, rows, 2 * pair), BF16)]
    if manual_fetch:
        scratch_shapes += [pltpu.VMEM((rows, 3 * D_CONV), F32), pltpu.VMEM((rows, 3 * D_RWKV), F32),
                           pltpu.VMEM((rows, D_TAIL), F32), pltpu.SemaphoreType.DMA((3,))]
    semantics = ("arbitrary",) if manual_fetch else ("parallel",)
    return pl.pallas_call(
        functools.partial(_mix_kernel, n=n, nseq=nseq, has_s0=s0 is not None, emit_state=emit_state,
                          manual_fetch=manual_fetch),
        grid=(B // nseq,),
        in_specs=in_specs,
        out_specs=out_specs,
        out_shape=out_shape,
        scratch_shapes=scratch_shapes,
        compiler_params=pltpu.CompilerParams(dimension_semantics=semantics,
                                             vmem_limit_bytes=VMEM_LIMIT_BYTES),
        name="conv_rwkv",
    )(*args)


def _residual_copy(x_hbm, x_buf, sem):
    rows = x_buf.shape[0]
    start = pl.multiple_of(pl.program_id(0) * rows, rows)
    return pltpu.make_async_copy(x_hbm.at[pl.ds(start, rows), :], x_buf, sem)


def _outproj_kernel(at_ref, cr_ref, w_ref, x_hbm, ga_ref, shf_ref, scf_ref, g1_ref, g2_ref,
                    x1_ref, h2_ref, x_ref, x_sem):
    kt = pl.program_id(1)

    @pl.when(kt == 0)
    def _():
        _residual_copy(x_hbm, x_ref, x_sem).start()
        x1_ref[...] = jnp.zeros(x1_ref.shape, F32)

    lhs = jnp.where(kt < 2, at_ref[...], cr_ref[...])
    x1_ref[...] += jnp.dot(lhs, w_ref[...].astype(BF16), preferred_element_type=F32)

    @pl.when(kt == pl.num_programs(1) - 1)
    def _():
        _residual_copy(x_hbm, x_ref, x_sem).wait()
        x1 = x_ref[...] + ga_ref[...] * _rms(x1_ref[...], g1_ref[...])
        x1_ref[...] = x1
        h2_ref[...] = (_rms(x1, g2_ref[...]) * (1.0 + scf_ref[...]) + shf_ref[...]).astype(BF16)


def _outproj_call(attn, cr, x, mod3, ng3, w_out, l, row_fn, tm):
    T = x.shape[0]
    tk = 512
    return pl.pallas_call(
        _outproj_kernel,
        grid=(T // tm, D_MODEL // tk),
        in_specs=[
            pl.BlockSpec((tm, tk), lambda i, k: (i, jnp.minimum(k, 1))),
            pl.BlockSpec((tm, tk), lambda i, k: (i, jnp.maximum(k - 2, 0))),
            pl.BlockSpec((None, tk, D_MODEL), lambda i, k: (l, k, 0)),
            pl.BlockSpec(memory_space=pl.ANY),
            _mod_spec(row_fn, 2), _mod_spec(row_fn, 3), _mod_spec(row_fn, 4),
            _ng_spec(l, 1), _ng_spec(l, 2),
        ],
        out_specs=[
            pl.BlockSpec((tm, D_MODEL), lambda i, k: (i, 0)),
            pl.BlockSpec((tm, D_MODEL), lambda i, k: (i, 0)),
        ],
        out_shape=[jax.ShapeDtypeStruct((T, D_MODEL), F32), jax.ShapeDtypeStruct((T, D_MODEL), BF16)],
        scratch_shapes=[pltpu.VMEM((tm, D_MODEL), F32), pltpu.SemaphoreType.DMA(())],
        compiler_params=_params(2),
        name="out_proj",
    )(attn, cr, w_out, x, mod3, mod3, mod3, ng3, ng3)


def _ffn_kernel(h_ref, x1_hbm, wg_ref, wu_ref, wd_ref, gf_ref, g3_ref, o_ref, x1_ref, x1_sem):
    f = pl.program_id(1)

    @pl.when(f == 0)
    def _():
        _residual_copy(x1_hbm, x1_ref, x1_sem).start()
        o_ref[...] = jnp.zeros(o_ref.shape, F32)

    h = h_ref[...]
    gate = _dot(h, wg_ref[...])
    up = _dot(h, wu_ref[...])
    o_ref[...] += _dot(gate * jax.nn.sigmoid(gate) * up, wd_ref[...])

    @pl.when(f == pl.num_programs(1) - 1)
    def _():
        _residual_copy(x1_hbm, x1_ref, x1_sem).wait()
        o_ref[...] = x1_ref[...] + gf_ref[...] * _rms(o_ref[...], g3_ref[...])


def _ffn_call(h2, x1, mod3, ng3, w_gate, w_up, w_down, l, row_fn, tm):
    T = x1.shape[0]
    tf = 256
    return pl.pallas_call(
        _ffn_kernel,
        grid=(T // tm, D_FF // tf),
        in_specs=[
            pl.BlockSpec((tm, D_MODEL), lambda i, f: (i, 0)),
            pl.BlockSpec(memory_space=pl.ANY),
            pl.BlockSpec((None, D_MODEL, tf), lambda i, f: (l, 0, f)),
            pl.BlockSpec((None, D_MODEL, tf), lambda i, f: (l, 0, f)),
            pl.BlockSpec((None, tf, D_MODEL), lambda i, f: (l, f, 0)),
            _mod_spec(row_fn, 5),
            _ng_spec(l, 3),
        ],
        out_specs=pl.BlockSpec((tm, D_MODEL), lambda i, f: (i, 0)),
        out_shape=jax.ShapeDtypeStruct((T, D_MODEL), F32),
        scratch_shapes=[pltpu.VMEM((tm, D_MODEL), F32), pltpu.SemaphoreType.DMA(())],
        compiler_params=_params(2),
        name="ffn",
    )(h2, x1, w_gate, w_up, w_down, mod3, ng3)


def _rope_tables(n):
    rows = n // GRID_W
    row = jnp.repeat(jnp.arange(rows), GRID_W).astype(F32)
    col = jnp.tile(jnp.arange(GRID_W), rows).astype(F32)
    half = HEAD_DIM // 2
    inv = 1.0 / (ROPE_THETA ** (jnp.arange(0, half, 2, dtype=F32) / half))
    ar = row[:, None] * inv
    ac = col[:, None] * inv
    ang = jnp.concatenate([ar, ar, ac, ac], axis=-1)
    return jnp.cos(ang), jnp.sin(ang)


ROW_TILE = 1024
INPROJ_ROW_TILE = 2048


def _layer(x, mod3, ng3, P, l, n, mix_seqs, row_fn_of_tm, rope_tabs=None, cache=None, s0=None, emit=False):
    proj, tail = _inproj_call(x, mod3, ng3, P['w_in_t'], P['w_tail_t'], l,
                              row_fn_of_tm(INPROJ_ROW_TILE // 2), INPROJ_ROW_TILE)
    attn_out = _attn_call(proj, P['q_norm'][l].reshape(1, -1), P['k_norm'][l].reshape(1, -1), l, n,
                          rope_tabs=rope_tabs, cache=cache, emit_kv=emit)
    mix_out = _mix_call(proj, tail, P, l, n, mix_seqs, s0=s0, emit_state=emit)
    attn = attn_out[0]
    cr = mix_out[0]
    x1, h2 = _outproj_call(attn, cr, x, mod3, ng3, P['w_out'], l, row_fn_of_tm(ROW_TILE), ROW_TILE)
    x2 = _ffn_call(h2, x1, mod3, ng3, P['w_gate'], P['w_up'], P['w_down'], l, row_fn_of_tm(ROW_TILE),
                   ROW_TILE)
    if emit:
        return x2, attn_out[1], attn_out[2], mix_out[1]
    return x2


def kernel(x_prompt, x_sample, cache_k, cache_v, state_rwkv, c, c_ctx, w_ada, b_ada, norm_g, w_in, q_norm, k_norm, conv_w, rwkv_mu, rwkv_w0, rwkv_w_up, rwkv_a0, rwkv_a_up, rwkv_g_up, rwkv_k_k, rwkv_k_a, rwkv_r_k, rwkv_ln_g, rwkv_ln_b, w_out, w_gate, w_up, w_down):
    P = {'q_norm': q_norm, 'k_norm': k_norm, 'conv_w': conv_w, 'rwkv_mu': rwkv_mu,
         'rwkv_w0': rwkv_w0, 'rwkv_w_up': rwkv_w_up, 'rwkv_a0': rwkv_a0, 'rwkv_a_up': rwkv_a_up,
         'rwkv_g_up': rwkv_g_up, 'rwkv_k_k': rwkv_k_k, 'rwkv_k_a': rwkv_k_a, 'rwkv_r_k': rwkv_r_k,
         'rwkv_ln_g': rwkv_ln_g, 'rwkv_ln_b': rwkv_ln_b, 'w_out': w_out,
         'w_gate': w_gate, 'w_up': w_up, 'w_down': w_down}
    w_in_t = jnp.swapaxes(w_in, 1, 2)
    P['w_in_t'] = w_in_t
    P['w_tail_t'] = w_in_t[:, D_MAIN:, :]
    batch, seq, _ = x_prompt.shape
    dec_batch, dec_seq, _ = x_sample.shape

    cc = jnp.zeros((8, D_MODEL), F32).at[0].set(c_ctx).at[1:1 + dec_batch].set(c)
    mod3 = _ada_call(cc, w_ada, b_ada).reshape(DEPTH * 8, 1, N_MOD * D_MODEL)
    ng3 = norm_g.reshape(DEPTH * 4, 1, D_MODEL)

    ck = cache_k.reshape(dec_batch, DEPTH, PAST_LEN, D_KV)
    cv = cache_v.reshape(dec_batch, DEPTH, PAST_LEN, D_KV)
    rope_tabs = _rope_tables(dec_seq)

    xp = x_prompt.reshape(batch * seq, D_MODEL)
    xs = x_sample.reshape(dec_batch * dec_seq, D_MODEL)
    ks, vs, ss = [], [], []
    for l in range(DEPTH):
        ctx_row = lambda tm, l=l: (lambda i: l * 8)
        smp_row = lambda tm, l=l: (lambda i: l * 8 + 1 + (i * tm) // dec_seq)
        xp, k_l, v_l, s_l = _layer(xp, mod3, ng3, P, l, seq, 2, ctx_row, emit=True)
        ks.append(k_l.reshape(batch, seq, N_KV_HEADS, HEAD_DIM))
        vs.append(v_l.reshape(batch, seq, N_KV_HEADS, HEAD_DIM))
        ss.append(s_l)
        xs = _layer(xs, mod3, ng3, P, l, dec_seq, 1, smp_row, rope_tabs=rope_tabs, cache=(ck, cv),
                    s0=state_rwkv)
    return (xp.reshape(batch, seq, D_MODEL), xs.reshape(dec_batch, dec_seq, D_MODEL),
            jnp.stack(ks, axis=1), jnp.stack(vs, axis=1), jnp.stack(ss, axis=1))
```

```python
import functools
import math

import jax
import jax.numpy as jnp
from jax import lax
from jax.experimental import pallas as pl
from jax.experimental.pallas import tpu as pltpu

D_MODEL = 2048
DEPTH = 2
GRID_W = 64
HEAD_DIM = 128
N_HEADS = 8
N_KV_HEADS = 2
GROUP = N_HEADS // N_KV_HEADS
D_ATTN = N_HEADS * HEAD_DIM
D_KV = N_KV_HEADS * HEAD_DIM
ROPE_THETA = 10000.0
D_CONV = 512
D_RWKV = 512
RWKV_HEAD = 64
RWKV_HEADS = D_RWKV // RWKV_HEAD
DECAY_RANK = 32
A_RANK = 32
GATE_RANK = 96
D_TAIL = DECAY_RANK + A_RANK + GATE_RANK
D_IN = D_ATTN + 2 * D_KV + 3 * D_CONV + 3 * D_RWKV + D_TAIL
D_MAIN = D_IN - D_TAIL
D_FF = 5632
N_MOD = 6
EPS = 1e-6
GN_EPS = 64e-5
PAST_LEN = 256

VMEM_LIMIT_BYTES = 56 * 1024 * 1024
CHUNK = 64
BF16 = jnp.bfloat16
F32 = jnp.float32


def _params(n_grid, **kw):
    sem = ("parallel",) + ("arbitrary",) * (n_grid - 1)
    return pltpu.CompilerParams(dimension_semantics=sem, vmem_limit_bytes=VMEM_LIMIT_BYTES, **kw)


def _rms(x, g):
    return x * lax.rsqrt(jnp.mean(x * x, axis=-1, keepdims=True) + EPS) * g


def _dot(a, b):
    return jnp.dot(a.astype(BF16), b.astype(BF16), preferred_element_type=F32)


_NN = (((1,), (0,)), ((), ()))
_NT = (((1,), (1,)), ((), ()))
_TN = (((0,), (0,)), ((), ()))


def _mm(a, b, dims=_NN):
    return lax.dot_general(a.astype(BF16), b.astype(BF16), dims, preferred_element_type=F32)


def _ada_kernel(c_ref, w_ref, b_ref, o_ref):
    c = c_ref[...]
    s = c * jax.nn.sigmoid(c)
    o_ref[...] = _dot(s, w_ref[...]) + b_ref[...]


def _ada_call(cc, w_ada, b_ada):
    tn = 512
    n_out = N_MOD * D_MODEL
    return pl.pallas_call(
        _ada_kernel,
        grid=(DEPTH, n_out // tn),
        in_specs=[
            pl.BlockSpec((8, D_MODEL), lambda l, n: (0, 0)),
            pl.BlockSpec((None, D_MODEL, tn), lambda l, n: (l, 0, n)),
            pl.BlockSpec((None, 1, tn), lambda l, n: (l, 0, n)),
        ],
        out_specs=pl.BlockSpec((None, 8, tn), lambda l, n: (l, 0, n)),
        out_shape=jax.ShapeDtypeStruct((DEPTH, 8, n_out), F32),
        compiler_params=_params(2),
        name="adaln",
    )(cc, w_ada, b_ada.reshape(DEPTH, 1, n_out))


def _mod_spec(row_fn, chunk):
    return pl.BlockSpec((None, 1, D_MODEL), lambda i, *_: (row_fn(i), 0, chunk))


def _ng_spec(l, j):
    return pl.BlockSpec((None, 1, D_MODEL), lambda i, *_: (l * 4 + j, 0, 0))


INPROJ_NORM_ROWS = 256
INPROJ_VMEM_LIMIT_BYTES = 60 * 1024 * 1024


def _row_tile_copy(x_hbm, x_buf, sem, tile):
    rows = x_buf.shape[0]
    return pltpu.make_async_copy(x_hbm.at[pl.ds(pl.multiple_of(tile * rows, rows), rows), :], x_buf, sem)


def _inproj_kernel(x_hbm, sha_ref, sca_ref, shb_ref, scb_ref, g_ref, w_ref, wt_ref, o_ref, ot_ref,
                   x_buf, h_scr, x_sem):
    i = pl.program_id(0)

    @pl.when(pl.program_id(1) == 0)
    def _():
        @pl.when(i == 0)
        def _():
            _row_tile_copy(x_hbm, x_buf, x_sem, 0).start()

        _row_tile_copy(x_hbm, x_buf, x_sem, i).wait()
        half = x_buf.shape[0] // 2
        for r0 in range(0, x_buf.shape[0], INPROJ_NORM_ROWS):
            rs = slice(r0, r0 + INPROJ_NORM_ROWS)
            sh_ref, sc_ref = (sha_ref, sca_ref) if r0 < half else (shb_ref, scb_ref)
            h = _rms(x_buf[rs, :], g_ref[...]) * (1.0 + sc_ref[...]) + sh_ref[...]
            h_scr[rs, :] = h.astype(BF16)

        @pl.when(i + 1 < pl.num_programs(0))
        def _():
            _row_tile_copy(x_hbm, x_buf, x_sem, i + 1).start()

        ot_ref[...] = _mm(h_scr[...], wt_ref[...], _NT)

    o_ref[...] = _mm(h_scr[...], w_ref[...], _NT)


def _inproj_call(x, mod3, ng3, w_in_t, w_tail_t, l, row_fn_half, tm):
    T = x.shape[0]
    tn = 512
    seg = lambda j: (lambda i: row_fn_half(2 * i + j))
    return pl.pallas_call(
        _inproj_kernel,
        grid=(T // tm, D_MAIN // tn),
        in_specs=[
            pl.BlockSpec(memory_space=pl.ANY),
            _mod_spec(seg(0), 0), _mod_spec(seg(0), 1),
            _mod_spec(seg(1), 0), _mod_spec(seg(1), 1),
            _ng_spec(l, 0),
            pl.BlockSpec((None, tn, D_MODEL), lambda i, n: (l, n, 0)),
            pl.BlockSpec((None, D_TAIL, D_MODEL), lambda i, n: (l, 0, 0)),
        ],
        out_specs=[
            pl.BlockSpec((tm, tn), lambda i, n: (i, n)),
            pl.BlockSpec((tm, D_TAIL), lambda i, n: (i, 0)),
        ],
        out_shape=[
            jax.ShapeDtypeStruct((T, D_MAIN), F32),
            jax.ShapeDtypeStruct((T, D_TAIL), F32),
        ],
        scratch_shapes=[pltpu.VMEM((tm, D_MODEL), F32), pltpu.VMEM((tm, D_MODEL), BF16),
                        pltpu.SemaphoreType.DMA(())],
        compiler_params=pltpu.CompilerParams(dimension_semantics=("arbitrary", "arbitrary"),
                                             vmem_limit_bytes=INPROJ_VMEM_LIMIT_BYTES),
        name="in_proj",
    )(x, mod3, mod3, mod3, mod3, ng3, w_in_t, w_tail_t)


def _rope(x, cos, sin):
    lane = lax.broadcasted_iota(jnp.int32, x.shape, 1)
    first = (lane % (HEAD_DIM // 2)) < (HEAD_DIM // 4)
    rot = jnp.where(first, -pltpu.roll(x, HEAD_DIM - HEAD_DIM // 4, 1), pltpu.roll(x, HEAD_DIM // 4, 1))
    return x * cos + rot * sin


def _attn_kernel(*refs, n, past, rope, emit_kv, qb):
    it = iter(refs)
    q_ref, k_ref, v_ref, qn_ref, kn_ref = (next(it) for _ in range(5))
    cos_ref = sin_ref = ck_ref = cv_ref = None
    if rope:
        cos_ref, sin_ref = next(it), next(it)
    if past:
        ck_ref, cv_ref = next(it), next(it)
    o_ref = next(it)
    if emit_kv:
        ko_ref, vo_ref = next(it), next(it)
    k_scr, v_scr = next(it), next(it)

    k = _rms(k_ref[...], kn_ref[...])
    v = v_ref[...]
    if emit_kv:
        ko_ref[...] = k
        vo_ref[...] = v
    if rope:
        k = _rope(k, cos_ref[...], sin_ref[...])
    k_scr[0:n, :] = k.astype(BF16)
    v_scr[0:n, :] = v.astype(BF16)
    if past:
        k_scr[n:n + past, :] = ck_ref[...].astype(BF16)
        v_scr[n:n + past, :] = cv_ref[...].astype(BF16)
    exp2_scale = HEAD_DIM ** -0.5 * math.log2(math.e)

    blocks_per_step = min(n // qb, 4)

    def block(b, carry):
        items = [(pl.ds(pl.multiple_of((b * blocks_per_step + j) * qb, qb), qb),
                  slice(g * HEAD_DIM, (g + 1) * HEAD_DIM))
                 for j in range(blocks_per_step) for g in range(GROUP)]

        def scores(item):
            rows, cols = item
            q = _rms(q_ref[rows, cols], qn_ref[...])
            if rope:
                q = _rope(q, cos_ref[rows, :], sin_ref[rows, :])
            return lax.dot_general(q.astype(BF16), k_scr[...], _NT, preferred_element_type=F32)

        def softmax(s):
            p = jnp.exp2((s - jnp.max(s, axis=-1, keepdims=True)) * exp2_scale)
            return p.astype(BF16), jnp.sum(p, axis=-1, keepdims=True)

        def values(item, pd):
            rows, cols = item
            o = jnp.dot(pd[0], v_scr[...], preferred_element_type=F32)
            o_ref[rows, cols] = (o / pd[1]).astype(BF16)

        if blocks_per_step == 1:
            ps = [softmax(s) for s in [scores(item) for item in items]]
            for item, pd in zip(items, ps):
                values(item, pd)
            return carry
        depth = 2
        ss, ps = {}, {}
        for i in range(len(items) + depth):
            if i < len(items):
                ss[i] = scores(items[i])
            if 1 <= i <= len(items):
                ps[i - 1] = softmax(ss.pop(i - 1))
            if i >= depth:
                values(items[i - depth], ps.pop(i - depth))
        return carry

    lax.fori_loop(0, n // (qb * blocks_per_step), block, 0)


def _attn_call(proj, qn, kn, l, n, rope_tabs=None, cache=None, emit_kv=False):
    T = proj.shape[0]
    B = T // n
    past = PAST_LEN if cache is not None else 0
    qw = GROUP * HEAD_DIM
    k_blk = D_ATTN // HEAD_DIM
    v_blk = (D_ATTN + D_KV) // HEAD_DIM
    in_specs = [
        pl.BlockSpec((n, qw), lambda b, h: (b, h)),
        pl.BlockSpec((n, HEAD_DIM), lambda b, h: (b, k_blk + h)),
        pl.BlockSpec((n, HEAD_DIM), lambda b, h: (b, v_blk + h)),
        pl.BlockSpec((1, HEAD_DIM), lambda b, h: (0, 0)),
        pl.BlockSpec((1, HEAD_DIM), lambda b, h: (0, 0)),
    ]
    args = [proj, proj, proj, qn, kn]
    if rope_tabs is not None:
        in_specs += [pl.BlockSpec((n, HEAD_DIM), lambda b, h: (0, 0))] * 2
        args += list(rope_tabs)
    if cache is not None:
        in_specs += [pl.BlockSpec((None, None, PAST_LEN, HEAD_DIM), lambda b, h: (b, l, 0, h))] * 2
        args += list(cache)
    out_specs = [pl.BlockSpec((n, qw), lambda b, h: (b, h))]
    out_shape = [jax.ShapeDtypeStruct((T, D_ATTN), BF16)]
    if emit_kv:
        out_specs += [pl.BlockSpec((None, n, HEAD_DIM), lambda b, h: (b, 0, h))] * 2
        out_shape += [jax.ShapeDtypeStruct((B, n, D_KV), F32)] * 2
    return pl.pallas_call(
        functools.partial(_attn_kernel, n=n, past=past, rope=rope_tabs is not None,
                          emit_kv=emit_kv, qb=256),
        grid=(B, N_KV_HEADS),
        in_specs=in_specs,
        out_specs=out_specs,
        out_shape=out_shape,
        scratch_shapes=[pltpu.VMEM((n + past, HEAD_DIM), BF16)] * 2,
        compiler_params=_params(2),
        name="attention",
    )(*args)


CONV_COL = D_ATTN + 2 * D_KV
RWKV_COL = CONV_COL + 3 * D_CONV
ROWS_A = 256
SUB = 16
CHUNKS_A = 2
assert CHUNK // SUB == 4


def _split2(x):
    hi = x.astype(BF16)
    return hi, (x - hi.astype(F32)).astype(BF16)


def _dot_x3(a, b):
    ah, al = _split2(a)
    bh, bl = _split2(b)
    mm = lambda p, q: jnp.dot(p, q, preferred_element_type=F32)
    return mm(ah, bh) + mm(ah, bl) + mm(al, bh)


def _split_sum(x, pieces, dot_piece):
    acc = None
    rest = x
    for _ in range(pieces):
        part = rest.astype(BF16)
        rest = rest - part.astype(F32)
        term = dot_piece(part)
        acc = term if acc is None else acc + term
    return acc


def _mix_kernel(*refs, n, nseq, has_s0, emit_state, manual_fetch):
    it = iter(refs)
    if manual_fetch:
        proj_hbm, tail_hbm = next(it), next(it)
    else:
        cv_ref, um_ref, ut_ref = next(it), next(it), next(it)
    (mum_ref, mut_ref, cw_ref, w0_ref, wup_ref, a0_ref, aup_ref, gup_ref, kk_ref, ka_ref,
     rk_ref, lng_ref, lnb_ref, seg_ref) = (next(it) for _ in range(14))
    s0_ref = next(it) if has_s0 else None
    o_ref = next(it)
    so_ref = next(it) if emit_state else None
    al_s, be_s, kd_s, r_s, v_s, p_s, bon_s, g_s, y_s, s_scr = (next(it) for _ in range(10))
    w2_s, w1_s, q_s = next(it), next(it), next(it)
    if manual_fetch:
        cv_ref, um_ref, ut_ref, in_sem = next(it), next(it), next(it), next(it)

        def input_copies(step):
            rows = pl.ds(pl.multiple_of(step * (nseq * n), nseq * n), nseq * n)
            return (
                pltpu.make_async_copy(proj_hbm.at[rows, pl.ds(CONV_COL, 3 * D_CONV)], cv_ref, in_sem.at[0]),
                pltpu.make_async_copy(proj_hbm.at[rows, pl.ds(RWKV_COL, 3 * D_RWKV)], um_ref, in_sem.at[1]),
                pltpu.make_async_copy(tail_hbm.at[rows, :], ut_ref, in_sem.at[2]))

        step = pl.program_id(0)

        @pl.when(step == 0)
        def _():
            for cp in input_copies(0):
                cp.start()

        for cp in input_copies(step):
            cp.wait()
    blocks_per_seq = n // ROWS_A
    n_blocks = nseq * blocks_per_seq
    n_chunks = n // CHUNK
    per_block = ROWS_A // CHUNK

    def seg_sum(x):
        return _split_sum(x, 2, lambda part: jnp.dot(part, seg_ref[...], preferred_element_type=F32))

    PAIR = 2 * RWKV_HEAD
    N_PAIRS = RWKV_HEADS // 2

    def pair_tiles(x):
        return [x[:, p * PAIR:(p + 1) * PAIR] for p in range(N_PAIRS)]

    bi = lax.broadcasted_iota(jnp.int32, (ROWS_A, ROWS_A), 0)
    bj = lax.broadcasted_iota(jnp.int32, (ROWS_A, ROWS_A), 1)
    same_chunk = (bi // CHUNK) == (bj // CHUNK)
    cum_mat = [(same_chunk & (bj <= bi)).astype(BF16), (same_chunk & (bj >= bi)).astype(BF16)]

    def pass1(c, carry):
        c0 = pl.multiple_of(c * ROWS_A, ROWS_A)
        rows = pl.ds(c0, ROWS_A)
        prev_row = pl.ds(jnp.maximum(c0 - 1, 0), 1)
        next_row = pl.ds(jnp.minimum(c0 + ROWS_A, nseq * n - 1), 1)
        seq_first = (c % blocks_per_seq) == 0
        seq_last = (c % blocks_per_seq) == blocks_per_seq - 1

        def neighbours(load):
            cur = load(rows)
            rid = lax.broadcasted_iota(jnp.int32, cur.shape, 0)
            before = jnp.where(seq_first, 0.0, load(prev_row))
            after = jnp.where(seq_last, 0.0, load(next_row))
            prev = jnp.where(rid == 0, before, pltpu.roll(cur, 1, 0))
            nxt = jnp.where(rid == ROWS_A - 1, after, pltpu.roll(cur, ROWS_A - 1, 0))
            return cur, prev, nxt

        z, zp, zn = neighbours(lambda rs: cv_ref[rs, D_CONV:2 * D_CONV] * cv_ref[rs, 2 * D_CONV:3 * D_CONV])
        cw = cw_ref[...]
        conv = cv_ref[rows, 0:D_CONV] * (zp * cw[0:1, :] + z * cw[1:2, :] + zn * cw[2:3, :])
        o_ref[rows, 0:D_CONV] = conv.astype(BF16)

        u, up, un = neighbours(lambda rs: um_ref[rs, :])
        u = u + mum_ref[...] * (0.5 * (up + un) - u)
        t, tp, tn = neighbours(lambda rs: ut_ref[rs, :])
        ut = t + mut_ref[...] * (0.5 * (tp + tn) - t)
        r = u[:, 0:D_RWKV]
        k = u[:, D_RWKV:2 * D_RWKV]
        v = u[:, 2 * D_RWKV:3 * D_RWKV]
        kk = k * kk_ref[...]
        kk = kk * lax.rsqrt(seg_sum(kk * kk) + 1e-12)
        for p, v_p in enumerate(pair_tiles(v)):
            v_s[p, rows, :] = v_p.astype(BF16)
        tw = jnp.tanh(ut)
        a_sum = jnp.zeros((ROWS_A, D_RWKV), F32)
        for d in range(2):
            zz = w0_ref[d:d + 1, :] + _dot_x3(tw, wup_ref[d])
            lw = -math.exp(-0.5) * jax.nn.sigmoid(zz)
            a = jax.nn.sigmoid(a0_ref[d:d + 1, :] + _dot_x3(ut, aup_ref[d]))
            a_sum = a_sum + a
            cum = _split_sum(lw, 3, lambda part: jnp.dot(cum_mat[d], part, preferred_element_type=F32))
            e_out = jnp.exp(-cum)
            kd = k * (1.0 + (a - 1.0) * ka_ref[...])
            scaled = ((al_s, -kk * jnp.exp(cum - lw)), (be_s, kk * a * e_out), (kd_s, kd * e_out),
                      (r_s, r * jnp.exp(cum)))
            for ref, val in scaled:
                for p, val_p in enumerate(pair_tiles(val)):
                    ref[d, p, rows, :] = val_p.astype(BF16)
            for j in range(per_block):
                last = j * CHUNK + (CHUNK - 1 if d == 0 else 0)
                for p, dec_p in enumerate(pair_tiles(jnp.exp(cum[last:last + 1, :]))):
                    p_s[d, c * per_block + j, p:p + 1, :] = dec_p
        kd_sum = k * (2.0 + (a_sum - 2.0) * ka_ref[...])
        bon_s[rows, :] = seg_sum(r * kd_sum * rk_ref[...]) * v
        g_s[rows, :] = _dot_x3(jax.nn.sigmoid(ut), gup_ref[...])
        y_s[rows, :] = jnp.zeros((ROWS_A, D_RWKV), F32)
        return carry

    lax.fori_loop(0, n_blocks, pass1, 0)

    if manual_fetch:
        @pl.when(step + 1 < pl.num_programs(0))
        def _():
            for cp in input_copies(step + 1):
                cp.start()

    for s in range(nseq):
        for d in range(2):
            for p in range(N_PAIRS):
                if has_s0:
                    s_scr[s, d, p] = jnp.concatenate([s0_ref[d, 2 * p], s0_ref[d, 2 * p + 1]], axis=1)
                else:
                    s_scr[s, d, p] = jnp.zeros((RWKV_HEAD, PAIR), F32)

    ri = lax.broadcasted_iota(jnp.int32, (CHUNK, PAIR), 0)
    lane = lax.broadcasted_iota(jnp.int32, (CHUNK, PAIR), 1)
    ci = lane % CHUNK
    low_half = lane < RWKV_HEAD
    eye = (ci == ri).astype(F32)
    diag_blk = (ri // SUB) == (ci // SUB)
    n_rounds = int(math.log2(SUB)) - 1

    def bd(x):
        zero = jnp.zeros_like(x)
        return jnp.concatenate([jnp.where(low_half, x, zero), jnp.where(low_half, zero, x)], axis=0)

    def pair_mm(a, b):
        return _mm(a, bd(b.astype(BF16)))

    def pair_mm_nt(a, b):
        return _mm(a, bd(b.astype(BF16)), _NT)

    def solve_step(c, carry):
        chains = []
        for s, d, j in ((s, d, j) for s in range(nseq) for d in range(2) for j in range(CHUNKS_A)):
            cc = s * n_chunks + c * CHUNKS_A + j
            rows = pl.ds(pl.multiple_of(cc * CHUNK, CHUNK), CHUNK)
            for p in range(N_PAIRS):
                chains.append(dict(
                    d=d, p=p, rows=rows,
                    strict=(ci < ri) if d == 0 else (ci > ri), incl=(ci <= ri) if d == 0 else (ci >= ri),
                    al=al_s[d, p, rows, :], be=be_s[d, p, rows, :], kd=kd_s[d, p, rows, :],
                    r=r_s[d, p, rows, :], v=v_s[p, rows, :]))

        def each(fn, *lists):
            return [fn(*args) for args in zip(chains, *lists)]

        alr = each(lambda ch: jnp.concatenate([ch['al'], ch['r']], axis=0))
        g = each(lambda ch, m: _mm(m, jnp.concatenate([bd(ch['be']), bd(ch['kd'])], axis=0), _NT), alr)
        a = each(lambda ch, m: jnp.where(ch['strict'], m[0:CHUNK, 0:PAIR], 0.0), g)
        a_ak = each(lambda ch, m: jnp.where(ch['strict'], m[0:CHUNK, PAIR:2 * PAIR], 0.0), g)
        q = each(lambda ch, m: jnp.concatenate(
            [jnp.where(ch['incl'], m[CHUNK:2 * CHUNK, 0:PAIR], 0.0),
             jnp.where(ch['incl'], m[CHUNK:2 * CHUNK, PAIR:2 * PAIR], 0.0)], axis=1).astype(BF16), g)
        akv = each(lambda ch, m: pair_mm(m, ch['v']), a_ak)
        xs = [jnp.where(diag_blk, m, 0.0) for m in a]
        low = [jnp.where(diag_blk, 0.0, m) for m in a]
        ts = [eye + x for x in xs]
        xs = [pair_mm(x, x) for x in xs]
        for _ in range(n_rounds - 1):
            z = [pair_mm(jnp.concatenate([x, t], axis=0), x) for x, t in zip(xs, ts)]
            xs = [m[0:CHUNK] for m in z]
            ts = [t + m[CHUNK:2 * CHUNK] for t, m in zip(ts, z)]
        ts = [t + pair_mm(t, x) for t, x in zip(ts, xs)]
        tl = each(lambda ch, t, lo, m: _mm(t, jnp.concatenate(
            [bd(lo.astype(BF16)), bd(m.astype(BF16)), bd(ch['al'])], axis=1)), ts, low, akv)
        ms = [x[:, 0:PAIR] for x in tl]
        sol = [x[:, PAIR:3 * PAIR] for x in tl]
        m2 = [pair_mm(m, m) for m in ms]
        ims = [eye + m for m in ms]
        ns = [im + pair_mm(im, mm2) for im, mm2 in zip(ims, m2)]
        w = [_mm(nn, jnp.concatenate([bd(x[:, 0:PAIR].astype(BF16)), bd(x[:, PAIR:2 * PAIR].astype(BF16))],
                                     axis=1)) for nn, x in zip(ns, sol)]
        for ch, x, qq in zip(chains, w, q):
            w2_s[ch['d'], ch['p'], ch['rows'], :] = x[:, 0:PAIR]
            w1_s[ch['d'], ch['p'], ch['rows'], :] = x[:, PAIR:2 * PAIR].astype(BF16)
            q_s[ch['d'], ch['p'], ch['rows'], :] = qq
        return carry

    lax.fori_loop(0, n_chunks // CHUNKS_A, solve_step, 0)

    def chunk_step(c, carry):
        chains = []
        for s, d in ((s, d) for s in range(nseq) for d in range(2)):
            cc = s * n_chunks + (c if d == 0 else n_chunks - 1 - c)
            rows = pl.ds(pl.multiple_of(cc * CHUNK, CHUNK), CHUNK)
            p_tile = p_s[d, cc]
            for p in range(N_PAIRS):
                chains.append(dict(
                    s=s, d=d, p=p, rows=rows, dec=p_tile[p:p + 1, :],
                    be=be_s[d, p, rows, :], kd=kd_s[d, p, rows, :], r=r_s[d, p, rows, :],
                    v=v_s[p, rows, :], w2=w2_s[d, p, rows, :], w1=w1_s[d, p, rows, :],
                    q=q_s[d, p, rows, :]))

        def each(fn, *lists):
            return [fn(*args) for args in zip(chains, *lists)]

        st = each(lambda ch: s_scr[ch['s'], ch['d'], ch['p']])
        st_bd = [bd(t.astype(BF16)) for t in st]
        ws = each(lambda ch, t: _mm(jnp.concatenate([ch['w1'], ch['r']], axis=0), t, _NT), st_bd)
        u = each(lambda ch, m: ch['w2'] + m[0:CHUNK], ws)
        ub = [x.astype(BF16) for x in u]
        ys = each(lambda ch, m, uu: m[CHUNK:2 * CHUNK]
                  + _mm(ch['q'], jnp.concatenate([bd(uu), bd(ch['v'])], axis=0)), ws, ub)
        full = each(lambda ch, uu: _mm(jnp.concatenate([uu, ch['v']], axis=0),
                                       jnp.concatenate([ch['be'], ch['kd']], axis=0), _TN), ub)
        s_new = each(lambda ch, t, m: (t + jnp.where(low_half, m[0:RWKV_HEAD], m[RWKV_HEAD:PAIR])) * ch['dec'],
                     st, full)
        for ch, t in zip(chains, s_new):
            s_scr[ch['s'], ch['d'], ch['p']] = t
        for ch, y in zip(chains, ys):
            y_s[ch['rows'], ch['p'] * PAIR:(ch['p'] + 1) * PAIR] += y
        return carry

    lax.fori_loop(0, n_chunks, chunk_step, 0)

    if emit_state:
        for s in range(nseq):
            for d in range(2):
                for p in range(N_PAIRS):
                    so_ref[s, d, 2 * p] = s_scr[s, d, p][:, 0:RWKV_HEAD]
                    so_ref[s, d, 2 * p + 1] = s_scr[s, d, p][:, RWKV_HEAD:PAIR]

    def pass3(c, carry):
        rows = pl.ds(pl.multiple_of(c * ROWS_A, ROWS_A), ROWS_A)
        y = y_s[rows, :]
        yc = y - seg_sum(y) * (1.0 / RWKV_HEAD)
        var = seg_sum(yc * yc) * (1.0 / RWKV_HEAD)
        yn = yc * lax.rsqrt(var + GN_EPS) * lng_ref[...] + lnb_ref[...]
        o_ref[rows, D_CONV:D_CONV + D_RWKV] = ((yn + bon_s[rows, :]) * g_s[rows, :]).astype(BF16)
        return carry

    lax.fori_loop(0, n_blocks, pass3, 0)


def _pad_rows(w, lo):
    return jnp.pad(w, [(0, 0)] * (w.ndim - 2) + [(lo, D_TAIL - lo - w.shape[-2]), (0, 0)])


def _mix_call(proj, tail, P, l, n, nseq, s0=None, emit_state=False):
    T = proj.shape[0]
    B = T // n
    rows = nseq * n
    assert s0 is None or nseq == 1
    full = lambda shape: pl.BlockSpec(shape, lambda b: (0,) * len(shape))
    lanes = lambda w: -(-w // 128) * 128
    in_bytes = rows * (3 * D_CONV + 3 * D_RWKV + lanes(D_TAIL)) * 4
    scratch_bytes = rows * (4 * 2 * D_RWKV * 2 + D_RWKV * 2 + 3 * D_RWKV * 4
                            + 2 * D_RWKV * (4 + 2 + 2 * 2))
    out_bytes = 2 * rows * (D_CONV + D_RWKV) * 2
    temporaries = 12 * 1024 * 1024
    manual_fetch = 2 * in_bytes + scratch_bytes + out_bytes + temporaries > VMEM_LIMIT_BYTES
    if manual_fetch:
        in_specs = [pl.BlockSpec(memory_space=pl.ANY), pl.BlockSpec(memory_space=pl.ANY)]
        token_args = [proj, tail]
    else:
        in_specs = [
            pl.BlockSpec((rows, 3 * D_CONV), lambda b: (b, CONV_COL // (3 * D_CONV))),
            pl.BlockSpec((rows, 3 * D_RWKV), lambda b: (b, RWKV_COL // (3 * D_RWKV))),
            pl.BlockSpec((rows, D_TAIL), lambda b: (b, 0)),
        ]
        token_args = [proj, proj, tail]
    in_specs += [
        full((1, 3 * D_RWKV)), full((1, D_TAIL)), full((3, D_CONV)),
        full((2, D_RWKV)), full((2, D_TAIL, D_RWKV)), full((2, D_RWKV)), full((2, D_TAIL, D_RWKV)),
        full((D_TAIL, D_RWKV)), full((1, D_RWKV)), full((1, D_RWKV)), full((1, D_RWKV)),
        full((1, D_RWKV)), full((1, D_RWKV)), full((D_RWKV, D_RWKV)),
    ]
    head = jnp.arange(D_RWKV) // RWKV_HEAD
    seg = (head[:, None] == head[None, :]).astype(BF16)
    mu = P['rwkv_mu'][l]
    row = lambda a: a.reshape(1, -1)
    args = token_args + [row(mu[:3 * D_RWKV]), row(mu[3 * D_RWKV:]), P['conv_w'][l],
            P['rwkv_w0'][l], _pad_rows(P['rwkv_w_up'][l], 0), P['rwkv_a0'][l],
            _pad_rows(P['rwkv_a_up'][l], DECAY_RANK), _pad_rows(P['rwkv_g_up'][l], DECAY_RANK + A_RANK),
            row(P['rwkv_k_k'][l]), row(P['rwkv_k_a'][l]), row(P['rwkv_r_k'][l]),
            row(P['rwkv_ln_g'][l]), row(P['rwkv_ln_b'][l]), seg]
    if s0 is not None:
        in_specs.append(pl.BlockSpec((None, None, 2, RWKV_HEADS, RWKV_HEAD, RWKV_HEAD),
                                     lambda b: (b, l, 0, 0, 0, 0)))
        args.append(s0)
    out_specs = [pl.BlockSpec((rows, D_CONV + D_RWKV), lambda b: (b, 0))]
    out_shape = [jax.ShapeDtypeStruct((T, D_CONV + D_RWKV), BF16)]
    state_shape = (2, RWKV_HEADS, RWKV_HEAD, RWKV_HEAD)
    if emit_state:
        out_specs.append(pl.BlockSpec((nseq,) + state_shape, lambda b: (b, 0, 0, 0, 0)))
        out_shape.append(jax.ShapeDtypeStruct((B,) + state_shape, F32))
    pair = 2 * RWKV_HEAD
    n_pairs = RWKV_HEADS // 2
    per_pair = lambda: pltpu.VMEM((2, n_pairs, rows, pair), BF16)
    scratch_shapes = [per_pair(), per_pair(), per_pair(), per_pair(),
                      pltpu.VMEM((n_pairs, rows, pair), BF16),
                      pltpu.VMEM((2, rows // CHUNK, n_pairs, pair), F32),
                      pltpu.VMEM((rows, D_RWKV), F32), pltpu.VMEM((rows, D_RWKV), F32),
                      pltpu.VMEM((rows, D_RWKV), F32),
                      pltpu.VMEM((nseq, 2, n_pairs, RWKV_HEAD, pair), F32),
                      pltpu.VMEM((2, n_pairs, rows, pair), F32), pltpu.VMEM((2, n_pairs, rows, pair), BF16),
                      pltpu.VMEM((2, n_pairs, rows, 2 * pair), BF16)]
    if manual_fetch:
        scratch_shapes += [pltpu.VMEM((rows, 3 * D_CONV), F32), pltpu.VMEM((rows, 3 * D_RWKV), F32),
                           pltpu.VMEM((rows, D_TAIL), F32), pltpu.SemaphoreType.DMA((3,))]
    semantics = ("arbitrary",) if manual_fetch else ("parallel",)
    return pl.pallas_call(
        functools.partial(_mix_kernel, n=n, nseq=nseq, has_s0=s0 is not None, emit_state=emit_state,
                          manual_fetch=manual_fetch),
        grid=(B // nseq,),
        in_specs=in_specs,
        out_specs=out_specs,
        out_shape=out_shape,
        scratch_shapes=scratch_shapes,
        compiler_params=pltpu.CompilerParams(dimension_semantics=semantics,
                                             vmem_limit_bytes=VMEM_LIMIT_BYTES),
        name="conv_rwkv",
    )(*args)


def _residual_copy(x_hbm, x_buf, sem):
    rows = x_buf.shape[0]
    start = pl.multiple_of(pl.program_id(0) * rows, rows)
    return pltpu.make_async_copy(x_hbm.at[pl.ds(start, rows), :], x_buf, sem)


def _outproj_kernel(at_ref, cr_ref, w_ref, x_hbm, ga_ref, shf_ref, scf_ref, g1_ref, g2_ref,
                    x1_ref, h2_ref, x_ref, x_sem):
    kt = pl.program_id(1)

    @pl.when(kt == 0)
    def _():
        _residual_copy(x_hbm, x_ref, x_sem).start()
        x1_ref[...] = jnp.zeros(x1_ref.shape, F32)

    lhs = jnp.where(kt < 2, at_ref[...], cr_ref[...])
    x1_ref[...] += jnp.dot(lhs, w_ref[...].astype(BF16), preferred_element_type=F32)

    @pl.when(kt == pl.num_programs(1) - 1)
    def _():
        _residual_copy(x_hbm, x_ref, x_sem).wait()
        x1 = x_ref[...] + ga_ref[...] * _rms(x1_ref[...], g1_ref[...])
        x1_ref[...] = x1
        h2_ref[...] = (_rms(x1, g2_ref[...]) * (1.0 + scf_ref[...]) + shf_ref[...]).astype(BF16)


def _outproj_call(attn, cr, x, mod3, ng3, w_out, l, row_fn, tm):
    T = x.shape[0]
    tk = 512
    return pl.pallas_call(
        _outproj_kernel,
        grid=(T // tm, D_MODEL // tk),
        in_specs=[
            pl.BlockSpec((tm, tk), lambda i, k: (i, jnp.minimum(k, 1))),
            pl.BlockSpec((tm, tk), lambda i, k: (i, jnp.maximum(k - 2, 0))),
            pl.BlockSpec((None, tk, D_MODEL), lambda i, k: (l, k, 0)),
            pl.BlockSpec(memory_space=pl.ANY),
            _mod_spec(row_fn, 2), _mod_spec(row_fn, 3), _mod_spec(row_fn, 4),
            _ng_spec(l, 1), _ng_spec(l, 2),
        ],
        out_specs=[
            pl.BlockSpec((tm, D_MODEL), lambda i, k: (i, 0)),
            pl.BlockSpec((tm, D_MODEL), lambda i, k: (i, 0)),
        ],
        out_shape=[jax.ShapeDtypeStruct((T, D_MODEL), F32), jax.ShapeDtypeStruct((T, D_MODEL), BF16)],
        scratch_shapes=[pltpu.VMEM((tm, D_MODEL), F32), pltpu.SemaphoreType.DMA(())],
        compiler_params=_params(2),
        name="out_proj",
    )(attn, cr, w_out, x, mod3, mod3, mod3, ng3, ng3)


def _ffn_kernel(h_ref, x1_hbm, wg_ref, wu_ref, wd_ref, gf_ref, g3_ref, o_ref, x1_ref, x1_sem):
    f = pl.program_id(1)

    @pl.when(f == 0)
    def _():
        _residual_copy(x1_hbm, x1_ref, x1_sem).start()
        o_ref[...] = jnp.zeros(o_ref.shape, F32)

    h = h_ref[...]
    gate = _dot(h, wg_ref[...])
    up = _dot(h, wu_ref[...])
    o_ref[...] += _dot(gate * jax.nn.sigmoid(gate) * up, wd_ref[...])

    @pl.when(f == pl.num_programs(1) - 1)
    def _():
        _residual_copy(x1_hbm, x1_ref, x1_sem).wait()
        o_ref[...] = x1_ref[...] + gf_ref[...] * _rms(o_ref[...], g3_ref[...])


def _ffn_call(h2, x1, mod3, ng3, w_gate, w_up, w_down, l, row_fn, tm):
    T = x1.shape[0]
    tf = 256
    return pl.pallas_call(
        _ffn_kernel,
        grid=(T // tm, D_FF // tf),
        in_specs=[
            pl.BlockSpec((tm, D_MODEL), lambda i, f: (i, 0)),
            pl.BlockSpec(memory_space=pl.ANY),
            pl.BlockSpec((None, D_MODEL, tf), lambda i, f: (l, 0, f)),
            pl.BlockSpec((None, D_MODEL, tf), lambda i, f: (l, 0, f)),
            pl.BlockSpec((None, tf, D_MODEL), lambda i, f: (l, f, 0)),
            _mod_spec(row_fn, 5),
            _ng_spec(l, 3),
        ],
        out_specs=pl.BlockSpec((tm, D_MODEL), lambda i, f: (i, 0)),
        out_shape=jax.ShapeDtypeStruct((T, D_MODEL), F32),
        scratch_shapes=[pltpu.VMEM((tm, D_MODEL), F32), pltpu.SemaphoreType.DMA(())],
        compiler_params=_params(2),
        name="ffn",
    )(h2, x1, w_gate, w_up, w_down, mod3, ng3)


def _rope_tables(n):
    rows = n // GRID_W
    row = jnp.repeat(jnp.arange(rows), GRID_W).astype(F32)
    col = jnp.tile(jnp.arange(GRID_W), rows).astype(F32)
    half = HEAD_DIM // 2
    inv = 1.0 / (ROPE_THETA ** (jnp.arange(0, half, 2, dtype=F32) / half))
    ar = row[:, None] * inv
    ac = col[:, None] * inv
    ang = jnp.concatenate([ar, ar, ac, ac], axis=-1)
    return jnp.cos(ang), jnp.sin(ang)


ROW_TILE = 1024
INPROJ_ROW_TILE = 2048


def _layer(x, mod3, ng3, P, l, n, mix_seqs, row_fn_of_tm, rope_tabs=None, cache=None, s0=None, emit=False):
    proj, tail = _inproj_call(x, mod3, ng3, P['w_in_t'], P['w_tail_t'], l,
                              row_fn_of_tm(INPROJ_ROW_TILE // 2), INPROJ_ROW_TILE)
    attn_out = _attn_call(proj, P['q_norm'][l].reshape(1, -1), P['k_norm'][l].reshape(1, -1), l, n,
                          rope_tabs=rope_tabs, cache=cache, emit_kv=emit)
    mix_out = _mix_call(proj, tail, P, l, n, mix_seqs, s0=s0, emit_state=emit)
    attn = attn_out[0]
    cr = mix_out[0]
    x1, h2 = _outproj_call(attn, cr, x, mod3, ng3, P['w_out'], l, row_fn_of_tm(ROW_TILE), ROW_TILE)
    x2 = _ffn_call(h2, x1, mod3, ng3, P['w_gate'], P['w_up'], P['w_down'], l, row_fn_of_tm(ROW_TILE),
                   ROW_TILE)
    if emit:
        return x2, attn_out[1], attn_out[2], mix_out[1]
    return x2


def kernel(x_prompt, x_sample, cache_k, cache_v, state_rwkv, c, c_ctx, w_ada, b_ada, norm_g, w_in, q_norm, k_norm, conv_w, rwkv_mu, rwkv_w0, rwkv_w_up, rwkv_a0, rwkv_a_up, rwkv_g_up, rwkv_k_k, rwkv_k_a, rwkv_r_k, rwkv_ln_g, rwkv_ln_b, w_out, w_gate, w_up, w_down):
    P = {'q_norm': q_norm, 'k_norm': k_norm, 'conv_w': conv_w, 'rwkv_mu': rwkv_mu,
         'rwkv_w0': rwkv_w0, 'rwkv_w_up': rwkv_w_up, 'rwkv_a0': rwkv_a0, 'rwkv_a_up': rwkv_a_up,
         'rwkv_g_up': rwkv_g_up, 'rwkv_k_k': rwkv_k_k, 'rwkv_k_a': rwkv_k_a, 'rwkv_r_k': rwkv_r_k,
         'rwkv_ln_g': rwkv_ln_g, 'rwkv_ln_b': rwkv_ln_b, 'w_out': w_out,
         'w_gate': w_gate, 'w_up': w_up, 'w_down': w_down}
    w_in_t = jnp.swapaxes(w_in, 1, 2)
    P['w_in_t'] = w_in_t
    P['w_tail_t'] = w_in_t[:, D_MAIN:, :]
    batch, seq, _ = x_prompt.shape
    dec_batch, dec_seq, _ = x_sample.shape

    cc = jnp.zeros((8, D_MODEL), F32).at[0].set(c_ctx).at[1:1 + dec_batch].set(c)
    mod3 = _ada_call(cc, w_ada, b_ada).reshape(DEPTH * 8, 1, N_MOD * D_MODEL)
    ng3 = norm_g.reshape(DEPTH * 4, 1, D_MODEL)

    ck = cache_k.reshape(dec_batch, DEPTH, PAST_LEN, D_KV)
    cv = cache_v.reshape(dec_batch, DEPTH, PAST_LEN, D_KV)
    rope_tabs = _rope_tables(dec_seq)

    xp = x_prompt.reshape(batch * seq, D_MODEL)
    xs = x_sample.reshape(dec_batch * dec_seq, D_MODEL)
    ks, vs, ss = [], [], []
    for l in range(DEPTH):
        ctx_row = lambda tm, l=l: (lambda i: l * 8)
        smp_row = lambda tm, l=l: (lambda i: l * 8 + 1 + (i * tm) // dec_seq)
        xp, k_l, v_l, s_l = _layer(xp, mod3, ng3, P, l, seq, 2, ctx_row, emit=True)
        ks.append(k_l.reshape(batch, seq, N_KV_HEADS, HEAD_DIM))
        vs.append(v_l.reshape(batch, seq, N_KV_HEADS, HEAD_DIM))
        ss.append(s_l)
        xs = _layer(xs, mod3, ng3, P, l, dec_seq, 1, smp_row, rope_tabs=rope_tabs, cache=(ck, cv),
                    s0=state_rwkv)
    return (xp.reshape(batch, seq, D_MODEL), xs.reshape(dec_batch, dec_seq, D_MODEL),
            jnp.stack(ks, axis=1), jnp.stack(vs, axis=1), jnp.stack(ss, axis=1))
```

```python
import functools
import math

import jax
import jax.numpy as jnp
from jax import lax
from jax.experimental import pallas as pl
from jax.experimental.pallas import tpu as pltpu

D_MODEL = 2048
DEPTH = 2
GRID_W = 64
HEAD_DIM = 128
N_HEADS = 8
N_KV_HEADS = 2
GROUP = N_HEADS // N_KV_HEADS
D_ATTN = N_HEADS * HEAD_DIM
D_KV = N_KV_HEADS * HEAD_DIM
ROPE_THETA = 10000.0
D_CONV = 512
D_RWKV = 512
RWKV_HEAD = 64
RWKV_HEADS = D_RWKV // RWKV_HEAD
DECAY_RANK = 32
A_RANK = 32
GATE_RANK = 96
D_TAIL = DECAY_RANK + A_RANK + GATE_RANK
D_IN = D_ATTN + 2 * D_KV + 3 * D_CONV + 3 * D_RWKV + D_TAIL
D_MAIN = D_IN - D_TAIL
D_FF = 5632
N_MOD = 6
EPS = 1e-6
GN_EPS = 64e-5
PAST_LEN = 256

VMEM_LIMIT_BYTES = 56 * 1024 * 1024
CHUNK = 64
BF16 = jnp.bfloat16
F32 = jnp.float32


def _params(n_grid, **kw):
    sem = ("parallel",) + ("arbitrary",) * (n_grid - 1)
    return pltpu.CompilerParams(dimension_semantics=sem, vmem_limit_bytes=VMEM_LIMIT_BYTES, **kw)


def _rms(x, g):
    return x * lax.rsqrt(jnp.mean(x * x, axis=-1, keepdims=True) + EPS) * g


def _dot(a, b):
    return jnp.dot(a.astype(BF16), b.astype(BF16), preferred_element_type=F32)


_NN = (((1,), (0,)), ((), ()))
_NT = (((1,), (1,)), ((), ()))
_TN = (((0,), (0,)), ((), ()))


def _mm(a, b, dims=_NN):
    return lax.dot_general(a.astype(BF16), b.astype(BF16), dims, preferred_element_type=F32)


def _ada_kernel(c_ref, w_ref, b_ref, o_ref):
    c = c_ref[...]
    s = c * jax.nn.sigmoid(c)
    o_ref[...] = _dot(s, w_ref[...]) + b_ref[...]


def _ada_call(cc, w_ada, b_ada):
    tn = 512
    n_out = N_MOD * D_MODEL
    return pl.pallas_call(
        _ada_kernel,
        grid=(DEPTH, n_out // tn),
        in_specs=[
            pl.BlockSpec((8, D_MODEL), lambda l, n: (0, 0)),
            pl.BlockSpec((None, D_MODEL, tn), lambda l, n: (l, 0, n)),
            pl.BlockSpec((None, 1, tn), lambda l, n: (l, 0, n)),
        ],
        out_specs=pl.BlockSpec((None, 8, tn), lambda l, n: (l, 0, n)),
        out_shape=jax.ShapeDtypeStruct((DEPTH, 8, n_out), F32),
        compiler_params=_params(2),
        name="adaln",
    )(cc, w_ada, b_ada.reshape(DEPTH, 1, n_out))


def _mod_spec(row_fn, chunk):
    return pl.BlockSpec((None, 1, D_MODEL), lambda i, *_: (row_fn(i), 0, chunk))


def _ng_spec(l, j):
    return pl.BlockSpec((None, 1, D_MODEL), lambda i, *_: (l * 4 + j, 0, 0))


INPROJ_NORM_ROWS = 256
INPROJ_VMEM_LIMIT_BYTES = 60 * 1024 * 1024


def _row_tile_copy(x_hbm, x_buf, sem, tile):
    rows = x_buf.shape[0]
    return pltpu.make_async_copy(x_hbm.at[pl.ds(pl.multiple_of(tile * rows, rows), rows), :], x_buf, sem)


def _inproj_kernel(x_hbm, sha_ref, sca_ref, shb_ref, scb_ref, g_ref, w_ref, wt_ref, o_ref, ot_ref,
                   x_buf, h_scr, x_sem):
    i = pl.program_id(0)

    @pl.when(pl.program_id(1) == 0)
    def _():
        @pl.when(i == 0)
        def _():
            _row_tile_copy(x_hbm, x_buf, x_sem, 0).start()

        _row_tile_copy(x_hbm, x_buf, x_sem, i).wait()
        half = x_buf.shape[0] // 2
        for r0 in range(0, x_buf.shape[0], INPROJ_NORM_ROWS):
            rs = slice(r0, r0 + INPROJ_NORM_ROWS)
            sh_ref, sc_ref = (sha_ref, sca_ref) if r0 < half else (shb_ref, scb_ref)
            h = _rms(x_buf[rs, :], g_ref[...] * (1.0 + sc_ref[...])) + sh_ref[...]
            h_scr[rs, :] = h.astype(BF16)

        @pl.when(i + 1 < pl.num_programs(0))
        def _():
            _row_tile_copy(x_hbm, x_buf, x_sem, i + 1).start()

        ot_ref[...] = _mm(h_scr[...], wt_ref[...], _NT)

    o_ref[...] = _mm(h_scr[...], w_ref[...], _NT)


def _inproj_call(x, mod3, ng3, w_in_t, w_tail_t, l, row_fn_half, tm):
    T = x.shape[0]
    tn = 512
    seg = lambda j: (lambda i: row_fn_half(2 * i + j))
    return pl.pallas_call(
        _inproj_kernel,
        grid=(T // tm, D_MAIN // tn),
        in_specs=[
            pl.BlockSpec(memory_space=pl.ANY),
            _mod_spec(seg(0), 0), _mod_spec(seg(0), 1),
            _mod_spec(seg(1), 0), _mod_spec(seg(1), 1),
            _ng_spec(l, 0),
            pl.BlockSpec((None, tn, D_MODEL), lambda i, n: (l, n, 0)),
            pl.BlockSpec((None, D_TAIL, D_MODEL), lambda i, n: (l, 0, 0)),
        ],
        out_specs=[
            pl.BlockSpec((tm, tn), lambda i, n: (i, n)),
            pl.BlockSpec((tm, D_TAIL), lambda i, n: (i, 0)),
        ],
        out_shape=[
            jax.ShapeDtypeStruct((T, D_MAIN), F32),
            jax.ShapeDtypeStruct((T, D_TAIL), F32),
        ],
        scratch_shapes=[pltpu.VMEM((tm, D_MODEL), F32), pltpu.VMEM((tm, D_MODEL), BF16),
                        pltpu.SemaphoreType.DMA(())],
        compiler_params=pltpu.CompilerParams(dimension_semantics=("arbitrary", "arbitrary"),
                                             vmem_limit_bytes=INPROJ_VMEM_LIMIT_BYTES),
        name="in_proj",
    )(x, mod3, mod3, mod3, mod3, ng3, w_in_t, w_tail_t)


def _rope(x, cos, sin):
    lane = lax.broadcasted_iota(jnp.int32, x.shape, 1)
    first = (lane % (HEAD_DIM // 2)) < (HEAD_DIM // 4)
    rot = jnp.where(first, -pltpu.roll(x, HEAD_DIM - HEAD_DIM // 4, 1), pltpu.roll(x, HEAD_DIM // 4, 1))
    return x * cos + rot * sin


def _attn_kernel(*refs, n, past, rope, emit_kv, qb):
    it = iter(refs)
    q_ref, k_ref, v_ref, qn_ref, kn_ref = (next(it) for _ in range(5))
    cos_ref = sin_ref = ck_ref = cv_ref = None
    if rope:
        cos_ref, sin_ref = next(it), next(it)
    if past:
        ck_ref, cv_ref = next(it), next(it)
    o_ref = next(it)
    if emit_kv:
        ko_ref, vo_ref = next(it), next(it)
    k_scr, v_scr = next(it), next(it)

    k = _rms(k_ref[...], kn_ref[...])
    v = v_ref[...]
    if emit_kv:
        ko_ref[...] = k
        vo_ref[...] = v
    if rope:
        k = _rope(k, cos_ref[...], sin_ref[...])
    k_scr[0:n, :] = k.astype(BF16)
    v_scr[0:n, :] = v.astype(BF16)
    if past:
        k_scr[n:n + past, :] = ck_ref[...].astype(BF16)
        v_scr[n:n + past, :] = cv_ref[...].astype(BF16)
    exp2_scale = HEAD_DIM ** -0.5 * math.log2(math.e)

    blocks_per_step = min(n // qb, 4)

    def block(b, carry):
        items = [(pl.ds(pl.multiple_of((b * blocks_per_step + j) * qb, qb), qb),
                  slice(g * HEAD_DIM, (g + 1) * HEAD_DIM))
                 for j in range(blocks_per_step) for g in range(GROUP)]

        def scores(item):
            rows, cols = item
            q = _rms(q_ref[rows, cols], qn_ref[...])
            if rope:
                q = _rope(q, cos_ref[rows, :], sin_ref[rows, :])
            return lax.dot_general(q.astype(BF16), k_scr[...], _NT, preferred_element_type=F32)

        def softmax(s):
            p = jnp.exp2((s - jnp.max(s, axis=-1, keepdims=True)) * exp2_scale)
            return p.astype(BF16), jnp.sum(p, axis=-1, keepdims=True)

        def values(item, pd):
            rows, cols = item
            o = jnp.dot(pd[0], v_scr[...], preferred_element_type=F32)
            o_ref[rows, cols] = (o / pd[1]).astype(BF16)

        if blocks_per_step == 1:
            ps = [softmax(s) for s in [scores(item) for item in items]]
            for item, pd in zip(items, ps):
                values(item, pd)
            return carry
        depth = 2
        ss, ps = {}, {}
        for i in range(len(items) + depth):
            if i < len(items):
                ss[i] = scores(items[i])
            if 1 <= i <= len(items):
                ps[i - 1] = softmax(ss.pop(i - 1))
            if i >= depth:
                values(items[i - depth], ps.pop(i - depth))
        return carry

    lax.fori_loop(0, n // (qb * blocks_per_step), block, 0)


def _attn_call(proj, qn, kn, l, n, rope_tabs=None, cache=None, emit_kv=False):
    T = proj.shape[0]
    B = T // n
    past = PAST_LEN if cache is not None else 0
    qw = GROUP * HEAD_DIM
    k_blk = D_ATTN // HEAD_DIM
    v_blk = (D_ATTN + D_KV) // HEAD_DIM
    in_specs = [
        pl.BlockSpec((n, qw), lambda b, h: (b, h)),
        pl.BlockSpec((n, HEAD_DIM), lambda b, h: (b, k_blk + h)),
        pl.BlockSpec((n, HEAD_DIM), lambda b, h: (b, v_blk + h)),
        pl.BlockSpec((1, HEAD_DIM), lambda b, h: (0, 0)),
        pl.BlockSpec((1, HEAD_DIM), lambda b, h: (0, 0)),
    ]
    args = [proj, proj, proj, qn, kn]
    if rope_tabs is not None:
        in_specs += [pl.BlockSpec((n, HEAD_DIM), lambda b, h: (0, 0))] * 2
        args += list(rope_tabs)
    if cache is not None:
        in_specs += [pl.BlockSpec((None, None, PAST_LEN, HEAD_DIM), lambda b, h: (b, l, 0, h))] * 2
        args += list(cache)
    out_specs = [pl.BlockSpec((n, qw), lambda b, h: (b, h))]
    out_shape = [jax.ShapeDtypeStruct((T, D_ATTN), BF16)]
    if emit_kv:
        out_specs += [pl.BlockSpec((None, n, HEAD_DIM), lambda b, h: (b, 0, h))] * 2
        out_shape += [jax.ShapeDtypeStruct((B, n, D_KV), F32)] * 2
    return pl.pallas_call(
        functools.partial(_attn_kernel, n=n, past=past, rope=rope_tabs is not None,
                          emit_kv=emit_kv, qb=256),
        grid=(B, N_KV_HEADS),
        in_specs=in_specs,
        out_specs=out_specs,
        out_shape=out_shape,
        scratch_shapes=[pltpu.VMEM((n + past, HEAD_DIM), BF16)] * 2,
        compiler_params=_params(2),
        name="attention",
    )(*args)


CONV_COL = D_ATTN + 2 * D_KV
RWKV_COL = CONV_COL + 3 * D_CONV
ROWS_A = 256
SUB = 16
CHUNKS_A = 2
assert CHUNK // SUB == 4


def _split2(x):
    hi = x.astype(BF16)
    return hi, (x - hi.astype(F32)).astype(BF16)


def _dot_x3(a, b):
    ah, al = _split2(a)
    bh, bl = _split2(b)
    mm = lambda p, q: jnp.dot(p, q, preferred_element_type=F32)
    return mm(ah, bh) + mm(ah, bl) + mm(al, bh)


def _split_sum(x, pieces, dot_piece):
    acc = None
    rest = x
    for _ in range(pieces):
        part = rest.astype(BF16)
        rest = rest - part.astype(F32)
        term = dot_piece(part)
        acc = term if acc is None else acc + term
    return acc


def _mix_kernel(*refs, n, nseq, has_s0, emit_state, manual_fetch):
    it = iter(refs)
    if manual_fetch:
        proj_hbm, tail_hbm = next(it), next(it)
    else:
        cv_ref, um_ref, ut_ref = next(it), next(it), next(it)
    (mum_ref, mut_ref, cw_ref, w0_ref, wup_ref, a0_ref, aup_ref, gup_ref, kk_ref, ka_ref,
     rk_ref, lng_ref, lnb_ref, seg_ref) = (next(it) for _ in range(14))
    s0_ref = next(it) if has_s0 else None
    o_ref = next(it)
    so_ref = next(it) if emit_state else None
    al_s, be_s, kd_s, r_s, v_s, p_s, bon_s, g_s, y_s, s_scr = (next(it) for _ in range(10))
    w2_s, w1_s, q_s = next(it), next(it), next(it)
    if manual_fetch:
        cv_ref, um_ref, ut_ref, in_sem = next(it), next(it), next(it), next(it)

        def input_copies(step):
            rows = pl.ds(pl.multiple_of(step * (nseq * n), nseq * n), nseq * n)
            return (
                pltpu.make_async_copy(proj_hbm.at[rows, pl.ds(CONV_COL, 3 * D_CONV)], cv_ref, in_sem.at[0]),
                pltpu.make_async_copy(proj_hbm.at[rows, pl.ds(RWKV_COL, 3 * D_RWKV)], um_ref, in_sem.at[1]),
                pltpu.make_async_copy(tail_hbm.at[rows, :], ut_ref, in_sem.at[2]))

        step = pl.program_id(0)

        @pl.when(step == 0)
        def _():
            for cp in input_copies(0):
                cp.start()

        for cp in input_copies(step):
            cp.wait()
    blocks_per_seq = n // ROWS_A
    n_blocks = nseq * blocks_per_seq
    n_chunks = n // CHUNK
    per_block = ROWS_A // CHUNK

    def seg_sum(x):
        return _split_sum(x, 2, lambda part: jnp.dot(part, seg_ref[...], preferred_element_type=F32))

    PAIR = 2 * RWKV_HEAD
    N_PAIRS = RWKV_HEADS // 2

    def pair_tiles(x):
        return [x[:, p * PAIR:(p + 1) * PAIR] for p in range(N_PAIRS)]

    bi = lax.broadcasted_iota(jnp.int32, (ROWS_A, ROWS_A), 0)
    bj = lax.broadcasted_iota(jnp.int32, (ROWS_A, ROWS_A), 1)
    same_chunk = (bi // CHUNK) == (bj // CHUNK)
    cum_mat = [(same_chunk & (bj <= bi)).astype(BF16), (same_chunk & (bj >= bi)).astype(BF16)]

    def pass1(c, carry):
        c0 = pl.multiple_of(c * ROWS_A, ROWS_A)
        rows = pl.ds(c0, ROWS_A)
        prev_row = pl.ds(jnp.maximum(c0 - 1, 0), 1)
        next_row = pl.ds(jnp.minimum(c0 + ROWS_A, nseq * n - 1), 1)
        seq_first = (c % blocks_per_seq) == 0
        seq_last = (c % blocks_per_seq) == blocks_per_seq - 1

        def neighbours(load):
            cur = load(rows)
            rid = lax.broadcasted_iota(jnp.int32, cur.shape, 0)
            before = jnp.where(seq_first, 0.0, load(prev_row))
            after = jnp.where(seq_last, 0.0, load(next_row))
            prev = jnp.where(rid == 0, before, pltpu.roll(cur, 1, 0))
            nxt = jnp.where(rid == ROWS_A - 1, after, pltpu.roll(cur, ROWS_A - 1, 0))
            return cur, prev, nxt

        z, zp, zn = neighbours(lambda rs: cv_ref[rs, D_CONV:2 * D_CONV] * cv_ref[rs, 2 * D_CONV:3 * D_CONV])
        cw = cw_ref[...]
        conv = cv_ref[rows, 0:D_CONV] * (zp * cw[0:1, :] + z * cw[1:2, :] + zn * cw[2:3, :])
        o_ref[rows, 0:D_CONV] = conv.astype(BF16)

        u, up, un = neighbours(lambda rs: um_ref[rs, :])
        u = u + mum_ref[...] * (0.5 * (up + un) - u)
        t, tp, tn = neighbours(lambda rs: ut_ref[rs, :])
        ut = t + mut_ref[...] * (0.5 * (tp + tn) - t)
        r = u[:, 0:D_RWKV]
        k = u[:, D_RWKV:2 * D_RWKV]
        v = u[:, 2 * D_RWKV:3 * D_RWKV]
        kk = k * kk_ref[...]
        kk = kk * lax.rsqrt(seg_sum(kk * kk) + 1e-12)
        for p, v_p in enumerate(pair_tiles(v)):
            v_s[p, rows, :] = v_p.astype(BF16)
        tw = jnp.tanh(ut)
        a_sum = jnp.zeros((ROWS_A, D_RWKV), F32)
        for d in range(2):
            zz = w0_ref[d:d + 1, :] + _dot_x3(tw, wup_ref[d])
            lw = -math.exp(-0.5) * jax.nn.sigmoid(zz)
            a = jax.nn.sigmoid(a0_ref[d:d + 1, :] + _dot_x3(ut, aup_ref[d]))
            a_sum = a_sum + a
            cum = _split_sum(lw, 3, lambda part: jnp.dot(cum_mat[d], part, preferred_element_type=F32))
            e_out = jnp.exp(-cum)
            kd = k * (1.0 + (a - 1.0) * ka_ref[...])
            scaled = ((al_s, -kk * jnp.exp(cum - lw)), (be_s, kk * a * e_out), (kd_s, kd * e_out),
                      (r_s, r * jnp.exp(cum)))
            for ref, val in scaled:
                for p, val_p in enumerate(pair_tiles(val)):
                    ref[d, p, rows, :] = val_p.astype(BF16)
            for j in range(per_block):
                last = j * CHUNK + (CHUNK - 1 if d == 0 else 0)
                for p, dec_p in enumerate(pair_tiles(jnp.exp(cum[last:last + 1, :]))):
                    p_s[d, c * per_block + j, p:p + 1, :] = dec_p
        kd_sum = k * (2.0 + (a_sum - 2.0) * ka_ref[...])
        bon_s[rows, :] = seg_sum(r * kd_sum * rk_ref[...]) * v
        g_s[rows, :] = _dot_x3(jax.nn.sigmoid(ut), gup_ref[...])
        y_s[rows, :] = jnp.zeros((ROWS_A, D_RWKV), F32)
        return carry

    lax.fori_loop(0, n_blocks, pass1, 0)

    if manual_fetch:
        @pl.when(step + 1 < pl.num_programs(0))
        def _():
            for cp in input_copies(step + 1):
                cp.start()

    for s in range(nseq):
        for d in range(2):
            for p in range(N_PAIRS):
                if has_s0:
                    s_scr[s, d, p] = jnp.concatenate([s0_ref[d, 2 * p], s0_ref[d, 2 * p + 1]], axis=1)
                else:
                    s_scr[s, d, p] = jnp.zeros((RWKV_HEAD, PAIR), F32)

    ri = lax.broadcasted_iota(jnp.int32, (CHUNK, PAIR), 0)
    lane = lax.broadcasted_iota(jnp.int32, (CHUNK, PAIR), 1)
    ci = lane % CHUNK
    low_half = lane < RWKV_HEAD
    eye = (ci == ri).astype(F32)
    diag_blk = (ri // SUB) == (ci // SUB)
    n_rounds = int(math.log2(SUB)) - 1

    def bd(x):
        zero = jnp.zeros_like(x)
        return jnp.concatenate([jnp.where(low_half, x, zero), jnp.where(low_half, zero, x)], axis=0)

    def pair_mm(a, b):
        return _mm(a, bd(b.astype(BF16)))

    def pair_mm_nt(a, b):
        return _mm(a, bd(b.astype(BF16)), _NT)

    def solve_step(c, carry):
        chains = []
        for s, d, j in ((s, d, j) for s in range(nseq) for d in range(2) for j in range(CHUNKS_A)):
            cc = s * n_chunks + c * CHUNKS_A + j
            rows = pl.ds(pl.multiple_of(cc * CHUNK, CHUNK), CHUNK)
            for p in range(N_PAIRS):
                chains.append(dict(
                    d=d, p=p, rows=rows,
                    strict=(ci < ri) if d == 0 else (ci > ri), incl=(ci <= ri) if d == 0 else (ci >= ri),
                    al=al_s[d, p, rows, :], be=be_s[d, p, rows, :], kd=kd_s[d, p, rows, :],
                    r=r_s[d, p, rows, :], v=v_s[p, rows, :]))

        def each(fn, *lists):
            return [fn(*args) for args in zip(chains, *lists)]

        alr = each(lambda ch: jnp.concatenate([ch['al'], ch['r']], axis=0))
        g = each(lambda ch, m: _mm(m, jnp.concatenate([bd(ch['be']), bd(ch['kd'])], axis=0), _NT), alr)
        a = each(lambda ch, m: jnp.where(ch['strict'], m[0:CHUNK, 0:PAIR], 0.0), g)
        a_ak = each(lambda ch, m: jnp.where(ch['strict'], m[0:CHUNK, PAIR:2 * PAIR], 0.0), g)
        q = each(lambda ch, m: jnp.concatenate(
            [jnp.where(ch['incl'], m[CHUNK:2 * CHUNK, 0:PAIR], 0.0),
             jnp.where(ch['incl'], m[CHUNK:2 * CHUNK, PAIR:2 * PAIR], 0.0)], axis=1).astype(BF16), g)
        akv = each(lambda ch, m: pair_mm(m, ch['v']), a_ak)
        xs = [jnp.where(diag_blk, m, 0.0) for m in a]
        low = [jnp.where(diag_blk, 0.0, m) for m in a]
        ts = [eye + x for x in xs]
        xs = [pair_mm(x, x) for x in xs]
        for _ in range(n_rounds - 1):
            z = [pair_mm(jnp.concatenate([x, t], axis=0), x) for x, t in zip(xs, ts)]
            xs = [m[0:CHUNK] for m in z]
            ts = [t + m[CHUNK:2 * CHUNK] for t, m in zip(ts, z)]
        ts = [t + pair_mm(t, x) for t, x in zip(ts, xs)]
        tl = each(lambda ch, t, lo, m: _mm(t, jnp.concatenate(
            [bd(lo.astype(BF16)), bd(m.astype(BF16)), bd(ch['al'])], axis=1)), ts, low, akv)
        ms = [x[:, 0:PAIR] for x in tl]
        sol = [x[:, PAIR:3 * PAIR] for x in tl]
        m2 = [pair_mm(m, m) for m in ms]
        ims = [eye + m for m in ms]
        ns = [im + pair_mm(im, mm2) for im, mm2 in zip(ims, m2)]
        w = [_mm(nn, jnp.concatenate([bd(x[:, 0:PAIR].astype(BF16)), bd(x[:, PAIR:2 * PAIR].astype(BF16))],
                                     axis=1)) for nn, x in zip(ns, sol)]
        for ch, x, qq in zip(chains, w, q):
            w2_s[ch['d'], ch['p'], ch['rows'], :] = x[:, 0:PAIR]
            w1_s[ch['d'], ch['p'], ch['rows'], :] = x[:, PAIR:2 * PAIR].astype(BF16)
            q_s[ch['d'], ch['p'], ch['rows'], :] = qq
        return carry

    lax.fori_loop(0, n_chunks // CHUNKS_A, solve_step, 0)

    def chunk_step(c, carry):
        chains = []
        for s, d in ((s, d) for s in range(nseq) for d in range(2)):
            cc = s * n_chunks + (c if d == 0 else n_chunks - 1 - c)
            rows = pl.ds(pl.multiple_of(cc * CHUNK, CHUNK), CHUNK)
            p_tile = p_s[d, cc]
            for p in range(N_PAIRS):
                chains.append(dict(
                    s=s, d=d, p=p, rows=rows, dec=p_tile[p:p + 1, :],
                    be=be_s[d, p, rows, :], kd=kd_s[d, p, rows, :], r=r_s[d, p, rows, :],
                    v=v_s[p, rows, :], w2=w2_s[d, p, rows, :], w1=w1_s[d, p, rows, :],
                    q=q_s[d, p, rows, :]))

        def each(fn, *lists):
            return [fn(*args) for args in zip(chains, *lists)]

        st = each(lambda ch: s_scr[ch['s'], ch['d'], ch['p']])
        st_bd = [bd(t.astype(BF16)) for t in st]
        ws = each(lambda ch, t: _mm(jnp.concatenate([ch['w1'], ch['r']], axis=0), t, _NT), st_bd)
        u = each(lambda ch, m: ch['w2'] + m[0:CHUNK], ws)
        ub = [x.astype(BF16) for x in u]
        ys = each(lambda ch, m, uu: m[CHUNK:2 * CHUNK]
                  + _mm(ch['q'], jnp.concatenate([bd(uu), bd(ch['v'])], axis=0)), ws, ub)
        full = each(lambda ch, uu: _mm(jnp.concatenate([uu, ch['v']], axis=0),
                                       jnp.concatenate([ch['be'], ch['kd']], axis=0), _TN), ub)
        s_new = each(lambda ch, t, m: (t + jnp.where(low_half, m[0:RWKV_HEAD], m[RWKV_HEAD:PAIR])) * ch['dec'],
                     st, full)
        for ch, t in zip(chains, s_new):
            s_scr[ch['s'], ch['d'], ch['p']] = t
        for ch, y in zip(chains, ys):
            y_s[ch['rows'], ch['p'] * PAIR:(ch['p'] + 1) * PAIR] += y
        return carry

    lax.fori_loop(0, n_chunks, chunk_step, 0)

    if emit_state:
        for s in range(nseq):
            for d in range(2):
                for p in range(N_PAIRS):
                    so_ref[s, d, 2 * p] = s_scr[s, d, p][:, 0:RWKV_HEAD]
                    so_ref[s, d, 2 * p + 1] = s_scr[s, d, p][:, RWKV_HEAD:PAIR]

    def pass3(c, carry):
        rows = pl.ds(pl.multiple_of(c * ROWS_A, ROWS_A), ROWS_A)
        y = y_s[rows, :]
        yc = y - seg_sum(y) * (1.0 / RWKV_HEAD)
        var = seg_sum(yc * yc) * (1.0 / RWKV_HEAD)
        yn = yc * lax.rsqrt(var + GN_EPS) * lng_ref[...] + lnb_ref[...]
        o_ref[rows, D_CONV:D_CONV + D_RWKV] = ((yn + bon_s[rows, :]) * g_s[rows, :]).astype(BF16)
        return carry

    lax.fori_loop(0, n_blocks, pass3, 0)


def _pad_rows(w, lo):
    return jnp.pad(w, [(0, 0)] * (w.ndim - 2) + [(lo, D_TAIL - lo - w.shape[-2]), (0, 0)])


def _mix_call(proj, tail, P, l, n, nseq, s0=None, emit_state=False):
    T = proj.shape[0]
    B = T // n
    rows = nseq * n
    assert s0 is None or nseq == 1
    full = lambda shape: pl.BlockSpec(shape, lambda b: (0,) * len(shape))
    lanes = lambda w: -(-w // 128) * 128
    in_bytes = rows * (3 * D_CONV + 3 * D_RWKV + lanes(D_TAIL)) * 4
    scratch_bytes = rows * (4 * 2 * D_RWKV * 2 + D_RWKV * 2 + 3 * D_RWKV * 4
                            + 2 * D_RWKV * (4 + 2 + 2 * 2))
    out_bytes = 2 * rows * (D_CONV + D_RWKV) * 2
    temporaries = 12 * 1024 * 1024
    manual_fetch = 2 * in_bytes + scratch_bytes + out_bytes + temporaries > VMEM_LIMIT_BYTES
    if manual_fetch:
        in_specs = [pl.BlockSpec(memory_space=pl.ANY), pl.BlockSpec(memory_space=pl.ANY)]
        token_args = [proj, tail]
    else:
        in_specs = [
            pl.BlockSpec((rows, 3 * D_CONV), lambda b: (b, CONV_COL // (3 * D_CONV))),
            pl.BlockSpec((rows, 3 * D_RWKV), lambda b: (b, RWKV_COL // (3 * D_RWKV))),
            pl.BlockSpec((rows, D_TAIL), lambda b: (b, 0)),
        ]
        token_args = [proj, proj, tail]
    in_specs += [
        full((1, 3 * D_RWKV)), full((1, D_TAIL)), full((3, D_CONV)),
        full((2, D_RWKV)), full((2, D_TAIL, D_RWKV)), full((2, D_RWKV)), full((2, D_TAIL, D_RWKV)),
        full((D_TAIL, D_RWKV)), full((1, D_RWKV)), full((1, D_RWKV)), full((1, D_RWKV)),
        full((1, D_RWKV)), full((1, D_RWKV)), full((D_RWKV, D_RWKV)),
    ]
    head = jnp.arange(D_RWKV) // RWKV_HEAD
    seg = (head[:, None] == head[None, :]).astype(BF16)
    mu = P['rwkv_mu'][l]
    row = lambda a: a.reshape(1, -1)
    args = token_args + [row(mu[:3 * D_RWKV]), row(mu[3 * D_RWKV:]), P['conv_w'][l],
            P['rwkv_w0'][l], _pad_rows(P['rwkv_w_up'][l], 0), P['rwkv_a0'][l],
            _pad_rows(P['rwkv_a_up'][l], DECAY_RANK), _pad_rows(P['rwkv_g_up'][l], DECAY_RANK + A_RANK),
            row(P['rwkv_k_k'][l]), row(P['rwkv_k_a'][l]), row(P['rwkv_r_k'][l]),
            row(P['rwkv_ln_g'][l]), row(P['rwkv_ln_b'][l]), seg]
    if s0 is not None:
        in_specs.append(pl.BlockSpec((None, None, 2, RWKV_HEADS, RWKV_HEAD, RWKV_HEAD),
                                     lambda b: (b, l, 0, 0, 0, 0)))
        args.append(s0)
    out_specs = [pl.BlockSpec((rows, D_CONV + D_RWKV), lambda b: (b, 0))]
    out_shape = [jax.ShapeDtypeStruct((T, D_CONV + D_RWKV), BF16)]
    state_shape = (2, RWKV_HEADS, RWKV_HEAD, RWKV_HEAD)
    if emit_state:
        out_specs.append(pl.BlockSpec((nseq,) + state_shape, lambda b: (b, 0, 0, 0, 0)))
        out_shape.append(jax.ShapeDtypeStruct((B,) + state_shape, F32))
    pair = 2 * RWKV_HEAD
    n_pairs = RWKV_HEADS // 2
    per_pair = lambda: pltpu.VMEM((2, n_pairs, rows, pair), BF16)
    scratch_shapes = [per_pair(), per_pair(), per_pair(), per_pair(),
                      pltpu.VMEM((n_pairs, rows, pair), BF16),
                      pltpu.VMEM((2, rows // CHUNK, n_pairs, pair), F32),
                      pltpu.VMEM((rows, D_RWKV), F32), pltpu.VMEM((rows, D_RWKV), F32),
                      pltpu.VMEM((rows, D_RWKV), F32),
                      pltpu.VMEM((nseq, 2, n_pairs, RWKV_HEAD, pair), F32),
                      pltpu.VMEM((2, n_pairs, rows, pair), F32), pltpu.VMEM((2, n_pairs, rows, pair), BF16),
                      pltpu.VMEM((2, n_pairs, rows, 2 * pair), BF16)]
    if manual_fetch:
        scratch_shapes += [pltpu.VMEM((rows, 3 * D_CONV), F32), pltpu.VMEM((rows, 3 * D_RWKV), F32),
                           pltpu.VMEM((rows, D_TAIL), F32), pltpu.SemaphoreType.DMA((3,))]
    semantics = ("arbitrary",) if manual_fetch else ("parallel",)
    return pl.pallas_call(
        functools.partial(_mix_kernel, n=n, nseq=nseq, has_s0=s0 is not None, emit_state=emit_state,
                          manual_fetch=manual_fetch),
        grid=(B // nseq,),
        in_specs=in_specs,
        out_specs=out_specs,
        out_shape=out_shape,
        scratch_shapes=scratch_shapes,
        compiler_params=pltpu.CompilerParams(dimension_semantics=semantics,
                                             vmem_limit_bytes=VMEM_LIMIT_BYTES),
        name="conv_rwkv",
    )(*args)


def _residual_copy(x_hbm, x_buf, sem):
    rows = x_buf.shape[0]
    start = pl.multiple_of(pl.program_id(0) * rows, rows)
    return pltpu.make_async_copy(x_hbm.at[pl.ds(start, rows), :], x_buf, sem)


def _outproj_kernel(at_ref, cr_ref, w_ref, x_hbm, ga_ref, shf_ref, scf_ref, g1_ref, g2_ref,
                    x1_ref, h2_ref, x_ref, x_sem):
    kt = pl.program_id(1)

    @pl.when(kt == 0)
    def _():
        _residual_copy(x_hbm, x_ref, x_sem).start()
        x1_ref[...] = jnp.zeros(x1_ref.shape, F32)

    lhs = jnp.where(kt < 2, at_ref[...], cr_ref[...])
    x1_ref[...] += jnp.dot(lhs, w_ref[...].astype(BF16), preferred_element_type=F32)

    @pl.when(kt == pl.num_programs(1) - 1)
    def _():
        _residual_copy(x_hbm, x_ref, x_sem).wait()
        x1 = x_ref[...] + _rms(x1_ref[...], ga_ref[...] * g1_ref[...])
        x1_ref[...] = x1
        h2_ref[...] = (_rms(x1, g2_ref[...] * (1.0 + scf_ref[...])) + shf_ref[...]).astype(BF16)


def _outproj_call(attn, cr, x, mod3, ng3, w_out, l, row_fn, tm):
    T = x.shape[0]
    tk = 512
    return pl.pallas_call(
        _outproj_kernel,
        grid=(T // tm, D_MODEL // tk),
        in_specs=[
            pl.BlockSpec((tm, tk), lambda i, k: (i, jnp.minimum(k, 1))),
            pl.BlockSpec((tm, tk), lambda i, k: (i, jnp.maximum(k - 2, 0))),
            pl.BlockSpec((None, tk, D_MODEL), lambda i, k: (l, k, 0)),
            pl.BlockSpec(memory_space=pl.ANY),
            _mod_spec(row_fn, 2), _mod_spec(row_fn, 3), _mod_spec(row_fn, 4),
            _ng_spec(l, 1), _ng_spec(l, 2),
        ],
        out_specs=[
            pl.BlockSpec((tm, D_MODEL), lambda i, k: (i, 0)),
            pl.BlockSpec((tm, D_MODEL), lambda i, k: (i, 0)),
        ],
        out_shape=[jax.ShapeDtypeStruct((T, D_MODEL), F32), jax.ShapeDtypeStruct((T, D_MODEL), BF16)],
        scratch_shapes=[pltpu.VMEM((tm, D_MODEL), F32), pltpu.SemaphoreType.DMA(())],
        compiler_params=_params(2),
        name="out_proj",
    )(attn, cr, w_out, x, mod3, mod3, mod3, ng3, ng3)


def _ffn_kernel(h_ref, x1_hbm, wg_ref, wu_ref, wd_ref, gf_ref, g3_ref, o_ref, x1_ref, x1_sem):
    f = pl.program_id(1)

    @pl.when(f == 0)
    def _():
        _residual_copy(x1_hbm, x1_ref, x1_sem).start()
        o_ref[...] = jnp.zeros(o_ref.shape, F32)

    h = h_ref[...]
    gate = _dot(h, wg_ref[...])
    up = _dot(h, wu_ref[...])
    o_ref[...] += _dot(gate * jax.nn.sigmoid(gate) * up, wd_ref[...])

    @pl.when(f == pl.num_programs(1) - 1)
    def _():
        _residual_copy(x1_hbm, x1_ref, x1_sem).wait()
        o_ref[...] = x1_ref[...] + _rms(o_ref[...], gf_ref[...] * g3_ref[...])


def _ffn_call(h2, x1, mod3, ng3, w_gate, w_up, w_down, l, row_fn, tm):
    T = x1.shape[0]
    tf = 256
    return pl.pallas_call(
        _ffn_kernel,
        grid=(T // tm, D_FF // tf),
        in_specs=[
            pl.BlockSpec((tm, D_MODEL), lambda i, f: (i, 0)),
            pl.BlockSpec(memory_space=pl.ANY),
            pl.BlockSpec((None, D_MODEL, tf), lambda i, f: (l, 0, f)),
            pl.BlockSpec((None, D_MODEL, tf), lambda i, f: (l, 0, f)),
            pl.BlockSpec((None, tf, D_MODEL), lambda i, f: (l, f, 0)),
            _mod_spec(row_fn, 5),
            _ng_spec(l, 3),
        ],
        out_specs=pl.BlockSpec((tm, D_MODEL), lambda i, f: (i, 0)),
        out_shape=jax.ShapeDtypeStruct((T, D_MODEL), F32),
        scratch_shapes=[pltpu.VMEM((tm, D_MODEL), F32), pltpu.SemaphoreType.DMA(())],
        compiler_params=_params(2),
        name="ffn",
    )(h2, x1, w_gate, w_up, w_down, mod3, ng3)


def _rope_tables(n):
    rows = n // GRID_W
    row = jnp.repeat(jnp.arange(rows), GRID_W).astype(F32)
    col = jnp.tile(jnp.arange(GRID_W), rows).astype(F32)
    half = HEAD_DIM // 2
    inv = 1.0 / (ROPE_THETA ** (jnp.arange(0, half, 2, dtype=F32) / half))
    ar = row[:, None] * inv
    ac = col[:, None] * inv
    ang = jnp.concatenate([ar, ar, ac, ac], axis=-1)
    return jnp.cos(ang), jnp.sin(ang)


ROW_TILE = 1024
INPROJ_ROW_TILE = 2048


def _layer(x, mod3, ng3, P, l, n, mix_seqs, row_fn_of_tm, rope_tabs=None, cache=None, s0=None, emit=False):
    proj, tail = _inproj_call(x, mod3, ng3, P['w_in_t'], P['w_tail_t'], l,
                              row_fn_of_tm(INPROJ_ROW_TILE // 2), INPROJ_ROW_TILE)
    attn_out = _attn_call(proj, P['q_norm'][l].reshape(1, -1), P['k_norm'][l].reshape(1, -1), l, n,
                          rope_tabs=rope_tabs, cache=cache, emit_kv=emit)
    mix_out = _mix_call(proj, tail, P, l, n, mix_seqs, s0=s0, emit_state=emit)
    attn = attn_out[0]
    cr = mix_out[0]
    x1, h2 = _outproj_call(attn, cr, x, mod3, ng3, P['w_out'], l, row_fn_of_tm(ROW_TILE), ROW_TILE)
    x2 = _ffn_call(h2, x1, mod3, ng3, P['w_gate'], P['w_up'], P['w_down'], l, row_fn_of_tm(ROW_TILE),
                   ROW_TILE)
    if emit:
        return x2, attn_out[1], attn_out[2], mix_out[1]
    return x2


def kernel(x_prompt, x_sample, cache_k, cache_v, state_rwkv, c, c_ctx, w_ada, b_ada, norm_g, w_in, q_norm, k_norm, conv_w, rwkv_mu, rwkv_w0, rwkv_w_up, rwkv_a0, rwkv_a_up, rwkv_g_up, rwkv_k_k, rwkv_k_a, rwkv_r_k, rwkv_ln_g, rwkv_ln_b, w_out, w_gate, w_up, w_down):
    P = {'q_norm': q_norm, 'k_norm': k_norm, 'conv_w': conv_w, 'rwkv_mu': rwkv_mu,
         'rwkv_w0': rwkv_w0, 'rwkv_w_up': rwkv_w_up, 'rwkv_a0': rwkv_a0, 'rwkv_a_up': rwkv_a_up,
         'rwkv_g_up': rwkv_g_up, 'rwkv_k_k': rwkv_k_k, 'rwkv_k_a': rwkv_k_a, 'rwkv_r_k': rwkv_r_k,
         'rwkv_ln_g': rwkv_ln_g, 'rwkv_ln_b': rwkv_ln_b, 'w_out': w_out,
         'w_gate': w_gate, 'w_up': w_up, 'w_down': w_down}
    w_in_t = jnp.swapaxes(w_in, 1, 2)
    P['w_in_t'] = w_in_t
    P['w_tail_t'] = w_in_t[:, D_MAIN:, :]
    batch, seq, _ = x_prompt.shape
    dec_batch, dec_seq, _ = x_sample.shape

    cc = jnp.zeros((8, D_MODEL), F32).at[0].set(c_ctx).at[1:1 + dec_batch].set(c)
    mod3 = _ada_call(cc, w_ada, b_ada).reshape(DEPTH * 8, 1, N_MOD * D_MODEL)
    ng3 = norm_g.reshape(DEPTH * 4, 1, D_MODEL)

    ck = cache_k.reshape(dec_batch, DEPTH, PAST_LEN, D_KV)
    cv = cache_v.reshape(dec_batch, DEPTH, PAST_LEN, D_KV)
    rope_tabs = _rope_tables(dec_seq)

    xp = x_prompt.reshape(batch * seq, D_MODEL)
    xs = x_sample.reshape(dec_batch * dec_seq, D_MODEL)
    ks, vs, ss = [], [], []
    for l in range(DEPTH):
        ctx_row = lambda tm, l=l: (lambda i: l * 8)
        smp_row = lambda tm, l=l: (lambda i: l * 8 + 1 + (i * tm) // dec_seq)
        xp, k_l, v_l, s_l = _layer(xp, mod3, ng3, P, l, seq, 2, ctx_row, emit=True)
        ks.append(k_l.reshape(batch, seq, N_KV_HEADS, HEAD_DIM))
        vs.append(v_l.reshape(batch, seq, N_KV_HEADS, HEAD_DIM))
        ss.append(s_l)
        xs = _layer(xs, mod3, ng3, P, l, dec_seq, 1, smp_row, rope_tabs=rope_tabs, cache=(ck, cv),
                    s0=state_rwkv)
    return (xp.reshape(batch, seq, D_MODEL), xs.reshape(dec_batch, dec_seq, D_MODEL),
            jnp.stack(ks, axis=1), jnp.stack(vs, axis=1), jnp.stack(ss, axis=1))
```

```python
import functools
import math

import jax
import jax.numpy as jnp
from jax import lax
from jax.experimental import pallas as pl
from jax.experimental.pallas import tpu as pltpu

D_MODEL = 2048
DEPTH = 2
GRID_W = 64
HEAD_DIM = 128
N_HEADS = 8
N_KV_HEADS = 2
GROUP = N_HEADS // N_KV_HEADS
D_ATTN = N_HEADS * HEAD_DIM
D_KV = N_KV_HEADS * HEAD_DIM
ROPE_THETA = 10000.0
D_CONV = 512
D_RWKV = 512
RWKV_HEAD = 64
RWKV_HEADS = D_RWKV // RWKV_HEAD
DECAY_RANK = 32
A_RANK = 32
GATE_RANK = 96
D_TAIL = DECAY_RANK + A_RANK + GATE_RANK
D_IN = D_ATTN + 2 * D_KV + 3 * D_CONV + 3 * D_RWKV + D_TAIL
D_MAIN = D_IN - D_TAIL
D_FF = 5632
N_MOD = 6
EPS = 1e-6
GN_EPS = 64e-5
PAST_LEN = 256

VMEM_LIMIT_BYTES = 56 * 1024 * 1024
CHUNK = 64
BF16 = jnp.bfloat16
F32 = jnp.float32


def _params(n_grid, **kw):
    sem = ("parallel",) + ("arbitrary",) * (n_grid - 1)
    return pltpu.CompilerParams(dimension_semantics=sem, vmem_limit_bytes=VMEM_LIMIT_BYTES, **kw)


def _rms(x, g):
    return x * lax.rsqrt(jnp.mean(x * x, axis=-1, keepdims=True) + EPS) * g


def _dot(a, b):
    return jnp.dot(a.astype(BF16), b.astype(BF16), preferred_element_type=F32)


_NN = (((1,), (0,)), ((), ()))
_NT = (((1,), (1,)), ((), ()))
_TN = (((0,), (0,)), ((), ()))


def _mm(a, b, dims=_NN):
    return lax.dot_general(a.astype(BF16), b.astype(BF16), dims, preferred_element_type=F32)


def _ada_kernel(c_ref, w_ref, b_ref, o_ref):
    c = c_ref[...]
    s = c * jax.nn.sigmoid(c)
    o_ref[...] = _dot(s, w_ref[...]) + b_ref[...]


def _ada_call(cc, w_ada, b_ada):
    tn = 512
    n_out = N_MOD * D_MODEL
    return pl.pallas_call(
        _ada_kernel,
        grid=(DEPTH, n_out // tn),
        in_specs=[
            pl.BlockSpec((8, D_MODEL), lambda l, n: (0, 0)),
            pl.BlockSpec((None, D_MODEL, tn), lambda l, n: (l, 0, n)),
            pl.BlockSpec((None, 1, tn), lambda l, n: (l, 0, n)),
        ],
        out_specs=pl.BlockSpec((None, 8, tn), lambda l, n: (l, 0, n)),
        out_shape=jax.ShapeDtypeStruct((DEPTH, 8, n_out), F32),
        compiler_params=_params(2),
        name="adaln",
    )(cc, w_ada, b_ada.reshape(DEPTH, 1, n_out))


def _mod_spec(row_fn, chunk):
    return pl.BlockSpec((None, 1, D_MODEL), lambda i, *_: (row_fn(i), 0, chunk))


def _ng_spec(l, j):
    return pl.BlockSpec((None, 1, D_MODEL), lambda i, *_: (l * 4 + j, 0, 0))


INPROJ_NORM_ROWS = 256
INPROJ_VMEM_LIMIT_BYTES = 60 * 1024 * 1024


def _row_tile_copy(x_hbm, x_buf, sem, tile):
    rows = x_buf.shape[0]
    return pltpu.make_async_copy(x_hbm.at[pl.ds(pl.multiple_of(tile * rows, rows), rows), :], x_buf, sem)


def _inproj_kernel(x_hbm, sha_ref, sca_ref, shb_ref, scb_ref, g_ref, w_ref, wt_ref, o_ref, ot_ref,
                   x_buf, h_scr, x_sem):
    i = pl.program_id(0)

    @pl.when(pl.program_id(1) == 0)
    def _():
        @pl.when(i == 0)
        def _():
            _row_tile_copy(x_hbm, x_buf, x_sem, 0).start()

        _row_tile_copy(x_hbm, x_buf, x_sem, i).wait()
        half = x_buf.shape[0] // 2
        for r0 in range(0, x_buf.shape[0], INPROJ_NORM_ROWS):
            rs = slice(r0, r0 + INPROJ_NORM_ROWS)
            sh_ref, sc_ref = (sha_ref, sca_ref) if r0 < half else (shb_ref, scb_ref)
            h = _rms(x_buf[rs, :], g_ref[...] * (1.0 + sc_ref[...])) + sh_ref[...]
            h_scr[rs, :] = h.astype(BF16)

        @pl.when(i + 1 < pl.num_programs(0))
        def _():
            _row_tile_copy(x_hbm, x_buf, x_sem, i + 1).start()

        ot_ref[...] = _mm(h_scr[...], wt_ref[...], _NT)

    o_ref[...] = _mm(h_scr[...], w_ref[...], _NT)


def _inproj_call(x, mod3, ng3, w_in_t, w_tail_t, l, row_fn_half, tm):
    T = x.shape[0]
    tn = 512
    seg = lambda j: (lambda i: row_fn_half(2 * i + j))
    return pl.pallas_call(
        _inproj_kernel,
        grid=(T // tm, D_MAIN // tn),
        in_specs=[
            pl.BlockSpec(memory_space=pl.ANY),
            _mod_spec(seg(0), 0), _mod_spec(seg(0), 1),
            _mod_spec(seg(1), 0), _mod_spec(seg(1), 1),
            _ng_spec(l, 0),
            pl.BlockSpec((None, tn, D_MODEL), lambda i, n: (l, n, 0)),
            pl.BlockSpec((None, D_TAIL, D_MODEL), lambda i, n: (l, 0, 0)),
        ],
        out_specs=[
            pl.BlockSpec((tm, tn), lambda i, n: (i, n)),
            pl.BlockSpec((tm, D_TAIL), lambda i, n: (i, 0)),
        ],
        out_shape=[
            jax.ShapeDtypeStruct((T, D_MAIN), F32),
            jax.ShapeDtypeStruct((T, D_TAIL), F32),
        ],
        scratch_shapes=[pltpu.VMEM((tm, D_MODEL), F32), pltpu.VMEM((tm, D_MODEL), BF16),
                        pltpu.SemaphoreType.DMA(())],
        compiler_params=pltpu.CompilerParams(dimension_semantics=("arbitrary", "arbitrary"),
                                             vmem_limit_bytes=INPROJ_VMEM_LIMIT_BYTES),
        name="in_proj",
    )(x, mod3, mod3, mod3, mod3, ng3, w_in_t, w_tail_t)


def _rope(x, cos, sin):
    lane = lax.broadcasted_iota(jnp.int32, x.shape, 1)
    first = (lane % (HEAD_DIM // 2)) < (HEAD_DIM // 4)
    rot = jnp.where(first, -pltpu.roll(x, HEAD_DIM - HEAD_DIM // 4, 1), pltpu.roll(x, HEAD_DIM // 4, 1))
    return x * cos + rot * sin


def _attn_kernel(*refs, n, past, rope, emit_kv, qb):
    it = iter(refs)
    q_ref, k_ref, v_ref, qn_ref, kn_ref = (next(it) for _ in range(5))
    cos_ref = sin_ref = ck_ref = cv_ref = None
    if rope:
        cos_ref, sin_ref = next(it), next(it)
    if past:
        ck_ref, cv_ref = next(it), next(it)
    o_ref = next(it)
    if emit_kv:
        ko_ref, vo_ref = next(it), next(it)
    k_scr, v_scr = next(it), next(it)

    k = _rms(k_ref[...], kn_ref[...])
    v = v_ref[...]
    if emit_kv:
        ko_ref[...] = k
        vo_ref[...] = v
    if rope:
        k = _rope(k, cos_ref[...], sin_ref[...])
    k_scr[0:n, :] = k.astype(BF16)
    v_scr[0:n, :] = v.astype(BF16)
    if past:
        k_scr[n:n + past, :] = ck_ref[...].astype(BF16)
        v_scr[n:n + past, :] = cv_ref[...].astype(BF16)
    exp2_scale = HEAD_DIM ** -0.5 * math.log2(math.e)

    blocks_per_step = min(n // qb, 4)

    def block(b, carry):
        items = [(pl.ds(pl.multiple_of((b * blocks_per_step + j) * qb, qb), qb),
                  slice(g * HEAD_DIM, (g + 1) * HEAD_DIM))
                 for j in range(blocks_per_step) for g in range(GROUP)]

        def scores(item):
            rows, cols = item
            q = _rms(q_ref[rows, cols], qn_ref[...])
            if rope:
                q = _rope(q, cos_ref[rows, :], sin_ref[rows, :])
            return lax.dot_general(q.astype(BF16), k_scr[...], _NT, preferred_element_type=F32)

        def softmax(s):
            p = jnp.exp2((s - jnp.max(s, axis=-1, keepdims=True)) * exp2_scale)
            return p.astype(BF16), jnp.sum(p, axis=-1, keepdims=True)

        def values(item, pd):
            rows, cols = item
            o = jnp.dot(pd[0], v_scr[...], preferred_element_type=F32)
            o_ref[rows, cols] = (o / pd[1]).astype(BF16)

        if blocks_per_step == 1:
            ps = [softmax(s) for s in [scores(item) for item in items]]
            for item, pd in zip(items, ps):
                values(item, pd)
            return carry
        depth = 2
        ss, ps = {}, {}
        for i in range(len(items) + depth):
            if i < len(items):
                ss[i] = scores(items[i])
            if 1 <= i <= len(items):
                ps[i - 1] = softmax(ss.pop(i - 1))
            if i >= depth:
                values(items[i - depth], ps.pop(i - depth))
        return carry

    lax.fori_loop(0, n // (qb * blocks_per_step), block, 0)


def _attn_call(proj, qn, kn, l, n, rope_tabs=None, cache=None, emit_kv=False):
    T = proj.shape[0]
    B = T // n
    past = PAST_LEN if cache is not None else 0
    qw = GROUP * HEAD_DIM
    k_blk = D_ATTN // HEAD_DIM
    v_blk = (D_ATTN + D_KV) // HEAD_DIM
    in_specs = [
        pl.BlockSpec((n, qw), lambda b, h: (b, h)),
        pl.BlockSpec((n, HEAD_DIM), lambda b, h: (b, k_blk + h)),
        pl.BlockSpec((n, HEAD_DIM), lambda b, h: (b, v_blk + h)),
        pl.BlockSpec((1, HEAD_DIM), lambda b, h: (0, 0)),
        pl.BlockSpec((1, HEAD_DIM), lambda b, h: (0, 0)),
    ]
    args = [proj, proj, proj, qn, kn]
    if rope_tabs is not None:
        in_specs += [pl.BlockSpec((n, HEAD_DIM), lambda b, h: (0, 0))] * 2
        args += list(rope_tabs)
    if cache is not None:
        in_specs += [pl.BlockSpec((None, None, PAST_LEN, HEAD_DIM), lambda b, h: (b, l, 0, h))] * 2
        args += list(cache)
    out_specs = [pl.BlockSpec((n, qw), lambda b, h: (b, h))]
    out_shape = [jax.ShapeDtypeStruct((T, D_ATTN), BF16)]
    if emit_kv:
        out_specs += [pl.BlockSpec((None, n, HEAD_DIM), lambda b, h: (b, 0, h))] * 2
        out_shape += [jax.ShapeDtypeStruct((B, n, D_KV), F32)] * 2
    return pl.pallas_call(
        functools.partial(_attn_kernel, n=n, past=past, rope=rope_tabs is not None,
                          emit_kv=emit_kv, qb=256),
        grid=(B, N_KV_HEADS),
        in_specs=in_specs,
        out_specs=out_specs,
        out_shape=out_shape,
        scratch_shapes=[pltpu.VMEM((n + past, HEAD_DIM), BF16)] * 2,
        compiler_params=_params(2),
        name="attention",
    )(*args)


CONV_COL = D_ATTN + 2 * D_KV
RWKV_COL = CONV_COL + 3 * D_CONV
ROWS_A = 256
SUB = 16
CHUNKS_A = 2
assert CHUNK // SUB == 4


def _split2(x):
    hi = x.astype(BF16)
    return hi, (x - hi.astype(F32)).astype(BF16)


def _dot_x3(a, b):
    ah, al = _split2(a)
    bh, bl = _split2(b)
    mm = lambda p, q: jnp.dot(p, q, preferred_element_type=F32)
    return mm(ah, bh) + mm(ah, bl) + mm(al, bh)


def _split_sum(x, pieces, dot_piece):
    acc = None
    rest = x
    for _ in range(pieces):
        part = rest.astype(BF16)
        rest = rest - part.astype(F32)
        term = dot_piece(part)
        acc = term if acc is None else acc + term
    return acc


def _mix_kernel(*refs, n, nseq, has_s0, emit_state, manual_fetch):
    it = iter(refs)
    if manual_fetch:
        proj_hbm, tail_hbm = next(it), next(it)
    else:
        cv_ref, um_ref, ut_ref = next(it), next(it), next(it)
    (mum_ref, mut_ref, cw_ref, w0_ref, wup_ref, a0_ref, aup_ref, gup_ref, kk_ref, ka_ref,
     rk_ref, lng_ref, lnb_ref, seg_ref) = (next(it) for _ in range(14))
    s0_ref = next(it) if has_s0 else None
    o_ref = next(it)
    so_ref = next(it) if emit_state else None
    al_s, be_s, kd_s, r_s, v_s, p_s, bon_s, g_s, y_s, s_scr = (next(it) for _ in range(10))
    w2_s, w1_s, q_s = next(it), next(it), next(it)
    if manual_fetch:
        cv_ref, um_ref, ut_ref, in_sem = next(it), next(it), next(it), next(it)

        def input_copies(step):
            rows = pl.ds(pl.multiple_of(step * (nseq * n), nseq * n), nseq * n)
            return (
                pltpu.make_async_copy(proj_hbm.at[rows, pl.ds(CONV_COL, 3 * D_CONV)], cv_ref, in_sem.at[0]),
                pltpu.make_async_copy(proj_hbm.at[rows, pl.ds(RWKV_COL, 3 * D_RWKV)], um_ref, in_sem.at[1]),
                pltpu.make_async_copy(tail_hbm.at[rows, :], ut_ref, in_sem.at[2]))

        step = pl.program_id(0)

        @pl.when(step == 0)
        def _():
            for cp in input_copies(0):
                cp.start()

        for cp in input_copies(step):
            cp.wait()
    blocks_per_seq = n // ROWS_A
    n_blocks = nseq * blocks_per_seq
    n_chunks = n // CHUNK
    per_block = ROWS_A // CHUNK

    def seg_sum(x):
        return _split_sum(x, 2, lambda part: jnp.dot(part, seg_ref[...], preferred_element_type=F32))

    PAIR = 2 * RWKV_HEAD
    N_PAIRS = RWKV_HEADS // 2

    def pair_tiles(x):
        return [x[:, p * PAIR:(p + 1) * PAIR] for p in range(N_PAIRS)]

    bi = lax.broadcasted_iota(jnp.int32, (ROWS_A, ROWS_A), 0)
    bj = lax.broadcasted_iota(jnp.int32, (ROWS_A, ROWS_A), 1)
    same_chunk = (bi // CHUNK) == (bj // CHUNK)
    cum_mat = [(same_chunk & (bj <= bi)).astype(BF16), (same_chunk & (bj >= bi)).astype(BF16)]

    def pass1(c, carry):
        c0 = pl.multiple_of(c * ROWS_A, ROWS_A)
        rows = pl.ds(c0, ROWS_A)
        prev_row = pl.ds(jnp.maximum(c0 - 1, 0), 1)
        next_row = pl.ds(jnp.minimum(c0 + ROWS_A, nseq * n - 1), 1)
        seq_first = (c % blocks_per_seq) == 0
        seq_last = (c % blocks_per_seq) == blocks_per_seq - 1

        def neighbours(load):
            cur = load(rows)
            rid = lax.broadcasted_iota(jnp.int32, cur.shape, 0)
            before = jnp.where(seq_first, 0.0, load(prev_row))
            after = jnp.where(seq_last, 0.0, load(next_row))
            prev = jnp.where(rid == 0, before, pltpu.roll(cur, 1, 0))
            nxt = jnp.where(rid == ROWS_A - 1, after, pltpu.roll(cur, ROWS_A - 1, 0))
            return cur, prev, nxt

        z, zp, zn = neighbours(lambda rs: cv_ref[rs, D_CONV:2 * D_CONV] * cv_ref[rs, 2 * D_CONV:3 * D_CONV])
        cw = cw_ref[...]
        conv = cv_ref[rows, 0:D_CONV] * (zp * cw[0:1, :] + z * cw[1:2, :] + zn * cw[2:3, :])
        o_ref[rows, 0:D_CONV] = conv.astype(BF16)

        u, up, un = neighbours(lambda rs: um_ref[rs, :])
        u = u * (1.0 - mum_ref[...]) + (0.5 * mum_ref[...]) * (up + un)
        t, tp, tn = neighbours(lambda rs: ut_ref[rs, :])
        ut = t * (1.0 - mut_ref[...]) + (0.5 * mut_ref[...]) * (tp + tn)
        r = u[:, 0:D_RWKV]
        k = u[:, D_RWKV:2 * D_RWKV]
        v = u[:, 2 * D_RWKV:3 * D_RWKV]
        kk = k * kk_ref[...]
        kk = kk * lax.rsqrt(seg_sum(kk * kk) + 1e-12)
        for p, v_p in enumerate(pair_tiles(v)):
            v_s[p, rows, :] = v_p.astype(BF16)
        tw = jnp.tanh(ut)
        a_sum = jnp.zeros((ROWS_A, D_RWKV), F32)
        for d in range(2):
            zz = w0_ref[d:d + 1, :] + _dot_x3(tw, wup_ref[d])
            lw = -math.exp(-0.5) * jax.nn.sigmoid(zz)
            a = jax.nn.sigmoid(a0_ref[d:d + 1, :] + _dot_x3(ut, aup_ref[d]))
            a_sum = a_sum + a
            cum = _split_sum(lw, 3, lambda part: jnp.dot(cum_mat[d], part, preferred_element_type=F32))
            e_out = jnp.exp(-cum)
            kd = k * (1.0 + (a - 1.0) * ka_ref[...])
            scaled = ((al_s, -kk * jnp.exp(cum - lw)), (be_s, kk * a * e_out), (kd_s, kd * e_out),
                      (r_s, r * jnp.exp(cum)))
            for ref, val in scaled:
                for p, val_p in enumerate(pair_tiles(val)):
                    ref[d, p, rows, :] = val_p.astype(BF16)
            for j in range(per_block):
                last = j * CHUNK + (CHUNK - 1 if d == 0 else 0)
                for p, dec_p in enumerate(pair_tiles(jnp.exp(cum[last:last + 1, :]))):
                    p_s[d, c * per_block + j, p:p + 1, :] = dec_p
        kd_sum = k * (2.0 + (a_sum - 2.0) * ka_ref[...])
        bon_s[rows, :] = seg_sum(r * kd_sum * rk_ref[...]) * v
        g_s[rows, :] = _dot_x3(jax.nn.sigmoid(ut), gup_ref[...])
        y_s[rows, :] = jnp.zeros((ROWS_A, D_RWKV), F32)
        return carry

    lax.fori_loop(0, n_blocks, pass1, 0)

    if manual_fetch:
        @pl.when(step + 1 < pl.num_programs(0))
        def _():
            for cp in input_copies(step + 1):
                cp.start()

    for s in range(nseq):
        for d in range(2):
            for p in range(N_PAIRS):
                if has_s0:
                    s_scr[s, d, p] = jnp.concatenate([s0_ref[d, 2 * p], s0_ref[d, 2 * p + 1]], axis=1)
                else:
                    s_scr[s, d, p] = jnp.zeros((RWKV_HEAD, PAIR), F32)

    ri = lax.broadcasted_iota(jnp.int32, (CHUNK, PAIR), 0)
    lane = lax.broadcasted_iota(jnp.int32, (CHUNK, PAIR), 1)
    ci = lane % CHUNK
    low_half = lane < RWKV_HEAD
    eye = (ci == ri).astype(F32)
    diag_blk = (ri // SUB) == (ci // SUB)
    n_rounds = int(math.log2(SUB)) - 1

    def bd(x):
        zero = jnp.zeros_like(x)
        return jnp.concatenate([jnp.where(low_half, x, zero), jnp.where(low_half, zero, x)], axis=0)

    def pair_mm(a, b):
        return _mm(a, bd(b.astype(BF16)))

    def pair_mm_nt(a, b):
        return _mm(a, bd(b.astype(BF16)), _NT)

    def solve_step(c, carry):
        chains = []
        for s, d, j in ((s, d, j) for s in range(nseq) for d in range(2) for j in range(CHUNKS_A)):
            cc = s * n_chunks + c * CHUNKS_A + j
            rows = pl.ds(pl.multiple_of(cc * CHUNK, CHUNK), CHUNK)
            for p in range(N_PAIRS):
                chains.append(dict(
                    d=d, p=p, rows=rows,
                    strict=(ci < ri) if d == 0 else (ci > ri), incl=(ci <= ri) if d == 0 else (ci >= ri),
                    al=al_s[d, p, rows, :], be=be_s[d, p, rows, :], kd=kd_s[d, p, rows, :],
                    r=r_s[d, p, rows, :], v=v_s[p, rows, :]))

        def each(fn, *lists):
            return [fn(*args) for args in zip(chains, *lists)]

        alr = each(lambda ch: jnp.concatenate([ch['al'], ch['r']], axis=0))
        g = each(lambda ch, m: _mm(m, jnp.concatenate([bd(ch['be']), bd(ch['kd'])], axis=0), _NT), alr)
        a = each(lambda ch, m: jnp.where(ch['strict'], m[0:CHUNK, 0:PAIR], 0.0), g)
        a_ak = each(lambda ch, m: jnp.where(ch['strict'], m[0:CHUNK, PAIR:2 * PAIR], 0.0), g)
        q = each(lambda ch, m: jnp.concatenate(
            [jnp.where(ch['incl'], m[CHUNK:2 * CHUNK, 0:PAIR], 0.0),
             jnp.where(ch['incl'], m[CHUNK:2 * CHUNK, PAIR:2 * PAIR], 0.0)], axis=1).astype(BF16), g)
        akv = each(lambda ch, m: pair_mm(m, ch['v']), a_ak)
        xs = [jnp.where(diag_blk, m, 0.0) for m in a]
        low = [jnp.where(diag_blk, 0.0, m) for m in a]
        ts = [eye + x for x in xs]
        xs = [pair_mm(x, x) for x in xs]
        for _ in range(n_rounds - 1):
            z = [pair_mm(jnp.concatenate([x, t], axis=0), x) for x, t in zip(xs, ts)]
            xs = [m[0:CHUNK] for m in z]
            ts = [t + m[CHUNK:2 * CHUNK] for t, m in zip(ts, z)]
        ts = [t + pair_mm(t, x) for t, x in zip(ts, xs)]
        tl = each(lambda ch, t, lo, m: _mm(t, jnp.concatenate(
            [bd(lo.astype(BF16)), bd(m.astype(BF16)), bd(ch['al'])], axis=1)), ts, low, akv)
        ms = [x[:, 0:PAIR] for x in tl]
        sol = [x[:, PAIR:3 * PAIR] for x in tl]
        m2 = [pair_mm(m, m) for m in ms]
        ims = [eye + m for m in ms]
        ns = [im + pair_mm(im, mm2) for im, mm2 in zip(ims, m2)]
        w = [_mm(nn, jnp.concatenate([bd(x[:, 0:PAIR].astype(BF16)), bd(x[:, PAIR:2 * PAIR].astype(BF16))],
                                     axis=1)) for nn, x in zip(ns, sol)]
        for ch, x, qq in zip(chains, w, q):
            w2_s[ch['d'], ch['p'], ch['rows'], :] = x[:, 0:PAIR]
            w1_s[ch['d'], ch['p'], ch['rows'], :] = x[:, PAIR:2 * PAIR].astype(BF16)
            q_s[ch['d'], ch['p'], ch['rows'], :] = qq
        return carry

    lax.fori_loop(0, n_chunks // CHUNKS_A, solve_step, 0)

    def chunk_step(c, carry):
        chains = []
        for s, d in ((s, d) for s in range(nseq) for d in range(2)):
            cc = s * n_chunks + (c if d == 0 else n_chunks - 1 - c)
            rows = pl.ds(pl.multiple_of(cc * CHUNK, CHUNK), CHUNK)
            p_tile = p_s[d, cc]
            for p in range(N_PAIRS):
                chains.append(dict(
                    s=s, d=d, p=p, rows=rows, dec=p_tile[p:p + 1, :],
                    be=be_s[d, p, rows, :], kd=kd_s[d, p, rows, :], r=r_s[d, p, rows, :],
                    v=v_s[p, rows, :], w2=w2_s[d, p, rows, :], w1=w1_s[d, p, rows, :],
                    q=q_s[d, p, rows, :]))

        def each(fn, *lists):
            return [fn(*args) for args in zip(chains, *lists)]

        st = each(lambda ch: s_scr[ch['s'], ch['d'], ch['p']])
        st_bd = [bd(t.astype(BF16)) for t in st]
        ws = each(lambda ch, t: _mm(jnp.concatenate([ch['w1'], ch['r']], axis=0), t, _NT), st_bd)
        u = each(lambda ch, m: ch['w2'] + m[0:CHUNK], ws)
        ub = [x.astype(BF16) for x in u]
        ys = each(lambda ch, m, uu: m[CHUNK:2 * CHUNK]
                  + _mm(ch['q'], jnp.concatenate([bd(uu), bd(ch['v'])], axis=0)), ws, ub)
        full = each(lambda ch, uu: _mm(jnp.concatenate([uu, ch['v']], axis=0),
                                       jnp.concatenate([ch['be'], ch['kd']], axis=0), _TN), ub)
        s_new = each(lambda ch, t, m: (t + jnp.where(low_half, m[0:RWKV_HEAD], m[RWKV_HEAD:PAIR])) * ch['dec'],
                     st, full)
        for ch, t in zip(chains, s_new):
            s_scr[ch['s'], ch['d'], ch['p']] = t
        for ch, y in zip(chains, ys):
            y_s[ch['rows'], ch['p'] * PAIR:(ch['p'] + 1) * PAIR] += y
        return carry

    lax.fori_loop(0, n_chunks, chunk_step, 0)

    if emit_state:
        for s in range(nseq):
            for d in range(2):
                for p in range(N_PAIRS):
                    so_ref[s, d, 2 * p] = s_scr[s, d, p][:, 0:RWKV_HEAD]
                    so_ref[s, d, 2 * p + 1] = s_scr[s, d, p][:, RWKV_HEAD:PAIR]

    def pass3(c, carry):
        rows = pl.ds(pl.multiple_of(c * ROWS_A, ROWS_A), ROWS_A)
        y = y_s[rows, :]
        yc = y - seg_sum(y) * (1.0 / RWKV_HEAD)
        var = seg_sum(yc * yc) * (1.0 / RWKV_HEAD)
        yn = yc * lax.rsqrt(var + GN_EPS) * lng_ref[...] + lnb_ref[...]
        o_ref[rows, D_CONV:D_CONV + D_RWKV] = ((yn + bon_s[rows, :]) * g_s[rows, :]).astype(BF16)
        return carry

    lax.fori_loop(0, n_blocks, pass3, 0)


def _pad_rows(w, lo):
    return jnp.pad(w, [(0, 0)] * (w.ndim - 2) + [(lo, D_TAIL - lo - w.shape[-2]), (0, 0)])


def _mix_call(proj, tail, P, l, n, nseq, s0=None, emit_state=False):
    T = proj.shape[0]
    B = T // n
    rows = nseq * n
    assert s0 is None or nseq == 1
    full = lambda shape: pl.BlockSpec(shape, lambda b: (0,) * len(shape))
    lanes = lambda w: -(-w // 128) * 128
    in_bytes = rows * (3 * D_CONV + 3 * D_RWKV + lanes(D_TAIL)) * 4
    scratch_bytes = rows * (4 * 2 * D_RWKV * 2 + D_RWKV * 2 + 3 * D_RWKV * 4
                            + 2 * D_RWKV * (4 + 2 + 2 * 2))
    out_bytes = 2 * rows * (D_CONV + D_RWKV) * 2
    temporaries = 12 * 1024 * 1024
    manual_fetch = 2 * in_bytes + scratch_bytes + out_bytes + temporaries > VMEM_LIMIT_BYTES
    if manual_fetch:
        in_specs = [pl.BlockSpec(memory_space=pl.ANY), pl.BlockSpec(memory_space=pl.ANY)]
        token_args = [proj, tail]
    else:
        in_specs = [
            pl.BlockSpec((rows, 3 * D_CONV), lambda b: (b, CONV_COL // (3 * D_CONV))),
            pl.BlockSpec((rows, 3 * D_RWKV), lambda b: (b, RWKV_COL // (3 * D_RWKV))),
            pl.BlockSpec((rows, D_TAIL), lambda b: (b, 0)),
        ]
        token_args = [proj, proj, tail]
    in_specs += [
        full((1, 3 * D_RWKV)), full((1, D_TAIL)), full((3, D_CONV)),
        full((2, D_RWKV)), full((2, D_TAIL, D_RWKV)), full((2, D_RWKV)), full((2, D_TAIL, D_RWKV)),
        full((D_TAIL, D_RWKV)), full((1, D_RWKV)), full((1, D_RWKV)), full((1, D_RWKV)),
        full((1, D_RWKV)), full((1, D_RWKV)), full((D_RWKV, D_RWKV)),
    ]
    head = jnp.arange(D_RWKV) // RWKV_HEAD
    seg = (head[:, None] == head[None, :]).astype(BF16)
    mu = P['rwkv_mu'][l]
    row = lambda a: a.reshape(1, -1)
    args = token_args + [row(mu[:3 * D_RWKV]), row(mu[3 * D_RWKV:]), P['conv_w'][l],
            P['rwkv_w0'][l], _pad_rows(P['rwkv_w_up'][l], 0), P['rwkv_a0'][l],
            _pad_rows(P['rwkv_a_up'][l], DECAY_RANK), _pad_rows(P['rwkv_g_up'][l], DECAY_RANK + A_RANK),
            row(P['rwkv_k_k'][l]), row(P['rwkv_k_a'][l]), row(P['rwkv_r_k'][l]),
            row(P['rwkv_ln_g'][l]), row(P['rwkv_ln_b'][l]), seg]
    if s0 is not None:
        in_specs.append(pl.BlockSpec((None, None, 2, RWKV_HEADS, RWKV_HEAD, RWKV_HEAD),
                                     lambda b: (b, l, 0, 0, 0, 0)))
        args.append(s0)
    out_specs = [pl.BlockSpec((rows, D_CONV + D_RWKV), lambda b: (b, 0))]
    out_shape = [jax.ShapeDtypeStruct((T, D_CONV + D_RWKV), BF16)]
    state_shape = (2, RWKV_HEADS, RWKV_HEAD, RWKV_HEAD)
    if emit_state:
        out_specs.append(pl.BlockSpec((nseq,) + state_shape, lambda b: (b, 0, 0, 0, 0)))
        out_shape.append(jax.ShapeDtypeStruct((B,) + state_shape, F32))
    pair = 2 * RWKV_HEAD
    n_pairs = RWKV_HEADS // 2
    per_pair = lambda: pltpu.VMEM((2, n_pairs, rows, pair), BF16)
    scratch_shapes = [per_pair(), per_pair(), per_pair(), per_pair(),
                      pltpu.VMEM((n_pairs, rows, pair), BF16),
                      pltpu.VMEM((2, rows // CHUNK, n_pairs, pair), F32),
                      pltpu.VMEM((rows, D_RWKV), F32), pltpu.VMEM((rows, D_RWKV), F32),
                      pltpu.VMEM((rows, D_RWKV), F32),
                      pltpu.VMEM((nseq, 2, n_pairs, RWKV_HEAD, pair), F32),
                      pltpu.VMEM((2, n_pairs, rows, pair), F32), pltpu.VMEM((2, n_pairs, rows, pair), BF16),
                      pltpu.VMEM((2, n_pairs, rows, 2 * pair), BF16)]
    if manual_fetch:
        scratch_shapes += [pltpu.VMEM((rows, 3 * D_CONV), F32), pltpu.VMEM((rows, 3 * D_RWKV), F32),
                           pltpu.VMEM((rows, D_TAIL), F32), pltpu.SemaphoreType.DMA((3,))]
    semantics = ("arbitrary",) if manual_fetch else ("parallel",)
    return pl.pallas_call(
        functools.partial(_mix_kernel, n=n, nseq=nseq, has_s0=s0 is not None, emit_state=emit_state,
                          manual_fetch=manual_fetch),
        grid=(B // nseq,),
        in_specs=in_specs,
        out_specs=out_specs,
        out_shape=out_shape,
        scratch_shapes=scratch_shapes,
        compiler_params=pltpu.CompilerParams(dimension_semantics=semantics,
                                             vmem_limit_bytes=VMEM_LIMIT_BYTES),
        name="conv_rwkv",
    )(*args)


def _residual_copy(x_hbm, x_buf, sem):
    rows = x_buf.shape[0]
    start = pl.multiple_of(pl.program_id(0) * rows, rows)
    return pltpu.make_async_copy(x_hbm.at[pl.ds(start, rows), :], x_buf, sem)


def _outproj_kernel(at_ref, cr_ref, w_ref, x_hbm, ga_ref, shf_ref, scf_ref, g1_ref, g2_ref,
                    x1_ref, h2_ref, x_ref, x_sem):
    kt = pl.program_id(1)

    @pl.when(kt == 0)
    def _():
        _residual_copy(x_hbm, x_ref, x_sem).start()
        x1_ref[...] = jnp.zeros(x1_ref.shape, F32)

    lhs = jnp.where(kt < 2, at_ref[...], cr_ref[...])
    x1_ref[...] += jnp.dot(lhs, w_ref[...].astype(BF16), preferred_element_type=F32)

    @pl.when(kt == pl.num_programs(1) - 1)
    def _():
        _residual_copy(x_hbm, x_ref, x_sem).wait()
        x1 = x_ref[...] + _rms(x1_ref[...], ga_ref[...] * g1_ref[...])
        x1_ref[...] = x1
        h2_ref[...] = (_rms(x1, g2_ref[...] * (1.0 + scf_ref[...])) + shf_ref[...]).astype(BF16)


def _outproj_call(attn, cr, x, mod3, ng3, w_out, l, row_fn, tm):
    T = x.shape[0]
    tk = 512
    return pl.pallas_call(
        _outproj_kernel,
        grid=(T // tm, D_MODEL // tk),
        in_specs=[
            pl.BlockSpec((tm, tk), lambda i, k: (i, jnp.minimum(k, 1))),
            pl.BlockSpec((tm, tk), lambda i, k: (i, jnp.maximum(k - 2, 0))),
            pl.BlockSpec((None, tk, D_MODEL), lambda i, k: (l, k, 0)),
            pl.BlockSpec(memory_space=pl.ANY),
            _mod_spec(row_fn, 2), _mod_spec(row_fn, 3), _mod_spec(row_fn, 4),
            _ng_spec(l, 1), _ng_spec(l, 2),
        ],
        out_specs=[
            pl.BlockSpec((tm, D_MODEL), lambda i, k: (i, 0)),
            pl.BlockSpec((tm, D_MODEL), lambda i, k: (i, 0)),
        ],
        out_shape=[jax.ShapeDtypeStruct((T, D_MODEL), F32), jax.ShapeDtypeStruct((T, D_MODEL), BF16)],
        scratch_shapes=[pltpu.VMEM((tm, D_MODEL), F32), pltpu.SemaphoreType.DMA(())],
        compiler_params=_params(2),
        name="out_proj",
    )(attn, cr, w_out, x, mod3, mod3, mod3, ng3, ng3)


def _ffn_kernel(h_ref, x1_hbm, wg_ref, wu_ref, wd_ref, gf_ref, g3_ref, o_ref, x1_ref, x1_sem):
    f = pl.program_id(1)

    @pl.when(f == 0)
    def _():
        _residual_copy(x1_hbm, x1_ref, x1_sem).start()
        o_ref[...] = jnp.zeros(o_ref.shape, F32)

    h = h_ref[...]
    gate = _dot(h, wg_ref[...])
    up = _dot(h, wu_ref[...])
    o_ref[...] += _dot(gate * jax.nn.sigmoid(gate) * up, wd_ref[...])

    @pl.when(f == pl.num_programs(1) - 1)
    def _():
        _residual_copy(x1_hbm, x1_ref, x1_sem).wait()
        o_ref[...] = x1_ref[...] + _rms(o_ref[...], gf_ref[...] * g3_ref[...])


def _ffn_call(h2, x1, mod3, ng3, w_gate, w_up, w_down, l, row_fn, tm):
    T = x1.shape[0]
    tf = 256
    return pl.pallas_call(
        _ffn_kernel,
        grid=(T // tm, D_FF // tf),
        in_specs=[
            pl.BlockSpec((tm, D_MODEL), lambda i, f: (i, 0)),
            pl.BlockSpec(memory_space=pl.ANY),
            pl.BlockSpec((None, D_MODEL, tf), lambda i, f: (l, 0, f)),
            pl.BlockSpec((None, D_MODEL, tf), lambda i, f: (l, 0, f)),
            pl.BlockSpec((None, tf, D_MODEL), lambda i, f: (l, f, 0)),
            _mod_spec(row_fn, 5),
            _ng_spec(l, 3),
        ],
        out_specs=pl.BlockSpec((tm, D_MODEL), lambda i, f: (i, 0)),
        out_shape=jax.ShapeDtypeStruct((T, D_MODEL), F32),
        scratch_shapes=[pltpu.VMEM((tm, D_MODEL), F32), pltpu.SemaphoreType.DMA(())],
        compiler_params=_params(2),
        name="ffn",
    )(h2, x1, w_gate, w_up, w_down, mod3, ng3)


def _rope_tables(n):
    rows = n // GRID_W
    row = jnp.repeat(jnp.arange(rows), GRID_W).astype(F32)
    col = jnp.tile(jnp.arange(GRID_W), rows).astype(F32)
    half = HEAD_DIM // 2
    inv = 1.0 / (ROPE_THETA ** (jnp.arange(0, half, 2, dtype=F32) / half))
    ar = row[:, None] * inv
    ac = col[:, None] * inv
    ang = jnp.concatenate([ar, ar, ac, ac], axis=-1)
    return jnp.cos(ang), jnp.sin(ang)


ROW_TILE = 1024
INPROJ_ROW_TILE = 2048


def _layer(x, mod3, ng3, P, l, n, mix_seqs, row_fn_of_tm, rope_tabs=None, cache=None, s0=None, emit=False):
    proj, tail = _inproj_call(x, mod3, ng3, P['w_in_t'], P['w_tail_t'], l,
                              row_fn_of_tm(INPROJ_ROW_TILE // 2), INPROJ_ROW_TILE)
    attn_out = _attn_call(proj, P['q_norm'][l].reshape(1, -1), P['k_norm'][l].reshape(1, -1), l, n,
                          rope_tabs=rope_tabs, cache=cache, emit_kv=emit)
    mix_out = _mix_call(proj, tail, P, l, n, mix_seqs, s0=s0, emit_state=emit)
    attn = attn_out[0]
    cr = mix_out[0]
    x1, h2 = _outproj_call(attn, cr, x, mod3, ng3, P['w_out'], l, row_fn_of_tm(ROW_TILE), ROW_TILE)
    x2 = _ffn_call(h2, x1, mod3, ng3, P['w_gate'], P['w_up'], P['w_down'], l, row_fn_of_tm(ROW_TILE),
                   ROW_TILE)
    if emit:
        return x2, attn_out[1], attn_out[2], mix_out[1]
    return x2


def kernel(x_prompt, x_sample, cache_k, cache_v, state_rwkv, c, c_ctx, w_ada, b_ada, norm_g, w_in, q_norm, k_norm, conv_w, rwkv_mu, rwkv_w0, rwkv_w_up, rwkv_a0, rwkv_a_up, rwkv_g_up, rwkv_k_k, rwkv_k_a, rwkv_r_k, rwkv_ln_g, rwkv_ln_b, w_out, w_gate, w_up, w_down):
    P = {'q_norm': q_norm, 'k_norm': k_norm, 'conv_w': conv_w, 'rwkv_mu': rwkv_mu,
         'rwkv_w0': rwkv_w0, 'rwkv_w_up': rwkv_w_up, 'rwkv_a0': rwkv_a0, 'rwkv_a_up': rwkv_a_up,
         'rwkv_g_up': rwkv_g_up, 'rwkv_k_k': rwkv_k_k, 'rwkv_k_a': rwkv_k_a, 'rwkv_r_k': rwkv_r_k,
         'rwkv_ln_g': rwkv_ln_g, 'rwkv_ln_b': rwkv_ln_b, 'w_out': w_out,
         'w_gate': w_gate, 'w_up': w_up, 'w_down': w_down}
    w_in_t = jnp.swapaxes(w_in, 1, 2)
    P['w_in_t'] = w_in_t
    P['w_tail_t'] = w_in_t[:, D_MAIN:, :]
    batch, seq, _ = x_prompt.shape
    dec_batch, dec_seq, _ = x_sample.shape

    cc = jnp.zeros((8, D_MODEL), F32).at[0].set(c_ctx).at[1:1 + dec_batch].set(c)
    mod3 = _ada_call(cc, w_ada, b_ada).reshape(DEPTH * 8, 1, N_MOD * D_MODEL)
    ng3 = norm_g.reshape(DEPTH * 4, 1, D_MODEL)

    ck = cache_k.reshape(dec_batch, DEPTH, PAST_LEN, D_KV)
    cv = cache_v.reshape(dec_batch, DEPTH, PAST_LEN, D_KV)
    rope_tabs = _rope_tables(dec_seq)

    xp = x_prompt.reshape(batch * seq, D_MODEL)
    xs = x_sample.reshape(dec_batch * dec_seq, D_MODEL)
    ks, vs, ss = [], [], []
    for l in range(DEPTH):
        ctx_row = lambda tm, l=l: (lambda i: l * 8)
        smp_row = lambda tm, l=l: (lambda i: l * 8 + 1 + (i * tm) // dec_seq)
        xp, k_l, v_l, s_l = _layer(xp, mod3, ng3, P, l, seq, 2, ctx_row, emit=True)
        ks.append(k_l.reshape(batch, seq, N_KV_HEADS, HEAD_DIM))
        vs.append(v_l.reshape(batch, seq, N_KV_HEADS, HEAD_DIM))
        ss.append(s_l)
        xs = _layer(xs, mod3, ng3, P, l, dec_seq, 1, smp_row, rope_tabs=rope_tabs, cache=(ck, cv),
                    s0=state_rwkv)
    return (xp.reshape(batch, seq, D_MODEL), xs.reshape(dec_batch, dec_seq, D_MODEL),
            jnp.stack(ks, axis=1), jnp.stack(vs, axis=1), jnp.stack(ss, axis=1))
```
